```python
import jax, jax.numpy as jnp
from jax import lax
import numpy as np

D_MODEL = 1024
BATCH = 16
SEQ = 256
DEPTH = 1
DEC_BATCH = 4
DEC_SEQ = 4096
PAST_LEN = 256

GRID_W = 64
N_FOURIER_GROUPS = 4
FOURIER_GROUP_W = 128
D_FOURIER = N_FOURIER_GROUPS * FOURIER_GROUP_W
N_HEADS = 4
DK_HEAD = 128
DV_HEAD = 256
DK_TOT = N_HEADS * DK_HEAD
DV_TOT = N_HEADS * DV_HEAD
GATE_RANK = 16
GATE_TEMP = 16.0
CHUNK = 32
D_FF = 2816
N_MOD = 9
RMS_EPS = 1e-6
POS_BASE = 10000.0
IN_COLS = D_FOURIER + 2 * DK_TOT + 2 * DV_TOT + 2 * GATE_RANK + 2 * D_MODEL

kernel_name = "hybrid_fnet_gla_macaron_dit_step"


def rms_norm(x, gain):
    x32 = x.astype(jnp.float32)
    y = x32 * lax.rsqrt(jnp.mean(x32 * x32, axis=-1, keepdims=True) + RMS_EPS)
    return (y * gain.astype(jnp.float32)).astype(x.dtype)


def modulate(h, shift, scale):
    return h * (1.0 + scale) + shift


def swiglu(h, w_gate, w_up, w_down):
    return (jax.nn.silu(h @ w_gate) * (h @ w_up)) @ w_down


def grid_pos_embed(n_tokens, dtype):
    rows = n_tokens // GRID_W
    row = jnp.repeat(jnp.arange(rows, dtype=jnp.float32), GRID_W)
    col = jnp.tile(jnp.arange(GRID_W, dtype=jnp.float32), rows)
    n_freq = D_MODEL // 4
    omega = POS_BASE ** (-jnp.arange(n_freq, dtype=jnp.float32) / n_freq)
    ra = row[:, None] * omega
    ca = col[:, None] * omega
    return jnp.concatenate([jnp.sin(ra), jnp.cos(ra), jnp.sin(ca), jnp.cos(ca)], axis=-1).astype(dtype)


def fourier_mix(f):
    B, T, _ = f.shape
    g = f.astype(jnp.float32).reshape(B, T, N_FOURIER_GROUPS, FOURIER_GROUP_W).transpose(0, 2, 1, 3)
    m = jnp.fft.fft2(g, norm="ortho").real
    return m.transpose(0, 2, 1, 3).reshape(B, T, D_FOURIER)


def _to_chunks(x):
    B, T, H, d = x.shape
    return x.reshape(B, T // CHUNK, CHUNK, H, d).transpose(1, 0, 3, 2, 4)


def gla_chunked(q, k, v, log_a, s0):
    B, T = q.shape[0], q.shape[1]
    mask = jnp.tril(jnp.ones((CHUNK, CHUNK), dtype=bool))[:, :, None]

    def step(S, inp):
        qc, kc, vc, ac = inp
        b = jnp.cumsum(ac, axis=2)
        o_inter = jnp.einsum('bhid,bhdv->bhiv', qc * jnp.exp(b), S)
        diff = b[:, :, :, None, :] - b[:, :, None, :, :]
        decay = jnp.where(mask, jnp.exp(jnp.minimum(diff, 0.0)), 0.0)
        scores = jnp.einsum('bhid,bhjd,bhijd->bhij', qc, kc, decay)
        o_intra = jnp.einsum('bhij,bhjv->bhiv', scores, vc)
        b_last = b[:, :, -1, :]
        S_new = jnp.exp(b_last)[..., None] * S + jnp.einsum(
            'bhjd,bhjv->bhdv', kc * jnp.exp(b_last[:, :, None, :] - b), vc)
        return S_new, o_inter + o_intra

    s_final, o = lax.scan(step, s0.astype(jnp.float32),
                          (_to_chunks(q), _to_chunks(k), _to_chunks(v), _to_chunks(log_a)))
    o = o.transpose(1, 0, 3, 2, 4).reshape(B, T, N_HEADS, DV_HEAD)
    return o, s_final


def mixer(h, s_fwd, s_bwd, w_in, w_alpha_fwd, b_alpha_fwd, w_alpha_bwd, b_alpha_bwd,
          gla_norm, w_proj_fourier, w_proj_gla, w_out):
    B, T, _ = h.shape
    z = h @ w_in
    cuts = [D_FOURIER, D_FOURIER + DK_TOT, D_FOURIER + 2 * DK_TOT, D_FOURIER + 2 * DK_TOT + DV_TOT,
            D_FOURIER + 2 * DK_TOT + 2 * DV_TOT, D_FOURIER + 2 * DK_TOT + 2 * DV_TOT + 2 * GATE_RANK]
    f, q, k, v, r, a_lr, g = jnp.split(z, cuts, axis=-1)

    branch_a = fourier_mix(f).astype(h.dtype) @ w_proj_fourier

    qh = q.astype(jnp.float32).reshape(B, T, N_HEADS, DK_HEAD) * (DK_HEAD ** -0.5)
    kh = k.astype(jnp.float32).reshape(B, T, N_HEADS, DK_HEAD)
    vh = v.astype(jnp.float32).reshape(B, T, N_HEADS, DV_HEAD)
    a32 = a_lr.astype(jnp.float32)
    la_f = jax.nn.log_sigmoid(a32[..., :GATE_RANK] @ w_alpha_fwd.astype(jnp.float32)
                              + b_alpha_fwd.astype(jnp.float32)) / GATE_TEMP
    la_b = jax.nn.log_sigmoid(a32[..., GATE_RANK:] @ w_alpha_bwd.astype(jnp.float32)
                              + b_alpha_bwd.astype(jnp.float32)) / GATE_TEMP
    la_f = la_f.reshape(B, T, N_HEADS, DK_HEAD)
    la_b = la_b.reshape(B, T, N_HEADS, DK_HEAD)
    o_f, sf = gla_chunked(qh, kh, vh, la_f, s_fwd)
    o_b, sb = gla_chunked(jnp.flip(qh, 1), jnp.flip(kh, 1), jnp.flip(vh, 1), jnp.flip(la_b, 1), s_bwd)
    o = o_f + jnp.flip(o_b, 1)
    o = o * lax.rsqrt(jnp.mean(o * o, axis=-1, keepdims=True) + RMS_EPS)
    o = o.reshape(B, T, DV_TOT) * gla_norm.astype(jnp.float32)
    o = (o * jax.nn.silu(r.astype(jnp.float32))).astype(h.dtype)
    branch_b = o @ w_proj_gla

    g_a, g_b = jnp.split(jax.nn.sigmoid(g), 2, axis=-1)
    y = (g_a * branch_a + g_b * branch_b) @ w_out
    return y, sf, sb


def trunk_layer(x, mod, s_fwd, s_bwd, norm_ffn1, w_ffn1_gate, w_ffn1_up, w_ffn1_down,
                norm_mix, w_in, w_alpha_fwd, b_alpha_fwd, w_alpha_bwd, b_alpha_bwd, gla_norm,
                w_proj_fourier, w_proj_gla, w_out, norm_ffn2, w_ffn2_gate, w_ffn2_up, w_ffn2_down):
    sh1, sc1, gt1, sh2, sc2, gt2, sh3, sc3, gt3 = jnp.split(mod, N_MOD, axis=-1)
    h = modulate(rms_norm(x, norm_ffn1), sh1, sc1)
    x = x + (0.5 * gt1 * swiglu(h, w_ffn1_gate, w_ffn1_up, w_ffn1_down)).astype(x.dtype)
    h = modulate(rms_norm(x, norm_mix), sh2, sc2).astype(x.dtype)
    y, sf, sb = mixer(h, s_fwd, s_bwd, w_in, w_alpha_fwd, b_alpha_fwd, w_alpha_bwd, b_alpha_bwd,
                      gla_norm, w_proj_fourier, w_proj_gla, w_out)
    x = x + (gt2 * y).astype(x.dtype)
    h = modulate(rms_norm(x, norm_ffn2), sh3, sc3)
    x = x + (0.5 * gt3 * swiglu(h, w_ffn2_gate, w_ffn2_up, w_ffn2_down)).astype(x.dtype)
    return x, sf, sb


def setup_inputs(seed: int = 0) -> dict:
    key = jax.random.key(seed)
    ks = jax.random.split(key, 32)
    L, D = DEPTH, D_MODEL

    def nrm(k, shape, scale):
        return jax.random.normal(k, shape, jnp.float32) * scale

    def gain(k, shape):
        return 1.0 + 0.02 * jax.random.normal(k, shape, jnp.float32)

    st_shape = (DEC_BATCH, L, N_HEADS, DK_HEAD, DV_HEAD)
    return {
        "x_prompt": nrm(ks[0], (BATCH, SEQ, D), 1.0),
        "x_sample": nrm(ks[1], (DEC_BATCH, DEC_SEQ, D), 1.0),
        "state_gla_fwd": nrm(ks[2], st_shape, 1.0),
        "state_gla_bwd": nrm(ks[3], st_shape, 1.0),
        "c": nrm(ks[4], (DEC_BATCH, D), 1.0),
        "c_ctx": nrm(ks[5], (D,), 1.0),
        "w_ada": nrm(ks[6], (L, D, N_MOD * D), D ** -0.5),
        "b_ada": nrm(ks[7], (L, N_MOD * D), 0.02),
        "norm_ffn1": gain(ks[8], (L, D)),
        "w_ffn1_gate": nrm(ks[9], (L, D, D_FF), D ** -0.5),
        "w_ffn1_up": nrm(ks[10], (L, D, D_FF), D ** -0.5),
        "w_ffn1_down": nrm(ks[11], (L, D_FF, D), D_FF ** -0.5),
        "norm_mix": gain(ks[12], (L, D)),
        "w_in": nrm(ks[13], (L, D, IN_COLS), D ** -0.5),
        "w_alpha_fwd": nrm(ks[14], (L, GATE_RANK, DK_TOT), GATE_RANK ** -0.5),
        "b_alpha_fwd": nrm(ks[15], (L, DK_TOT), 0.1),
        "w_alpha_bwd": nrm(ks[16], (L, GATE_RANK, DK_TOT), GATE_RANK ** -0.5),
        "b_alpha_bwd": nrm(ks[17], (L, DK_TOT), 0.1),
        "gla_norm": gain(ks[18], (L, DV_TOT)),
        "w_proj_fourier": nrm(ks[19], (L, D_FOURIER, D), D_FOURIER ** -0.5),
        "w_proj_gla": nrm(ks[20], (L, DV_TOT, D), DV_TOT ** -0.5),
        "w_out": nrm(ks[21], (L, D, D), D ** -0.5),
        "norm_ffn2": gain(ks[22], (L, D)),
        "w_ffn2_gate": nrm(ks[23], (L, D, D_FF), D ** -0.5),
        "w_ffn2_up": nrm(ks[24], (L, D, D_FF), D ** -0.5),
        "w_ffn2_down": nrm(ks[25], (L, D_FF, D), D_FF ** -0.5),
        "final_norm": gain(ks[26], (D,)),
    }


def reference(x_prompt, x_sample, state_gla_fwd, state_gla_bwd, c, c_ctx, w_ada, b_ada,
              norm_ffn1, w_ffn1_gate, w_ffn1_up, w_ffn1_down, norm_mix, w_in,
              w_alpha_fwd, b_alpha_fwd, w_alpha_bwd, b_alpha_bwd, gla_norm,
              w_proj_fourier, w_proj_gla, w_out, norm_ffn2, w_ffn2_gate, w_ffn2_up, w_ffn2_down,
              final_norm):
    xc = x_prompt
    xl = x_sample + grid_pos_embed(x_sample.shape[1], x_sample.dtype)
    zero_state = jnp.zeros((x_prompt.shape[0], N_HEADS, DK_HEAD, DV_HEAD), jnp.float32)
    new_fwd, new_bwd = [], []
    for l in range(DEPTH):
        prm = (norm_ffn1[l], w_ffn1_gate[l], w_ffn1_up[l], w_ffn1_down[l], norm_mix[l], w_in[l],
               w_alpha_fwd[l], b_alpha_fwd[l], w_alpha_bwd[l], b_alpha_bwd[l], gla_norm[l],
               w_proj_fourier[l], w_proj_gla[l], w_out[l], norm_ffn2[l], w_ffn2_gate[l],
               w_ffn2_up[l], w_ffn2_down[l])
        mod_ctx = (jax.nn.silu(c_ctx) @ w_ada[l] + b_ada[l])[None, None, :]
        mod_lat = (jax.nn.silu(c) @ w_ada[l] + b_ada[l])[:, None, :]
        xc, sf, sb = trunk_layer(xc, mod_ctx, zero_state, zero_state, *prm)
        new_fwd.append(sf)
        new_bwd.append(sb)
        xl, _, _ = trunk_layer(xl, mod_lat, state_gla_fwd[:, l], state_gla_bwd[:, l], *prm)
    y_prompt = rms_norm(xc, final_norm)
    y_sample = rms_norm(xl, final_norm)
    new_state_fwd = jnp.stack(new_fwd, axis=1)
    new_state_bwd = jnp.stack(new_bwd, axis=1)
    return (y_prompt, y_sample, new_state_fwd, new_state_bwd)
```

```python
import functools
import math

import numpy as np
import jax
import jax.numpy as jnp
from jax import lax
from jax.experimental import pallas as pl
from jax.experimental.pallas import tpu as pltpu

F32 = jnp.float32
BF16 = jnp.bfloat16

GRID_W = 64
N_GROUPS = 4
GROUP_W = 128
D_FOURIER = N_GROUPS * GROUP_W
N_HEADS = 4
DK = 128
DV = 256
DK_TOT = N_HEADS * DK
DV_TOT = N_HEADS * DV
GATE_RANK = 16
GATE_TEMP = 16.0
RMS_EPS = 1e-6
POS_BASE = 10000.0
N_MOD = 9

TOKEN_TILE = 512
GLA_CHUNK = 64
ALR_PAD = 128
VMEM_LIMIT = 56 * 1024 * 1024
HIGHEST = lax.Precision.HIGHEST


def _params(n_axes):
    return pltpu.CompilerParams(dimension_semantics=("arbitrary",) * n_axes,
                                vmem_limit_bytes=VMEM_LIMIT)


def _resident(shape):
    nd = len(shape)
    return pl.BlockSpec(shape, lambda *_: (0,) * nd, pipeline_mode=pl.Buffered(1))


def _sigmoid(x):
    return 1.0 / (1.0 + jnp.exp(-x))


def _rms(x, gain):
    return x * lax.rsqrt(jnp.mean(x * x, axis=-1, keepdims=True) + RMS_EPS) * gain


def _dot(a, b):
    return jnp.dot(a, b, preferred_element_type=F32)


def _ada_kernel(c_ref, w_ref, b_ref, o_ref):
    c = c_ref[...]
    s = (c * _sigmoid(c)).astype(BF16)
    o_ref[...] = _dot(s, w_ref[...].astype(BF16)) + b_ref[...]


def _ada(cvecs, w_ada, b_ada):
    rows, d = cvecs.shape
    n = w_ada.shape[1]
    tn = d
    return pl.pallas_call(
        _ada_kernel,
        grid=(n // tn,),
        in_specs=[pl.BlockSpec((rows, d), lambda j: (0, 0)),
                  pl.BlockSpec((d, tn), lambda j: (0, j)),
                  pl.BlockSpec((1, tn), lambda j: (0, j))],
        out_specs=pl.BlockSpec((rows, tn), lambda j: (0, j)),
        out_shape=jax.ShapeDtypeStruct((rows, n), F32),
        compiler_params=_params(1),
        name="ada",
    )(cvecs, w_ada, b_ada.reshape(1, n))


def _ffn_kernel(*refs, first_mod_row, add_pos, final_norm):
    x_ref, mod_ref, gain_ref, wg_ref, wu_ref, wd_ref = refs[:6]
    rest = list(refs[6:])
    pos_ref = rest.pop(0) if add_pos else None
    fn_ref = rest.pop(0) if final_norm else None
    o_ref = rest.pop(0)

    x = x_ref[...]
    if add_pos:
        x = x + pos_ref[...]
    shift = mod_ref[first_mod_row:first_mod_row + 1, :]
    scale = mod_ref[first_mod_row + 1:first_mod_row + 2, :]
    gate = mod_ref[first_mod_row + 2:first_mod_row + 3, :]
    h = (_rms(x, gain_ref[...]) * (1.0 + scale) + shift).astype(BF16)
    g = _dot(h, wg_ref[...])
    u = _dot(h, wu_ref[...])
    a = (g * _sigmoid(g) * u).astype(BF16)
    y = _dot(a, wd_ref[...])
    xn = x + (0.5 * gate) * y
    if final_norm:
        xn = _rms(xn, fn_ref[...])
    o_ref[...] = xn


def _ffn(x, mod, gain, wg, wu, wd, *, tiles_per_group, first_mod_row, pos=None, final_gain=None):
    n, d = x.shape
    dff = wg.shape[1]
    tm = TOKEN_TILE
    add_pos = pos is not None
    final_norm = final_gain is not None
    in_specs = [pl.BlockSpec((tm, d), lambda i: (i, 0)),
                pl.BlockSpec((None, N_MOD, d), lambda i: (i // tiles_per_group, 0, 0)),
                _resident((1, d)), _resident((d, dff)), _resident((d, dff)), _resident((dff, d))]
    args = [x, mod, gain.reshape(1, d), wg, wu, wd]
    if add_pos:
        pos_tiles = pos.shape[0] // tm
        in_specs.append(pl.BlockSpec((tm, d), lambda i: (i % pos_tiles, 0)))
        args.append(pos)
    if final_norm:
        in_specs.append(_resident((1, d)))
        args.append(final_gain.reshape(1, d))
    return pl.pallas_call(
        functools.partial(_ffn_kernel, first_mod_row=first_mod_row, add_pos=add_pos, final_norm=final_norm),
        grid=(n // tm,),
        in_specs=in_specs,
        out_specs=pl.BlockSpec((tm, d), lambda i: (i, 0)),
        out_shape=jax.ShapeDtypeStruct((n, d), F32),
        compiler_params=_params(1),
        name="ffn",
    )(*args)


_C_F = 0
_C_Q = _C_F + D_FOURIER
_C_K = _C_Q + DK_TOT
_C_V = _C_K + DK_TOT
_C_R = _C_V + DV_TOT
_C_G = _C_R + DV_TOT


def _mixin_kernel(x_ref, mod_ref, gain_ref, w_ref, wal_ref, bal_ref,
                  f_ref, q_ref, k_ref, v_ref, r_ref, g_ref, laf_ref, lab_ref, *, d_model):
    c_a = _C_G + 2 * d_model
    x = x_ref[...]
    shift = mod_ref[3:4, :]
    scale = mod_ref[4:5, :]
    h = (_rms(x, gain_ref[...]) * (1.0 + scale) + shift).astype(BF16)

    def proj(lo, hi):
        return _dot(h, w_ref[:, lo:hi])

    f_ref[...] = proj(_C_F, _C_Q)
    q_ref[...] = proj(_C_Q, _C_K) * (DK ** -0.5)
    k_ref[...] = proj(_C_K, _C_V)
    v_ref[...] = proj(_C_V, _C_R).astype(BF16)
    r_ref[...] = proj(_C_R, _C_G)
    g_ref[...] = proj(_C_G, c_a)
    alr = proj(c_a, c_a + ALR_PAD).astype(BF16)
    pre = _dot(alr, wal_ref[...]) + bal_ref[...]
    la = (jnp.minimum(pre, 0.0) - jnp.log1p(jnp.exp(-jnp.abs(pre)))) * (1.0 / GATE_TEMP)
    laf_ref[...] = la[:, :DK_TOT]
    lab_ref[...] = la[:, DK_TOT:]


def _mixin(x, mod, gain, w_in_packed, w_alpha, b_alpha, *, tiles_per_group):
    n, d = x.shape
    tm = TOKEN_TILE
    ncols = w_in_packed.shape[1]

    def tok(width):
        return pl.BlockSpec((tm, width), lambda i: (i, 0))

    outs = [(D_FOURIER, F32), (DK_TOT, F32), (DK_TOT, F32), (DV_TOT, BF16), (DV_TOT, F32),
            (2 * d, F32), (DK_TOT, F32), (DK_TOT, F32)]
    return pl.pallas_call(
        functools.partial(_mixin_kernel, d_model=d),
        grid=(n // tm,),
        in_specs=[tok(d),
                  pl.BlockSpec((None, N_MOD, d), lambda i: (i // tiles_per_group, 0, 0)),
                  _resident((1, d)), _resident((d, ncols)),
                  _resident((ALR_PAD, 2 * DK_TOT)), _resident((1, 2 * DK_TOT))],
        out_specs=[tok(w) for w, _ in outs],
        out_shape=[jax.ShapeDtypeStruct((n, w), dt) for w, dt in outs],
        compiler_params=_params(1),
        name="mixin",
    )(x, mod, gain.reshape(1, d), w_in_packed, w_alpha, b_alpha)


def _dft_cos_sin(n):
    k = np.arange(n)
    ang = 2.0 * np.pi * ((k[:, None] * k[None, :]) % n) / n
    return np.cos(ang), np.sin(ang)


def _width_dft_table(t_len):
    c, s = _dft_cos_sin(GROUP_W)
    return (np.concatenate([c, -s], axis=1) / math.sqrt(t_len * GROUP_W)).astype(np.float32)


def _width_dft(x, cw_ref):
    zr, zi = [], []
    for g in range(N_GROUPS):
        z = jnp.dot(x[:, g * GROUP_W:(g + 1) * GROUP_W], cw_ref[...], precision=HIGHEST,
                    preferred_element_type=F32)
        zr.append(z[:, :GROUP_W])
        zi.append(z[:, GROUP_W:])
    return jnp.concatenate(zr, axis=1), jnp.concatenate(zi, axis=1)


def _fourier_dense_kernel(x_ref, cw_ref, ft_ref, o_ref):
    zr, zi = _width_dft(x_ref[...], cw_ref)
    zz = jnp.concatenate([zr, zi], axis=0)
    o_ref[...] = jnp.dot(ft_ref[...], zz, precision=HIGHEST, preferred_element_type=F32)


def _fourier_dense(f, nb, t_len):
    c, s = _dft_cos_sin(t_len)
    ft = jnp.asarray(np.concatenate([c, s], axis=1).astype(np.float32))
    cw = jnp.asarray(_width_dft_table(t_len))
    return pl.pallas_call(
        _fourier_dense_kernel,
        grid=(nb,),
        in_specs=[pl.BlockSpec((t_len, D_FOURIER), lambda b: (b, 0)),
                  _resident(cw.shape), _resident(ft.shape)],
        out_specs=pl.BlockSpec((t_len, D_FOURIER), lambda b: (b, 0)),
        out_shape=jax.ShapeDtypeStruct(f.shape, F32),
        compiler_params=_params(1),
        name="fourier_dense",
    )(f, cw, ft)


FFT_N = 64
FFT_ROWS = 8


def _fourier_stage1_kernel(x_ref, cw_ref, m_ref, ar_ref, ai_ref):
    for j in range(FFT_ROWS):
        zr, zi = _width_dft(x_ref[j], cw_ref)
        zz = jnp.concatenate([zr, zi], axis=0)
        a = jnp.dot(m_ref[j], zz, precision=HIGHEST, preferred_element_type=F32)
        ar_ref[j] = a[:FFT_N]
        ai_ref[j] = a[FFT_N:]


def _fourier_stage2_kernel(ar_ref, ai_ref, f2_ref, o_ref):
    for j in range(FFT_ROWS):
        aa = jnp.concatenate([ar_ref[j], ai_ref[j]], axis=0)
        o_ref[j] = jnp.dot(f2_ref[...], aa, precision=HIGHEST, preferred_element_type=F32)


def _fourier_two_stage(f, nb):
    n = FFT_N
    t_len = n * n
    s1 = np.arange(n)[:, None]
    t1 = np.arange(n)[None, :]
    mats = []
    for t2 in range(n):
        ang = 2.0 * np.pi * ((s1 * t1 * n + s1 * t2) % t_len) / t_len
        c, s = np.cos(ang), np.sin(ang)
        mats.append(np.block([[c, s], [-s, c]]))
    m_tab = jnp.asarray(np.stack(mats).astype(np.float32))
    c2, s2 = _dft_cos_sin(n)
    f2 = jnp.asarray(np.concatenate([c2, s2], axis=1).astype(np.float32))
    cw = jnp.asarray(_width_dft_table(t_len))

    x = f.reshape(nb, n, n, D_FOURIER).transpose(0, 2, 1, 3)
    blk = pl.BlockSpec((None, FFT_ROWS, n, D_FOURIER), lambda b, j: (b, j, 0, 0))
    shape4 = jax.ShapeDtypeStruct((nb, n, n, D_FOURIER), F32)
    ar, ai = pl.pallas_call(
        _fourier_stage1_kernel,
        grid=(nb, n // FFT_ROWS),
        in_specs=[blk, _resident(cw.shape),
                  pl.BlockSpec((FFT_ROWS, 2 * n, 2 * n), lambda b, j: (j, 0, 0))],
        out_specs=[blk, blk],
        out_shape=[shape4, shape4],
        compiler_params=_params(2),
        name="fourier_stage1",
    )(x, cw, m_tab)
    ar = ar.transpose(0, 2, 1, 3)
    ai = ai.transpose(0, 2, 1, 3)
    out = pl.pallas_call(
        _fourier_stage2_kernel,
        grid=(nb, n // FFT_ROWS),
        in_specs=[blk, blk, _resident(f2.shape)],
        out_specs=blk,
        out_shape=shape4,
        compiler_params=_params(2),
        name="fourier_stage2",
    )(ar, ai, f2)
    return out.transpose(0, 2, 1, 3).reshape(nb * t_len, D_FOURIER)


def _gla_tables(chunk, forward):
    levels = int(math.log2(chunk))
    idx = np.arange(chunk)
    i = idx[:, None]
    t = idx[None, :]
    rows = []
    for l in range(levels + 1):
        w = 1 << l
        bs = i - i % w
        be = bs + w - 1
        rows.append(((t >= bs) & (t <= i)) if forward else ((t >= i) & (t <= be)))
    for l in range(1, levels + 1):
        w = 1 << l
        bs = i - i % w
        be = bs + w - 1
        rows.append(((t > i) & (t <= be)) if forward else ((t >= bs) & (t < i)))
    summat = np.concatenate(rows, axis=0).astype(np.float32)
    j = t
    masks = [i == j]
    for l in range(levels):
        same = (i >> (l + 1)) == (j >> (l + 1))
        bi = (i >> l) & 1
        bj = (j >> l) & 1
        masks.append(same & (bi == 1) & (bj == 0) if forward else same & (bi == 0) & (bj == 1))
    return summat, np.stack(masks).astype(np.float32)


def _gla_chunk(q_ref, k_ref, v_ref, la_ref, sum_ref, mask_ref, st_ref, o_ref, row0, *, forward, accumulate):
    c = GLA_CHUNK
    levels = int(math.log2(c))
    rows = pl.ds(row0, c)
    q = q_ref[rows, :]
    k = k_ref[rows, :]
    v = v_ref[rows, :]
    la = la_ref[rows, :]
    la_hi = la.astype(BF16)
    la_lo = (la - la_hi.astype(F32)).astype(BF16)
    ex = jnp.exp(_dot(sum_ref[...], la_hi) + _dot(sum_ref[...], la_lo))

    def dfac(l):
        return ex[l * c:(l + 1) * c]

    def efac(l):
        return ex[(levels + l) * c:(levels + l + 1) * c]

    nt = (((1,), (1,)), ((), ()))
    kb = k.astype(BF16)
    scores = lax.dot_general(q.astype(BF16), kb, nt, preferred_element_type=F32) * mask_ref[0]
    for l in range(levels):
        ql = (q * dfac(l)).astype(BF16)
        kl = kb if l == 0 else (k * efac(l)).astype(BF16)
        scores = scores + lax.dot_general(ql, kl, nt, preferred_element_type=F32) * mask_ref[l + 1]

    st = st_ref[...]
    d_top = dfac(levels)
    q_top = (q * d_top).astype(BF16)
    k_top = (k * efac(levels)).astype(BF16)
    o = _dot(scores.astype(BF16), v) + lax.dot_general(q_top, st.astype(BF16), nt, preferred_element_type=F32)
    total = d_top[c - 1:c, :] if forward else d_top[0:1, :]
    tn = (((0,), (0,)), ((), ()))
    st_ref[...] = st * total + lax.dot_general(v, k_top, tn, preferred_element_type=F32)
    if accumulate:
        o_ref[rows, :] += o
    else:
        o_ref[rows, :] = o


def _gla_kernel(q_ref, k_ref, v_ref, laf_ref, lab_ref, s0f_ref, s0b_ref, sumf_ref, sumb_ref,
                maskf_ref, maskb_ref, o_ref, sf_ref, sb_ref, stf_ref, stb_ref, *, n_chunks):
    stf_ref[...] = s0f_ref[...].T
    stb_ref[...] = s0b_ref[...].T
    half = n_chunks // 2

    def make_body(accumulate):
        def body(i, carry):
            rf = pl.multiple_of(i * GLA_CHUNK, GLA_CHUNK)
            rb = pl.multiple_of((n_chunks - 1 - i) * GLA_CHUNK, GLA_CHUNK)
            _gla_chunk(q_ref, k_ref, v_ref, laf_ref, sumf_ref, maskf_ref, stf_ref, o_ref, rf,
                       forward=True, accumulate=accumulate)
            _gla_chunk(q_ref, k_ref, v_ref, lab_ref, sumb_ref, maskb_ref, stb_ref, o_ref, rb,
                       forward=False, accumulate=accumulate)
            return carry
        return body

    lax.fori_loop(0, half, make_body(False), 0)
    lax.fori_loop(half, n_chunks, make_body(True), 0)
    sf_ref[...] = stf_ref[...].T
    sb_ref[...] = stb_ref[...].T


def _gla(q, k, v, laf, lab, s0f, s0b, nb, t_len):
    assert t_len % (2 * GLA_CHUNK) == 0
    n_chunks = t_len // GLA_CHUNK
    sumf, maskf = _gla_tables(GLA_CHUNK, True)
    sumb, maskb = _gla_tables(GLA_CHUNK, False)
    consts = [jnp.asarray(sumf, BF16), jnp.asarray(sumb, BF16), jnp.asarray(maskf), jnp.asarray(maskb)]

    def seq(width):
        return pl.BlockSpec((t_len, width), lambda b, h: (b, h))

    state = pl.BlockSpec((None, None, DK, DV), lambda b, h: (b, h, 0, 0))
    n = nb * t_len
    return pl.pallas_call(
        functools.partial(_gla_kernel, n_chunks=n_chunks),
        grid=(nb, N_HEADS),
        in_specs=[seq(DK), seq(DK), seq(DV), seq(DK), seq(DK), state, state]
                 + [_resident(a.shape) for a in consts],
        out_specs=[seq(DV), state, state],
        out_shape=[jax.ShapeDtypeStruct((n, DV_TOT), F32),
                   jax.ShapeDtypeStruct((nb, N_HEADS, DK, DV), F32),
                   jax.ShapeDtypeStruct((nb, N_HEADS, DK, DV), F32)],
        scratch_shapes=[pltpu.VMEM((DV, DK), F32), pltpu.VMEM((DV, DK), F32)],
        compiler_params=_params(2),
        name="gla",
    )(q, k, v, laf, lab, s0f, s0b, *consts)


def _mixout_kernel(x_ref, mod_ref, m_ref, o_ref, r_ref, g_ref, gn_ref, wpf_ref, wpg_ref, wout_ref, y_ref,
                   *, d_model):
    branch_a = _dot(m_ref[...].astype(BF16), wpf_ref[...])
    o = o_ref[...]
    parts = []
    for h in range(N_HEADS):
        oh = o[:, h * DV:(h + 1) * DV]
        parts.append(oh * lax.rsqrt(jnp.mean(oh * oh, axis=-1, keepdims=True) + RMS_EPS))
    r = r_ref[...]
    on = jnp.concatenate(parts, axis=1) * gn_ref[...] * (r * _sigmoid(r))
    branch_b = _dot(on.astype(BF16), wpg_ref[...])
    g = _sigmoid(g_ref[...])
    merged = g[:, :d_model] * branch_a + g[:, d_model:] * branch_b
    y = _dot(merged.astype(BF16), wout_ref[...])
    y_ref[...] = x_ref[...] + mod_ref[5:6, :] * y


def _mixout(x, mod, m, o, r, g, gla_norm, wpf, wpg, wout, *, tiles_per_group):
    n, d = x.shape
    tm = TOKEN_TILE

    def tok(width):
        return pl.BlockSpec((tm, width), lambda i: (i, 0))

    return pl.pallas_call(
        functools.partial(_mixout_kernel, d_model=d),
        grid=(n // tm,),
        in_specs=[tok(d),
                  pl.BlockSpec((None, N_MOD, d), lambda i: (i // tiles_per_group, 0, 0)),
                  tok(D_FOURIER), tok(DV_TOT), tok(DV_TOT), tok(2 * d),
                  _resident((1, DV_TOT)), _resident(wpf.shape), _resident(wpg.shape), _resident(wout.shape)],
        out_specs=tok(d),
        out_shape=jax.ShapeDtypeStruct((n, d), F32),
        compiler_params=_params(1),
        name="mixout",
    )(x, mod, m, o, r, g, gla_norm.reshape(1, DV_TOT), wpf, wpg, wout)


def _grid_pos_embed(n_tokens, d_model):
    rows = n_tokens // GRID_W
    row = jnp.repeat(jnp.arange(rows, dtype=F32), GRID_W)
    col = jnp.tile(jnp.arange(GRID_W, dtype=F32), rows)
    n_freq = d_model // 4
    omega = POS_BASE ** (-jnp.arange(n_freq, dtype=F32) / n_freq)
    ra = row[:, None] * omega
    ca = col[:, None] * omega
    return jnp.concatenate([jnp.sin(ra), jnp.cos(ra), jnp.sin(ca), jnp.cos(ca)], axis=-1)


def _trunk_layer(x, mod, s0f, s0b, nb, t_len, w, *, pos, final_gain, two_stage_fft):
    tiles = max(t_len // TOKEN_TILE, 1) if mod.shape[0] > 1 else x.shape[0] // TOKEN_TILE
    x = _ffn(x, mod, w["norm_ffn1"], w["ffn1_gate"], w["ffn1_up"], w["ffn1_down"],
             tiles_per_group=tiles, first_mod_row=0, pos=pos)
    f, q, k, v, r, g, laf, lab = _mixin(x, mod, w["norm_mix"], w["w_in"], w["w_alpha"], w["b_alpha"],
                                        tiles_per_group=tiles)
    m = _fourier_two_stage(f, nb) if two_stage_fft else _fourier_dense(f, nb, t_len)
    o, sf, sb = _gla(q, k, v, laf, lab, s0f, s0b, nb, t_len)
    x = _mixout(x, mod, m, o, r, g, w["gla_norm"], w["proj_fourier"], w["proj_gla"], w["w_out"],
                tiles_per_group=tiles)
    x = _ffn(x, mod, w["norm_ffn2"], w["ffn2_gate"], w["ffn2_up"], w["ffn2_down"],
             tiles_per_group=tiles, first_mod_row=6, final_gain=final_gain)
    return x, sf, sb


def kernel(x_prompt, x_sample, state_gla_fwd, state_gla_bwd, c, c_ctx, w_ada, b_ada, norm_ffn1, w_ffn1_gate, w_ffn1_up, w_ffn1_down, norm_mix, w_in, w_alpha_fwd, b_alpha_fwd, w_alpha_bwd, b_alpha_bwd, gla_norm, w_proj_fourier, w_proj_gla, w_out, norm_ffn2, w_ffn2_gate, w_ffn2_up, w_ffn2_down, final_norm):
    nb_ctx, t_ctx, d = x_prompt.shape
    nb_lat, t_lat, _ = x_sample.shape
    depth = w_ada.shape[0]
    assert t_lat == FFT_N * FFT_N and t_ctx % TOKEN_TILE in (0, t_ctx) and TOKEN_TILE % t_ctx in (0, TOKEN_TILE)

    xc = x_prompt.reshape(nb_ctx * t_ctx, d)
    xl = x_sample.reshape(nb_lat * t_lat, d)
    pos = _grid_pos_embed(t_lat, d)
    zero_state = jnp.zeros((nb_ctx, N_HEADS, DK, DV), F32)
    pad_rows = (-(nb_lat + 1)) % 8
    cvecs = jnp.concatenate([c, c_ctx[None, :], jnp.zeros((pad_rows, d), F32)], axis=0)

    new_fwd, new_bwd = [], []
    for l in range(depth):
        cut = _C_G
        w_in_l = w_in[l]
        w_in_packed = jnp.concatenate(
            [w_in_l[:, :cut], w_in_l[:, cut + 2 * GATE_RANK:], w_in_l[:, cut:cut + 2 * GATE_RANK],
             jnp.zeros((d, ALR_PAD - 2 * GATE_RANK), F32)], axis=1).astype(BF16)
        w_alpha = jnp.zeros((ALR_PAD, 2 * DK_TOT), F32)
        w_alpha = w_alpha.at[:GATE_RANK, :DK_TOT].set(w_alpha_fwd[l])
        w_alpha = w_alpha.at[GATE_RANK:2 * GATE_RANK, DK_TOT:].set(w_alpha_bwd[l]).astype(BF16)
        b_alpha = jnp.concatenate([b_alpha_fwd[l], b_alpha_bwd[l]]).reshape(1, 2 * DK_TOT)
        w = {
            "norm_ffn1": norm_ffn1[l], "ffn1_gate": w_ffn1_gate[l].astype(BF16),
            "ffn1_up": w_ffn1_up[l].astype(BF16), "ffn1_down": w_ffn1_down[l].astype(BF16),
            "norm_mix": norm_mix[l], "w_in": w_in_packed, "w_alpha": w_alpha, "b_alpha": b_alpha,
            "gla_norm": gla_norm[l], "proj_fourier": w_proj_fourier[l].astype(BF16),
            "proj_gla": w_proj_gla[l].astype(BF16), "w_out": w_out[l].astype(BF16),
            "norm_ffn2": norm_ffn2[l], "ffn2_gate": w_ffn2_gate[l].astype(BF16),
            "ffn2_up": w_ffn2_up[l].astype(BF16), "ffn2_down": w_ffn2_down[l].astype(BF16),
        }
        last = l == depth - 1
        mod = _ada(cvecs, w_ada[l], b_ada[l]).reshape(-1, N_MOD, d)
        mod_lat = mod[:nb_lat]
        mod_ctx = mod[nb_lat:nb_lat + 1]
        xc, sf, sb = _trunk_layer(xc, mod_ctx, zero_state, zero_state, nb_ctx, t_ctx, w,
                                  pos=None, final_gain=final_norm if last else None, two_stage_fft=False)
        new_fwd.append(sf)
        new_bwd.append(sb)
        xl, _, _ = _trunk_layer(xl, mod_lat, state_gla_fwd[:, l], state_gla_bwd[:, l], nb_lat, t_lat, w,
                                pos=pos if l == 0 else None, final_gain=final_norm if last else None,
                                two_stage_fft=True)
    y_prompt = xc.reshape(nb_ctx, t_ctx, d)
    y_sample = xl.reshape(nb_lat, t_lat, d)
    return (y_prompt, y_sample, jnp.stack(new_fwd, axis=1), jnp.stack(new_bwd, axis=1))
```

```python
import functools
import math

import numpy as np
import jax
import jax.numpy as jnp
from jax import lax
from jax.experimental import pallas as pl
from jax.experimental.pallas import tpu as pltpu

F32 = jnp.float32
BF16 = jnp.bfloat16

GRID_W = 64
N_GROUPS = 4
GROUP_W = 128
D_FOURIER = N_GROUPS * GROUP_W
N_HEADS = 4
DK = 128
DV = 256
DK_TOT = N_HEADS * DK
DV_TOT = N_HEADS * DV
GATE_RANK = 16
GATE_TEMP = 16.0
RMS_EPS = 1e-6
POS_BASE = 10000.0
N_MOD = 9

TOKEN_TILE = 512
GLA_CHUNK = 64
GLA_LEVELS = 6
GLA_GROUP = 2
ALR_PAD = 128
VMEM_LIMIT = 56 * 1024 * 1024
HIGHEST = lax.Precision.HIGHEST


def _params(n_axes):
    return pltpu.CompilerParams(dimension_semantics=("arbitrary",) * n_axes,
                                vmem_limit_bytes=VMEM_LIMIT)


def _resident(shape):
    nd = len(shape)
    return pl.BlockSpec(shape, lambda *_: (0,) * nd, pipeline_mode=pl.Buffered(1))


def _sigmoid(x):
    return 1.0 / (1.0 + jnp.exp(-x))


def _rms(x, gain):
    return x * lax.rsqrt(jnp.mean(x * x, axis=-1, keepdims=True) + RMS_EPS) * gain


def _dot(a, b):
    return jnp.dot(a, b, preferred_element_type=F32)


def _ada_kernel(c_ref, w_ref, b_ref, o_ref):
    c = c_ref[...]
    s = (c * _sigmoid(c)).astype(BF16)
    o_ref[...] = _dot(s, w_ref[...].astype(BF16)) + b_ref[...]


def _ada(cvecs, w_ada, b_ada):
    rows, d = cvecs.shape
    n = w_ada.shape[1]
    tn = d
    return pl.pallas_call(
        _ada_kernel,
        grid=(n // tn,),
        in_specs=[pl.BlockSpec((rows, d), lambda j: (0, 0)),
                  pl.BlockSpec((d, tn), lambda j: (0, j)),
                  pl.BlockSpec((1, tn), lambda j: (0, j))],
        out_specs=pl.BlockSpec((rows, tn), lambda j: (0, j)),
        out_shape=jax.ShapeDtypeStruct((rows, n), F32),
        compiler_params=_params(1),
        name="ada",
    )(cvecs, w_ada, b_ada.reshape(1, n))


def _ffn_kernel(*refs, first_mod_row, add_pos, final_norm):
    x_ref, mod_ref, gain_ref, wg_ref, wu_ref, wd_ref = refs[:6]
    rest = list(refs[6:])
    pos_ref = rest.pop(0) if add_pos else None
    fn_ref = rest.pop(0) if final_norm else None
    o_ref = rest.pop(0)

    x = x_ref[...]
    if add_pos:
        x = x + pos_ref[...]
    shift = mod_ref[first_mod_row:first_mod_row + 1, :]
    scale = mod_ref[first_mod_row + 1:first_mod_row + 2, :]
    gate = mod_ref[first_mod_row + 2:first_mod_row + 3, :]
    h = (_rms(x, gain_ref[...]) * (1.0 + scale) + shift).astype(BF16)
    g = _dot(h, wg_ref[...])
    u = _dot(h, wu_ref[...])
    a = (g * _sigmoid(g) * u).astype(BF16)
    y = _dot(a, wd_ref[...])
    xn = x + (0.5 * gate) * y
    if final_norm:
        xn = _rms(xn, fn_ref[...])
    o_ref[...] = xn


def _ffn(x, mod, gain, wg, wu, wd, *, tiles_per_group, first_mod_row, pos=None, final_gain=None):
    n, d = x.shape
    dff = wg.shape[1]
    tm = TOKEN_TILE
    add_pos = pos is not None
    final_norm = final_gain is not None
    in_specs = [pl.BlockSpec((tm, d), lambda i: (i, 0)),
                pl.BlockSpec((None, N_MOD, d), lambda i: (i // tiles_per_group, 0, 0)),
                _resident((1, d)), _resident((d, dff)), _resident((d, dff)), _resident((dff, d))]
    args = [x, mod, gain.reshape(1, d), wg, wu, wd]
    if add_pos:
        pos_tiles = pos.shape[0] // tm
        in_specs.append(pl.BlockSpec((tm, d), lambda i: (i % pos_tiles, 0)))
        args.append(pos)
    if final_norm:
        in_specs.append(_resident((1, d)))
        args.append(final_gain.reshape(1, d))
    return pl.pallas_call(
        functools.partial(_ffn_kernel, first_mod_row=first_mod_row, add_pos=add_pos, final_norm=final_norm),
        grid=(n // tm,),
        in_specs=in_specs,
        out_specs=pl.BlockSpec((tm, d), lambda i: (i, 0)),
        out_shape=jax.ShapeDtypeStruct((n, d), F32),
        compiler_params=_params(1),
        name="ffn",
    )(*args)


_C_F = 0
_C_Q = _C_F + D_FOURIER
_C_K = _C_Q + DK_TOT
_C_V = _C_K + DK_TOT
_C_R = _C_V + DV_TOT
_C_G = _C_R + DV_TOT


def _mixin_kernel(x_ref, mod_ref, gain_ref, w_ref, wal_ref, bal_ref,
                  f_ref, q_ref, k_ref, v_ref, r_ref, g_ref, laf_ref, lab_ref, *, d_model):
    c_a = _C_G + 2 * d_model
    x = x_ref[...]
    shift = mod_ref[3:4, :]
    scale = mod_ref[4:5, :]
    h = (_rms(x, gain_ref[...]) * (1.0 + scale) + shift).astype(BF16)

    def proj(lo, hi):
        return _dot(h, w_ref[:, lo:hi])

    f_ref[...] = proj(_C_F, _C_Q)
    q_ref[...] = proj(_C_Q, _C_K) * (DK ** -0.5)
    k_ref[...] = proj(_C_K, _C_V)
    v_ref[...] = proj(_C_V, _C_R).astype(BF16)
    r_ref[...] = proj(_C_R, _C_G)
    g_ref[...] = proj(_C_G, c_a)
    alr = proj(c_a, c_a + ALR_PAD).astype(BF16)
    pre = _dot(alr, wal_ref[...]) + bal_ref[...]
    la = (jnp.minimum(pre, 0.0) - jnp.log1p(jnp.exp(-jnp.abs(pre)))) * (math.log2(math.e) / GATE_TEMP)
    laf_ref[...] = la[:, :DK_TOT]
    lab_ref[...] = la[:, DK_TOT:]


def _mixin(x, mod, gain, w_in_packed, w_alpha, b_alpha, *, tiles_per_group):
    n, d = x.shape
    tm = TOKEN_TILE
    ncols = w_in_packed.shape[1]

    def tok(width):
        return pl.BlockSpec((tm, width), lambda i: (i, 0))

    outs = [(D_FOURIER, F32), (DK_TOT, F32), (DK_TOT, F32), (DV_TOT, BF16), (DV_TOT, F32),
            (2 * d, F32), (DK_TOT, F32), (DK_TOT, F32)]
    return pl.pallas_call(
        functools.partial(_mixin_kernel, d_model=d),
        grid=(n // tm,),
        in_specs=[tok(d),
                  pl.BlockSpec((None, N_MOD, d), lambda i: (i // tiles_per_group, 0, 0)),
                  _resident((1, d)), _resident((d, ncols)),
                  _resident((ALR_PAD, 2 * DK_TOT)), _resident((1, 2 * DK_TOT))],
        out_specs=[tok(w) for w, _ in outs],
        out_shape=[jax.ShapeDtypeStruct((n, w), dt) for w, dt in outs],
        compiler_params=_params(1),
        name="mixin",
    )(x, mod, gain.reshape(1, d), w_in_packed, w_alpha, b_alpha)


def _dft_cos_sin(n):
    k = np.arange(n)
    ang = 2.0 * np.pi * ((k[:, None] * k[None, :]) % n) / n
    return np.cos(ang), np.sin(ang)


def _width_dft_table(t_len):
    c, s = _dft_cos_sin(GROUP_W)
    return (np.concatenate([c, -s], axis=1) / math.sqrt(t_len * GROUP_W)).astype(np.float32)


def _width_dft(x, cw_ref):
    zr, zi = [], []
    for g in range(N_GROUPS):
        z = jnp.dot(x[:, g * GROUP_W:(g + 1) * GROUP_W], cw_ref[...], precision=HIGHEST,
                    preferred_element_type=F32)
        zr.append(z[:, :GROUP_W])
        zi.append(z[:, GROUP_W:])
    return jnp.concatenate(zr, axis=1), jnp.concatenate(zi, axis=1)


def _fourier_dense_kernel(x_ref, cw_ref, ft_ref, o_ref):
    zr, zi = _width_dft(x_ref[...], cw_ref)
    zz = jnp.concatenate([zr, zi], axis=0)
    o_ref[...] = jnp.dot(ft_ref[...], zz, precision=HIGHEST, preferred_element_type=F32)


def _fourier_dense(f, nb, t_len):
    c, s = _dft_cos_sin(t_len)
    ft = jnp.asarray(np.concatenate([c, s], axis=1).astype(np.float32))
    cw = jnp.asarray(_width_dft_table(t_len))
    return pl.pallas_call(
        _fourier_dense_kernel,
        grid=(nb,),
        in_specs=[pl.BlockSpec((t_len, D_FOURIER), lambda b: (b, 0)),
                  _resident(cw.shape), _resident(ft.shape)],
        out_specs=pl.BlockSpec((t_len, D_FOURIER), lambda b: (b, 0)),
        out_shape=jax.ShapeDtypeStruct(f.shape, F32),
        compiler_params=_params(1),
        name="fourier_dense",
    )(f, cw, ft)


FFT_N = 64
FFT_ROWS = 8


def _fourier_stage1_kernel(x_ref, cw_ref, m_ref, ar_ref, ai_ref):
    for j in range(FFT_ROWS):
        zr, zi = _width_dft(x_ref[j], cw_ref)
        zz = jnp.concatenate([zr, zi], axis=0)
        a = jnp.dot(m_ref[j], zz, precision=HIGHEST, preferred_element_type=F32)
        ar_ref[j] = a[:FFT_N]
        ai_ref[j] = a[FFT_N:]


def _fourier_stage2_kernel(ar_ref, ai_ref, f2_ref, o_ref):
    for j in range(FFT_ROWS):
        aa = jnp.concatenate([ar_ref[j], ai_ref[j]], axis=0)
        o_ref[j] = jnp.dot(f2_ref[...], aa, precision=HIGHEST, preferred_element_type=F32)


def _fourier_two_stage(f, nb):
    n = FFT_N
    t_len = n * n
    s1 = np.arange(n)[:, None]
    t1 = np.arange(n)[None, :]
    mats = []
    for t2 in range(n):
        ang = 2.0 * np.pi * ((s1 * t1 * n + s1 * t2) % t_len) / t_len
        c, s = np.cos(ang), np.sin(ang)
        mats.append(np.block([[c, s], [-s, c]]))
    m_tab = jnp.asarray(np.stack(mats).astype(np.float32))
    c2, s2 = _dft_cos_sin(n)
    f2 = jnp.asarray(np.concatenate([c2, s2], axis=1).astype(np.float32))
    cw = jnp.asarray(_width_dft_table(t_len))

    x = f.reshape(nb, n, n, D_FOURIER).transpose(0, 2, 1, 3)
    blk = pl.BlockSpec((None, FFT_ROWS, n, D_FOURIER), lambda b, j: (b, j, 0, 0))
    shape4 = jax.ShapeDtypeStruct((nb, n, n, D_FOURIER), F32)
    ar, ai = pl.pallas_call(
        _fourier_stage1_kernel,
        grid=(nb, n // FFT_ROWS),
        in_specs=[blk, _resident(cw.shape),
                  pl.BlockSpec((FFT_ROWS, 2 * n, 2 * n), lambda b, j: (j, 0, 0))],
        out_specs=[blk, blk],
        out_shape=[shape4, shape4],
        compiler_params=_params(2),
        name="fourier_stage1",
    )(x, cw, m_tab)
    ar = ar.transpose(0, 2, 1, 3)
    ai = ai.transpose(0, 2, 1, 3)
    out = pl.pallas_call(
        _fourier_stage2_kernel,
        grid=(nb, n // FFT_ROWS),
        in_specs=[blk, blk, _resident(f2.shape)],
        out_specs=blk,
        out_shape=shape4,
        compiler_params=_params(2),
        name="fourier_stage2",
    )(ar, ai, f2)
    return out.transpose(0, 2, 1, 3).reshape(nb * t_len, D_FOURIER)


def _gla_tables(chunk):
    levels = GLA_LEVELS
    idx = np.arange(chunk)
    i = idx[:, None]
    t = idx[None, :]
    rows = []
    for l in range(1, levels + 1):
        w = 1 << l
        rows.append((t >= i - i % w) & (t <= i))
    for l in range(1, levels + 1):
        w = 1 << l
        rows.append((t > i) & (t <= i - i % w + w - 1))
    summat = np.concatenate(rows, axis=0).astype(np.float32)
    summat = np.concatenate([summat, summat], axis=1)
    j = t
    masks = [i == j]
    for l in range(levels):
        same = (i >> (l + 1)) == (j >> (l + 1))
        masks.append(same & (((i >> l) & 1) == 1) & (((j >> l) & 1) == 0))
    mask_f = np.stack(masks).astype(np.float32)
    return summat, mask_f, np.transpose(mask_f, (0, 2, 1)).copy()


_NT = (((1,), (1,)), ((), ()))
_TN = (((0,), (0,)), ((), ()))


def _gla_prepare(q_ref, k_ref, laf_ref, lab_ref, sum_ref, qs_ref, ks_ref, tot_ref, rows_f, rows_b):
    c = GLA_CHUNK
    levels = GLA_LEVELS
    n = len(rows_f)
    args, las = [], []
    for g in range(n):
        laf = laf_ref[pl.ds(rows_f[g], c), :]
        lab = lab_ref[pl.ds(rows_b[g], c), :]
        la = jnp.concatenate([laf, lab], axis=1)
        hi = la.astype(BF16)
        lo = (la - hi.astype(F32)).astype(BF16)
        args.append(_dot(sum_ref[...], jnp.concatenate([hi, lo], axis=0)))
        las.append((laf, lab))
    for g in range(n):
        a = args[g]
        laf, lab = las[g]

        def dsum(l):
            return a[(l - 1) * c:l * c]

        def esum(l):
            return a[(levels + l - 1) * c:(levels + l) * c]

        for fwd in (True, False):
            chain = 2 * g + (0 if fwd else 1)
            rows = pl.ds(rows_f[g] if fwd else rows_b[g], c)
            q = q_ref[rows, :]
            k = k_ref[rows, :]
            la = laf if fwd else lab
            qs_ref[chain, 0:c, :] = q.astype(BF16)
            ks_ref[chain, 0:c, :] = k.astype(BF16)
            for l in range(levels + 1):
                if l == 0:
                    d, e = jnp.exp2(la), None
                elif fwd:
                    d, e = jnp.exp2(dsum(l)[:, :DK]), jnp.exp2(esum(l)[:, :DK])
                else:
                    d, e = jnp.exp2(esum(l)[:, DK:] + la), jnp.exp2(dsum(l)[:, DK:] - la)
                qs_ref[chain, (l + 1) * c:(l + 2) * c, :] = (q * d).astype(BF16)
                if e is not None:
                    ks_ref[chain, l * c:(l + 1) * c, :] = (k * e).astype(BF16)
            tot_ref[chain] = jnp.broadcast_to(d[c - 1:c, :] if fwd else d[0:1, :], (8, DK))


def _gla_apply(v_ref, maskf_ref, maskb_ref, qs_ref, ks_ref, tot_ref, stf_ref, stb_ref, o_ref,
               rows_f, rows_b, *, accumulate):
    c = GLA_CHUNK
    levels = GLA_LEVELS
    n = len(rows_f)
    scores = []
    for chain in range(2 * n):
        m_ref = maskf_ref if chain % 2 == 0 else maskb_ref
        s = None
        for l in range(levels + 1):
            kl = max(l - 1, 0)
            p = lax.dot_general(qs_ref[chain, l * c:(l + 1) * c, :], ks_ref[chain, kl * c:(kl + 1) * c, :], _NT,
                                preferred_element_type=F32) * m_ref[l]
            s = p if s is None else s + p
        scores.append(s.astype(BF16))
    for chain in range(2 * n):
        fwd = chain % 2 == 0
        g = chain // 2
        rows = pl.ds(rows_f[g] if fwd else rows_b[g], c)
        st_ref = stf_ref if fwd else stb_ref
        v = v_ref[rows, :]
        st = st_ref[...]
        q_top = qs_ref[chain, (levels + 1) * c:(levels + 2) * c, :]
        k_top = ks_ref[chain, levels * c:(levels + 1) * c, :]
        o = _dot(scores[chain], v) + lax.dot_general(q_top, st.astype(BF16), _NT, preferred_element_type=F32)
        st_ref[...] = st * tot_ref[chain][0:1, :] + lax.dot_general(v, k_top, _TN, preferred_element_type=F32)
        if accumulate:
            o_ref[rows, :] += o
        else:
            o_ref[rows, :] = o


def _gla_kernel(q_ref, k_ref, v_ref, laf_ref, lab_ref, s0f_ref, s0b_ref, sum_ref, maskf_ref, maskb_ref,
                o_ref, sf_ref, sb_ref, stf_ref, stb_ref, qs0_ref, ks0_ref, tot0_ref, qs1_ref, ks1_ref, tot1_ref,
                *, n_chunks, group):
    stf_ref[...] = s0f_ref[...].T
    stb_ref[...] = s0b_ref[...].T
    steps = n_chunks // group
    slots = ((qs0_ref, ks0_ref, tot0_ref), (qs1_ref, ks1_ref, tot1_ref))

    def rows_of(i):
        rows_f = [pl.multiple_of((i * group + g) * GLA_CHUNK, GLA_CHUNK) for g in range(group)]
        rows_b = [pl.multiple_of((n_chunks - 1 - i * group - g) * GLA_CHUNK, GLA_CHUNK) for g in range(group)]
        return rows_f, rows_b

    def prepare(i, slot):
        _gla_prepare(q_ref, k_ref, laf_ref, lab_ref, sum_ref, *slots[slot], *rows_of(i))

    def apply(i, slot, accumulate):
        _gla_apply(v_ref, maskf_ref, maskb_ref, *slots[slot], stf_ref, stb_ref, o_ref, *rows_of(i),
                   accumulate=accumulate)

    def pair(j, accumulate, last):
        prepare(2 * j + 1, 1)
        apply(2 * j, 0, accumulate[0])
        if not last:
            prepare(2 * j + 2, 0)
        apply(2 * j + 1, 1, accumulate[1])

    n_pairs = steps // 2
    prepare(0, 0)
    if n_pairs == 1:
        pair(0, (False, True), True)
    else:
        def body(accumulate):
            def f(j, carry):
                pair(j, (accumulate, accumulate), False)
                return carry
            return f
        lax.fori_loop(0, n_pairs // 2, body(False), 0)
        lax.fori_loop(n_pairs // 2, n_pairs - 1, body(True), 0)
        pair(n_pairs - 1, (True, True), True)
    sf_ref[...] = stf_ref[...].T
    sb_ref[...] = stb_ref[...].T


def _gla(q, k, v, laf, lab, s0f, s0b, nb, t_len):
    group = GLA_GROUP
    n_chunks = t_len // GLA_CHUNK
    steps = n_chunks // group
    assert t_len % GLA_CHUNK == 0 and n_chunks % group == 0 and (steps == 2 or steps % 4 == 0)
    summat, mask_f, mask_b = _gla_tables(GLA_CHUNK)
    consts = [jnp.asarray(summat, BF16), jnp.asarray(mask_f), jnp.asarray(mask_b)]

    def seq(width):
        return pl.BlockSpec((t_len, width), lambda b, h: (b, h))

    state = pl.BlockSpec((None, None, DK, DV), lambda b, h: (b, h, 0, 0))
    n = nb * t_len
    slot = [pltpu.VMEM((2 * group, (GLA_LEVELS + 2) * GLA_CHUNK, DK), BF16),
            pltpu.VMEM((2 * group, (GLA_LEVELS + 1) * GLA_CHUNK, DK), BF16),
            pltpu.VMEM((2 * group, 8, DK), F32)]
    return pl.pallas_call(
        functools.partial(_gla_kernel, n_chunks=n_chunks, group=group),
        grid=(nb, N_HEADS),
        in_specs=[seq(DK), seq(DK), seq(DV), seq(DK), seq(DK), state, state]
                 + [_resident(a.shape) for a in consts],
        out_specs=[seq(DV), state, state],
        out_shape=[jax.ShapeDtypeStruct((n, DV_TOT), F32),
                   jax.ShapeDtypeStruct((nb, N_HEADS, DK, DV), F32),
                   jax.ShapeDtypeStruct((nb, N_HEADS, DK, DV), F32)],
        scratch_shapes=[pltpu.VMEM((DV, DK), F32), pltpu.VMEM((DV, DK), F32)] + slot + slot,
        compiler_params=_params(2),
        name="gla",
    )(q, k, v, laf, lab, s0f, s0b, *consts)


def _mixout_kernel(x_ref, mod_ref, m_ref, o_ref, r_ref, g_ref, gn_ref, wpf_ref, wpg_ref, wout_ref, y_ref,
                   *, d_model):
    branch_a = _dot(m_ref[...].astype(BF16), wpf_ref[...])
    o = o_ref[...]
    parts = []
    for h in range(N_HEADS):
        oh = o[:, h * DV:(h + 1) * DV]
        parts.append(oh * lax.rsqrt(jnp.mean(oh * oh, axis=-1, keepdims=True) + RMS_EPS))
    r = r_ref[...]
    on = jnp.concatenate(parts, axis=1) * gn_ref[...] * (r * _sigmoid(r))
    branch_b = _dot(on.astype(BF16), wpg_ref[...])
    g = _sigmoid(g_ref[...])
    merged = g[:, :d_model] * branch_a + g[:, d_model:] * branch_b
    y = _dot(merged.astype(BF16), wout_ref[...])
    y_ref[...] = x_ref[...] + mod_ref[5:6, :] * y


def _mixout(x, mod, m, o, r, g, gla_norm, wpf, wpg, wout, *, tiles_per_group):
    n, d = x.shape
    tm = TOKEN_TILE

    def tok(width):
        return pl.BlockSpec((tm, width), lambda i: (i, 0))

    return pl.pallas_call(
        functools.partial(_mixout_kernel, d_model=d),
        grid=(n // tm,),
        in_specs=[tok(d),
                  pl.BlockSpec((None, N_MOD, d), lambda i: (i // tiles_per_group, 0, 0)),
                  tok(D_FOURIER), tok(DV_TOT), tok(DV_TOT), tok(2 * d),
                  _resident((1, DV_TOT)), _resident(wpf.shape), _resident(wpg.shape), _resident(wout.shape)],
        out_specs=tok(d),
        out_shape=jax.ShapeDtypeStruct((n, d), F32),
        compiler_params=_params(1),
        name="mixout",
    )(x, mod, m, o, r, g, gla_norm.reshape(1, DV_TOT), wpf, wpg, wout)


def _grid_pos_embed(n_tokens, d_model):
    rows = n_tokens // GRID_W
    row = jnp.repeat(jnp.arange(rows, dtype=F32), GRID_W)
    col = jnp.tile(jnp.arange(GRID_W, dtype=F32), rows)
    n_freq = d_model // 4
    omega = POS_BASE ** (-jnp.arange(n_freq, dtype=F32) / n_freq)
    ra = row[:, None] * omega
    ca = col[:, None] * omega
    return jnp.concatenate([jnp.sin(ra), jnp.cos(ra), jnp.sin(ca), jnp.cos(ca)], axis=-1)


def _trunk_layer(x, mod, s0f, s0b, nb, t_len, w, *, pos, final_gain, two_stage_fft):
    tiles = max(t_len // TOKEN_TILE, 1) if mod.shape[0] > 1 else x.shape[0] // TOKEN_TILE
    x = _ffn(x, mod, w["norm_ffn1"], w["ffn1_gate"], w["ffn1_up"], w["ffn1_down"],
             tiles_per_group=tiles, first_mod_row=0, pos=pos)
    f, q, k, v, r, g, laf, lab = _mixin(x, mod, w["norm_mix"], w["w_in"], w["w_alpha"], w["b_alpha"],
                                        tiles_per_group=tiles)
    m = _fourier_two_stage(f, nb) if two_stage_fft else _fourier_dense(f, nb, t_len)
    o, sf, sb = _gla(q, k, v, laf, lab, s0f, s0b, nb, t_len)
    x = _mixout(x, mod, m, o, r, g, w["gla_norm"], w["proj_fourier"], w["proj_gla"], w["w_out"],
                tiles_per_group=tiles)
    x = _ffn(x, mod, w["norm_ffn2"], w["ffn2_gate"], w["ffn2_up"], w["ffn2_down"],
             tiles_per_group=tiles, first_mod_row=6, final_gain=final_gain)
    return x, sf, sb


def kernel(x_prompt, x_sample, state_gla_fwd, state_gla_bwd, c, c_ctx, w_ada, b_ada, norm_ffn1, w_ffn1_gate, w_ffn1_up, w_ffn1_down, norm_mix, w_in, w_alpha_fwd, b_alpha_fwd, w_alpha_bwd, b_alpha_bwd, gla_norm, w_proj_fourier, w_proj_gla, w_out, norm_ffn2, w_ffn2_gate, w_ffn2_up, w_ffn2_down, final_norm):
    nb_ctx, t_ctx, d = x_prompt.shape
    nb_lat, t_lat, _ = x_sample.shape
    depth = w_ada.shape[0]
    assert t_lat == FFT_N * FFT_N and t_ctx % TOKEN_TILE in (0, t_ctx) and TOKEN_TILE % t_ctx in (0, TOKEN_TILE)

    xc = x_prompt.reshape(nb_ctx * t_ctx, d)
    xl = x_sample.reshape(nb_lat * t_lat, d)
    pos = _grid_pos_embed(t_lat, d)
    zero_state = jnp.zeros((nb_ctx, N_HEADS, DK, DV), F32)
    pad_rows = (-(nb_lat + 1)) % 8
    cvecs = jnp.concatenate([c, c_ctx[None, :], jnp.zeros((pad_rows, d), F32)], axis=0)

    new_fwd, new_bwd = [], []
    for l in range(depth):
        cut = _C_G
        w_in_l = w_in[l]
        w_in_packed = jnp.concatenate(
            [w_in_l[:, :cut], w_in_l[:, cut + 2 * GATE_RANK:], w_in_l[:, cut:cut + 2 * GATE_RANK],
             jnp.zeros((d, ALR_PAD - 2 * GATE_RANK), F32)], axis=1).astype(BF16)
        w_alpha = jnp.zeros((ALR_PAD, 2 * DK_TOT), F32)
        w_alpha = w_alpha.at[:GATE_RANK, :DK_TOT].set(w_alpha_fwd[l])
        w_alpha = w_alpha.at[GATE_RANK:2 * GATE_RANK, DK_TOT:].set(w_alpha_bwd[l]).astype(BF16)
        b_alpha = jnp.concatenate([b_alpha_fwd[l], b_alpha_bwd[l]]).reshape(1, 2 * DK_TOT)
        w = {
            "norm_ffn1": norm_ffn1[l], "ffn1_gate": w_ffn1_gate[l].astype(BF16),
            "ffn1_up": w_ffn1_up[l].astype(BF16), "ffn1_down": w_ffn1_down[l].astype(BF16),
            "norm_mix": norm_mix[l], "w_in": w_in_packed, "w_alpha": w_alpha, "b_alpha": b_alpha,
            "gla_norm": gla_norm[l], "proj_fourier": w_proj_fourier[l].astype(BF16),
            "proj_gla": w_proj_gla[l].astype(BF16), "w_out": w_out[l].astype(BF16),
            "norm_ffn2": norm_ffn2[l], "ffn2_gate": w_ffn2_gate[l].astype(BF16),
            "ffn2_up": w_ffn2_up[l].astype(BF16), "ffn2_down": w_ffn2_down[l].astype(BF16),
        }
        last = l == depth - 1
        mod = _ada(cvecs, w_ada[l], b_ada[l]).reshape(-1, N_MOD, d)
        mod_lat = mod[:nb_lat]
        mod_ctx = mod[nb_lat:nb_lat + 1]
        xc, sf, sb = _trunk_layer(xc, mod_ctx, zero_state, zero_state, nb_ctx, t_ctx, w,
                                  pos=None, final_gain=final_norm if last else None, two_stage_fft=False)
        new_fwd.append(sf)
        new_bwd.append(sb)
        xl, _, _ = _trunk_layer(xl, mod_lat, state_gla_fwd[:, l], state_gla_bwd[:, l], nb_lat, t_lat, w,
                                pos=pos if l == 0 else None, final_gain=final_norm if last else None,
                                two_stage_fft=True)
    y_prompt = xc.reshape(nb_ctx, t_ctx, d)
    y_sample = xl.reshape(nb_lat, t_lat, d)
    return (y_prompt, y_sample, jnp.stack(new_fwd, axis=1), jnp.stack(new_bwd, axis=1))
```

```python
import functools
import math

import numpy as np
import jax
import jax.numpy as jnp
from jax import lax
from jax.experimental import pallas as pl
from jax.experimental.pallas import tpu as pltpu

F32 = jnp.float32
BF16 = jnp.bfloat16

GRID_W = 64
N_GROUPS = 4
GROUP_W = 128
D_FOURIER = N_GROUPS * GROUP_W
N_HEADS = 4
DK = 128
DV = 256
DK_TOT = N_HEADS * DK
DV_TOT = N_HEADS * DV
GATE_RANK = 16
GATE_TEMP = 16.0
RMS_EPS = 1e-6
POS_BASE = 10000.0
N_MOD = 9

TOKEN_TILE = 512
GLA_CHUNK = 64
GLA_LEVELS = 6
GLA_GROUP = 2
LANES = 128
ALR_PAD = LANES
FFT_N = 64
FFT_ROWS = 8
VMEM_LIMIT = 56 * 1024 * 1024


def _params(n_axes):
    return pltpu.CompilerParams(dimension_semantics=("arbitrary",) * n_axes,
                                vmem_limit_bytes=VMEM_LIMIT)


def _resident(shape):
    nd = len(shape)
    return pl.BlockSpec(shape, lambda *_: (0,) * nd, pipeline_mode=pl.Buffered(1))


def _sigmoid(x):
    return 1.0 / (1.0 + jnp.exp(-x))


def _rms(x, gain):
    return x * lax.rsqrt(jnp.mean(x * x, axis=-1, keepdims=True) + RMS_EPS) * gain


def _dot(a, b):
    return jnp.dot(a, b, preferred_element_type=F32)


def _ada_kernel(c_ref, w_ref, b_ref, o_ref):
    c = c_ref[...]
    s = (c * _sigmoid(c)).astype(BF16)
    o_ref[...] = _dot(s, w_ref[...].astype(BF16)) + b_ref[...]


def _ada(cvecs, w_ada, b_ada):
    rows, d = cvecs.shape
    n = w_ada.shape[1]
    tn = d
    return pl.pallas_call(
        _ada_kernel,
        grid=(n // tn,),
        in_specs=[pl.BlockSpec((rows, d), lambda j: (0, 0)),
                  pl.BlockSpec((d, tn), lambda j: (0, j)),
                  pl.BlockSpec((1, tn), lambda j: (0, j))],
        out_specs=pl.BlockSpec((rows, tn), lambda j: (0, j)),
        out_shape=jax.ShapeDtypeStruct((rows, n), F32),
        compiler_params=_params(1),
        name="ada",
    )(cvecs, w_ada, b_ada.reshape(1, n))


def _ffn_kernel(*refs, first_mod_row, add_pos, final_norm):
    x_ref, mod_ref, gain_ref, wg_ref, wu_ref, wd_ref = refs[:6]
    rest = list(refs[6:])
    prow_ref, pcol_ref = (rest.pop(0), rest.pop(0)) if add_pos else (None, None)
    fn_ref = rest.pop(0) if final_norm else None
    o_ref = rest.pop(0)

    x = x_ref[...]
    if add_pos:
        tm, d = x.shape
        rows = tm // GRID_W
        prow = jnp.broadcast_to(prow_ref[...][:, None, :], (rows, GRID_W, d // 2))
        pcol = jnp.broadcast_to(pcol_ref[...][None, :, :], (rows, GRID_W, d // 2))
        x = x + jnp.concatenate([prow, pcol], axis=-1).reshape(tm, d)
    shift = mod_ref[first_mod_row:first_mod_row + 1, :]
    scale = mod_ref[first_mod_row + 1:first_mod_row + 2, :]
    gate = mod_ref[first_mod_row + 2:first_mod_row + 3, :]
    h = (_rms(x, gain_ref[...]) * (1.0 + scale) + shift).astype(BF16)
    g = _dot(h, wg_ref[...])
    u = _dot(h, wu_ref[...])
    a = (g * _sigmoid(g) * u).astype(BF16)
    y = _dot(a, wd_ref[...])
    xn = x + (0.5 * gate) * y
    if final_norm:
        xn = _rms(xn, fn_ref[...])
    o_ref[...] = xn


def _ffn(x, mod, gain, wg, wu, wd, *, tiles_per_group, first_mod_row, pos=None, final_gain=None):
    n, d = x.shape
    dff = wg.shape[1]
    tm = TOKEN_TILE
    add_pos = pos is not None
    final_norm = final_gain is not None
    in_specs = [pl.BlockSpec((tm, d), lambda i: (i, 0)),
                pl.BlockSpec((None, N_MOD, d), lambda i: (i // tiles_per_group, 0, 0)),
                _resident((1, d)), _resident((d, dff)), _resident((d, dff)), _resident((dff, d))]
    args = [x, mod, gain.reshape(1, d), wg, wu, wd]
    if add_pos:
        pos_row, pos_col = pos
        rows = tm // GRID_W
        row_tiles = pos_row.shape[0] // rows
        assert tm % GRID_W == 0 and rows % 8 == 0 and pos_col.shape[0] == GRID_W
        in_specs += [pl.BlockSpec((rows, d // 2), lambda i: (i % row_tiles, 0)), _resident(pos_col.shape)]
        args += [pos_row, pos_col]
    if final_norm:
        in_specs.append(_resident((1, d)))
        args.append(final_gain.reshape(1, d))
    return pl.pallas_call(
        functools.partial(_ffn_kernel, first_mod_row=first_mod_row, add_pos=add_pos, final_norm=final_norm),
        grid=(n // tm,),
        in_specs=in_specs,
        out_specs=pl.BlockSpec((tm, d), lambda i: (i, 0)),
        out_shape=jax.ShapeDtypeStruct((n, d), F32),
        compiler_params=_params(1),
        name="ffn",
    )(*args)


_C_F = 0
_C_Q = _C_F + D_FOURIER
_C_K = _C_Q + DK_TOT
_C_V = _C_K + DK_TOT
_C_R = _C_V + DV_TOT
_C_G = _C_R + DV_TOT


def _mixin_kernel(x_ref, mod_ref, gain_ref, w_ref, wal_ref, bal_ref,
                  f_ref, q_ref, k_ref, v_ref, r_ref, g_ref, laf_ref, lab_ref, *scratch, d_model, fft_layout):
    c_a = _C_G + 2 * d_model
    x = x_ref[...]
    shift = mod_ref[3:4, :]
    scale = mod_ref[4:5, :]
    h = (_rms(x, gain_ref[...]) * (1.0 + scale) + shift).astype(BF16)

    def proj(lo, hi):
        return _dot(h, w_ref[:, lo:hi])

    f = proj(_C_F, _C_Q)
    if fft_layout:
        slab_ref, = scratch
        rows = x.shape[0] // FFT_N
        for a in range(rows):
            for g in range(D_FOURIER // LANES):
                slab_ref[g, pl.ds(a, FFT_N, stride=rows), :] = f[a * FFT_N:(a + 1) * FFT_N, g * LANES:(g + 1) * LANES]
        for g in range(D_FOURIER // LANES):
            f_ref[:, :, g * LANES:(g + 1) * LANES] = slab_ref[g].reshape(FFT_N, rows, LANES)
    else:
        f_ref[...] = f
    q_ref[...] = proj(_C_Q, _C_K) * (DK ** -0.5)
    k_ref[...] = proj(_C_K, _C_V)
    v_ref[...] = proj(_C_V, _C_R).astype(BF16)
    r_ref[...] = proj(_C_R, _C_G)
    g_ref[...] = proj(_C_G, c_a)
    alr = proj(c_a, c_a + ALR_PAD).astype(BF16)
    pre = _dot(alr, wal_ref[...]) + bal_ref[...]
    la = (jnp.minimum(pre, 0.0) - jnp.log1p(jnp.exp(-jnp.abs(pre)))) * (math.log2(math.e) / GATE_TEMP)
    laf_ref[...] = la[:, :DK_TOT]
    lab_ref[...] = la[:, DK_TOT:]


def _mixin(x, mod, gain, w_in_packed, w_alpha, b_alpha, *, tiles_per_group, fft_layout):
    n, d = x.shape
    tm = TOKEN_TILE
    ncols = w_in_packed.shape[1]

    def tok(width):
        return pl.BlockSpec((tm, width), lambda i: (i, 0))

    outs = [(D_FOURIER, F32), (DK_TOT, F32), (DK_TOT, F32), (DV_TOT, BF16), (DV_TOT, F32),
            (2 * d, F32), (DK_TOT, F32), (DK_TOT, F32)]
    out_specs = [tok(w) for w, _ in outs]
    out_shape = [jax.ShapeDtypeStruct((n, w), dt) for w, dt in outs]
    scratch = []
    if fft_layout:
        rows = tm // FFT_N
        tiles = FFT_N // rows
        assert tm % FFT_N == 0 and rows % 8 == 0 and n % (FFT_N * FFT_N) == 0
        out_specs[0] = pl.BlockSpec((None, FFT_N, rows, D_FOURIER), lambda i: (i // tiles, 0, i % tiles, 0))
        out_shape[0] = jax.ShapeDtypeStruct((n // (FFT_N * FFT_N), FFT_N, FFT_N, D_FOURIER), F32)
        scratch = [pltpu.VMEM((D_FOURIER // LANES, tm, LANES), F32)]
    return pl.pallas_call(
        functools.partial(_mixin_kernel, d_model=d, fft_layout=fft_layout),
        grid=(n // tm,),
        in_specs=[tok(d),
                  pl.BlockSpec((None, N_MOD, d), lambda i: (i // tiles_per_group, 0, 0)),
                  _resident((1, d)), _resident((d, ncols)),
                  _resident((ALR_PAD, 2 * DK_TOT)), _resident((1, 2 * DK_TOT))],
        out_specs=out_specs,
        out_shape=out_shape,
        scratch_shapes=scratch,
        compiler_params=_params(1),
        name="mixin",
    )(x, mod, gain.reshape(1, d), w_in_packed, w_alpha, b_alpha)


def _dft_cos_sin(n):
    k = np.arange(n)
    ang = 2.0 * np.pi * ((k[:, None] * k[None, :]) % n) / n
    return np.cos(ang), np.sin(ang)


def _split_hi_lo(x):
    hi = x.astype(BF16)
    return hi, (x - hi.astype(F32)).astype(BF16)


def _table_pieces(table, axis):
    hi, lo = _split_hi_lo(jnp.asarray(table.astype(np.float32)))
    return jnp.concatenate([hi, hi], axis=axis), lo


def _dot_data_table(x, cat_ref, lo_ref):
    hi, lo = _split_hi_lo(x)
    return _dot(jnp.concatenate([hi, lo], axis=1), cat_ref[...]) + _dot(hi, lo_ref[...])


def _dot_table_data(cat, lo_t, x):
    hi, lo = _split_hi_lo(x)
    return _dot(cat, jnp.concatenate([hi, lo], axis=0)) + _dot(lo_t, hi)


def _width_dft_table(t_len):
    c, s = _dft_cos_sin(GROUP_W)
    return np.concatenate([c, -s], axis=1) / math.sqrt(t_len * GROUP_W)


def _width_dft(x, cwc_ref, cwl_ref):
    zr, zi = [], []
    for g in range(N_GROUPS):
        z = _dot_data_table(x[:, g * GROUP_W:(g + 1) * GROUP_W], cwc_ref, cwl_ref)
        zr.append(z[:, :GROUP_W])
        zi.append(z[:, GROUP_W:])
    return jnp.concatenate(zr, axis=1), jnp.concatenate(zi, axis=1)


def _fourier_dense_kernel(x_ref, cwc_ref, cwl_ref, ftc_ref, ftl_ref, o_ref):
    zr, zi = _width_dft(x_ref[...], cwc_ref, cwl_ref)
    zz = jnp.concatenate([zr, zi], axis=0)
    o_ref[...] = _dot_table_data(ftc_ref[...], ftl_ref[...], zz)


def _fourier_dense(f, nb, t_len):
    c, s = _dft_cos_sin(t_len)
    consts = [*_table_pieces(_width_dft_table(t_len), 0), *_table_pieces(np.concatenate([c, s], axis=1), 1)]
    return pl.pallas_call(
        _fourier_dense_kernel,
        grid=(nb,),
        in_specs=[pl.BlockSpec((t_len, D_FOURIER), lambda b: (b, 0))] + [_resident(a.shape) for a in consts],
        out_specs=pl.BlockSpec((t_len, D_FOURIER), lambda b: (b, 0)),
        out_shape=jax.ShapeDtypeStruct(f.shape, F32),
        compiler_params=_params(1),
        name="fourier_dense",
    )(f, *consts)


def _fourier_stage1_kernel(x_ref, cwc_ref, cwl_ref, mc_ref, ml_ref, ar_ref, ai_ref, sr_ref, si_ref):
    for j in range(FFT_ROWS):
        zr, zi = _width_dft(x_ref[j], cwc_ref, cwl_ref)
        a = _dot_table_data(mc_ref[j], ml_ref[j], jnp.concatenate([zr, zi], axis=0))
        for g in range(D_FOURIER // LANES):
            lanes = slice(g * LANES, (g + 1) * LANES)
            sr_ref[g, pl.ds(j, FFT_N, stride=FFT_ROWS), :] = a[:FFT_N, lanes]
            si_ref[g, pl.ds(j, FFT_N, stride=FFT_ROWS), :] = a[FFT_N:, lanes]
    for g in range(D_FOURIER // LANES):
        lanes = slice(g * LANES, (g + 1) * LANES)
        ar_ref[:, :, lanes] = sr_ref[g].reshape(FFT_N, FFT_ROWS, LANES)
        ai_ref[:, :, lanes] = si_ref[g].reshape(FFT_N, FFT_ROWS, LANES)


def _fourier_stage2_kernel(ar_ref, ai_ref, f2c_ref, f2l_ref, o_ref):
    for j in range(FFT_ROWS):
        aa = jnp.concatenate([ar_ref[j], ai_ref[j]], axis=0)
        o_ref[j] = _dot_table_data(f2c_ref[...], f2l_ref[...], aa)


def _fourier_two_stage(x, nb):
    n = FFT_N
    t_len = n * n
    s1 = np.arange(n)[:, None]
    t1 = np.arange(n)[None, :]
    mats = []
    for t2 in range(n):
        ang = 2.0 * np.pi * ((s1 * t1 * n + s1 * t2) % t_len) / t_len
        c, s = np.cos(ang), np.sin(ang)
        mats.append(np.block([[c, s], [-s, c]]))
    m_cat, m_lo = _table_pieces(np.stack(mats), 2)
    c2, s2 = _dft_cos_sin(n)
    f2 = _table_pieces(np.concatenate([c2, s2], axis=1), 1)
    cw = _table_pieces(_width_dft_table(t_len), 0)

    rows_blk = pl.BlockSpec((None, FFT_ROWS, n, D_FOURIER), lambda b, j: (b, j, 0, 0))
    cols_blk = pl.BlockSpec((None, n, FFT_ROWS, D_FOURIER), lambda b, j: (b, 0, j, 0))
    shape4 = jax.ShapeDtypeStruct((nb, n, n, D_FOURIER), F32)
    slabs = pltpu.VMEM((D_FOURIER // LANES, n * FFT_ROWS, LANES), F32)
    ar, ai = pl.pallas_call(
        _fourier_stage1_kernel,
        grid=(nb, n // FFT_ROWS),
        in_specs=[rows_blk, _resident(cw[0].shape), _resident(cw[1].shape),
                  pl.BlockSpec((FFT_ROWS,) + m_cat.shape[1:], lambda b, j: (j, 0, 0)),
                  pl.BlockSpec((FFT_ROWS,) + m_lo.shape[1:], lambda b, j: (j, 0, 0))],
        out_specs=[cols_blk, cols_blk],
        out_shape=[shape4, shape4],
        scratch_shapes=[slabs, slabs],
        compiler_params=_params(2),
        name="fourier_stage1",
    )(x, *cw, m_cat, m_lo)
    return pl.pallas_call(
        _fourier_stage2_kernel,
        grid=(nb, n // FFT_ROWS),
        in_specs=[rows_blk, rows_blk, _resident(f2[0].shape), _resident(f2[1].shape)],
        out_specs=rows_blk,
        out_shape=shape4,
        compiler_params=_params(2),
        name="fourier_stage2",
    )(ar, ai, *f2)


def _gla_tables(chunk):
    levels = GLA_LEVELS
    idx = np.arange(chunk)
    i = idx[:, None]
    t = idx[None, :]
    rows = []
    for l in range(1, levels + 1):
        w = 1 << l
        rows.append((t >= i - i % w) & (t <= i))
    for l in range(1, levels + 1):
        w = 1 << l
        rows.append((t > i) & (t <= i - i % w + w - 1))
    summat = np.concatenate(rows, axis=0).astype(np.float32)
    summat = np.concatenate([summat, summat], axis=1)
    j = t
    masks = [i == j]
    for l in range(levels):
        same = (i >> (l + 1)) == (j >> (l + 1))
        masks.append(same & (((i >> l) & 1) == 1) & (((j >> l) & 1) == 0))
    mask_f = np.stack(masks).astype(np.float32)
    return summat, mask_f, np.transpose(mask_f, (0, 2, 1)).copy()


_NT = (((1,), (1,)), ((), ()))
_TN = (((0,), (0,)), ((), ()))


def _gla_prepare(q_ref, k_ref, laf_ref, lab_ref, sum_ref, qs_ref, ks_ref, tot_ref, rows_f, rows_b):
    c = GLA_CHUNK
    levels = GLA_LEVELS
    n = len(rows_f)
    args, las = [], []
    for g in range(n):
        laf = laf_ref[pl.ds(rows_f[g], c), :]
        lab = lab_ref[pl.ds(rows_b[g], c), :]
        la = jnp.concatenate([laf, lab], axis=1)
        hi = la.astype(BF16)
        lo = (la - hi.astype(F32)).astype(BF16)
        args.append(_dot(sum_ref[...], jnp.concatenate([hi, lo], axis=0)))
        las.append((laf, lab))
    for g in range(n):
        a = args[g]
        laf, lab = las[g]

        def dsum(l):
            return a[(l - 1) * c:l * c]

        def esum(l):
            return a[(levels + l - 1) * c:(levels + l) * c]

        for fwd in (True, False):
            chain = 2 * g + (0 if fwd else 1)
            rows = pl.ds(rows_f[g] if fwd else rows_b[g], c)
            q = q_ref[rows, :]
            k = k_ref[rows, :]
            la = laf if fwd else lab
            qs_ref[chain, 0:c, :] = q.astype(BF16)
            ks_ref[chain, 0:c, :] = k.astype(BF16)
            for l in range(levels + 1):
                if l == 0:
                    d, e = jnp.exp2(la), None
                elif fwd:
                    d, e = jnp.exp2(dsum(l)[:, :DK]), jnp.exp2(esum(l)[:, :DK])
                else:
                    d, e = jnp.exp2(esum(l)[:, DK:] + la), jnp.exp2(dsum(l)[:, DK:] - la)
                qs_ref[chain, (l + 1) * c:(l + 2) * c, :] = (q * d).astype(BF16)
                if e is not None:
                    ks_ref[chain, l * c:(l + 1) * c, :] = (k * e).astype(BF16)
            tot_ref[chain] = jnp.broadcast_to(d[c - 1:c, :] if fwd else d[0:1, :], (8, DK))


def _gla_apply(v_ref, maskf_ref, maskb_ref, qs_ref, ks_ref, tot_ref, stf_ref, stb_ref, o_ref,
               rows_f, rows_b, *, accumulate):
    c = GLA_CHUNK
    levels = GLA_LEVELS
    n = len(rows_f)
    scores = []
    for chain in range(2 * n):
        m_ref = maskf_ref if chain % 2 == 0 else maskb_ref
        s = None
        for l in range(levels + 1):
            kl = max(l - 1, 0)
            p = lax.dot_general(qs_ref[chain, l * c:(l + 1) * c, :], ks_ref[chain, kl * c:(kl + 1) * c, :], _NT,
                                preferred_element_type=F32) * m_ref[l]
            s = p if s is None else s + p
        scores.append(s.astype(BF16))
    for chain in range(2 * n):
        fwd = chain % 2 == 0
        g = chain // 2
        rows = pl.ds(rows_f[g] if fwd else rows_b[g], c)
        st_ref = stf_ref if fwd else stb_ref
        v = v_ref[rows, :]
        st = st_ref[...]
        q_top = qs_ref[chain, (levels + 1) * c:(levels + 2) * c, :]
        k_top = ks_ref[chain, levels * c:(levels + 1) * c, :]
        o = _dot(scores[chain], v) + lax.dot_general(q_top, st.astype(BF16), _NT, preferred_element_type=F32)
        st_ref[...] = st * tot_ref[chain][0:1, :] + lax.dot_general(v, k_top, _TN, preferred_element_type=F32)
        if accumulate:
            o_ref[rows, :] += o
        else:
            o_ref[rows, :] = o


def _gla_kernel(q_ref, k_ref, v_ref, laf_ref, lab_ref, s0f_ref, s0b_ref, sum_ref, maskf_ref, maskb_ref,
                o_ref, sf_ref, sb_ref, stf_ref, stb_ref, qs0_ref, ks0_ref, tot0_ref, qs1_ref, ks1_ref, tot1_ref,
                *, n_chunks, group):
    stf_ref[...] = s0f_ref[...].T
    stb_ref[...] = s0b_ref[...].T
    steps = n_chunks // group
    slots = ((qs0_ref, ks0_ref, tot0_ref), (qs1_ref, ks1_ref, tot1_ref))

    def rows_of(i):
        rows_f = [pl.multiple_of((i * group + g) * GLA_CHUNK, GLA_CHUNK) for g in range(group)]
        rows_b = [pl.multiple_of((n_chunks - 1 - i * group - g) * GLA_CHUNK, GLA_CHUNK) for g in range(group)]
        return rows_f, rows_b

    def prepare(i, slot):
        _gla_prepare(q_ref, k_ref, laf_ref, lab_ref, sum_ref, *slots[slot], *rows_of(i))

    def apply(i, slot, accumulate):
        _gla_apply(v_ref, maskf_ref, maskb_ref, *slots[slot], stf_ref, stb_ref, o_ref, *rows_of(i),
                   accumulate=accumulate)

    def pair(j, accumulate, last):
        prepare(2 * j + 1, 1)
        apply(2 * j, 0, accumulate[0])
        if not last:
            prepare(2 * j + 2, 0)
        apply(2 * j + 1, 1, accumulate[1])

    n_pairs = steps // 2
    prepare(0, 0)
    if n_pairs == 1:
        pair(0, (False, True), True)
    else:
        def body(accumulate):
            def f(j, carry):
                pair(j, (accumulate, accumulate), False)
                return carry
            return f
        lax.fori_loop(0, n_pairs // 2, body(False), 0)
        lax.fori_loop(n_pairs // 2, n_pairs - 1, body(True), 0)
        pair(n_pairs - 1, (True, True), True)
    sf_ref[...] = stf_ref[...].T
    sb_ref[...] = stb_ref[...].T


def _gla(q, k, v, laf, lab, s0f, s0b, nb, t_len):
    group = GLA_GROUP
    n_chunks = t_len // GLA_CHUNK
    steps = n_chunks // group
    assert t_len % GLA_CHUNK == 0 and n_chunks % group == 0 and (steps == 2 or steps % 4 == 0)
    summat, mask_f, mask_b = _gla_tables(GLA_CHUNK)
    consts = [jnp.asarray(summat, BF16), jnp.asarray(mask_f), jnp.asarray(mask_b)]

    def seq(width):
        return pl.BlockSpec((t_len, width), lambda b, h: (b, h))

    state = pl.BlockSpec((None, None, DK, DV), lambda b, h: (b, h, 0, 0))
    n = nb * t_len
    slot = [pltpu.VMEM((2 * group, (GLA_LEVELS + 2) * GLA_CHUNK, DK), BF16),
            pltpu.VMEM((2 * group, (GLA_LEVELS + 1) * GLA_CHUNK, DK), BF16),
            pltpu.VMEM((2 * group, 8, DK), F32)]
    return pl.pallas_call(
        functools.partial(_gla_kernel, n_chunks=n_chunks, group=group),
        grid=(nb, N_HEADS),
        in_specs=[seq(DK), seq(DK), seq(DV), seq(DK), seq(DK), state, state]
                 + [_resident(a.shape) for a in consts],
        out_specs=[seq(DV), state, state],
        out_shape=[jax.ShapeDtypeStruct((n, DV_TOT), F32),
                   jax.ShapeDtypeStruct((nb, N_HEADS, DK, DV), F32),
                   jax.ShapeDtypeStruct((nb, N_HEADS, DK, DV), F32)],
        scratch_shapes=[pltpu.VMEM((DV, DK), F32), pltpu.VMEM((DV, DK), F32)] + slot + slot,
        compiler_params=_params(2),
        name="gla",
    )(q, k, v, laf, lab, s0f, s0b, *consts)


def _mixout_kernel(x_ref, mod_ref, m_ref, o_ref, r_ref, g_ref, gn_ref, wpf_ref, wpg_ref, wout_ref, y_ref,
                   *scratch, d_model, fft_layout):
    if fft_layout:
        slab_ref, = scratch
        rows = m_ref.shape[1]
        for g in range(D_FOURIER // LANES):
            slab_ref[g] = m_ref[:, :, g * LANES:(g + 1) * LANES].reshape(FFT_N * rows, LANES)
        m = jnp.concatenate(
            [jnp.concatenate([slab_ref[g, pl.ds(s, FFT_N, stride=rows), :] for g in range(D_FOURIER // LANES)],
                             axis=1) for s in range(rows)], axis=0)
    else:
        m = m_ref[...]
    branch_a = _dot(m.astype(BF16), wpf_ref[...])
    o = o_ref[...]
    parts = []
    for h in range(N_HEADS):
        oh = o[:, h * DV:(h + 1) * DV]
        parts.append(oh * lax.rsqrt(jnp.mean(oh * oh, axis=-1, keepdims=True) + RMS_EPS))
    r = r_ref[...]
    on = jnp.concatenate(parts, axis=1) * gn_ref[...] * (r * _sigmoid(r))
    branch_b = _dot(on.astype(BF16), wpg_ref[...])
    g = _sigmoid(g_ref[...])
    merged = g[:, :d_model] * branch_a + g[:, d_model:] * branch_b
    y = _dot(merged.astype(BF16), wout_ref[...])
    y_ref[...] = x_ref[...] + mod_ref[5:6, :] * y


def _mixout(x, mod, m, o, r, g, gla_norm, wpf, wpg, wout, *, tiles_per_group, fft_layout):
    n, d = x.shape
    tm = TOKEN_TILE

    def tok(width):
        return pl.BlockSpec((tm, width), lambda i: (i, 0))

    m_spec = tok(D_FOURIER)
    scratch = []
    if fft_layout:
        rows = tm // FFT_N
        tiles = FFT_N // rows
        m_spec = pl.BlockSpec((None, FFT_N, rows, D_FOURIER), lambda i: (i // tiles, 0, i % tiles, 0))
        scratch = [pltpu.VMEM((D_FOURIER // LANES, tm, LANES), F32)]
    return pl.pallas_call(
        functools.partial(_mixout_kernel, d_model=d, fft_layout=fft_layout),
        grid=(n // tm,),
        in_specs=[tok(d),
                  pl.BlockSpec((None, N_MOD, d), lambda i: (i // tiles_per_group, 0, 0)),
                  m_spec, tok(DV_TOT), tok(DV_TOT), tok(2 * d),
                  _resident((1, DV_TOT)), _resident(wpf.shape), _resident(wpg.shape), _resident(wout.shape)],
        out_specs=tok(d),
        out_shape=jax.ShapeDtypeStruct((n, d), F32),
        scratch_shapes=scratch,
        compiler_params=_params(1),
        name="mixout",
    )(x, mod, m, o, r, g, gla_norm.reshape(1, DV_TOT), wpf, wpg, wout)


def _grid_pos_tables(n_tokens, d_model):
    n_freq = d_model // 4
    omega = POS_BASE ** (-jnp.arange(n_freq, dtype=F32) / n_freq)
    ra = jnp.arange(n_tokens // GRID_W, dtype=F32)[:, None] * omega
    ca = jnp.arange(GRID_W, dtype=F32)[:, None] * omega
    return (jnp.concatenate([jnp.sin(ra), jnp.cos(ra)], axis=-1),
            jnp.concatenate([jnp.sin(ca), jnp.cos(ca)], axis=-1))


def _trunk_layer(x, mod, s0f, s0b, nb, t_len, w, *, pos, final_gain, two_stage_fft):
    tiles = max(t_len // TOKEN_TILE, 1) if mod.shape[0] > 1 else x.shape[0] // TOKEN_TILE
    x = _ffn(x, mod, w["norm_ffn1"], w["ffn1_gate"], w["ffn1_up"], w["ffn1_down"],
             tiles_per_group=tiles, first_mod_row=0, pos=pos)
    f, q, k, v, r, g, laf, lab = _mixin(x, mod, w["norm_mix"], w["w_in"], w["w_alpha"], w["b_alpha"],
                                        tiles_per_group=tiles, fft_layout=two_stage_fft)
    m = _fourier_two_stage(f, nb) if two_stage_fft else _fourier_dense(f, nb, t_len)
    o, sf, sb = _gla(q, k, v, laf, lab, s0f, s0b, nb, t_len)
    x = _mixout(x, mod, m, o, r, g, w["gla_norm"], w["proj_fourier"], w["proj_gla"], w["w_out"],
                tiles_per_group=tiles, fft_layout=two_stage_fft)
    x = _ffn(x, mod, w["norm_ffn2"], w["ffn2_gate"], w["ffn2_up"], w["ffn2_down"],
             tiles_per_group=tiles, first_mod_row=6, final_gain=final_gain)
    return x, sf, sb


def kernel(x_prompt, x_sample, state_gla_fwd, state_gla_bwd, c, c_ctx, w_ada, b_ada, norm_ffn1, w_ffn1_gate, w_ffn1_up, w_ffn1_down, norm_mix, w_in, w_alpha_fwd, b_alpha_fwd, w_alpha_bwd, b_alpha_bwd, gla_norm, w_proj_fourier, w_proj_gla, w_out, norm_ffn2, w_ffn2_gate, w_ffn2_up, w_ffn2_down, final_norm):
    nb_ctx, t_ctx, d = x_prompt.shape
    nb_lat, t_lat, _ = x_sample.shape
    depth = w_ada.shape[0]
    assert t_lat == FFT_N * FFT_N and (nb_ctx * t_ctx) % TOKEN_TILE == 0 and t_lat % TOKEN_TILE == 0

    xc = x_prompt.reshape(nb_ctx * t_ctx, d)
    xl = x_sample.reshape(nb_lat * t_lat, d)
    pos = _grid_pos_tables(t_lat, d)
    zero_state = jnp.zeros((nb_ctx, N_HEADS, DK, DV), F32)
    pad_rows = (-(nb_lat + 1)) % 8
    cvecs = jnp.concatenate([c, c_ctx[None, :], jnp.zeros((pad_rows, d), F32)], axis=0)

    new_fwd, new_bwd = [], []
    for l in range(depth):
        cut = _C_G
        w_in_l = w_in[l]
        w_in_packed = jnp.concatenate(
            [w_in_l[:, :cut], w_in_l[:, cut + 2 * GATE_RANK:], w_in_l[:, cut:cut + 2 * GATE_RANK],
             jnp.zeros((d, ALR_PAD - 2 * GATE_RANK), F32)], axis=1).astype(BF16)
        w_alpha = jnp.zeros((ALR_PAD, 2 * DK_TOT), F32)
        w_alpha = w_alpha.at[:GATE_RANK, :DK_TOT].set(w_alpha_fwd[l])
        w_alpha = w_alpha.at[GATE_RANK:2 * GATE_RANK, DK_TOT:].set(w_alpha_bwd[l]).astype(BF16)
        b_alpha = jnp.concatenate([b_alpha_fwd[l], b_alpha_bwd[l]]).reshape(1, 2 * DK_TOT)
        w = {
            "norm_ffn1": norm_ffn1[l], "ffn1_gate": w_ffn1_gate[l].astype(BF16),
            "ffn1_up": w_ffn1_up[l].astype(BF16), "ffn1_down": w_ffn1_down[l].astype(BF16),
            "norm_mix": norm_mix[l], "w_in": w_in_packed, "w_alpha": w_alpha, "b_alpha": b_alpha,
            "gla_norm": gla_norm[l], "proj_fourier": w_proj_fourier[l].astype(BF16),
            "proj_gla": w_proj_gla[l].astype(BF16), "w_out": w_out[l].astype(BF16),
            "norm_ffn2": norm_ffn2[l], "ffn2_gate": w_ffn2_gate[l].astype(BF16),
            "ffn2_up": w_ffn2_up[l].astype(BF16), "ffn2_down": w_ffn2_down[l].astype(BF16),
        }
        last = l == depth - 1
        mod = _ada(cvecs, w_ada[l], b_ada[l]).reshape(-1, N_MOD, d)
        mod_lat = mod[:nb_lat]
        mod_ctx = mod[nb_lat:nb_lat + 1]
        xc, sf, sb = _trunk_layer(xc, mod_ctx, zero_state, zero_state, nb_ctx, t_ctx, w,
                                  pos=None, final_gain=final_norm if last else None, two_stage_fft=False)
        new_fwd.append(sf)
        new_bwd.append(sb)
        xl, _, _ = _trunk_layer(xl, mod_lat, state_gla_fwd[:, l], state_gla_bwd[:, l], nb_lat, t_lat, w,
                                pos=pos if l == 0 else None, final_gain=final_norm if last else None,
                                two_stage_fft=True)
    y_prompt = xc.reshape(nb_ctx, t_ctx, d)
    y_sample = xl.reshape(nb_lat, t_lat, d)
    return (y_prompt, y_sample, jnp.stack(new_fwd, axis=1), jnp.stack(new_bwd, axis=1))
```

```python
import functools
import math

import numpy as np
import jax
import jax.numpy as jnp
from jax import lax
from jax.experimental import pallas as pl
from jax.experimental.pallas import tpu as pltpu

F32 = jnp.float32
BF16 = jnp.bfloat16

GRID_W = 64
N_GROUPS = 4
GROUP_W = 128
D_FOURIER = N_GROUPS * GROUP_W
N_HEADS = 4
DK = 128
DV = 256
DK_TOT = N_HEADS * DK
DV_TOT = N_HEADS * DV
GATE_RANK = 16
GATE_TEMP = 16.0
RMS_EPS = 1e-6
POS_BASE = 10000.0
N_MOD = 9

TOKEN_TILE = 512
GLA_CHUNK = 64
GLA_LEVELS = 6
GLA_GROUP = 2
LANES = 128
ALR_PAD = LANES
FFT_N = 64
FFT_ROWS = 8
VMEM_LIMIT = 56 * 1024 * 1024


def _params(n_axes):
    return pltpu.CompilerParams(dimension_semantics=("arbitrary",) * n_axes,
                                vmem_limit_bytes=VMEM_LIMIT)


def _resident(shape):
    nd = len(shape)
    return pl.BlockSpec(shape, lambda *_: (0,) * nd, pipeline_mode=pl.Buffered(1))


def _sigmoid(x):
    return 1.0 / (1.0 + jnp.exp(-x))


def _rms(x, gain):
    return x * lax.rsqrt(jnp.mean(x * x, axis=-1, keepdims=True) + RMS_EPS) * gain


def _dot(a, b):
    return jnp.dot(a, b, preferred_element_type=F32)


def _ada_kernel(c_ref, w_ref, b_ref, o_ref):
    c = c_ref[...]
    s = (c * _sigmoid(c)).astype(BF16)
    o_ref[...] = _dot(s, w_ref[...].astype(BF16)) + b_ref[...]


def _ada(cvecs, w_ada, b_ada):
    rows, d = cvecs.shape
    n = w_ada.shape[1]
    tn = d
    return pl.pallas_call(
        _ada_kernel,
        grid=(n // tn,),
        in_specs=[pl.BlockSpec((rows, d), lambda j: (0, 0)),
                  pl.BlockSpec((d, tn), lambda j: (0, j)),
                  pl.BlockSpec((1, tn), lambda j: (0, j))],
        out_specs=pl.BlockSpec((rows, tn), lambda j: (0, j)),
        out_shape=jax.ShapeDtypeStruct((rows, n), F32),
        compiler_params=_params(1),
        name="ada",
    )(cvecs, w_ada, b_ada.reshape(1, n))


def _ffn_kernel(*refs, first_mod_row, add_pos, final_norm):
    x_ref, mod_ref, gain_ref, wg_ref, wu_ref, wd_ref = refs[:6]
    rest = list(refs[6:])
    prow_ref, pcol_ref = (rest.pop(0), rest.pop(0)) if add_pos else (None, None)
    fn_ref = rest.pop(0) if final_norm else None
    o_ref = rest.pop(0)

    x = x_ref[...]
    if add_pos:
        tm, d = x.shape
        rows = tm // GRID_W
        prow = jnp.broadcast_to(prow_ref[...][:, None, :], (rows, GRID_W, d // 2))
        pcol = jnp.broadcast_to(pcol_ref[...][None, :, :], (rows, GRID_W, d // 2))
        x = x + jnp.concatenate([prow, pcol], axis=-1).reshape(tm, d)
    shift = mod_ref[first_mod_row:first_mod_row + 1, :]
    scale = mod_ref[first_mod_row + 1:first_mod_row + 2, :]
    gate = mod_ref[first_mod_row + 2:first_mod_row + 3, :]
    h = (_rms(x, gain_ref[...]) * (1.0 + scale) + shift).astype(BF16)
    g = _dot(h, wg_ref[...])
    u = _dot(h, wu_ref[...])
    a = (g * _sigmoid(g) * u).astype(BF16)
    y = _dot(a, wd_ref[...])
    xn = x + (0.5 * gate) * y
    if final_norm:
        xn = _rms(xn, fn_ref[...])
    o_ref[...] = xn


def _ffn(x, mod, gain, wg, wu, wd, *, tiles_per_group, first_mod_row, pos=None, final_gain=None):
    n, d = x.shape
    dff = wg.shape[1]
    tm = TOKEN_TILE
    add_pos = pos is not None
    final_norm = final_gain is not None
    in_specs = [pl.BlockSpec((tm, d), lambda i: (i, 0)),
                pl.BlockSpec((None, N_MOD, d), lambda i: (i // tiles_per_group, 0, 0)),
                _resident((1, d)), _resident((d, dff)), _resident((d, dff)), _resident((dff, d))]
    args = [x, mod, gain.reshape(1, d), wg, wu, wd]
    if add_pos:
        pos_row, pos_col = pos
        rows = tm // GRID_W
        row_tiles = pos_row.shape[0] // rows
        assert tm % GRID_W == 0 and rows % 8 == 0 and pos_col.shape[0] == GRID_W
        in_specs += [pl.BlockSpec((rows, d // 2), lambda i: (i % row_tiles, 0)), _resident(pos_col.shape)]
        args += [pos_row, pos_col]
    if final_norm:
        in_specs.append(_resident((1, d)))
        args.append(final_gain.reshape(1, d))
    return pl.pallas_call(
        functools.partial(_ffn_kernel, first_mod_row=first_mod_row, add_pos=add_pos, final_norm=final_norm),
        grid=(n // tm,),
        in_specs=in_specs,
        out_specs=pl.BlockSpec((tm, d), lambda i: (i, 0)),
        out_shape=jax.ShapeDtypeStruct((n, d), F32),
        compiler_params=_params(1),
        name="ffn",
    )(*args)


_C_F = 0
_C_Q = _C_F + D_FOURIER
_C_K = _C_Q + DK_TOT
_C_V = _C_K + DK_TOT
_C_R = _C_V + DV_TOT
_C_G = _C_R + DV_TOT


def _mixin_kernel(x_ref, mod_ref, gain_ref, w_ref, wal_ref, bal_ref,
                  f_ref, q_ref, k_ref, v_ref, r_ref, g_ref, laf_ref, lab_ref, *scratch, d_model, fft_layout):
    c_a = _C_G + 2 * d_model
    x = x_ref[...]
    shift = mod_ref[3:4, :]
    scale = mod_ref[4:5, :]
    h = (_rms(x, gain_ref[...]) * (1.0 + scale) + shift).astype(BF16)

    def proj(lo, hi):
        return _dot(h, w_ref[:, lo:hi])

    alr = proj(c_a, c_a + ALR_PAD).astype(BF16)
    pre = _dot(alr, wal_ref[...]) + bal_ref[...]
    la = (jnp.minimum(pre, 0.0) - jnp.log1p(jnp.exp(-jnp.abs(pre)))) * (math.log2(math.e) / GATE_TEMP)
    laf_ref[...] = la[:, :DK_TOT]
    lab_ref[...] = la[:, DK_TOT:]

    f = proj(_C_F, _C_Q)
    if fft_layout:
        slab_ref, = scratch
        rows = x.shape[0] // FFT_N
        for a in range(rows):
            for g in range(D_FOURIER // LANES):
                slab_ref[g, pl.ds(a, FFT_N, stride=rows), :] = f[a * FFT_N:(a + 1) * FFT_N, g * LANES:(g + 1) * LANES]
        for g in range(D_FOURIER // LANES):
            f_ref[:, :, g * LANES:(g + 1) * LANES] = slab_ref[g].reshape(FFT_N, rows, LANES)
    else:
        f_ref[...] = f
    q_ref[...] = proj(_C_Q, _C_K) * (DK ** -0.5)
    k_ref[...] = proj(_C_K, _C_V)
    v_ref[...] = proj(_C_V, _C_R).astype(BF16)
    r_ref[...] = proj(_C_R, _C_G)
    g_ref[...] = proj(_C_G, c_a)


def _mixin(x, mod, gain, w_in_packed, w_alpha, b_alpha, *, tiles_per_group, fft_layout):
    n, d = x.shape
    tm = TOKEN_TILE
    ncols = w_in_packed.shape[1]

    def tok(width):
        return pl.BlockSpec((tm, width), lambda i: (i, 0))

    outs = [(D_FOURIER, F32), (DK_TOT, F32), (DK_TOT, F32), (DV_TOT, BF16), (DV_TOT, F32),
            (2 * d, F32), (DK_TOT, F32), (DK_TOT, F32)]
    out_specs = [tok(w) for w, _ in outs]
    out_shape = [jax.ShapeDtypeStruct((n, w), dt) for w, dt in outs]
    scratch = []
    if fft_layout:
        rows = tm // FFT_N
        tiles = FFT_N // rows
        assert tm % FFT_N == 0 and rows % 8 == 0 and n % (FFT_N * FFT_N) == 0
        out_specs[0] = pl.BlockSpec((None, FFT_N, rows, D_FOURIER), lambda i: (i // tiles, 0, i % tiles, 0))
        out_shape[0] = jax.ShapeDtypeStruct((n // (FFT_N * FFT_N), FFT_N, FFT_N, D_FOURIER), F32)
        scratch = [pltpu.VMEM((D_FOURIER // LANES, tm, LANES), F32)]
    return pl.pallas_call(
        functools.partial(_mixin_kernel, d_model=d, fft_layout=fft_layout),
        grid=(n // tm,),
        in_specs=[tok(d),
                  pl.BlockSpec((None, N_MOD, d), lambda i: (i // tiles_per_group, 0, 0)),
                  _resident((1, d)), _resident((d, ncols)),
                  _resident((ALR_PAD, 2 * DK_TOT)), _resident((1, 2 * DK_TOT))],
        out_specs=out_specs,
        out_shape=out_shape,
        scratch_shapes=scratch,
        compiler_params=_params(1),
        name="mixin",
    )(x, mod, gain.reshape(1, d), w_in_packed, w_alpha, b_alpha)


def _dft_cos_sin(n):
    k = np.arange(n)
    ang = 2.0 * np.pi * ((k[:, None] * k[None, :]) % n) / n
    return np.cos(ang), np.sin(ang)


def _split_hi_lo(x):
    hi = x.astype(BF16)
    return hi, (x - hi.astype(F32)).astype(BF16)


def _table_pieces(table, axis):
    hi, lo = _split_hi_lo(jnp.asarray(table.astype(np.float32)))
    return jnp.concatenate([hi, hi], axis=axis), lo


def _dot_data_table(x, cat_ref, lo_ref):
    hi, lo = _split_hi_lo(x)
    return _dot(jnp.concatenate([hi, lo], axis=1), cat_ref[...]) + _dot(hi, lo_ref[...])


def _dot_table_data(cat, lo_t, x):
    hi, lo = _split_hi_lo(x)
    return _dot(cat, jnp.concatenate([hi, lo], axis=0)) + _dot(lo_t, hi)


def _width_dft_table(t_len):
    c, s = _dft_cos_sin(GROUP_W)
    return np.concatenate([c, -s], axis=1) / math.sqrt(t_len * GROUP_W)


def _width_dft(x, cwc_ref, cwl_ref):
    zr, zi = [], []
    for g in range(N_GROUPS):
        z = _dot_data_table(x[:, g * GROUP_W:(g + 1) * GROUP_W], cwc_ref, cwl_ref)
        zr.append(z[:, :GROUP_W])
        zi.append(z[:, GROUP_W:])
    return jnp.concatenate(zr, axis=1), jnp.concatenate(zi, axis=1)


def _fourier_dense_kernel(x_ref, cwc_ref, cwl_ref, ftc_ref, ftl_ref, o_ref):
    zr, zi = _width_dft(x_ref[...], cwc_ref, cwl_ref)
    zz = jnp.concatenate([zr, zi], axis=0)
    o_ref[...] = _dot_table_data(ftc_ref[...], ftl_ref[...], zz)


def _fourier_dense(f, nb, t_len):
    c, s = _dft_cos_sin(t_len)
    consts = [*_table_pieces(_width_dft_table(t_len), 0), *_table_pieces(np.concatenate([c, s], axis=1), 1)]
    return pl.pallas_call(
        _fourier_dense_kernel,
        grid=(nb,),
        in_specs=[pl.BlockSpec((t_len, D_FOURIER), lambda b: (b, 0))] + [_resident(a.shape) for a in consts],
        out_specs=pl.BlockSpec((t_len, D_FOURIER), lambda b: (b, 0)),
        out_shape=jax.ShapeDtypeStruct(f.shape, F32),
        compiler_params=_params(1),
        name="fourier_dense",
    )(f, *consts)


def _fourier_stage1_kernel(x_ref, cwc_ref, cwl_ref, mc_ref, ml_ref, ar_ref, ai_ref, sr_ref, si_ref):
    zs = [_width_dft(x_ref[j], cwc_ref, cwl_ref) for j in range(FFT_ROWS)]
    for j in range(FFT_ROWS):
        a = _dot_table_data(mc_ref[j], ml_ref[j], jnp.concatenate(zs[j], axis=0))
        for g in range(D_FOURIER // LANES):
            lanes = slice(g * LANES, (g + 1) * LANES)
            sr_ref[g, pl.ds(j, FFT_N, stride=FFT_ROWS), :] = a[:FFT_N, lanes]
            si_ref[g, pl.ds(j, FFT_N, stride=FFT_ROWS), :] = a[FFT_N:, lanes]
    for g in range(D_FOURIER // LANES):
        lanes = slice(g * LANES, (g + 1) * LANES)
        ar_ref[:, :, lanes] = sr_ref[g].reshape(FFT_N, FFT_ROWS, LANES)
        ai_ref[:, :, lanes] = si_ref[g].reshape(FFT_N, FFT_ROWS, LANES)


def _fourier_stage2_kernel(ar_ref, ai_ref, f2c_ref, f2l_ref, o_ref):
    for j in range(FFT_ROWS):
        aa = jnp.concatenate([ar_ref[j], ai_ref[j]], axis=0)
        o_ref[j] = _dot_table_data(f2c_ref[...], f2l_ref[...], aa)


def _fourier_two_stage(x, nb):
    n = FFT_N
    t_len = n * n
    s1 = np.arange(n)[:, None]
    t1 = np.arange(n)[None, :]
    mats = []
    for t2 in range(n):
        ang = 2.0 * np.pi * ((s1 * t1 * n + s1 * t2) % t_len) / t_len
        c, s = np.cos(ang), np.sin(ang)
        mats.append(np.block([[c, s], [-s, c]]))
    m_cat, m_lo = _table_pieces(np.stack(mats), 2)
    c2, s2 = _dft_cos_sin(n)
    f2 = _table_pieces(np.concatenate([c2, s2], axis=1), 1)
    cw = _table_pieces(_width_dft_table(t_len), 0)

    rows_blk = pl.BlockSpec((None, FFT_ROWS, n, D_FOURIER), lambda b, j: (b, j, 0, 0))
    cols_blk = pl.BlockSpec((None, n, FFT_ROWS, D_FOURIER), lambda b, j: (b, 0, j, 0))
    shape4 = jax.ShapeDtypeStruct((nb, n, n, D_FOURIER), F32)
    slabs = pltpu.VMEM((D_FOURIER // LANES, n * FFT_ROWS, LANES), F32)
    ar, ai = pl.pallas_call(
        _fourier_stage1_kernel,
        grid=(nb, n // FFT_ROWS),
        in_specs=[rows_blk, _resident(cw[0].shape), _resident(cw[1].shape),
                  pl.BlockSpec((FFT_ROWS,) + m_cat.shape[1:], lambda b, j: (j, 0, 0)),
                  pl.BlockSpec((FFT_ROWS,) + m_lo.shape[1:], lambda b, j: (j, 0, 0))],
        out_specs=[cols_blk, cols_blk],
        out_shape=[shape4, shape4],
        scratch_shapes=[slabs, slabs],
        compiler_params=_params(2),
        name="fourier_stage1",
    )(x, *cw, m_cat, m_lo)
    return pl.pallas_call(
        _fourier_stage2_kernel,
        grid=(nb, n // FFT_ROWS),
        in_specs=[rows_blk, rows_blk, _resident(f2[0].shape), _resident(f2[1].shape)],
        out_specs=rows_blk,
        out_shape=shape4,
        compiler_params=_params(2),
        name="fourier_stage2",
    )(ar, ai, *f2)


def _gla_masks(chunk):
    idx = np.arange(chunk)
    i = idx[:, None]
    j = idx[None, :]
    masks = [i == j]
    for l in range(GLA_LEVELS):
        same = (i >> (l + 1)) == (j >> (l + 1))
        masks.append(same & (((i >> l) & 1) == 1) & (((j >> l) & 1) == 0))
    mask_f = np.stack(masks).astype(np.float32)
    return mask_f, np.transpose(mask_f, (0, 2, 1)).copy()


_NT = (((1,), (1,)), ((), ()))
_TN = (((0,), (0,)), ((), ()))
SUBLANES = 8


def _block_sums(la, forward):
    c = la.shape[0]
    nv = c // SUBLANES
    x = la.reshape(nv, SUBLANES, DK)
    sub = lax.broadcasted_iota(jnp.int32, (1, SUBLANES, DK), 1)
    near = x
    tot = x
    out = {}
    w = 1
    while w < SUBLANES:
        right = (sub & w) != 0
        partner = jnp.where(right, pltpu.roll(tot, w, axis=1), pltpu.roll(tot, SUBLANES - w, axis=1))
        near = near + jnp.where(right if forward else jnp.logical_not(right), partner, 0.0)
        tot = tot + partner
        w *= 2
        out[w] = (near, tot - near)
    blk = 1
    while w < c:
        takes = (lambda v: v & blk) if forward else (lambda v: not v & blk)
        near = jnp.stack([near[v] + tot[v ^ blk] if takes(v) else near[v] for v in range(nv)])
        tot = jnp.stack([tot[v] + tot[v ^ blk] for v in range(nv)])
        w *= 2
        blk *= 2
        out[w] = (near, tot - near)
    return {w: (a.reshape(c, DK), b.reshape(c, DK)) for w, (a, b) in out.items()}


def _gla_prepare(q_ref, k_ref, laf_ref, lab_ref, qs_ref, ks_ref, tot_ref, rows_f, rows_b):
    c = GLA_CHUNK
    for g in range(len(rows_f)):
        for fwd in (True, False):
            chain = 2 * g + (0 if fwd else 1)
            rows = pl.ds(rows_f[g] if fwd else rows_b[g], c)
            q = q_ref[rows, :]
            k = k_ref[rows, :]
            la = (laf_ref if fwd else lab_ref)[rows, :]
            sums = _block_sums(la, fwd)
            qs_ref[chain, 0:c, :] = q.astype(BF16)
            ks_ref[chain, 0:c, :] = k.astype(BF16)
            for l in range(GLA_LEVELS + 1):
                if l == 0:
                    d, e = jnp.exp2(la), None
                else:
                    near, far = sums[1 << l]
                    d, e = jnp.exp2(near), jnp.exp2(far)
                qs_ref[chain, (l + 1) * c:(l + 2) * c, :] = (q * d).astype(BF16)
                if e is not None:
                    ks_ref[chain, l * c:(l + 1) * c, :] = (k * e).astype(BF16)
            tot_ref[chain] = jnp.broadcast_to(d[c - 1:c, :] if fwd else d[0:1, :], (SUBLANES, DK))


def _gla_apply(v_ref, maskf_ref, maskb_ref, qs_ref, ks_ref, tot_ref, stf_ref, stb_ref, o_ref,
               rows_f, rows_b, *, accumulate):
    c = GLA_CHUNK
    levels = GLA_LEVELS
    n = len(rows_f)
    scores = []
    for chain in range(2 * n):
        m_ref = maskf_ref if chain % 2 == 0 else maskb_ref
        s = None
        for l in range(levels + 1):
            kl = max(l - 1, 0)
            p = lax.dot_general(qs_ref[chain, l * c:(l + 1) * c, :], ks_ref[chain, kl * c:(kl + 1) * c, :], _NT,
                                preferred_element_type=F32) * m_ref[l]
            s = p if s is None else s + p
        scores.append(s.astype(BF16))
    for chain in range(2 * n):
        fwd = chain % 2 == 0
        g = chain // 2
        rows = pl.ds(rows_f[g] if fwd else rows_b[g], c)
        st_ref = stf_ref if fwd else stb_ref
        v = v_ref[rows, :]
        st = st_ref[...]
        q_top = qs_ref[chain, (levels + 1) * c:(levels + 2) * c, :]
        k_top = ks_ref[chain, levels * c:(levels + 1) * c, :]
        o = _dot(scores[chain], v) + _dot(q_top, st.astype(BF16))
        tot = jnp.broadcast_to(tot_ref[chain][0:1, :], (DK, DK)).T
        st_ref[...] = (st * jnp.concatenate([tot] * (DV // DK), axis=1)
                       + lax.dot_general(k_top, v, _TN, preferred_element_type=F32))
        if accumulate:
            o_ref[rows, :] += o
        else:
            o_ref[rows, :] = o


def _gla_kernel(q_ref, k_ref, v_ref, laf_ref, lab_ref, s0f_ref, s0b_ref, maskf_ref, maskb_ref,
                o_ref, sf_ref, sb_ref, qs0_ref, ks0_ref, tot0_ref, qs1_ref, ks1_ref, tot1_ref,
                *, n_chunks, group):
    sf_ref[...] = s0f_ref[...]
    sb_ref[...] = s0b_ref[...]
    steps = n_chunks // group
    slots = ((qs0_ref, ks0_ref, tot0_ref), (qs1_ref, ks1_ref, tot1_ref))

    def rows_of(i):
        rows_f = [pl.multiple_of((i * group + g) * GLA_CHUNK, GLA_CHUNK) for g in range(group)]
        rows_b = [pl.multiple_of((n_chunks - 1 - i * group - g) * GLA_CHUNK, GLA_CHUNK) for g in range(group)]
        return rows_f, rows_b

    def prepare(i, slot):
        _gla_prepare(q_ref, k_ref, laf_ref, lab_ref, *slots[slot], *rows_of(i))

    def apply(i, slot, accumulate):
        _gla_apply(v_ref, maskf_ref, maskb_ref, *slots[slot], sf_ref, sb_ref, o_ref, *rows_of(i),
                   accumulate=accumulate)

    def pair(j, accumulate, last):
        prepare(2 * j + 1, 1)
        apply(2 * j, 0, accumulate[0])
        if not last:
            prepare(2 * j + 2, 0)
        apply(2 * j + 1, 1, accumulate[1])

    n_pairs = steps // 2
    prepare(0, 0)
    if n_pairs == 1:
        pair(0, (False, True), True)
    else:
        def body(accumulate):
            def f(j, carry):
                pair(j, (accumulate, accumulate), False)
                return carry
            return f
        lax.fori_loop(0, n_pairs // 2, body(False), 0)
        lax.fori_loop(n_pairs // 2, n_pairs - 1, body(True), 0)
        pair(n_pairs - 1, (True, True), True)


def _gla(q, k, v, laf, lab, s0f, s0b, nb, t_len):
    group = GLA_GROUP
    n_chunks = t_len // GLA_CHUNK
    steps = n_chunks // group
    assert t_len % GLA_CHUNK == 0 and n_chunks % group == 0 and (steps == 2 or steps % 4 == 0)
    consts = [jnp.asarray(m) for m in _gla_masks(GLA_CHUNK)]

    def seq(width):
        return pl.BlockSpec((t_len, width), lambda b, h: (b, h))

    state = pl.BlockSpec((None, None, DK, DV), lambda b, h: (b, h, 0, 0))
    n = nb * t_len
    slot = [pltpu.VMEM((2 * group, (GLA_LEVELS + 2) * GLA_CHUNK, DK), BF16),
            pltpu.VMEM((2 * group, (GLA_LEVELS + 1) * GLA_CHUNK, DK), BF16),
            pltpu.VMEM((2 * group, 8, DK), F32)]
    return pl.pallas_call(
        functools.partial(_gla_kernel, n_chunks=n_chunks, group=group),
        grid=(nb, N_HEADS),
        in_specs=[seq(DK), seq(DK), seq(DV), seq(DK), seq(DK), state, state]
                 + [_resident(a.shape) for a in consts],
        out_specs=[seq(DV), state, state],
        out_shape=[jax.ShapeDtypeStruct((n, DV_TOT), F32),
                   jax.ShapeDtypeStruct((nb, N_HEADS, DK, DV), F32),
                   jax.ShapeDtypeStruct((nb, N_HEADS, DK, DV), F32)],
        scratch_shapes=slot + slot,
        compiler_params=_params(2),
        name="gla",
    )(q, k, v, laf, lab, s0f, s0b, *consts)


def _mixout_kernel(x_ref, mod_ref, m_ref, o_ref, r_ref, g_ref, gn_ref, wpf_ref, wpg_ref, wout_ref, y_ref,
                   *scratch, d_model, fft_layout):
    if fft_layout:
        slab_ref, = scratch
        rows = m_ref.shape[1]
        for g in range(D_FOURIER // LANES):
            slab_ref[g] = m_ref[:, :, g * LANES:(g + 1) * LANES].reshape(FFT_N * rows, LANES)
        m = jnp.concatenate(
            [jnp.concatenate([slab_ref[g, pl.ds(s, FFT_N, stride=rows), :] for g in range(D_FOURIER // LANES)],
                             axis=1) for s in range(rows)], axis=0)
    else:
        m = m_ref[...]
    tm = m.shape[0]
    halves = [slice(0, tm // 2), slice(tm // 2, tm)]
    branch_a = [_dot(m[rows].astype(BF16), wpf_ref[...]) for rows in halves]
    branch_b = []
    for rows in halves:
        o = o_ref[rows, :]
        parts = []
        for h in range(N_HEADS):
            oh = o[:, h * DV:(h + 1) * DV]
            parts.append(oh * lax.rsqrt(jnp.mean(oh * oh, axis=-1, keepdims=True) + RMS_EPS))
        r = r_ref[rows, :]
        on = jnp.concatenate(parts, axis=1) * gn_ref[...] * (r * _sigmoid(r))
        branch_b.append(_dot(on.astype(BF16), wpg_ref[...]))
    for i, rows in enumerate(halves):
        g = _sigmoid(g_ref[rows, :])
        merged = g[:, :d_model] * branch_a[i] + g[:, d_model:] * branch_b[i]
        y = _dot(merged.astype(BF16), wout_ref[...])
        y_ref[rows, :] = x_ref[rows, :] + mod_ref[5:6, :] * y


def _mixout(x, mod, m, o, r, g, gla_norm, wpf, wpg, wout, *, tiles_per_group, fft_layout):
    n, d = x.shape
    tm = TOKEN_TILE

    def tok(width):
        return pl.BlockSpec((tm, width), lambda i: (i, 0))

    m_spec = tok(D_FOURIER)
    scratch = []
    if fft_layout:
        rows = tm // FFT_N
        tiles = FFT_N // rows
        m_spec = pl.BlockSpec((None, FFT_N, rows, D_FOURIER), lambda i: (i // tiles, 0, i % tiles, 0))
        scratch = [pltpu.VMEM((D_FOURIER // LANES, tm, LANES), F32)]
    return pl.pallas_call(
        functools.partial(_mixout_kernel, d_model=d, fft_layout=fft_layout),
        grid=(n // tm,),
        in_specs=[tok(d),
                  pl.BlockSpec((None, N_MOD, d), lambda i: (i // tiles_per_group, 0, 0)),
                  m_spec, tok(DV_TOT), tok(DV_TOT), tok(2 * d),
                  _resident((1, DV_TOT)), _resident(wpf.shape), _resident(wpg.shape), _resident(wout.shape)],
        out_specs=tok(d),
        out_shape=jax.ShapeDtypeStruct((n, d), F32),
        scratch_shapes=scratch,
        compiler_params=_params(1),
        name="mixout",
    )(x, mod, m, o, r, g, gla_norm.reshape(1, DV_TOT), wpf, wpg, wout)


def _grid_pos_tables(n_tokens, d_model):
    n_freq = d_model // 4
    omega = POS_BASE ** (-jnp.arange(n_freq, dtype=F32) / n_freq)
    ra = jnp.arange(n_tokens // GRID_W, dtype=F32)[:, None] * omega
    ca = jnp.arange(GRID_W, dtype=F32)[:, None] * omega
    return (jnp.concatenate([jnp.sin(ra), jnp.cos(ra)], axis=-1),
            jnp.concatenate([jnp.sin(ca), jnp.cos(ca)], axis=-1))


def _trunk_layer(x, mod, s0f, s0b, nb, t_len, w, *, pos, final_gain, two_stage_fft):
    tiles = max(t_len // TOKEN_TILE, 1) if mod.shape[0] > 1 else x.shape[0] // TOKEN_TILE
    x = _ffn(x, mod, w["norm_ffn1"], w["ffn1_gate"], w["ffn1_up"], w["ffn1_down"],
             tiles_per_group=tiles, first_mod_row=0, pos=pos)
    f, q, k, v, r, g, laf, lab = _mixin(x, mod, w["norm_mix"], w["w_in"], w["w_alpha"], w["b_alpha"],
                                        tiles_per_group=tiles, fft_layout=two_stage_fft)
    m = _fourier_two_stage(f, nb) if two_stage_fft else _fourier_dense(f, nb, t_len)
    o, sf, sb = _gla(q, k, v, laf, lab, s0f, s0b, nb, t_len)
    x = _mixout(x, mod, m, o, r, g, w["gla_norm"], w["proj_fourier"], w["proj_gla"], w["w_out"],
                tiles_per_group=tiles, fft_layout=two_stage_fft)
    x = _ffn(x, mod, w["norm_ffn2"], w["ffn2_gate"], w["ffn2_up"], w["ffn2_down"],
             tiles_per_group=tiles, first_mod_row=6, final_gain=final_gain)
    return x, sf, sb


def kernel(x_prompt, x_sample, state_gla_fwd, state_gla_bwd, c, c_ctx, w_ada, b_ada, norm_ffn1, w_ffn1_gate, w_ffn1_up, w_ffn1_down, norm_mix, w_in, w_alpha_fwd, b_alpha_fwd, w_alpha_bwd, b_alpha_bwd, gla_norm, w_proj_fourier, w_proj_gla, w_out, norm_ffn2, w_ffn2_gate, w_ffn2_up, w_ffn2_down, final_norm):
    nb_ctx, t_ctx, d = x_prompt.shape
    nb_lat, t_lat, _ = x_sample.shape
    depth = w_ada.shape[0]
    assert t_lat == FFT_N * FFT_N and (nb_ctx * t_ctx) % TOKEN_TILE == 0 and t_lat % TOKEN_TILE == 0

    xc = x_prompt.reshape(nb_ctx * t_ctx, d)
    xl = x_sample.reshape(nb_lat * t_lat, d)
    pos = _grid_pos_tables(t_lat, d)
    zero_state = jnp.zeros((nb_ctx, N_HEADS, DK, DV), F32)
    pad_rows = (-(nb_lat + 1)) % 8
    cvecs = jnp.concatenate([c, c_ctx[None, :], jnp.zeros((pad_rows, d), F32)], axis=0)

    new_fwd, new_bwd = [], []
    for l in range(depth):
        cut = _C_G
        w_in_l = w_in[l]
        w_in_packed = jnp.concatenate(
            [w_in_l[:, :cut], w_in_l[:, cut + 2 * GATE_RANK:], w_in_l[:, cut:cut + 2 * GATE_RANK],
             jnp.zeros((d, ALR_PAD - 2 * GATE_RANK), F32)], axis=1).astype(BF16)
        w_alpha = jnp.zeros((ALR_PAD, 2 * DK_TOT), F32)
        w_alpha = w_alpha.at[:GATE_RANK, :DK_TOT].set(w_alpha_fwd[l])
        w_alpha = w_alpha.at[GATE_RANK:2 * GATE_RANK, DK_TOT:].set(w_alpha_bwd[l]).astype(BF16)
        b_alpha = jnp.concatenate([b_alpha_fwd[l], b_alpha_bwd[l]]).reshape(1, 2 * DK_TOT)
        w = {
            "norm_ffn1": norm_ffn1[l], "ffn1_gate": w_ffn1_gate[l].astype(BF16),
            "ffn1_up": w_ffn1_up[l].astype(BF16), "ffn1_down": w_ffn1_down[l].astype(BF16),
            "norm_mix": norm_mix[l], "w_in": w_in_packed, "w_alpha": w_alpha, "b_alpha": b_alpha,
            "gla_norm": gla_norm[l], "proj_fourier": w_proj_fourier[l].astype(BF16),
            "proj_gla": w_proj_gla[l].astype(BF16), "w_out": w_out[l].astype(BF16),
            "norm_ffn2": norm_ffn2[l], "ffn2_gate": w_ffn2_gate[l].astype(BF16),
            "ffn2_up": w_ffn2_up[l].astype(BF16), "ffn2_down": w_ffn2_down[l].astype(BF16),
        }
        last = l == depth - 1
        mod = _ada(cvecs, w_ada[l], b_ada[l]).reshape(-1, N_MOD, d)
        mod_lat = mod[:nb_lat]
        mod_ctx = mod[nb_lat:nb_lat + 1]
        xc, sf, sb = _trunk_layer(xc, mod_ctx, zero_state, zero_state, nb_ctx, t_ctx, w,
                                  pos=None, final_gain=final_norm if last else None, two_stage_fft=False)
        new_fwd.append(sf)
        new_bwd.append(sb)
        xl, _, _ = _trunk_layer(xl, mod_lat, state_gla_fwd[:, l], state_gla_bwd[:, l], nb_lat, t_lat, w,
                                pos=pos if l == 0 else None, final_gain=final_norm if last else None,
                                two_stage_fft=True)
    y_prompt = xc.reshape(nb_ctx, t_ctx, d)
    y_sample = xl.reshape(nb_lat, t_lat, d)
    return (y_prompt, y_sample, jnp.stack(new_fwd, axis=1), jnp.stack(new_bwd, axis=1))
```

```python
import functools
import math

import numpy as np
import jax
import jax.numpy as jnp
from jax import lax
from jax.experimental import pallas as pl
from jax.experimental.pallas import tpu as pltpu

F32 = jnp.float32
BF16 = jnp.bfloat16

GRID_W = 64
N_GROUPS = 4
GROUP_W = 128
D_FOURIER = N_GROUPS * GROUP_W
N_HEADS = 4
DK = 128
DV = 256
DK_TOT = N_HEADS * DK
DV_TOT = N_HEADS * DV
GATE_RANK = 16
GATE_TEMP = 16.0
RMS_EPS = 1e-6
POS_BASE = 10000.0
N_MOD = 9

TOKEN_TILE = 512
GLA_CHUNK = 64
GLA_LEVELS = 6
GLA_GROUP = 2
LANES = 128
ALR_PAD = LANES
FFT_N = 64
FFT_ROWS = 8
VMEM_LIMIT = 56 * 1024 * 1024


def _params(n_axes):
    return pltpu.CompilerParams(dimension_semantics=("arbitrary",) * n_axes,
                                vmem_limit_bytes=VMEM_LIMIT)


def _resident(shape):
    nd = len(shape)
    return pl.BlockSpec(shape, lambda *_: (0,) * nd, pipeline_mode=pl.Buffered(1))


LOG2_E = math.log2(math.e)


def _sigmoid(x):
    return 1.0 / (1.0 + jnp.exp2(x * (-LOG2_E)))


def _rms(x, gain):
    return x * lax.rsqrt(jnp.mean(x * x, axis=-1, keepdims=True) + RMS_EPS) * gain


def _modulated_rms(x, gain, scale, shift):
    return x * lax.rsqrt(jnp.mean(x * x, axis=-1, keepdims=True) + RMS_EPS) * (gain * (1.0 + scale)) + shift


def _dot(a, b):
    return jnp.dot(a, b, preferred_element_type=F32)


def _ada_kernel(c_ref, w_ref, b_ref, o_ref):
    c = c_ref[...]
    s = (c * _sigmoid(c)).astype(BF16)
    o_ref[...] = _dot(s, w_ref[...].astype(BF16)) + b_ref[...]


def _ada(cvecs, w_ada, b_ada):
    rows, d = cvecs.shape
    n = w_ada.shape[1]
    tn = d
    return pl.pallas_call(
        _ada_kernel,
        grid=(n // tn,),
        in_specs=[pl.BlockSpec((rows, d), lambda j: (0, 0)),
                  pl.BlockSpec((d, tn), lambda j: (0, j)),
                  pl.BlockSpec((1, tn), lambda j: (0, j))],
        out_specs=pl.BlockSpec((rows, tn), lambda j: (0, j)),
        out_shape=jax.ShapeDtypeStruct((rows, n), F32),
        compiler_params=_params(1),
        name="ada",
    )(cvecs, w_ada, b_ada.reshape(1, n))


def _ffn_kernel(*refs, first_mod_row, add_pos, final_norm):
    x_ref, mod_ref, gain_ref, wg_ref, wu_ref, wd_ref = refs[:6]
    rest = list(refs[6:])
    prow_ref, pcol_ref = (rest.pop(0), rest.pop(0)) if add_pos else (None, None)
    fn_ref = rest.pop(0) if final_norm else None
    o_ref = rest.pop(0)

    x = x_ref[...]
    if add_pos:
        tm, d = x.shape
        rows = tm // GRID_W
        prow = jnp.broadcast_to(prow_ref[...][:, None, :], (rows, GRID_W, d // 2))
        pcol = jnp.broadcast_to(pcol_ref[...][None, :, :], (rows, GRID_W, d // 2))
        x = x + jnp.concatenate([prow, pcol], axis=-1).reshape(tm, d)
    shift = mod_ref[first_mod_row:first_mod_row + 1, :]
    scale = mod_ref[first_mod_row + 1:first_mod_row + 2, :]
    gate = mod_ref[first_mod_row + 2:first_mod_row + 3, :]
    h = _modulated_rms(x, gain_ref[...], scale, shift).astype(BF16)
    g = _dot(h, wg_ref[...])
    u = _dot(h, wu_ref[...])
    a = (g * _sigmoid(g) * u).astype(BF16)
    y = _dot(a, wd_ref[...])
    xn = x + (0.5 * gate) * y
    if final_norm:
        xn = _rms(xn, fn_ref[...])
    o_ref[...] = xn


def _ffn(x, mod, gain, wg, wu, wd, *, tiles_per_group, first_mod_row, pos=None, final_gain=None):
    n, d = x.shape
    dff = wg.shape[1]
    tm = TOKEN_TILE
    add_pos = pos is not None
    final_norm = final_gain is not None
    in_specs = [pl.BlockSpec((tm, d), lambda i: (i, 0)),
                pl.BlockSpec((None, N_MOD, d), lambda i: (i // tiles_per_group, 0, 0)),
                _resident((1, d)), _resident((d, dff)), _resident((d, dff)), _resident((dff, d))]
    args = [x, mod, gain.reshape(1, d), wg, wu, wd]
    if add_pos:
        pos_row, pos_col = pos
        rows = tm // GRID_W
        row_tiles = pos_row.shape[0] // rows
        assert tm % GRID_W == 0 and rows % 8 == 0 and pos_col.shape[0] == GRID_W
        in_specs += [pl.BlockSpec((rows, d // 2), lambda i: (i % row_tiles, 0)), _resident(pos_col.shape)]
        args += [pos_row, pos_col]
    if final_norm:
        in_specs.append(_resident((1, d)))
        args.append(final_gain.reshape(1, d))
    return pl.pallas_call(
        functools.partial(_ffn_kernel, first_mod_row=first_mod_row, add_pos=add_pos, final_norm=final_norm),
        grid=(n // tm,),
        in_specs=in_specs,
        out_specs=pl.BlockSpec((tm, d), lambda i: (i, 0)),
        out_shape=jax.ShapeDtypeStruct((n, d), F32),
        compiler_params=_params(1),
        name="ffn",
    )(*args)


_C_F = 0
_C_Q = _C_F + D_FOURIER
_C_K = _C_Q + DK_TOT
_C_V = _C_K + DK_TOT
_C_R = _C_V + DV_TOT
_C_G = _C_R + DV_TOT


def _mixin_kernel(x_ref, mod_ref, gain_ref, w_ref, wt_ref, wal_ref, bal_ref,
                  f_ref, q_ref, k_ref, v_ref, r_ref, g_ref, laf_ref, lab_ref, *scratch, d_model, fft_layout):
    x = x_ref[...]
    shift = mod_ref[3:4, :]
    scale = mod_ref[4:5, :]
    h = _modulated_rms(x, gain_ref[...], scale, shift).astype(BF16)

    def proj(lo, hi):
        return _dot(h, w_ref[:, lo:hi])

    alr = _dot(h, wt_ref[:, 2 * d_model:2 * d_model + ALR_PAD]).astype(BF16)
    pre = _dot(alr, wal_ref[...]) + bal_ref[...]
    la = (jnp.minimum(pre, 0.0) * (LOG2_E / GATE_TEMP)
          - jnp.log2(1.0 + jnp.exp2(jnp.abs(pre) * (-LOG2_E))) * (1.0 / GATE_TEMP))
    laf_ref[...] = la[:, :DK_TOT]
    lab_ref[...] = la[:, DK_TOT:]

    f = proj(_C_F, _C_Q)
    if fft_layout:
        slab_ref, = scratch
        rows = x.shape[0] // FFT_N
        for a in range(rows):
            for g in range(D_FOURIER // LANES):
                slab_ref[g, pl.ds(a, FFT_N, stride=rows), :] = f[a * FFT_N:(a + 1) * FFT_N, g * LANES:(g + 1) * LANES]
        for g in range(D_FOURIER // LANES):
            f_ref[:, :, g * LANES:(g + 1) * LANES] = slab_ref[g].reshape(FFT_N, rows, LANES)
    else:
        f_ref[...] = f
    q_ref[...] = proj(_C_Q, _C_K) * (DK ** -0.5)
    k_ref[...] = proj(_C_K, _C_V)
    v_ref[...] = proj(_C_V, _C_R).astype(BF16)
    r_ref[...] = proj(_C_R, _C_G)
    g_ref[...] = _dot(h, wt_ref[:, :2 * d_model])


def _mixin(x, mod, gain, w_main, w_tail, w_alpha, b_alpha, *, tiles_per_group, fft_layout):
    n, d = x.shape
    tm = TOKEN_TILE

    def tok(width):
        return pl.BlockSpec((tm, width), lambda i: (i, 0))

    outs = [(D_FOURIER, F32), (DK_TOT, F32), (DK_TOT, F32), (DV_TOT, BF16), (DV_TOT, F32),
            (2 * d, F32), (DK_TOT, F32), (DK_TOT, F32)]
    out_specs = [tok(w) for w, _ in outs]
    out_shape = [jax.ShapeDtypeStruct((n, w), dt) for w, dt in outs]
    scratch = []
    if fft_layout:
        rows = tm // FFT_N
        tiles = FFT_N // rows
        assert tm % FFT_N == 0 and rows % 8 == 0 and n % (FFT_N * FFT_N) == 0
        out_specs[0] = pl.BlockSpec((None, FFT_N, rows, D_FOURIER), lambda i: (i // tiles, 0, i % tiles, 0))
        out_shape[0] = jax.ShapeDtypeStruct((n // (FFT_N * FFT_N), FFT_N, FFT_N, D_FOURIER), F32)
        scratch = [pltpu.VMEM((D_FOURIER // LANES, tm, LANES), F32)]
    return pl.pallas_call(
        functools.partial(_mixin_kernel, d_model=d, fft_layout=fft_layout),
        grid=(n // tm,),
        in_specs=[tok(d),
                  pl.BlockSpec((None, N_MOD, d), lambda i: (i // tiles_per_group, 0, 0)),
                  _resident((1, d)), _resident(w_main.shape), _resident(w_tail.shape),
                  _resident((ALR_PAD, 2 * DK_TOT)), _resident((1, 2 * DK_TOT))],
        out_specs=out_specs,
        out_shape=out_shape,
        scratch_shapes=scratch,
        compiler_params=_params(1),
        name="mixin",
    )(x, mod, gain.reshape(1, d), w_main, w_tail, w_alpha, b_alpha)


def _dft_cos_sin(n):
    k = np.arange(n)
    ang = 2.0 * np.pi * ((k[:, None] * k[None, :]) % n) / n
    return np.cos(ang), np.sin(ang)


def _split_hi_lo(x):
    hi = x.astype(BF16)
    return hi, (x - hi.astype(F32)).astype(BF16)


def _table_pieces(table, axis):
    hi, lo = _split_hi_lo(jnp.asarray(table.astype(np.float32)))
    return jnp.concatenate([hi, hi], axis=axis), lo


def _dot_data_table(x, cat_ref, lo_ref):
    hi, lo = _split_hi_lo(x)
    return _dot(jnp.concatenate([hi, lo], axis=1), cat_ref[...]) + _dot(hi, lo_ref[...])


def _dot_table_data(cat, lo_t, x):
    hi, lo = _split_hi_lo(x)
    return _dot(cat, jnp.concatenate([hi, lo], axis=0)) + _dot(lo_t, hi)


def _width_dft_table(t_len):
    c, s = _dft_cos_sin(GROUP_W)
    return np.concatenate([c, -s], axis=1) / math.sqrt(t_len * GROUP_W)


def _width_dft(x, cwc_ref, cwl_ref):
    zr, zi = [], []
    for g in range(N_GROUPS):
        z = _dot_data_table(x[:, g * GROUP_W:(g + 1) * GROUP_W], cwc_ref, cwl_ref)
        zr.append(z[:, :GROUP_W])
        zi.append(z[:, GROUP_W:])
    return jnp.concatenate(zr, axis=1), jnp.concatenate(zi, axis=1)


def _fourier_dense_kernel(x_ref, cwc_ref, cwl_ref, ftc_ref, ftl_ref, o_ref):
    zr, zi = _width_dft(x_ref[...], cwc_ref, cwl_ref)
    zz = jnp.concatenate([zr, zi], axis=0)
    o_ref[...] = _dot_table_data(ftc_ref[...], ftl_ref[...], zz)


def _fourier_dense(f, nb, t_len):
    c, s = _dft_cos_sin(t_len)
    consts = [*_table_pieces(_width_dft_table(t_len), 0), *_table_pieces(np.concatenate([c, s], axis=1), 1)]
    return pl.pallas_call(
        _fourier_dense_kernel,
        grid=(nb,),
        in_specs=[pl.BlockSpec((t_len, D_FOURIER), lambda b: (b, 0))] + [_resident(a.shape) for a in consts],
        out_specs=pl.BlockSpec((t_len, D_FOURIER), lambda b: (b, 0)),
        out_shape=jax.ShapeDtypeStruct(f.shape, F32),
        compiler_params=_params(1),
        name="fourier_dense",
    )(f, *consts)


def _fourier_stage1_kernel(x_ref, cwc_ref, cwl_ref, mc_ref, ml_ref, ar_ref, ai_ref, sr_ref, si_ref):
    zs = [_width_dft(x_ref[j], cwc_ref, cwl_ref) for j in range(FFT_ROWS)]
    for j in range(FFT_ROWS):
        a = _dot_table_data(mc_ref[j], ml_ref[j], jnp.concatenate(zs[j], axis=0))
        for g in range(D_FOURIER // LANES):
            lanes = slice(g * LANES, (g + 1) * LANES)
            sr_ref[g, pl.ds(j, FFT_N, stride=FFT_ROWS), :] = a[:FFT_N, lanes]
            si_ref[g, pl.ds(j, FFT_N, stride=FFT_ROWS), :] = a[FFT_N:, lanes]
    for g in range(D_FOURIER // LANES):
        lanes = slice(g * LANES, (g + 1) * LANES)
        ar_ref[:, :, lanes] = sr_ref[g].reshape(FFT_N, FFT_ROWS, LANES)
        ai_ref[:, :, lanes] = si_ref[g].reshape(FFT_N, FFT_ROWS, LANES)


def _fourier_stage2_kernel(ar_ref, ai_ref, f2c_ref, f2l_ref, o_ref):
    for j in range(FFT_ROWS):
        aa = jnp.concatenate([ar_ref[j], ai_ref[j]], axis=0)
        o_ref[j] = _dot_table_data(f2c_ref[...], f2l_ref[...], aa)


def _fourier_two_stage(x, nb):
    n = FFT_N
    t_len = n * n
    s1 = np.arange(n)[:, None]
    t1 = np.arange(n)[None, :]
    mats = []
    for t2 in range(n):
        ang = 2.0 * np.pi * ((s1 * t1 * n + s1 * t2) % t_len) / t_len
        c, s = np.cos(ang), np.sin(ang)
        mats.append(np.block([[c, s], [-s, c]]))
    m_cat, m_lo = _table_pieces(np.stack(mats), 2)
    c2, s2 = _dft_cos_sin(n)
    f2 = _table_pieces(np.concatenate([c2, s2], axis=1), 1)
    cw = _table_pieces(_width_dft_table(t_len), 0)

    rows_blk = pl.BlockSpec((None, FFT_ROWS, n, D_FOURIER), lambda b, j: (b, j, 0, 0))
    cols_blk = pl.BlockSpec((None, n, FFT_ROWS, D_FOURIER), lambda b, j: (b, 0, j, 0))
    shape4 = jax.ShapeDtypeStruct((nb, n, n, D_FOURIER), F32)
    slabs = pltpu.VMEM((D_FOURIER // LANES, n * FFT_ROWS, LANES), F32)
    ar, ai = pl.pallas_call(
        _fourier_stage1_kernel,
        grid=(nb, n // FFT_ROWS),
        in_specs=[rows_blk, _resident(cw[0].shape), _resident(cw[1].shape),
                  pl.BlockSpec((FFT_ROWS,) + m_cat.shape[1:], lambda b, j: (j, 0, 0)),
                  pl.BlockSpec((FFT_ROWS,) + m_lo.shape[1:], lambda b, j: (j, 0, 0))],
        out_specs=[cols_blk, cols_blk],
        out_shape=[shape4, shape4],
        scratch_shapes=[slabs, slabs],
        compiler_params=_params(2),
        name="fourier_stage1",
    )(x, *cw, m_cat, m_lo)
    return pl.pallas_call(
        _fourier_stage2_kernel,
        grid=(nb, n // FFT_ROWS),
        in_specs=[rows_blk, rows_blk, _resident(f2[0].shape), _resident(f2[1].shape)],
        out_specs=rows_blk,
        out_shape=shape4,
        compiler_params=_params(2),
        name="fourier_stage2",
    )(ar, ai, *f2)


def _gla_masks(chunk):
    idx = np.arange(chunk)
    i = idx[:, None]
    j = idx[None, :]
    masks = [i == j]
    for l in range(GLA_LEVELS):
        same = (i >> (l + 1)) == (j >> (l + 1))
        masks.append(same & (((i >> l) & 1) == 1) & (((j >> l) & 1) == 0))
    mask_f = np.stack(masks).astype(np.float32)
    return mask_f, np.transpose(mask_f, (0, 2, 1)).copy()


_NT = (((1,), (1,)), ((), ()))
_TN = (((0,), (0,)), ((), ()))
SUBLANES = 8


def _block_sums(la, forward):
    c = la.shape[0]
    nv = c // SUBLANES
    x = la.reshape(nv, SUBLANES, DK)
    sub = lax.broadcasted_iota(jnp.int32, (1, SUBLANES, DK), 1)
    near = x
    tot = x
    out = {}
    w = 1
    while w < SUBLANES:
        right = (sub & w) != 0
        partner = jnp.where(right, pltpu.roll(tot, w, axis=1), pltpu.roll(tot, SUBLANES - w, axis=1))
        near = near + jnp.where(right if forward else jnp.logical_not(right), partner, 0.0)
        tot = tot + partner
        w *= 2
        out[w] = (near, tot - near)
    blk = 1
    while w < c:
        takes = (lambda v: v & blk) if forward else (lambda v: not v & blk)
        near = jnp.stack([near[v] + tot[v ^ blk] if takes(v) else near[v] for v in range(nv)])
        tot = jnp.stack([tot[v] + tot[v ^ blk] for v in range(nv)])
        w *= 2
        blk *= 2
        out[w] = (near, tot - near)
    return {w: (a.reshape(c, DK), b.reshape(c, DK)) for w, (a, b) in out.items()}


def _gla_prepare(q_ref, k_ref, laf_ref, lab_ref, qs_ref, ks_ref, tot_ref, rows_f, rows_b):
    c = GLA_CHUNK
    for g in range(len(rows_f)):
        for fwd in (True, False):
            chain = 2 * g + (0 if fwd else 1)
            rows = pl.ds(rows_f[g] if fwd else rows_b[g], c)
            q = q_ref[rows, :]
            k = k_ref[rows, :]
            la = (laf_ref if fwd else lab_ref)[rows, :]
            sums = _block_sums(la, fwd)
            qs_ref[chain, 0:c, :] = q.astype(BF16)
            ks_ref[chain, 0:c, :] = k.astype(BF16)
            for l in range(GLA_LEVELS + 1):
                if l == 0:
                    d, e = jnp.exp2(la), None
                else:
                    near, far = sums[1 << l]
                    d, e = jnp.exp2(near), jnp.exp2(far)
                qs_ref[chain, (l + 1) * c:(l + 2) * c, :] = (q * d).astype(BF16)
                if e is not None:
                    ks_ref[chain, l * c:(l + 1) * c, :] = (k * e).astype(BF16)
            tot_ref[chain] = jnp.broadcast_to(d[c - 1:c, :] if fwd else d[0:1, :], (SUBLANES, DK))


def _gla_apply(v_ref, maskf_ref, maskb_ref, qs_ref, ks_ref, tot_ref, stf_ref, stb_ref, o_ref,
               rows_f, rows_b, *, accumulate):
    c = GLA_CHUNK
    levels = GLA_LEVELS
    n = len(rows_f)
    scores = []
    for chain in range(2 * n):
        m_ref = maskf_ref if chain % 2 == 0 else maskb_ref
        s = None
        for l in range(levels + 1):
            kl = max(l - 1, 0)
            p = lax.dot_general(qs_ref[chain, l * c:(l + 1) * c, :], ks_ref[chain, kl * c:(kl + 1) * c, :], _NT,
                                preferred_element_type=F32) * m_ref[l]
            s = p if s is None else s + p
        scores.append(s.astype(BF16))
    for chain in range(2 * n):
        fwd = chain % 2 == 0
        g = chain // 2
        rows = pl.ds(rows_f[g] if fwd else rows_b[g], c)
        st_ref = stf_ref if fwd else stb_ref
        v = v_ref[rows, :]
        st = st_ref[...]
        q_top = qs_ref[chain, (levels + 1) * c:(levels + 2) * c, :]
        k_top = ks_ref[chain, levels * c:(levels + 1) * c, :]
        o = _dot(scores[chain], v) + _dot(q_top, st.astype(BF16))
        tot = jnp.broadcast_to(tot_ref[chain][0:1, :], (DK, DK)).T
        st_ref[...] = (st * jnp.concatenate([tot] * (DV // DK), axis=1)
                       + lax.dot_general(k_top, v, _TN, preferred_element_type=F32))
        if accumulate:
            o_ref[rows, :] += o
        else:
            o_ref[rows, :] = o


def _gla_kernel(q_ref, k_ref, v_ref, laf_ref, lab_ref, s0f_ref, s0b_ref, maskf_ref, maskb_ref,
                o_ref, sf_ref, sb_ref, qs0_ref, ks0_ref, tot0_ref, qs1_ref, ks1_ref, tot1_ref,
                *, n_chunks, group):
    sf_ref[...] = s0f_ref[...]
    sb_ref[...] = s0b_ref[...]
    steps = n_chunks // group
    slots = ((qs0_ref, ks0_ref, tot0_ref), (qs1_ref, ks1_ref, tot1_ref))

    def rows_of(i):
        rows_f = [pl.multiple_of((i * group + g) * GLA_CHUNK, GLA_CHUNK) for g in range(group)]
        rows_b = [pl.multiple_of((n_chunks - 1 - i * group - g) * GLA_CHUNK, GLA_CHUNK) for g in range(group)]
        return rows_f, rows_b

    def prepare(i, slot):
        _gla_prepare(q_ref, k_ref, laf_ref, lab_ref, *slots[slot], *rows_of(i))

    def apply(i, slot, accumulate):
        _gla_apply(v_ref, maskf_ref, maskb_ref, *slots[slot], sf_ref, sb_ref, o_ref, *rows_of(i),
                   accumulate=accumulate)

    def pair(j, accumulate, last):
        prepare(2 * j + 1, 1)
        apply(2 * j, 0, accumulate[0])
        if not last:
            prepare(2 * j + 2, 0)
        apply(2 * j + 1, 1, accumulate[1])

    n_pairs = steps // 2
    prepare(0, 0)
    if n_pairs == 1:
        pair(0, (False, True), True)
    else:
        def body(accumulate):
            def f(j, carry):
                pair(j, (accumulate, accumulate), False)
                return carry
            return f
        lax.fori_loop(0, n_pairs // 2, body(False), 0)
        lax.fori_loop(n_pairs // 2, n_pairs - 1, body(True), 0)
        pair(n_pairs - 1, (True, True), True)


def _gla(q, k, v, laf, lab, s0f, s0b, nb, t_len):
    group = GLA_GROUP
    n_chunks = t_len // GLA_CHUNK
    steps = n_chunks // group
    assert t_len % GLA_CHUNK == 0 and n_chunks % group == 0 and (steps == 2 or steps % 4 == 0)
    consts = [jnp.asarray(m) for m in _gla_masks(GLA_CHUNK)]

    def seq(width):
        return pl.BlockSpec((t_len, width), lambda b, h: (b, h))

    state = pl.BlockSpec((None, None, DK, DV), lambda b, h: (b, h, 0, 0))
    n = nb * t_len
    slot = [pltpu.VMEM((2 * group, (GLA_LEVELS + 2) * GLA_CHUNK, DK), BF16),
            pltpu.VMEM((2 * group, (GLA_LEVELS + 1) * GLA_CHUNK, DK), BF16),
            pltpu.VMEM((2 * group, 8, DK), F32)]
    return pl.pallas_call(
        functools.partial(_gla_kernel, n_chunks=n_chunks, group=group),
        grid=(nb, N_HEADS),
        in_specs=[seq(DK), seq(DK), seq(DV), seq(DK), seq(DK), state, state]
                 + [_resident(a.shape) for a in consts],
        out_specs=[seq(DV), state, state],
        out_shape=[jax.ShapeDtypeStruct((n, DV_TOT), F32),
                   jax.ShapeDtypeStruct((nb, N_HEADS, DK, DV), F32),
                   jax.ShapeDtypeStruct((nb, N_HEADS, DK, DV), F32)],
        scratch_shapes=slot + slot,
        compiler_params=_params(2),
        name="gla",
    )(q, k, v, laf, lab, s0f, s0b, *consts)


def _mixout_kernel(x_ref, mod_ref, m_ref, o_ref, r_ref, g_ref, gn_ref, wpf_ref, wpg_ref, wout_ref, y_ref,
                   *scratch, d_model, fft_layout):
    if fft_layout:
        slab_ref, = scratch
        rows = m_ref.shape[1]
        for g in range(D_FOURIER // LANES):
            slab_ref[g] = m_ref[:, :, g * LANES:(g + 1) * LANES].reshape(FFT_N * rows, LANES)
        m = jnp.concatenate(
            [jnp.concatenate([slab_ref[g, pl.ds(s, FFT_N, stride=rows), :] for g in range(D_FOURIER // LANES)],
                             axis=1) for s in range(rows)], axis=0)
    else:
        m = m_ref[...]
    branch_a = _dot(m.astype(BF16), wpf_ref[...])
    o = o_ref[...]
    parts = []
    for h in range(N_HEADS):
        oh = o[:, h * DV:(h + 1) * DV]
        parts.append(oh * lax.rsqrt(jnp.mean(oh * oh, axis=-1, keepdims=True) + RMS_EPS))
    r = r_ref[...]
    on = jnp.concatenate(parts, axis=1) * (gn_ref[...] * r * _sigmoid(r))
    branch_b = _dot(on.astype(BF16), wpg_ref[...])
    g = _sigmoid(g_ref[...])
    merged = g[:, :d_model] * branch_a + g[:, d_model:] * branch_b
    y = _dot(merged.astype(BF16), wout_ref[...])
    y_ref[...] = x_ref[...] + mod_ref[5:6, :] * y


def _mixout(x, mod, m, o, r, g, gla_norm, wpf, wpg, wout, *, tiles_per_group, fft_layout):
    n, d = x.shape
    tm = TOKEN_TILE

    def tok(width):
        return pl.BlockSpec((tm, width), lambda i: (i, 0))

    m_spec = tok(D_FOURIER)
    scratch = []
    if fft_layout:
        rows = tm // FFT_N
        tiles = FFT_N // rows
        m_spec = pl.BlockSpec((None, FFT_N, rows, D_FOURIER), lambda i: (i // tiles, 0, i % tiles, 0))
        scratch = [pltpu.VMEM((D_FOURIER // LANES, tm, LANES), F32)]
    return pl.pallas_call(
        functools.partial(_mixout_kernel, d_model=d, fft_layout=fft_layout),
        grid=(n // tm,),
        in_specs=[tok(d),
                  pl.BlockSpec((None, N_MOD, d), lambda i: (i // tiles_per_group, 0, 0)),
                  m_spec, tok(DV_TOT), tok(DV_TOT), tok(2 * d),
                  _resident((1, DV_TOT)), _resident(wpf.shape), _resident(wpg.shape), _resident(wout.shape)],
        out_specs=tok(d),
        out_shape=jax.ShapeDtypeStruct((n, d), F32),
        scratch_shapes=scratch,
        compiler_params=_params(1),
        name="mixout",
    )(x, mod, m, o, r, g, gla_norm.reshape(1, DV_TOT), wpf, wpg, wout)


def _grid_pos_tables(n_tokens, d_model):
    n_freq = d_model // 4
    omega = POS_BASE ** (-jnp.arange(n_freq, dtype=F32) / n_freq)
    ra = jnp.arange(n_tokens // GRID_W, dtype=F32)[:, None] * omega
    ca = jnp.arange(GRID_W, dtype=F32)[:, None] * omega
    return (jnp.concatenate([jnp.sin(ra), jnp.cos(ra)], axis=-1),
            jnp.concatenate([jnp.sin(ca), jnp.cos(ca)], axis=-1))


def _trunk_layer(x, mod, s0f, s0b, nb, t_len, w, *, pos, final_gain, two_stage_fft):
    tiles = max(t_len // TOKEN_TILE, 1) if mod.shape[0] > 1 else x.shape[0] // TOKEN_TILE
    x = _ffn(x, mod, w["norm_ffn1"], w["ffn1_gate"], w["ffn1_up"], w["ffn1_down"],
             tiles_per_group=tiles, first_mod_row=0, pos=pos)
    f, q, k, v, r, g, laf, lab = _mixin(x, mod, w["norm_mix"], w["w_main"], w["w_tail"], w["w_alpha"],
                                        w["b_alpha"],
                                        tiles_per_group=tiles, fft_layout=two_stage_fft)
    m = _fourier_two_stage(f, nb) if two_stage_fft else _fourier_dense(f, nb, t_len)
    o, sf, sb = _gla(q, k, v, laf, lab, s0f, s0b, nb, t_len)
    x = _mixout(x, mod, m, o, r, g, w["gla_norm"], w["proj_fourier"], w["proj_gla"], w["w_out"],
                tiles_per_group=tiles, fft_layout=two_stage_fft)
    x = _ffn(x, mod, w["norm_ffn2"], w["ffn2_gate"], w["ffn2_up"], w["ffn2_down"],
             tiles_per_group=tiles, first_mod_row=6, final_gain=final_gain)
    return x, sf, sb


def kernel(x_prompt, x_sample, state_gla_fwd, state_gla_bwd, c, c_ctx, w_ada, b_ada, norm_ffn1, w_ffn1_gate, w_ffn1_up, w_ffn1_down, norm_mix, w_in, w_alpha_fwd, b_alpha_fwd, w_alpha_bwd, b_alpha_bwd, gla_norm, w_proj_fourier, w_proj_gla, w_out, norm_ffn2, w_ffn2_gate, w_ffn2_up, w_ffn2_down, final_norm):
    nb_ctx, t_ctx, d = x_prompt.shape
    nb_lat, t_lat, _ = x_sample.shape
    depth = w_ada.shape[0]
    assert t_lat == FFT_N * FFT_N and (nb_ctx * t_ctx) % TOKEN_TILE == 0 and t_lat % TOKEN_TILE == 0

    xc = x_prompt.reshape(nb_ctx * t_ctx, d)
    xl = x_sample.reshape(nb_lat * t_lat, d)
    pos = _grid_pos_tables(t_lat, d)
    zero_state = jnp.zeros((nb_ctx, N_HEADS, DK, DV), F32)
    pad_rows = (-(nb_lat + 1)) % 8
    cvecs = jnp.concatenate([c, c_ctx[None, :], jnp.zeros((pad_rows, d), F32)], axis=0)

    new_fwd, new_bwd = [], []
    for l in range(depth):
        cut = _C_G
        w_in_l = w_in[l].astype(BF16)
        w_main = w_in_l[:, :cut]
        w_tail = jnp.concatenate(
            [w_in_l[:, cut + 2 * GATE_RANK:], w_in_l[:, cut:cut + 2 * GATE_RANK],
             jnp.zeros((d, ALR_PAD - 2 * GATE_RANK), BF16)], axis=1)
        w_alpha = jnp.zeros((ALR_PAD, 2 * DK_TOT), F32)
        w_alpha = w_alpha.at[:GATE_RANK, :DK_TOT].set(w_alpha_fwd[l])
        w_alpha = w_alpha.at[GATE_RANK:2 * GATE_RANK, DK_TOT:].set(w_alpha_bwd[l]).astype(BF16)
        b_alpha = jnp.concatenate([b_alpha_fwd[l], b_alpha_bwd[l]]).reshape(1, 2 * DK_TOT)
        w = {
            "norm_ffn1": norm_ffn1[l], "ffn1_gate": w_ffn1_gate[l].astype(BF16),
            "ffn1_up": w_ffn1_up[l].astype(BF16), "ffn1_down": w_ffn1_down[l].astype(BF16),
            "norm_mix": norm_mix[l], "w_main": w_main, "w_tail": w_tail, "w_alpha": w_alpha, "b_alpha": b_alpha,
            "gla_norm": gla_norm[l], "proj_fourier": w_proj_fourier[l].astype(BF16),
            "proj_gla": w_proj_gla[l].astype(BF16), "w_out": w_out[l].astype(BF16),
            "norm_ffn2": norm_ffn2[l], "ffn2_gate": w_ffn2_gate[l].astype(BF16),
            "ffn2_up": w_ffn2_up[l].astype(BF16), "ffn2_down": w_ffn2_down[l].astype(BF16),
        }
        last = l == depth - 1
        mod = _ada(cvecs, w_ada[l], b_ada[l]).reshape(-1, N_MOD, d)
        mod_lat = mod[:nb_lat]
        mod_ctx = mod[nb_lat:nb_lat + 1]
        xc, sf, sb = _trunk_layer(xc, mod_ctx, zero_state, zero_state, nb_ctx, t_ctx, w,
                                  pos=None, final_gain=final_norm if last else None, two_stage_fft=False)
        new_fwd.append(sf)
        new_bwd.append(sb)
        xl, _, _ = _trunk_layer(xl, mod_lat, state_gla_fwd[:, l], state_gla_bwd[:, l], nb_lat, t_lat, w,
                                pos=pos if l == 0 else None, final_gain=final_norm if last else None,
                                two_stage_fft=True)
    y_prompt = xc.reshape(nb_ctx, t_ctx, d)
    y_sample = xl.reshape(nb_lat, t_lat, d)
    return (y_prompt, y_sample, jnp.stack(new_fwd, axis=1), jnp.stack(new_bwd, axis=1))
```

```python
import functools
import math

import numpy as np
import jax
import jax.numpy as jnp
from jax import lax
from jax.experimental import pallas as pl
from jax.experimental.pallas import tpu as pltpu

F32 = jnp.float32
BF16 = jnp.bfloat16

GRID_W = 64
N_GROUPS = 4
GROUP_W = 128
D_FOURIER = N_GROUPS * GROUP_W
N_HEADS = 4
DK = 128
DV = 256
DK_TOT = N_HEADS * DK
DV_TOT = N_HEADS * DV
GATE_RANK = 16
GATE_TEMP = 16.0
RMS_EPS = 1e-6
POS_BASE = 10000.0
N_MOD = 9

TOKEN_TILE = 512
GLA_CHUNK = 64
GLA_LEVELS = 6
GLA_GROUP = 2
LANES = 128
ALR_PAD = LANES
FFT_N = 64
FFT_ROWS = 8
VMEM_LIMIT = 56 * 1024 * 1024


def _params(n_axes):
    return pltpu.CompilerParams(dimension_semantics=("arbitrary",) * n_axes,
                                vmem_limit_bytes=VMEM_LIMIT)


def _resident(shape):
    nd = len(shape)
    return pl.BlockSpec(shape, lambda *_: (0,) * nd, pipeline_mode=pl.Buffered(1))


LOG2_E = math.log2(math.e)


def _sigmoid(x):
    return 1.0 / (1.0 + jnp.exp2(x * (-LOG2_E)))


def _rms(x, gain):
    return x * lax.rsqrt(jnp.mean(x * x, axis=-1, keepdims=True) + RMS_EPS) * gain


def _modulated_rms(x, gain, scale, shift):
    return x * lax.rsqrt(jnp.mean(x * x, axis=-1, keepdims=True) + RMS_EPS) * (gain * (1.0 + scale)) + shift


def _dot(a, b):
    return jnp.dot(a, b, preferred_element_type=F32)


def _ada_kernel(c_ref, w_ref, b_ref, o_ref):
    c = c_ref[...]
    s = (c * _sigmoid(c)).astype(BF16)
    o_ref[...] = _dot(s, w_ref[...].astype(BF16)) + b_ref[...]


def _ada(cvecs, w_ada, b_ada):
    rows, d = cvecs.shape
    n = w_ada.shape[1]
    tn = d
    return pl.pallas_call(
        _ada_kernel,
        grid=(n // tn,),
        in_specs=[pl.BlockSpec((rows, d), lambda j: (0, 0)),
                  pl.BlockSpec((d, tn), lambda j: (0, j)),
                  pl.BlockSpec((1, tn), lambda j: (0, j))],
        out_specs=pl.BlockSpec((rows, tn), lambda j: (0, j)),
        out_shape=jax.ShapeDtypeStruct((rows, n), F32),
        compiler_params=_params(1),
        name="ada",
    )(cvecs, w_ada, b_ada.reshape(1, n))


def _ffn_kernel(*refs, first_mod_row, add_pos, final_norm):
    x_ref, mod_ref, gain_ref, wg_ref, wu_ref, wd_ref = refs[:6]
    rest = list(refs[6:])
    prow_ref, pcol_ref = (rest.pop(0), rest.pop(0)) if add_pos else (None, None)
    fn_ref = rest.pop(0) if final_norm else None
    o_ref = rest.pop(0)

    x = x_ref[...]
    if add_pos:
        tm, d = x.shape
        rows = tm // GRID_W
        prow = jnp.broadcast_to(prow_ref[...][:, None, :], (rows, GRID_W, d // 2))
        pcol = jnp.broadcast_to(pcol_ref[...][None, :, :], (rows, GRID_W, d // 2))
        x = x + jnp.concatenate([prow, pcol], axis=-1).reshape(tm, d)
    shift = mod_ref[first_mod_row:first_mod_row + 1, :]
    scale = mod_ref[first_mod_row + 1:first_mod_row + 2, :]
    gate = mod_ref[first_mod_row + 2:first_mod_row + 3, :]
    h = _modulated_rms(x, gain_ref[...], scale, shift).astype(BF16)
    g = _dot(h, wg_ref[...])
    u = _dot(h, wu_ref[...])
    a = (g * _sigmoid(g) * u).astype(BF16)
    y = _dot(a, wd_ref[...])
    xn = x + (0.5 * gate) * y
    if final_norm:
        xn = _rms(xn, fn_ref[...])
    o_ref[...] = xn


def _ffn(x, mod, gain, wg, wu, wd, *, tiles_per_group, first_mod_row, pos=None, final_gain=None):
    n, d = x.shape
    dff = wg.shape[1]
    tm = TOKEN_TILE
    add_pos = pos is not None
    final_norm = final_gain is not None
    in_specs = [pl.BlockSpec((tm, d), lambda i: (i, 0)),
                pl.BlockSpec((None, N_MOD, d), lambda i: (i // tiles_per_group, 0, 0)),
                _resident((1, d)), _resident((d, dff)), _resident((d, dff)), _resident((dff, d))]
    args = [x, mod, gain.reshape(1, d), wg, wu, wd]
    if add_pos:
        pos_row, pos_col = pos
        rows = tm // GRID_W
        row_tiles = pos_row.shape[0] // rows
        assert tm % GRID_W == 0 and rows % 8 == 0 and pos_col.shape[0] == GRID_W
        in_specs += [pl.BlockSpec((rows, d // 2), lambda i: (i % row_tiles, 0)), _resident(pos_col.shape)]
        args += [pos_row, pos_col]
    if final_norm:
        in_specs.append(_resident((1, d)))
        args.append(final_gain.reshape(1, d))
    return pl.pallas_call(
        functools.partial(_ffn_kernel, first_mod_row=first_mod_row, add_pos=add_pos, final_norm=final_norm),
        grid=(n // tm,),
        in_specs=in_specs,
        out_specs=pl.BlockSpec((tm, d), lambda i: (i, 0)),
        out_shape=jax.ShapeDtypeStruct((n, d), F32),
        compiler_params=_params(1),
        name="ffn",
    )(*args)


_C_F = 0
_C_Q = _C_F + D_FOURIER
_C_K = _C_Q + DK_TOT
_C_V = _C_K + DK_TOT
_C_R = _C_V + DV_TOT
_C_G = _C_R + DV_TOT


def _mixin_kernel(x_ref, mod_ref, gain_ref, w_ref, wt_ref, wal_ref, bal_ref,
                  f_ref, q_ref, k_ref, v_ref, r_ref, g_ref, laf_ref, lab_ref, *scratch, d_model, fft_layout):
    x = x_ref[...]
    shift = mod_ref[3:4, :]
    scale = mod_ref[4:5, :]
    h = _modulated_rms(x, gain_ref[...], scale, shift).astype(BF16)

    def proj(lo, hi):
        return _dot(h, w_ref[:, lo:hi])

    alr = _dot(h, wt_ref[:, 2 * d_model:2 * d_model + ALR_PAD]).astype(BF16)
    pre = _dot(alr, wal_ref[...]) + bal_ref[...]
    la = (jnp.minimum(pre, 0.0) * (LOG2_E / GATE_TEMP)
          - jnp.log2(1.0 + jnp.exp2(jnp.abs(pre) * (-LOG2_E))) * (1.0 / GATE_TEMP))
    laf_ref[...] = la[:, :DK_TOT]
    lab_ref[...] = la[:, DK_TOT:]

    f = proj(_C_F, _C_Q)
    if fft_layout:
        slab_ref, = scratch
        rows = x.shape[0] // FFT_N
        for a in range(rows):
            for g in range(D_FOURIER // LANES):
                slab_ref[g, pl.ds(a, FFT_N, stride=rows), :] = f[a * FFT_N:(a + 1) * FFT_N, g * LANES:(g + 1) * LANES]
        for g in range(D_FOURIER // LANES):
            f_ref[:, :, g * LANES:(g + 1) * LANES] = slab_ref[g].reshape(FFT_N, rows, LANES)
    else:
        f_ref[...] = f
    q_ref[...] = proj(_C_Q, _C_K) * (DK ** -0.5)
    k_ref[...] = proj(_C_K, _C_V)
    v_ref[...] = proj(_C_V, _C_R).astype(BF16)
    r_ref[...] = proj(_C_R, _C_G).astype(BF16)
    g_ref[...] = _dot(h, wt_ref[:, :2 * d_model]).astype(BF16)


def _mixin(x, mod, gain, w_main, w_tail, w_alpha, b_alpha, *, tiles_per_group, fft_layout):
    n, d = x.shape
    tm = TOKEN_TILE

    def tok(width):
        return pl.BlockSpec((tm, width), lambda i: (i, 0))

    outs = [(D_FOURIER, F32), (DK_TOT, F32), (DK_TOT, F32), (DV_TOT, BF16), (DV_TOT, BF16),
            (2 * d, BF16), (DK_TOT, F32), (DK_TOT, F32)]
    out_specs = [tok(w) for w, _ in outs]
    out_shape = [jax.ShapeDtypeStruct((n, w), dt) for w, dt in outs]
    scratch = []
    if fft_layout:
        rows = tm // FFT_N
        tiles = FFT_N // rows
        assert tm % FFT_N == 0 and rows % 8 == 0 and n % (FFT_N * FFT_N) == 0
        out_specs[0] = pl.BlockSpec((None, FFT_N, rows, D_FOURIER), lambda i: (i // tiles, 0, i % tiles, 0))
        out_shape[0] = jax.ShapeDtypeStruct((n // (FFT_N * FFT_N), FFT_N, FFT_N, D_FOURIER), F32)
        scratch = [pltpu.VMEM((D_FOURIER // LANES, tm, LANES), F32)]
    return pl.pallas_call(
        functools.partial(_mixin_kernel, d_model=d, fft_layout=fft_layout),
        grid=(n // tm,),
        in_specs=[tok(d),
                  pl.BlockSpec((None, N_MOD, d), lambda i: (i // tiles_per_group, 0, 0)),
                  _resident((1, d)), _resident(w_main.shape), _resident(w_tail.shape),
                  _resident((ALR_PAD, 2 * DK_TOT)), _resident((1, 2 * DK_TOT))],
        out_specs=out_specs,
        out_shape=out_shape,
        scratch_shapes=scratch,
        compiler_params=_params(1),
        name="mixin",
    )(x, mod, gain.reshape(1, d), w_main, w_tail, w_alpha, b_alpha)


def _dft_cos_sin(n):
    k = np.arange(n)
    ang = 2.0 * np.pi * ((k[:, None] * k[None, :]) % n) / n
    return np.cos(ang), np.sin(ang)


def _split_hi_lo(x):
    hi = x.astype(BF16)
    return hi, (x - hi.astype(F32)).astype(BF16)


def _table_pieces(table, axis):
    hi, lo = _split_hi_lo(jnp.asarray(table.astype(np.float32)))
    return jnp.concatenate([hi, hi], axis=axis), lo


def _dot_data_table(x, cat_ref, lo_ref):
    hi, lo = _split_hi_lo(x)
    return _dot(jnp.concatenate([hi, lo], axis=1), cat_ref[...]) + _dot(hi, lo_ref[...])


def _dot_table_data(cat, lo_t, x):
    hi, lo = _split_hi_lo(x)
    return _dot(cat, jnp.concatenate([hi, lo], axis=0)) + _dot(lo_t, hi)


def _width_dft_table(t_len):
    c, s = _dft_cos_sin(GROUP_W)
    return np.concatenate([c, -s], axis=1) / math.sqrt(t_len * GROUP_W)


def _width_dft(x, cwc_ref, cwl_ref):
    zr, zi = [], []
    for g in range(N_GROUPS):
        z = _dot_data_table(x[:, g * GROUP_W:(g + 1) * GROUP_W], cwc_ref, cwl_ref)
        zr.append(z[:, :GROUP_W])
        zi.append(z[:, GROUP_W:])
    return jnp.concatenate(zr, axis=1), jnp.concatenate(zi, axis=1)


def _fourier_dense_kernel(x_ref, cwc_ref, cwl_ref, ftc_ref, ftl_ref, o_ref):
    zr, zi = _width_dft(x_ref[...], cwc_ref, cwl_ref)
    zz = jnp.concatenate([zr, zi], axis=0)
    o_ref[...] = _dot_table_data(ftc_ref[...], ftl_ref[...], zz)


def _fourier_dense(f, nb, t_len):
    c, s = _dft_cos_sin(t_len)
    consts = [*_table_pieces(_width_dft_table(t_len), 0), *_table_pieces(np.concatenate([c, s], axis=1), 1)]
    return pl.pallas_call(
        _fourier_dense_kernel,
        grid=(nb,),
        in_specs=[pl.BlockSpec((t_len, D_FOURIER), lambda b: (b, 0))] + [_resident(a.shape) for a in consts],
        out_specs=pl.BlockSpec((t_len, D_FOURIER), lambda b: (b, 0)),
        out_shape=jax.ShapeDtypeStruct(f.shape, F32),
        compiler_params=_params(1),
        name="fourier_dense",
    )(f, *consts)


def _fourier_stage1_kernel(x_ref, cwc_ref, cwl_ref, mc_ref, ml_ref, ar_ref, ai_ref, sr_ref, si_ref):
    zs = [_width_dft(x_ref[j], cwc_ref, cwl_ref) for j in range(FFT_ROWS)]
    for j in range(FFT_ROWS):
        a = _dot_table_data(mc_ref[j], ml_ref[j], jnp.concatenate(zs[j], axis=0))
        for g in range(D_FOURIER // LANES):
            lanes = slice(g * LANES, (g + 1) * LANES)
            sr_ref[g, pl.ds(j, FFT_N, stride=FFT_ROWS), :] = a[:FFT_N, lanes]
            si_ref[g, pl.ds(j, FFT_N, stride=FFT_ROWS), :] = a[FFT_N:, lanes]
    for g in range(D_FOURIER // LANES):
        lanes = slice(g * LANES, (g + 1) * LANES)
        ar_ref[:, :, lanes] = sr_ref[g].reshape(FFT_N, FFT_ROWS, LANES)
        ai_ref[:, :, lanes] = si_ref[g].reshape(FFT_N, FFT_ROWS, LANES)


def _fourier_stage2_kernel(ar_ref, ai_ref, f2c_ref, f2l_ref, o_ref):
    for j in range(FFT_ROWS):
        aa = jnp.concatenate([ar_ref[j], ai_ref[j]], axis=0)
        o_ref[j] = _dot_table_data(f2c_ref[...], f2l_ref[...], aa)


def _fourier_two_stage(x, nb):
    n = FFT_N
    t_len = n * n
    s1 = np.arange(n)[:, None]
    t1 = np.arange(n)[None, :]
    mats = []
    for t2 in range(n):
        ang = 2.0 * np.pi * ((s1 * t1 * n + s1 * t2) % t_len) / t_len
        c, s = np.cos(ang), np.sin(ang)
        mats.append(np.block([[c, s], [-s, c]]))
    m_cat, m_lo = _table_pieces(np.stack(mats), 2)
    c2, s2 = _dft_cos_sin(n)
    f2 = _table_pieces(np.concatenate([c2, s2], axis=1), 1)
    cw = _table_pieces(_width_dft_table(t_len), 0)

    rows_blk = pl.BlockSpec((None, FFT_ROWS, n, D_FOURIER), lambda b, j: (b, j, 0, 0))
    cols_blk = pl.BlockSpec((None, n, FFT_ROWS, D_FOURIER), lambda b, j: (b, 0, j, 0))
    shape4 = jax.ShapeDtypeStruct((nb, n, n, D_FOURIER), F32)
    slabs = pltpu.VMEM((D_FOURIER // LANES, n * FFT_ROWS, LANES), F32)
    ar, ai = pl.pallas_call(
        _fourier_stage1_kernel,
        grid=(nb, n // FFT_ROWS),
        in_specs=[rows_blk, _resident(cw[0].shape), _resident(cw[1].shape),
                  pl.BlockSpec((FFT_ROWS,) + m_cat.shape[1:], lambda b, j: (j, 0, 0)),
                  pl.BlockSpec((FFT_ROWS,) + m_lo.shape[1:], lambda b, j: (j, 0, 0))],
        out_specs=[cols_blk, cols_blk],
        out_shape=[shape4, shape4],
        scratch_shapes=[slabs, slabs],
        compiler_params=_params(2),
        name="fourier_stage1",
    )(x, *cw, m_cat, m_lo)
    return pl.pallas_call(
        _fourier_stage2_kernel,
        grid=(nb, n // FFT_ROWS),
        in_specs=[rows_blk, rows_blk, _resident(f2[0].shape), _resident(f2[1].shape)],
        out_specs=rows_blk,
        out_shape=shape4,
        compiler_params=_params(2),
        name="fourier_stage2",
    )(ar, ai, *f2)


def _gla_masks(chunk):
    idx = np.arange(chunk)
    i = idx[:, None]
    j = idx[None, :]
    masks = [i == j]
    for l in range(GLA_LEVELS):
        same = (i >> (l + 1)) == (j >> (l + 1))
        masks.append(same & (((i >> l) & 1) == 1) & (((j >> l) & 1) == 0))
    mask_f = np.stack(masks).astype(np.float32)
    return mask_f, np.transpose(mask_f, (0, 2, 1)).copy()


_NT = (((1,), (1,)), ((), ()))
_TN = (((0,), (0,)), ((), ()))
SUBLANES = 8


def _block_sums(la, forward):
    c = la.shape[0]
    nv = c // SUBLANES
    x = la.reshape(nv, SUBLANES, DK)
    sub = lax.broadcasted_iota(jnp.int32, (1, SUBLANES, DK), 1)
    near = x
    tot = x
    out = {}
    w = 1
    while w < SUBLANES:
        right = (sub & w) != 0
        partner = jnp.where(right, pltpu.roll(tot, w, axis=1), pltpu.roll(tot, SUBLANES - w, axis=1))
        near = near + jnp.where(right if forward else jnp.logical_not(right), partner, 0.0)
        tot = tot + partner
        w *= 2
        out[w] = (near, tot - near)
    blk = 1
    while w < c:
        takes = (lambda v: v & blk) if forward else (lambda v: not v & blk)
        near = jnp.stack([near[v] + tot[v ^ blk] if takes(v) else near[v] for v in range(nv)])
        tot = jnp.stack([tot[v] + tot[v ^ blk] for v in range(nv)])
        w *= 2
        blk *= 2
        out[w] = (near, tot - near)
    return {w: (a.reshape(c, DK), b.reshape(c, DK)) for w, (a, b) in out.items()}


def _gla_prepare(q_ref, k_ref, laf_ref, lab_ref, qs_ref, ks_ref, tot_ref, rows_f, rows_b):
    c = GLA_CHUNK
    for g in range(len(rows_f)):
        for fwd in (True, False):
            chain = 2 * g + (0 if fwd else 1)
            rows = pl.ds(rows_f[g] if fwd else rows_b[g], c)
            q = q_ref[rows, :]
            k = k_ref[rows, :]
            la = (laf_ref if fwd else lab_ref)[rows, :]
            sums = _block_sums(la, fwd)
            qs_ref[chain, 0:c, :] = q.astype(BF16)
            ks_ref[chain, 0:c, :] = k.astype(BF16)
            for l in range(GLA_LEVELS + 1):
                if l == 0:
                    d, e = jnp.exp2(la), None
                else:
                    near, far = sums[1 << l]
                    d, e = jnp.exp2(near), jnp.exp2(far)
                qs_ref[chain, (l + 1) * c:(l + 2) * c, :] = (q * d).astype(BF16)
                if e is not None:
                    ks_ref[chain, l * c:(l + 1) * c, :] = (k * e).astype(BF16)
            tot_ref[chain] = jnp.broadcast_to(d[c - 1:c, :] if fwd else d[0:1, :], (SUBLANES, DK))


def _gla_apply(v_ref, maskf_ref, maskb_ref, qs_ref, ks_ref, tot_ref, stf_ref, stb_ref, o_ref,
               rows_f, rows_b, *, accumulate):
    c = GLA_CHUNK
    levels = GLA_LEVELS
    n = len(rows_f)
    scores = []
    for chain in range(2 * n):
        m_ref = maskf_ref if chain % 2 == 0 else maskb_ref
        s = None
        for l in range(levels + 1):
            kl = max(l - 1, 0)
            p = lax.dot_general(qs_ref[chain, l * c:(l + 1) * c, :], ks_ref[chain, kl * c:(kl + 1) * c, :], _NT,
                                preferred_element_type=F32) * m_ref[l]
            s = p if s is None else s + p
        scores.append(s.astype(BF16))
    for chain in range(2 * n):
        fwd = chain % 2 == 0
        g = chain // 2
        rows = pl.ds(rows_f[g] if fwd else rows_b[g], c)
        st_ref = stf_ref if fwd else stb_ref
        v = v_ref[rows, :]
        st = st_ref[...]
        q_top = qs_ref[chain, (levels + 1) * c:(levels + 2) * c, :]
        k_top = ks_ref[chain, levels * c:(levels + 1) * c, :]
        o = _dot(scores[chain], v) + _dot(q_top, st.astype(BF16))
        tot = jnp.broadcast_to(tot_ref[chain][0:1, :], (DK, DK)).T
        st_ref[...] = (st * jnp.concatenate([tot] * (DV // DK), axis=1)
                       + lax.dot_general(k_top, v, _TN, preferred_element_type=F32))
        if accumulate:
            o_ref[rows, :] += o
        else:
            o_ref[rows, :] = o


def _gla_kernel(q_ref, k_ref, v_ref, laf_ref, lab_ref, s0f_ref, s0b_ref, maskf_ref, maskb_ref,
                o_ref, sf_ref, sb_ref, qs0_ref, ks0_ref, tot0_ref, qs1_ref, ks1_ref, tot1_ref,
                *, n_chunks, group):
    sf_ref[...] = s0f_ref[...]
    sb_ref[...] = s0b_ref[...]
    steps = n_chunks // group
    slots = ((qs0_ref, ks0_ref, tot0_ref), (qs1_ref, ks1_ref, tot1_ref))

    def rows_of(i):
        rows_f = [pl.multiple_of((i * group + g) * GLA_CHUNK, GLA_CHUNK) for g in range(group)]
        rows_b = [pl.multiple_of((n_chunks - 1 - i * group - g) * GLA_CHUNK, GLA_CHUNK) for g in range(group)]
        return rows_f, rows_b

    def prepare(i, slot):
        _gla_prepare(q_ref, k_ref, laf_ref, lab_ref, *slots[slot], *rows_of(i))

    def apply(i, slot, accumulate):
        _gla_apply(v_ref, maskf_ref, maskb_ref, *slots[slot], sf_ref, sb_ref, o_ref, *rows_of(i),
                   accumulate=accumulate)

    def pair(j, accumulate, last):
        prepare(2 * j + 1, 1)
        apply(2 * j, 0, accumulate[0])
        if not last:
            prepare(2 * j + 2, 0)
        apply(2 * j + 1, 1, accumulate[1])

    n_pairs = steps // 2
    prepare(0, 0)
    if n_pairs == 1:
        pair(0, (False, True), True)
    else:
        def body(accumulate):
            def f(j, carry):
                pair(j, (accumulate, accumulate), False)
                return carry
            return f
        lax.fori_loop(0, n_pairs // 2, body(False), 0)
        lax.fori_loop(n_pairs // 2, n_pairs - 1, body(True), 0)
        pair(n_pairs - 1, (True, True), True)


def _gla(q, k, v, laf, lab, s0f, s0b, nb, t_len):
    group = GLA_GROUP
    n_chunks = t_len // GLA_CHUNK
    steps = n_chunks // group
    assert t_len % GLA_CHUNK == 0 and n_chunks % group == 0 and (steps == 2 or steps % 4 == 0)
    consts = [jnp.asarray(m) for m in _gla_masks(GLA_CHUNK)]

    def seq(width):
        return pl.BlockSpec((t_len, width), lambda b, h: (b, h))

    state = pl.BlockSpec((None, None, DK, DV), lambda b, h: (b, h, 0, 0))
    n = nb * t_len
    slot = [pltpu.VMEM((2 * group, (GLA_LEVELS + 2) * GLA_CHUNK, DK), BF16),
            pltpu.VMEM((2 * group, (GLA_LEVELS + 1) * GLA_CHUNK, DK), BF16),
            pltpu.VMEM((2 * group, 8, DK), F32)]
    return pl.pallas_call(
        functools.partial(_gla_kernel, n_chunks=n_chunks, group=group),
        grid=(nb, N_HEADS),
        in_specs=[seq(DK), seq(DK), seq(DV), seq(DK), seq(DK), state, state]
                 + [_resident(a.shape) for a in consts],
        out_specs=[seq(DV), state, state],
        out_shape=[jax.ShapeDtypeStruct((n, DV_TOT), F32),
                   jax.ShapeDtypeStruct((nb, N_HEADS, DK, DV), F32),
                   jax.ShapeDtypeStruct((nb, N_HEADS, DK, DV), F32)],
        scratch_shapes=slot + slot,
        compiler_params=_params(2),
        name="gla",
    )(q, k, v, laf, lab, s0f, s0b, *consts)


def _mixout_kernel(x_ref, mod_ref, m_ref, o_ref, r_ref, g_ref, gn_ref, wpf_ref, wpg_ref, wout_ref, y_ref,
                   *scratch, d_model, fft_layout):
    if fft_layout:
        slab_ref, = scratch
        rows = m_ref.shape[1]
        for g in range(D_FOURIER // LANES):
            slab_ref[g] = m_ref[:, :, g * LANES:(g + 1) * LANES].reshape(FFT_N * rows, LANES)
        m = jnp.concatenate(
            [jnp.concatenate([slab_ref[g, pl.ds(s, FFT_N, stride=rows), :] for g in range(D_FOURIER // LANES)],
                             axis=1) for s in range(rows)], axis=0)
    else:
        m = m_ref[...]
    branch_a = _dot(m.astype(BF16), wpf_ref[...])
    o = o_ref[...]
    parts = []
    for h in range(N_HEADS):
        oh = o[:, h * DV:(h + 1) * DV]
        parts.append(oh * lax.rsqrt(jnp.mean(oh * oh, axis=-1, keepdims=True) + RMS_EPS))
    r = r_ref[...].astype(F32)
    on = jnp.concatenate(parts, axis=1) * (gn_ref[...] * r * _sigmoid(r))
    branch_b = _dot(on.astype(BF16), wpg_ref[...])
    g = _sigmoid(g_ref[...].astype(F32))
    merged = g[:, :d_model] * branch_a + g[:, d_model:] * branch_b
    y = _dot(merged.astype(BF16), wout_ref[...])
    y_ref[...] = x_ref[...] + mod_ref[5:6, :] * y


def _mixout(x, mod, m, o, r, g, gla_norm, wpf, wpg, wout, *, tiles_per_group, fft_layout):
    n, d = x.shape
    tm = TOKEN_TILE

    def tok(width):
        return pl.BlockSpec((tm, width), lambda i: (i, 0))

    m_spec = tok(D_FOURIER)
    scratch = []
    if fft_layout:
        rows = tm // FFT_N
        tiles = FFT_N // rows
        m_spec = pl.BlockSpec((None, FFT_N, rows, D_FOURIER), lambda i: (i // tiles, 0, i % tiles, 0))
        scratch = [pltpu.VMEM((D_FOURIER // LANES, tm, LANES), F32)]
    return pl.pallas_call(
        functools.partial(_mixout_kernel, d_model=d, fft_layout=fft_layout),
        grid=(n // tm,),
        in_specs=[tok(d),
                  pl.BlockSpec((None, N_MOD, d), lambda i: (i // tiles_per_group, 0, 0)),
                  m_spec, tok(DV_TOT), tok(DV_TOT), tok(2 * d),
                  _resident((1, DV_TOT)), _resident(wpf.shape), _resident(wpg.shape), _resident(wout.shape)],
        out_specs=tok(d),
        out_shape=jax.ShapeDtypeStruct((n, d), F32),
        scratch_shapes=scratch,
        compiler_params=_params(1),
        name="mixout",
    )(x, mod, m, o, r, g, gla_norm.reshape(1, DV_TOT), wpf, wpg, wout)


def _grid_pos_tables(n_tokens, d_model):
    n_freq = d_model // 4
    omega = POS_BASE ** (-jnp.arange(n_freq, dtype=F32) / n_freq)
    ra = jnp.arange(n_tokens // GRID_W, dtype=F32)[:, None] * omega
    ca = jnp.arange(GRID_W, dtype=F32)[:, None] * omega
    return (jnp.concatenate([jnp.sin(ra), jnp.cos(ra)], axis=-1),
            jnp.concatenate([jnp.sin(ca), jnp.cos(ca)], axis=-1))


def _trunk_layer(x, mod, s0f, s0b, nb, t_len, w, *, pos, final_gain, two_stage_fft):
    tiles = max(t_len // TOKEN_TILE, 1) if mod.shape[0] > 1 else x.shape[0] // TOKEN_TILE
    x = _ffn(x, mod, w["norm_ffn1"], w["ffn1_gate"], w["ffn1_up"], w["ffn1_down"],
             tiles_per_group=tiles, first_mod_row=0, pos=pos)
    f, q, k, v, r, g, laf, lab = _mixin(x, mod, w["norm_mix"], w["w_main"], w["w_tail"], w["w_alpha"],
                                        w["b_alpha"],
                                        tiles_per_group=tiles, fft_layout=two_stage_fft)
    m = _fourier_two_stage(f, nb) if two_stage_fft else _fourier_dense(f, nb, t_len)
    o, sf, sb = _gla(q, k, v, laf, lab, s0f, s0b, nb, t_len)
    x = _mixout(x, mod, m, o, r, g, w["gla_norm"], w["proj_fourier"], w["proj_gla"], w["w_out"],
                tiles_per_group=tiles, fft_layout=two_stage_fft)
    x = _ffn(x, mod, w["norm_ffn2"], w["ffn2_gate"], w["ffn2_up"], w["ffn2_down"],
             tiles_per_group=tiles, first_mod_row=6, final_gain=final_gain)
    return x, sf, sb


def kernel(x_prompt, x_sample, state_gla_fwd, state_gla_bwd, c, c_ctx, w_ada, b_ada, norm_ffn1, w_ffn1_gate, w_ffn1_up, w_ffn1_down, norm_mix, w_in, w_alpha_fwd, b_alpha_fwd, w_alpha_bwd, b_alpha_bwd, gla_norm, w_proj_fourier, w_proj_gla, w_out, norm_ffn2, w_ffn2_gate, w_ffn2_up, w_ffn2_down, final_norm):
    nb_ctx, t_ctx, d = x_prompt.shape
    nb_lat, t_lat, _ = x_sample.shape
    depth = w_ada.shape[0]
    assert t_lat == FFT_N * FFT_N and (nb_ctx * t_ctx) % TOKEN_TILE == 0 and t_lat % TOKEN_TILE == 0

    xc = x_prompt.reshape(nb_ctx * t_ctx, d)
    xl = x_sample.reshape(nb_lat * t_lat, d)
    pos = _grid_pos_tables(t_lat, d)
    zero_state = jnp.zeros((nb_ctx, N_HEADS, DK, DV), F32)
    pad_rows = (-(nb_lat + 1)) % 8
    cvecs = jnp.concatenate([c, c_ctx[None, :], jnp.zeros((pad_rows, d), F32)], axis=0)

    new_fwd, new_bwd = [], []
    for l in range(depth):
        cut = _C_G
        w_in_l = w_in[l]
        w_main = w_in_l[:, :cut].astype(BF16)
        w_tail = jnp.concatenate(
            [w_in_l[:, cut + 2 * GATE_RANK:].astype(BF16), w_in_l[:, cut:cut + 2 * GATE_RANK].astype(BF16),
             jnp.zeros((d, ALR_PAD - 2 * GATE_RANK), BF16)], axis=1)
        w_alpha = jnp.zeros((ALR_PAD, 2 * DK_TOT), F32)
        w_alpha = w_alpha.at[:GATE_RANK, :DK_TOT].set(w_alpha_fwd[l])
        w_alpha = w_alpha.at[GATE_RANK:2 * GATE_RANK, DK_TOT:].set(w_alpha_bwd[l]).astype(BF16)
        b_alpha = jnp.concatenate([b_alpha_fwd[l], b_alpha_bwd[l]]).reshape(1, 2 * DK_TOT)
        w = {
            "norm_ffn1": norm_ffn1[l], "ffn1_gate": w_ffn1_gate[l].astype(BF16),
            "ffn1_up": w_ffn1_up[l].astype(BF16), "ffn1_down": w_ffn1_down[l].astype(BF16),
            "norm_mix": norm_mix[l], "w_main": w_main, "w_tail": w_tail, "w_alpha": w_alpha, "b_alpha": b_alpha,
            "gla_norm": gla_norm[l], "proj_fourier": w_proj_fourier[l].astype(BF16),
            "proj_gla": w_proj_gla[l].astype(BF16), "w_out": w_out[l].astype(BF16),
            "norm_ffn2": norm_ffn2[l], "ffn2_gate": w_ffn2_gate[l].astype(BF16),
            "ffn2_up": w_ffn2_up[l].astype(BF16), "ffn2_down": w_ffn2_down[l].astype(BF16),
        }
        last = l == depth - 1
        mod = _ada(cvecs, w_ada[l], b_ada[l]).reshape(-1, N_MOD, d)
        mod_lat = mod[:nb_lat]
        mod_ctx = mod[nb_lat:nb_lat + 1]
        xc, sf, sb = _trunk_layer(xc, mod_ctx, zero_state, zero_state, nb_ctx, t_ctx, w,
                                  pos=None, final_gain=final_norm if last else None, two_stage_fft=False)
        new_fwd.append(sf)
        new_bwd.append(sb)
        xl, _, _ = _trunk_layer(xl, mod_lat, state_gla_fwd[:, l], state_gla_bwd[:, l], nb_lat, t_lat, w,
                                pos=pos if l == 0 else None, final_gain=final_norm if last else None,
                                two_stage_fft=True)
    y_prompt = xc.reshape(nb_ctx, t_ctx, d)
    y_sample = xl.reshape(nb_lat, t_lat, d)
    return (y_prompt, y_sample, jnp.stack(new_fwd, axis=1), jnp.stack(new_bwd, axis=1))
```

```python
import functools
import math

import numpy as np
import jax
import jax.numpy as jnp
from jax import lax
from jax.experimental import pallas as pl
from jax.experimental.pallas import tpu as pltpu

F32 = jnp.float32
BF16 = jnp.bfloat16

GRID_W = 64
N_GROUPS = 4
GROUP_W = 128
D_FOURIER = N_GROUPS * GROUP_W
N_HEADS = 4
DK = 128
DV = 256
DK_TOT = N_HEADS * DK
DV_TOT = N_HEADS * DV
GATE_RANK = 16
GATE_TEMP = 16.0
RMS_EPS = 1e-6
POS_BASE = 10000.0
N_MOD = 9

TOKEN_TILE = 512
GLA_CHUNK = 64
GLA_LEVELS = 6
GLA_GROUP = 2
LANES = 128
ALR_PAD = LANES
FFT_N = 64
FFT_ROWS = 8
VMEM_LIMIT = 56 * 1024 * 1024


def _params(n_axes):
    return pltpu.CompilerParams(dimension_semantics=("arbitrary",) * n_axes,
                                vmem_limit_bytes=VMEM_LIMIT)


def _resident(shape):
    nd = len(shape)
    return pl.BlockSpec(shape, lambda *_: (0,) * nd, pipeline_mode=pl.Buffered(1))


LOG2_E = math.log2(math.e)


def _sigmoid(x):
    return 1.0 / (1.0 + jnp.exp2(x * (-LOG2_E)))


def _rms(x, gain):
    return x * lax.rsqrt(jnp.mean(x * x, axis=-1, keepdims=True) + RMS_EPS) * gain


def _modulated_rms(x, gain, scale, shift):
    return x * lax.rsqrt(jnp.mean(x * x, axis=-1, keepdims=True) + RMS_EPS) * (gain * (1.0 + scale)) + shift


def _dot(a, b):
    return jnp.dot(a, b, preferred_element_type=F32)


def _ada_kernel(c_ref, w_ref, b_ref, o_ref):
    c = c_ref[...]
    s = (c * _sigmoid(c)).astype(BF16)
    o_ref[...] = _dot(s, w_ref[...].astype(BF16)) + b_ref[...]


def _ada(cvecs, w_ada, b_ada):
    rows, d = cvecs.shape
    n = w_ada.shape[1]
    tn = d
    return pl.pallas_call(
        _ada_kernel,
        grid=(n // tn,),
        in_specs=[pl.BlockSpec((rows, d), lambda j: (0, 0)),
                  pl.BlockSpec((d, tn), lambda j: (0, j)),
                  pl.BlockSpec((1, tn), lambda j: (0, j))],
        out_specs=pl.BlockSpec((rows, tn), lambda j: (0, j)),
        out_shape=jax.ShapeDtypeStruct((rows, n), F32),
        compiler_params=_params(1),
        name="ada",
    )(cvecs, w_ada, b_ada.reshape(1, n))


def _ffn_kernel(*refs, first_mod_row, add_pos, final_norm):
    x_ref, mod_ref, gain_ref, wg_ref, wu_ref, wd_ref = refs[:6]
    rest = list(refs[6:])
    prow_ref, pcol_ref = (rest.pop(0), rest.pop(0)) if add_pos else (None, None)
    fn_ref = rest.pop(0) if final_norm else None
    o_ref = rest.pop(0)

    x = x_ref[...]
    if add_pos:
        tm, d = x.shape
        rows = tm // GRID_W
        prow = jnp.broadcast_to(prow_ref[...][:, None, :], (rows, GRID_W, d // 2))
        pcol = jnp.broadcast_to(pcol_ref[...][None, :, :], (rows, GRID_W, d // 2))
        x = x + jnp.concatenate([prow, pcol], axis=-1).reshape(tm, d)
    shift = mod_ref[first_mod_row:first_mod_row + 1, :]
    scale = mod_ref[first_mod_row + 1:first_mod_row + 2, :]
    gate = mod_ref[first_mod_row + 2:first_mod_row + 3, :]
    h = _modulated_rms(x, gain_ref[...], scale, shift).astype(BF16)
    g = _dot(h, wg_ref[...])
    u = _dot(h, wu_ref[...])
    a = (g * _sigmoid(g) * u).astype(BF16)
    y = _dot(a, wd_ref[...])
    xn = x + (0.5 * gate) * y
    if final_norm:
        xn = _rms(xn, fn_ref[...])
    o_ref[...] = xn


def _ffn(x, mod, gain, wg, wu, wd, *, tiles_per_group, first_mod_row, pos=None, final_gain=None):
    n, d = x.shape
    dff = wg.shape[1]
    tm = TOKEN_TILE
    add_pos = pos is not None
    final_norm = final_gain is not None
    in_specs = [pl.BlockSpec((tm, d), lambda i: (i, 0)),
                pl.BlockSpec((None, N_MOD, d), lambda i: (i // tiles_per_group, 0, 0)),
                _resident((1, d)), _resident((d, dff)), _resident((d, dff)), _resident((dff, d))]
    args = [x, mod, gain.reshape(1, d), wg, wu, wd]
    if add_pos:
        pos_row, pos_col = pos
        rows = tm // GRID_W
        row_tiles = pos_row.shape[0] // rows
        assert tm % GRID_W == 0 and rows % 8 == 0 and pos_col.shape[0] == GRID_W
        in_specs += [pl.BlockSpec((rows, d // 2), lambda i: (i % row_tiles, 0)), _resident(pos_col.shape)]
        args += [pos_row, pos_col]
    if final_norm:
        in_specs.append(_resident((1, d)))
        args.append(final_gain.reshape(1, d))
    return pl.pallas_call(
        functools.partial(_ffn_kernel, first_mod_row=first_mod_row, add_pos=add_pos, final_norm=final_norm),
        grid=(n // tm,),
        in_specs=in_specs,
        out_specs=pl.BlockSpec((tm, d), lambda i: (i, 0)),
        out_shape=jax.ShapeDtypeStruct((n, d), F32),
        compiler_params=_params(1),
        name="ffn",
    )(*args)


_C_F = 0
_C_Q = _C_F + D_FOURIER
_C_K = _C_Q + DK_TOT
_C_V = _C_K + DK_TOT
_C_R = _C_V + DV_TOT
_C_G = _C_R + DV_TOT


def _mixin_kernel(x_ref, mod_ref, gain_ref, w_ref, wt_ref, wal_ref, bal_ref,
                  f_ref, q_ref, k_ref, v_ref, r_ref, g_ref, laf_ref, lab_ref, *scratch, d_model, fft_layout):
    x = x_ref[...]
    shift = mod_ref[3:4, :]
    scale = mod_ref[4:5, :]
    h = _modulated_rms(x, gain_ref[...], scale, shift).astype(BF16)

    def proj(lo, hi):
        return _dot(h, w_ref[:, lo:hi])

    alr = _dot(h, wt_ref[:, 2 * d_model:2 * d_model + ALR_PAD]).astype(BF16)
    pre = _dot(alr, wal_ref[...]) + bal_ref[...]
    la = (jnp.minimum(pre, 0.0) * (LOG2_E / GATE_TEMP)
          - jnp.log2(1.0 + jnp.exp2(jnp.abs(pre) * (-LOG2_E))) * (1.0 / GATE_TEMP))
    laf_ref[...] = la[:, :DK_TOT]
    lab_ref[...] = la[:, DK_TOT:]

    f = proj(_C_F, _C_Q)
    if fft_layout:
        slab_ref, = scratch
        rows = x.shape[0] // FFT_N
        for a in range(rows):
            for g in range(D_FOURIER // LANES):
                slab_ref[g, pl.ds(a, FFT_N, stride=rows), :] = f[a * FFT_N:(a + 1) * FFT_N, g * LANES:(g + 1) * LANES]
        for g in range(D_FOURIER // LANES):
            f_ref[:, :, g * LANES:(g + 1) * LANES] = slab_ref[g].reshape(FFT_N, rows, LANES)
    else:
        f_ref[...] = f
    q_ref[...] = proj(_C_Q, _C_K) * (DK ** -0.5)
    k_ref[...] = proj(_C_K, _C_V)
    v_ref[...] = proj(_C_V, _C_R).astype(BF16)
    r_ref[...] = proj(_C_R, _C_G).astype(BF16)
    g_ref[...] = _dot(h, wt_ref[:, :2 * d_model]).astype(BF16)


def _mixin(x, mod, gain, w_main, w_tail, w_alpha, b_alpha, *, tiles_per_group, fft_layout):
    n, d = x.shape
    tm = TOKEN_TILE

    def tok(width):
        return pl.BlockSpec((tm, width), lambda i: (i, 0))

    outs = [(D_FOURIER, F32), (DK_TOT, F32), (DK_TOT, F32), (DV_TOT, BF16), (DV_TOT, BF16),
            (2 * d, BF16), (DK_TOT, F32), (DK_TOT, F32)]
    out_specs = [tok(w) for w, _ in outs]
    out_shape = [jax.ShapeDtypeStruct((n, w), dt) for w, dt in outs]
    scratch = []
    if fft_layout:
        rows = tm // FFT_N
        tiles = FFT_N // rows
        assert tm % FFT_N == 0 and rows % 8 == 0 and n % (FFT_N * FFT_N) == 0
        out_specs[0] = pl.BlockSpec((None, FFT_N, rows, D_FOURIER), lambda i: (i // tiles, 0, i % tiles, 0))
        out_shape[0] = jax.ShapeDtypeStruct((n // (FFT_N * FFT_N), FFT_N, FFT_N, D_FOURIER), F32)
        scratch = [pltpu.VMEM((D_FOURIER // LANES, tm, LANES), F32)]
    return pl.pallas_call(
        functools.partial(_mixin_kernel, d_model=d, fft_layout=fft_layout),
        grid=(n // tm,),
        in_specs=[tok(d),
                  pl.BlockSpec((None, N_MOD, d), lambda i: (i // tiles_per_group, 0, 0)),
                  _resident((1, d)), _resident(w_main.shape), _resident(w_tail.shape),
                  _resident((ALR_PAD, 2 * DK_TOT)), _resident((1, 2 * DK_TOT))],
        out_specs=out_specs,
        out_shape=out_shape,
        scratch_shapes=scratch,
        compiler_params=_params(1),
        name="mixin",
    )(x, mod, gain.reshape(1, d), w_main, w_tail, w_alpha, b_alpha)


def _dft_cos_sin(n):
    k = np.arange(n)
    ang = 2.0 * np.pi * ((k[:, None] * k[None, :]) % n) / n
    return np.cos(ang), np.sin(ang)


def _split_hi_lo(x):
    hi = x.astype(BF16)
    return hi, (x - hi.astype(F32)).astype(BF16)


def _table_pieces(table, axis):
    hi, lo = _split_hi_lo(jnp.asarray(table.astype(np.float32)))
    return jnp.concatenate([hi, hi], axis=axis), lo


def _dot_data_table(x, cat_ref, lo_ref):
    hi, lo = _split_hi_lo(x)
    return _dot(jnp.concatenate([hi, lo], axis=1), cat_ref[...]) + _dot(hi, lo_ref[...])


def _dot_table_data(cat, lo_t, x):
    hi, lo = _split_hi_lo(x)
    return _dot(cat, jnp.concatenate([hi, lo], axis=0)) + _dot(lo_t, hi)


def _width_dft_table(t_len):
    c, s = _dft_cos_sin(GROUP_W)
    return np.concatenate([c, -s], axis=1) / math.sqrt(t_len * GROUP_W)


def _width_dft(x, cwc_ref, cwl_ref):
    zr, zi = [], []
    for g in range(N_GROUPS):
        z = _dot_data_table(x[:, g * GROUP_W:(g + 1) * GROUP_W], cwc_ref, cwl_ref)
        zr.append(z[:, :GROUP_W])
        zi.append(z[:, GROUP_W:])
    return jnp.concatenate(zr, axis=1), jnp.concatenate(zi, axis=1)


def _fourier_dense_kernel(x_ref, cwc_ref, cwl_ref, ftc_ref, ftl_ref, o_ref):
    zr, zi = _width_dft(x_ref[...], cwc_ref, cwl_ref)
    zz = jnp.concatenate([zr, zi], axis=0)
    o_ref[...] = _dot_table_data(ftc_ref[...], ftl_ref[...], zz)


def _fourier_dense(f, nb, t_len):
    c, s = _dft_cos_sin(t_len)
    consts = [*_table_pieces(_width_dft_table(t_len), 0), *_table_pieces(np.concatenate([c, s], axis=1), 1)]
    return pl.pallas_call(
        _fourier_dense_kernel,
        grid=(nb,),
        in_specs=[pl.BlockSpec((t_len, D_FOURIER), lambda b: (b, 0))] + [_resident(a.shape) for a in consts],
        out_specs=pl.BlockSpec((t_len, D_FOURIER), lambda b: (b, 0)),
        out_shape=jax.ShapeDtypeStruct(f.shape, F32),
        compiler_params=_params(1),
        name="fourier_dense",
    )(f, *consts)


def _fourier_stage1_kernel(x_ref, cwc_ref, cwl_ref, mc_ref, ml_ref, ar_ref, ai_ref, sr_ref, si_ref):
    zr, zi = _width_dft(x_ref[...].reshape(FFT_ROWS * FFT_N, D_FOURIER), cwc_ref, cwl_ref)
    for j in range(FFT_ROWS):
        rows = slice(j * FFT_N, (j + 1) * FFT_N)
        a = _dot_table_data(mc_ref[j], ml_ref[j], jnp.concatenate([zr[rows], zi[rows]], axis=0))
        for g in range(D_FOURIER // LANES):
            lanes = slice(g * LANES, (g + 1) * LANES)
            sr_ref[g, pl.ds(j, FFT_N, stride=FFT_ROWS), :] = a[:FFT_N, lanes]
            si_ref[g, pl.ds(j, FFT_N, stride=FFT_ROWS), :] = a[FFT_N:, lanes]
    for g in range(D_FOURIER // LANES):
        lanes = slice(g * LANES, (g + 1) * LANES)
        ar_ref[:, :, lanes] = sr_ref[g].reshape(FFT_N, FFT_ROWS, LANES)
        ai_ref[:, :, lanes] = si_ref[g].reshape(FFT_N, FFT_ROWS, LANES)


def _fourier_stage2_kernel(ar_ref, ai_ref, f2c_ref, f2l_ref, o_ref):
    for j in range(FFT_ROWS):
        aa = jnp.concatenate([ar_ref[j], ai_ref[j]], axis=0)
        o_ref[j] = _dot_table_data(f2c_ref[...], f2l_ref[...], aa)


def _fourier_two_stage(x, nb):
    n = FFT_N
    t_len = n * n
    s1 = np.arange(n)[:, None]
    t1 = np.arange(n)[None, :]
    mats = []
    for t2 in range(n):
        ang = 2.0 * np.pi * ((s1 * t1 * n + s1 * t2) % t_len) / t_len
        c, s = np.cos(ang), np.sin(ang)
        mats.append(np.block([[c, s], [-s, c]]))
    m_cat, m_lo = _table_pieces(np.stack(mats), 2)
    c2, s2 = _dft_cos_sin(n)
    f2 = _table_pieces(np.concatenate([c2, s2], axis=1), 1)
    cw = _table_pieces(_width_dft_table(t_len), 0)

    rows_blk = pl.BlockSpec((None, FFT_ROWS, n, D_FOURIER), lambda b, j: (b, j, 0, 0))
    cols_blk = pl.BlockSpec((None, n, FFT_ROWS, D_FOURIER), lambda b, j: (b, 0, j, 0))
    shape4 = jax.ShapeDtypeStruct((nb, n, n, D_FOURIER), F32)
    slabs = pltpu.VMEM((D_FOURIER // LANES, n * FFT_ROWS, LANES), F32)
    ar, ai = pl.pallas_call(
        _fourier_stage1_kernel,
        grid=(nb, n // FFT_ROWS),
        in_specs=[rows_blk, _resident(cw[0].shape), _resident(cw[1].shape),
                  pl.BlockSpec((FFT_ROWS,) + m_cat.shape[1:], lambda b, j: (j, 0, 0)),
                  pl.BlockSpec((FFT_ROWS,) + m_lo.shape[1:], lambda b, j: (j, 0, 0))],
        out_specs=[cols_blk, cols_blk],
        out_shape=[shape4, shape4],
        scratch_shapes=[slabs, slabs],
        compiler_params=_params(2),
        name="fourier_stage1",
    )(x, *cw, m_cat, m_lo)
    return pl.pallas_call(
        _fourier_stage2_kernel,
        grid=(nb, n // FFT_ROWS),
        in_specs=[rows_blk, rows_blk, _resident(f2[0].shape), _resident(f2[1].shape)],
        out_specs=rows_blk,
        out_shape=shape4,
        compiler_params=_params(2),
        name="fourier_stage2",
    )(ar, ai, *f2)


def _gla_masks(chunk):
    idx = np.arange(chunk)
    i = idx[:, None]
    j = idx[None, :]
    masks = [i == j]
    for l in range(GLA_LEVELS):
        same = (i >> (l + 1)) == (j >> (l + 1))
        masks.append(same & (((i >> l) & 1) == 1) & (((j >> l) & 1) == 0))
    mask_f = np.stack(masks).astype(np.float32)
    return mask_f, np.transpose(mask_f, (0, 2, 1)).copy()


_NT = (((1,), (1,)), ((), ()))
_TN = (((0,), (0,)), ((), ()))
SUBLANES = 8


def _block_sums(la, forward):
    c = la.shape[0]
    nv = c // SUBLANES
    x = la.reshape(nv, SUBLANES, DK)
    sub = lax.broadcasted_iota(jnp.int32, (1, SUBLANES, DK), 1)
    near = x
    tot = x
    out = {}
    w = 1
    while w < SUBLANES:
        right = (sub & w) != 0
        partner = jnp.where(right, pltpu.roll(tot, w, axis=1), pltpu.roll(tot, SUBLANES - w, axis=1))
        near = near + jnp.where(right if forward else jnp.logical_not(right), partner, 0.0)
        tot = tot + partner
        w *= 2
        out[w] = (near, tot - near)
    blk = 1
    while w < c:
        takes = (lambda v: v & blk) if forward else (lambda v: not v & blk)
        near = jnp.stack([near[v] + tot[v ^ blk] if takes(v) else near[v] for v in range(nv)])
        tot = jnp.stack([tot[v] + tot[v ^ blk] for v in range(nv)])
        w *= 2
        blk *= 2
        out[w] = (near, tot - near)
    return {w: (a.reshape(c, DK), b.reshape(c, DK)) for w, (a, b) in out.items()}


def _gla_prepare(q_ref, k_ref, laf_ref, lab_ref, qs_ref, ks_ref, tot_ref, rows_f, rows_b):
    c = GLA_CHUNK
    for g in range(len(rows_f)):
        for fwd in (True, False):
            chain = 2 * g + (0 if fwd else 1)
            rows = pl.ds(rows_f[g] if fwd else rows_b[g], c)
            q = q_ref[rows, :]
            k = k_ref[rows, :]
            la = (laf_ref if fwd else lab_ref)[rows, :]
            sums = _block_sums(la, fwd)
            qs_ref[chain, 0:c, :] = q.astype(BF16)
            ks_ref[chain, 0:c, :] = k.astype(BF16)
            for l in range(GLA_LEVELS + 1):
                if l == 0:
                    d, e = jnp.exp2(la), None
                else:
                    near, far = sums[1 << l]
                    d, e = jnp.exp2(near), jnp.exp2(far)
                qs_ref[chain, (l + 1) * c:(l + 2) * c, :] = (q * d).astype(BF16)
                if e is not None:
                    ks_ref[chain, l * c:(l + 1) * c, :] = (k * e).astype(BF16)
            tot_ref[chain] = jnp.broadcast_to(d[c - 1:c, :] if fwd else d[0:1, :], (SUBLANES, DK))


def _gla_apply(v_ref, maskf_ref, maskb_ref, sf_ref, sb_ref, o_ref, work):
    c = GLA_CHUNK
    levels = GLA_LEVELS
    chains = []
    for qs_ref, ks_ref, tot_ref, rows_f, rows_b, accumulate in work:
        for chain in range(2 * len(rows_f)):
            fwd = chain % 2 == 0
            rows = pl.ds((rows_f if fwd else rows_b)[chain // 2], c)
            chains.append((qs_ref, ks_ref, tot_ref, chain, fwd, rows, accumulate))
    scores = []
    for qs_ref, ks_ref, tot_ref, chain, fwd, rows, accumulate in chains:
        m_ref = maskf_ref if fwd else maskb_ref
        s = None
        for l in range(levels + 1):
            kl = max(l - 1, 0)
            p = lax.dot_general(qs_ref[chain, l * c:(l + 1) * c, :], ks_ref[chain, kl * c:(kl + 1) * c, :], _NT,
                                preferred_element_type=F32) * m_ref[l]
            s = p if s is None else s + p
        scores.append(s.astype(BF16))
    for (qs_ref, ks_ref, tot_ref, chain, fwd, rows, accumulate), s in zip(chains, scores):
        st_ref = sf_ref if fwd else sb_ref
        v = v_ref[rows, :]
        st = st_ref[...]
        q_top = qs_ref[chain, (levels + 1) * c:(levels + 2) * c, :]
        k_top = ks_ref[chain, levels * c:(levels + 1) * c, :]
        o = _dot(s, v) + _dot(q_top, st.astype(BF16))
        tot = jnp.broadcast_to(tot_ref[chain][0:1, :], (DK, DK)).T
        st_ref[...] = (st * jnp.concatenate([tot] * (DV // DK), axis=1)
                       + lax.dot_general(k_top, v, _TN, preferred_element_type=F32))
        if accumulate:
            o_ref[rows, :] += o
        else:
            o_ref[rows, :] = o


def _gla_kernel(q_ref, k_ref, v_ref, laf_ref, lab_ref, s0f_ref, s0b_ref, maskf_ref, maskb_ref,
                o_ref, sf_ref, sb_ref, qs0_ref, ks0_ref, tot0_ref, qs1_ref, ks1_ref, tot1_ref,
                *, n_chunks, group):
    sf_ref[...] = s0f_ref[...]
    sb_ref[...] = s0b_ref[...]
    steps = n_chunks // group
    slots = ((qs0_ref, ks0_ref, tot0_ref), (qs1_ref, ks1_ref, tot1_ref))

    def rows_of(i):
        rows_f = [pl.multiple_of((i * group + g) * GLA_CHUNK, GLA_CHUNK) for g in range(group)]
        rows_b = [pl.multiple_of((n_chunks - 1 - i * group - g) * GLA_CHUNK, GLA_CHUNK) for g in range(group)]
        return rows_f, rows_b

    def prepare(i, slot):
        _gla_prepare(q_ref, k_ref, laf_ref, lab_ref, *slots[slot], *rows_of(i))

    def apply(items):
        _gla_apply(v_ref, maskf_ref, maskb_ref, sf_ref, sb_ref, o_ref,
                   [(*slots[slot], *rows_of(i), accumulate) for i, slot, accumulate in items])

    def pair(j, accumulate, last):
        apply([(2 * j, 0, accumulate[0]), (2 * j + 1, 1, accumulate[1])])
        if not last:
            prepare(2 * j + 2, 0)
            prepare(2 * j + 3, 1)

    n_pairs = steps // 2
    prepare(0, 0)
    prepare(1, 1)
    if n_pairs == 1:
        pair(0, (False, True), True)
    else:
        def body(accumulate):
            def f(j, carry):
                pair(j, (accumulate, accumulate), False)
                return carry
            return f
        lax.fori_loop(0, n_pairs // 2, body(False), 0)
        lax.fori_loop(n_pairs // 2, n_pairs - 1, body(True), 0)
        pair(n_pairs - 1, (True, True), True)


def _gla(q, k, v, laf, lab, s0f, s0b, nb, t_len):
    group = GLA_GROUP
    n_chunks = t_len // GLA_CHUNK
    steps = n_chunks // group
    assert t_len % GLA_CHUNK == 0 and n_chunks % group == 0 and (steps == 2 or steps % 4 == 0)
    consts = [jnp.asarray(m) for m in _gla_masks(GLA_CHUNK)]

    def seq(width):
        return pl.BlockSpec((t_len, width), lambda b, h: (b, h))

    state = pl.BlockSpec((None, None, DK, DV), lambda b, h: (b, h, 0, 0))
    n = nb * t_len
    slot = [pltpu.VMEM((2 * group, (GLA_LEVELS + 2) * GLA_CHUNK, DK), BF16),
            pltpu.VMEM((2 * group, (GLA_LEVELS + 1) * GLA_CHUNK, DK), BF16),
            pltpu.VMEM((2 * group, 8, DK), F32)]
    return pl.pallas_call(
        functools.partial(_gla_kernel, n_chunks=n_chunks, group=group),
        grid=(nb, N_HEADS),
        in_specs=[seq(DK), seq(DK), seq(DV), seq(DK), seq(DK), state, state]
                 + [_resident(a.shape) for a in consts],
        out_specs=[seq(DV), state, state],
        out_shape=[jax.ShapeDtypeStruct((n, DV_TOT), F32),
                   jax.ShapeDtypeStruct((nb, N_HEADS, DK, DV), F32),
                   jax.ShapeDtypeStruct((nb, N_HEADS, DK, DV), F32)],
        scratch_shapes=slot + slot,
        compiler_params=_params(2),
        name="gla",
    )(q, k, v, laf, lab, s0f, s0b, *consts)


def _mixout_kernel(x_ref, mod_ref, m_ref, o_ref, r_ref, g_ref, gn_ref, wpf_ref, wpg_ref, wout_ref, y_ref,
                   *scratch, d_model, fft_layout):
    if fft_layout:
        slab_ref, = scratch
        rows = m_ref.shape[1]
        for g in range(D_FOURIER // LANES):
            slab_ref[g] = m_ref[:, :, g * LANES:(g + 1) * LANES].reshape(FFT_N * rows, LANES)
        m = jnp.concatenate(
            [jnp.concatenate([slab_ref[g, pl.ds(s, FFT_N, stride=rows), :] for g in range(D_FOURIER // LANES)],
                             axis=1) for s in range(rows)], axis=0)
    else:
        m = m_ref[...]
    branch_a = _dot(m.astype(BF16), wpf_ref[...])
    o = o_ref[...]
    parts = []
    for h in range(N_HEADS):
        oh = o[:, h * DV:(h + 1) * DV]
        parts.append(oh * lax.rsqrt(jnp.mean(oh * oh, axis=-1, keepdims=True) + RMS_EPS))
    r = r_ref[...].astype(F32)
    on = jnp.concatenate(parts, axis=1) * (gn_ref[...] * r * _sigmoid(r))
    branch_b = _dot(on.astype(BF16), wpg_ref[...])
    g = _sigmoid(g_ref[...].astype(F32))
    merged = g[:, :d_model] * branch_a + g[:, d_model:] * branch_b
    y = _dot(merged.astype(BF16), wout_ref[...])
    y_ref[...] = x_ref[...] + mod_ref[5:6, :] * y


def _mixout(x, mod, m, o, r, g, gla_norm, wpf, wpg, wout, *, tiles_per_group, fft_layout):
    n, d = x.shape
    tm = TOKEN_TILE

    def tok(width):
        return pl.BlockSpec((tm, width), lambda i: (i, 0))

    m_spec = tok(D_FOURIER)
    scratch = []
    if fft_layout:
        rows = tm // FFT_N
        tiles = FFT_N // rows
        m_spec = pl.BlockSpec((None, FFT_N, rows, D_FOURIER), lambda i: (i // tiles, 0, i % tiles, 0))
        scratch = [pltpu.VMEM((D_FOURIER // LANES, tm, LANES), F32)]
    return pl.pallas_call(
        functools.partial(_mixout_kernel, d_model=d, fft_layout=fft_layout),
        grid=(n // tm,),
        in_specs=[tok(d),
                  pl.BlockSpec((None, N_MOD, d), lambda i: (i // tiles_per_group, 0, 0)),
                  m_spec, tok(DV_TOT), tok(DV_TOT), tok(2 * d),
                  _resident((1, DV_TOT)), _resident(wpf.shape), _resident(wpg.shape), _resident(wout.shape)],
        out_specs=tok(d),
        out_shape=jax.ShapeDtypeStruct((n, d), F32),
        scratch_shapes=scratch,
        compiler_params=_params(1),
        name="mixout",
    )(x, mod, m, o, r, g, gla_norm.reshape(1, DV_TOT), wpf, wpg, wout)


def _grid_pos_tables(n_tokens, d_model):
    n_freq = d_model // 4
    omega = POS_BASE ** (-jnp.arange(n_freq, dtype=F32) / n_freq)
    ra = jnp.arange(n_tokens // GRID_W, dtype=F32)[:, None] * omega
    ca = jnp.arange(GRID_W, dtype=F32)[:, None] * omega
    return (jnp.concatenate([jnp.sin(ra), jnp.cos(ra)], axis=-1),
            jnp.concatenate([jnp.sin(ca), jnp.cos(ca)], axis=-1))


def _trunk_layer(x, mod, s0f, s0b, nb, t_len, w, *, pos, final_gain, two_stage_fft):
    tiles = max(t_len // TOKEN_TILE, 1) if mod.shape[0] > 1 else x.shape[0] // TOKEN_TILE
    x = _ffn(x, mod, w["norm_ffn1"], w["ffn1_gate"], w["ffn1_up"], w["ffn1_down"],
             tiles_per_group=tiles, first_mod_row=0, pos=pos)
    f, q, k, v, r, g, laf, lab = _mixin(x, mod, w["norm_mix"], w["w_main"], w["w_tail"], w["w_alpha"],
                                        w["b_alpha"],
                                        tiles_per_group=tiles, fft_layout=two_stage_fft)
    m = _fourier_two_stage(f, nb) if two_stage_fft else _fourier_dense(f, nb, t_len)
    o, sf, sb = _gla(q, k, v, laf, lab, s0f, s0b, nb, t_len)
    x = _mixout(x, mod, m, o, r, g, w["gla_norm"], w["proj_fourier"], w["proj_gla"], w["w_out"],
                tiles_per_group=tiles, fft_layout=two_stage_fft)
    x = _ffn(x, mod, w["norm_ffn2"], w["ffn2_gate"], w["ffn2_up"], w["ffn2_down"],
             tiles_per_group=tiles, first_mod_row=6, final_gain=final_gain)
    return x, sf, sb


def kernel(x_prompt, x_sample, state_gla_fwd, state_gla_bwd, c, c_ctx, w_ada, b_ada, norm_ffn1, w_ffn1_gate, w_ffn1_up, w_ffn1_down, norm_mix, w_in, w_alpha_fwd, b_alpha_fwd, w_alpha_bwd, b_alpha_bwd, gla_norm, w_proj_fourier, w_proj_gla, w_out, norm_ffn2, w_ffn2_gate, w_ffn2_up, w_ffn2_down, final_norm):
    nb_ctx, t_ctx, d = x_prompt.shape
    nb_lat, t_lat, _ = x_sample.shape
    depth = w_ada.shape[0]
    assert t_lat == FFT_N * FFT_N and (nb_ctx * t_ctx) % TOKEN_TILE == 0 and t_lat % TOKEN_TILE == 0

    xc = x_prompt.reshape(nb_ctx * t_ctx, d)
    xl = x_sample.reshape(nb_lat * t_lat, d)
    pos = _grid_pos_tables(t_lat, d)
    zero_state = jnp.zeros((nb_ctx, N_HEADS, DK, DV), F32)
    pad_rows = (-(nb_lat + 1)) % 8
    cvecs = jnp.concatenate([c, c_ctx[None, :], jnp.zeros((pad_rows, d), F32)], axis=0)

    new_fwd, new_bwd = [], []
    for l in range(depth):
        cut = _C_G
        w_in_l = w_in[l]
        w_main = w_in_l[:, :cut].astype(BF16)
        w_tail = jnp.concatenate(
            [w_in_l[:, cut + 2 * GATE_RANK:].astype(BF16), w_in_l[:, cut:cut + 2 * GATE_RANK].astype(BF16),
             jnp.zeros((d, ALR_PAD - 2 * GATE_RANK), BF16)], axis=1)
        w_alpha = jnp.zeros((ALR_PAD, 2 * DK_TOT), F32)
        w_alpha = w_alpha.at[:GATE_RANK, :DK_TOT].set(w_alpha_fwd[l])
        w_alpha = w_alpha.at[GATE_RANK:2 * GATE_RANK, DK_TOT:].set(w_alpha_bwd[l]).astype(BF16)
        b_alpha = jnp.concatenate([b_alpha_fwd[l], b_alpha_bwd[l]]).reshape(1, 2 * DK_TOT)
        w = {
            "norm_ffn1": norm_ffn1[l], "ffn1_gate": w_ffn1_gate[l].astype(BF16),
            "ffn1_up": w_ffn1_up[l].astype(BF16), "ffn1_down": w_ffn1_down[l].astype(BF16),
            "norm_mix": norm_mix[l], "w_main": w_main, "w_tail": w_tail, "w_alpha": w_alpha, "b_alpha": b_alpha,
            "gla_norm": gla_norm[l], "proj_fourier": w_proj_fourier[l].astype(BF16),
            "proj_gla": w_proj_gla[l].astype(BF16), "w_out": w_out[l].astype(BF16),
            "norm_ffn2": norm_ffn2[l], "ffn2_gate": w_ffn2_gate[l].astype(BF16),
            "ffn2_up": w_ffn2_up[l].astype(BF16), "ffn2_down": w_ffn2_down[l].astype(BF16),
        }
        last = l == depth - 1
        mod = _ada(cvecs, w_ada[l], b_ada[l]).reshape(-1, N_MOD, d)
        mod_lat = mod[:nb_lat]
        mod_ctx = mod[nb_lat:nb_lat + 1]
        xc, sf, sb = _trunk_layer(xc, mod_ctx, zero_state, zero_state, nb_ctx, t_ctx, w,
                                  pos=None, final_gain=final_norm if last else None, two_stage_fft=False)
        new_fwd.append(sf)
        new_bwd.append(sb)
        xl, _, _ = _trunk_layer(xl, mod_lat, state_gla_fwd[:, l], state_gla_bwd[:, l], nb_lat, t_lat, w,
                                pos=pos if l == 0 else None, final_gain=final_norm if last else None,
                                two_stage_fft=True)
    y_prompt = xc.reshape(nb_ctx, t_ctx, d)
    y_sample = xl.reshape(nb_lat, t_lat, d)
    return (y_prompt, y_sample, jnp.stack(new_fwd, axis=1), jnp.stack(new_bwd, axis=1))
```

```python
import functools
import math

import numpy as np
import jax
import jax.numpy as jnp
from jax import lax
from jax.experimental import pallas as pl
from jax.experimental.pallas import tpu as pltpu

F32 = jnp.float32
BF16 = jnp.bfloat16

GRID_W = 64
N_GROUPS = 4
GROUP_W = 128
D_FOURIER = N_GROUPS * GROUP_W
N_HEADS = 4
DK = 128
DV = 256
DK_TOT = N_HEADS * DK
DV_TOT = N_HEADS * DV
GATE_RANK = 16
GATE_TEMP = 16.0
RMS_EPS = 1e-6
POS_BASE = 10000.0
N_MOD = 9

TOKEN_TILE = 512
GLA_CHUNK = 64
GLA_LEVELS = 6
GLA_GROUP = 2
LANES = 128
ALR_PAD = LANES
FFT_N = 64
FFT_ROWS = 8
VMEM_LIMIT = 56 * 1024 * 1024


def _params(n_axes):
    return pltpu.CompilerParams(dimension_semantics=("arbitrary",) * n_axes,
                                vmem_limit_bytes=VMEM_LIMIT)


def _resident(shape):
    nd = len(shape)
    return pl.BlockSpec(shape, lambda *_: (0,) * nd, pipeline_mode=pl.Buffered(1))


LOG2_E = math.log2(math.e)


def _sigmoid(x):
    return 1.0 / (1.0 + jnp.exp2(x * (-LOG2_E)))


def _rms(x, gain):
    return x * lax.rsqrt(jnp.mean(x * x, axis=-1, keepdims=True) + RMS_EPS) * gain


def _modulated_rms(x, gain, scale, shift):
    return x * lax.rsqrt(jnp.mean(x * x, axis=-1, keepdims=True) + RMS_EPS) * (gain * (1.0 + scale)) + shift


def _dot(a, b):
    return jnp.dot(a, b, preferred_element_type=F32)


def _ada_kernel(c_ref, w_ref, b_ref, o_ref):
    c = c_ref[...]
    s = (c * _sigmoid(c)).astype(BF16)
    o_ref[...] = _dot(s, w_ref[...].astype(BF16)) + b_ref[...]


def _ada(cvecs, w_ada, b_ada):
    rows, d = cvecs.shape
    n = w_ada.shape[1]
    tn = d
    return pl.pallas_call(
        _ada_kernel,
        grid=(n // tn,),
        in_specs=[pl.BlockSpec((rows, d), lambda j: (0, 0)),
                  pl.BlockSpec((d, tn), lambda j: (0, j)),
                  pl.BlockSpec((1, tn), lambda j: (0, j))],
        out_specs=pl.BlockSpec((rows, tn), lambda j: (0, j)),
        out_shape=jax.ShapeDtypeStruct((rows, n), F32),
        compiler_params=_params(1),
        name="ada",
    )(cvecs, w_ada, b_ada.reshape(1, n))


def _ffn_kernel(*refs, first_mod_row, add_pos, final_norm):
    x_ref, mod_ref, gain_ref, wg_ref, wu_ref, wd_ref = refs[:6]
    rest = list(refs[6:])
    prow_ref, pcol_ref = (rest.pop(0), rest.pop(0)) if add_pos else (None, None)
    fn_ref = rest.pop(0) if final_norm else None
    o_ref = rest.pop(0)

    x = x_ref[...]
    if add_pos:
        tm, d = x.shape
        rows = tm // GRID_W
        prow = jnp.broadcast_to(prow_ref[...][:, None, :], (rows, GRID_W, d // 2))
        pcol = jnp.broadcast_to(pcol_ref[...][None, :, :], (rows, GRID_W, d // 2))
        x = x + jnp.concatenate([prow, pcol], axis=-1).reshape(tm, d)
    shift = mod_ref[first_mod_row:first_mod_row + 1, :]
    scale = mod_ref[first_mod_row + 1:first_mod_row + 2, :]
    gate = mod_ref[first_mod_row + 2:first_mod_row + 3, :]
    h = _modulated_rms(x, gain_ref[...], scale, shift).astype(BF16)
    g = _dot(h, wg_ref[...])
    u = _dot(h, wu_ref[...])
    a = (g * _sigmoid(g) * u).astype(BF16)
    y = _dot(a, wd_ref[...])
    xn = x + (0.5 * gate) * y
    if final_norm:
        xn = _rms(xn, fn_ref[...])
    o_ref[...] = xn


def _ffn(x, mod, gain, wg, wu, wd, *, tiles_per_group, first_mod_row, pos=None, final_gain=None):
    n, d = x.shape
    dff = wg.shape[1]
    tm = TOKEN_TILE
    add_pos = pos is not None
    final_norm = final_gain is not None
    in_specs = [pl.BlockSpec((tm, d), lambda i: (i, 0)),
                pl.BlockSpec((None, N_MOD, d), lambda i: (i // tiles_per_group, 0, 0)),
                _resident((1, d)), _resident((d, dff)), _resident((d, dff)), _resident((dff, d))]
    args = [x, mod, gain.reshape(1, d), wg, wu, wd]
    if add_pos:
        pos_row, pos_col = pos
        rows = tm // GRID_W
        row_tiles = pos_row.shape[0] // rows
        assert tm % GRID_W == 0 and rows % 8 == 0 and pos_col.shape[0] == GRID_W
        in_specs += [pl.BlockSpec((rows, d // 2), lambda i: (i % row_tiles, 0)), _resident(pos_col.shape)]
        args += [pos_row, pos_col]
    if final_norm:
        in_specs.append(_resident((1, d)))
        args.append(final_gain.reshape(1, d))
    return pl.pallas_call(
        functools.partial(_ffn_kernel, first_mod_row=first_mod_row, add_pos=add_pos, final_norm=final_norm),
        grid=(n // tm,),
        in_specs=in_specs,
        out_specs=pl.BlockSpec((tm, d), lambda i: (i, 0)),
        out_shape=jax.ShapeDtypeStruct((n, d), F32),
        compiler_params=_params(1),
        name="ffn",
    )(*args)


_C_F = 0
_C_Q = _C_F + D_FOURIER
_C_K = _C_Q + DK_TOT
_C_V = _C_K + DK_TOT
_C_R = _C_V + DV_TOT
_C_G = _C_R + DV_TOT


def _mixin_kernel(x_ref, mod_ref, gain_ref, w_ref, wt_ref, wal_ref, bal_ref,
                  f_ref, q_ref, k_ref, v_ref, r_ref, g_ref, laf_ref, lab_ref, *scratch, d_model, fft_layout):
    x = x_ref[...]
    shift = mod_ref[3:4, :]
    scale = mod_ref[4:5, :]
    h = _modulated_rms(x, gain_ref[...], scale, shift).astype(BF16)

    def proj(lo, hi):
        return _dot(h, w_ref[:, lo:hi])

    alr = _dot(h, wt_ref[:, 2 * d_model:2 * d_model + ALR_PAD]).astype(BF16)
    pre = _dot(alr, wal_ref[...]) + bal_ref[...]
    la = (jnp.minimum(pre, 0.0) * (LOG2_E / GATE_TEMP)
          - jnp.log2(1.0 + jnp.exp2(jnp.abs(pre) * (-LOG2_E))) * (1.0 / GATE_TEMP))
    laf_ref[...] = la[:, :DK_TOT]
    lab_ref[...] = la[:, DK_TOT:]

    f = proj(_C_F, _C_Q)
    if fft_layout:
        slab_ref, = scratch
        rows = x.shape[0] // FFT_N
        for a in range(rows):
            for g in range(D_FOURIER // LANES):
                slab_ref[g, pl.ds(a, FFT_N, stride=rows), :] = f[a * FFT_N:(a + 1) * FFT_N, g * LANES:(g + 1) * LANES]
        for g in range(D_FOURIER // LANES):
            f_ref[:, :, g * LANES:(g + 1) * LANES] = slab_ref[g].reshape(FFT_N, rows, LANES)
    else:
        f_ref[...] = f
    q_ref[...] = proj(_C_Q, _C_K) * (DK ** -0.5)
    k_ref[...] = proj(_C_K, _C_V)
    v_ref[...] = proj(_C_V, _C_R).astype(BF16)
    r_ref[...] = proj(_C_R, _C_G).astype(BF16)
    g_ref[...] = _dot(h, wt_ref[:, :2 * d_model]).astype(BF16)


def _mixin(x, mod, gain, w_main, w_tail, w_alpha, b_alpha, *, tiles_per_group, fft_layout):
    n, d = x.shape
    tm = TOKEN_TILE

    def tok(width):
        return pl.BlockSpec((tm, width), lambda i: (i, 0))

    outs = [(D_FOURIER, F32), (DK_TOT, F32), (DK_TOT, F32), (DV_TOT, BF16), (DV_TOT, BF16),
            (2 * d, BF16), (DK_TOT, F32), (DK_TOT, F32)]
    out_specs = [tok(w) for w, _ in outs]
    out_shape = [jax.ShapeDtypeStruct((n, w), dt) for w, dt in outs]
    scratch = []
    if fft_layout:
        rows = tm // FFT_N
        tiles = FFT_N // rows
        assert tm % FFT_N == 0 and rows % 8 == 0 and n % (FFT_N * FFT_N) == 0
        out_specs[0] = pl.BlockSpec((None, FFT_N, rows, D_FOURIER), lambda i: (i // tiles, 0, i % tiles, 0))
        out_shape[0] = jax.ShapeDtypeStruct((n // (FFT_N * FFT_N), FFT_N, FFT_N, D_FOURIER), F32)
        scratch = [pltpu.VMEM((D_FOURIER // LANES, tm, LANES), F32)]
    return pl.pallas_call(
        functools.partial(_mixin_kernel, d_model=d, fft_layout=fft_layout),
        grid=(n // tm,),
        in_specs=[tok(d),
                  pl.BlockSpec((None, N_MOD, d), lambda i: (i // tiles_per_group, 0, 0)),
                  _resident((1, d)), _resident(w_main.shape), _resident(w_tail.shape),
                  _resident((ALR_PAD, 2 * DK_TOT)), _resident((1, 2 * DK_TOT))],
        out_specs=out_specs,
        out_shape=out_shape,
        scratch_shapes=scratch,
        compiler_params=_params(1),
        name="mixin",
    )(x, mod, gain.reshape(1, d), w_main, w_tail, w_alpha, b_alpha)


def _dft_cos_sin(n):
    k = np.arange(n)
    ang = 2.0 * np.pi * ((k[:, None] * k[None, :]) % n) / n
    return np.cos(ang), np.sin(ang)


def _split_hi_lo(x):
    hi = x.astype(BF16)
    return hi, (x - hi.astype(F32)).astype(BF16)


def _table_pieces(table, axis):
    hi, lo = _split_hi_lo(jnp.asarray(table.astype(np.float32)))
    return jnp.concatenate([hi, hi], axis=axis), lo


def _dot_data_table(x, cat_ref, lo_ref):
    hi, lo = _split_hi_lo(x)
    return _dot(jnp.concatenate([hi, lo], axis=1), cat_ref[...]) + _dot(hi, lo_ref[...])


def _dot_table_data(cat, lo_t, x):
    hi, lo = _split_hi_lo(x)
    return _dot(cat, jnp.concatenate([hi, lo], axis=0)) + _dot(lo_t, hi)


def _width_dft_table(t_len):
    c, s = _dft_cos_sin(GROUP_W)
    return np.concatenate([c, -s], axis=1) / math.sqrt(t_len * GROUP_W)


def _width_dft(x, cwc_ref, cwl_ref):
    zr, zi = [], []
    for g in range(N_GROUPS):
        z = _dot_data_table(x[:, g * GROUP_W:(g + 1) * GROUP_W], cwc_ref, cwl_ref)
        zr.append(z[:, :GROUP_W])
        zi.append(z[:, GROUP_W:])
    return jnp.concatenate(zr, axis=1), jnp.concatenate(zi, axis=1)


def _fourier_dense_kernel(x_ref, cwc_ref, cwl_ref, ftc_ref, ftl_ref, o_ref):
    zr, zi = _width_dft(x_ref[...], cwc_ref, cwl_ref)
    zz = jnp.concatenate([zr, zi], axis=0)
    o_ref[...] = _dot_table_data(ftc_ref[...], ftl_ref[...], zz)


def _fourier_dense(f, nb, t_len):
    c, s = _dft_cos_sin(t_len)
    consts = [*_table_pieces(_width_dft_table(t_len), 0), *_table_pieces(np.concatenate([c, s], axis=1), 1)]
    return pl.pallas_call(
        _fourier_dense_kernel,
        grid=(nb,),
        in_specs=[pl.BlockSpec((t_len, D_FOURIER), lambda b: (b, 0))] + [_resident(a.shape) for a in consts],
        out_specs=pl.BlockSpec((t_len, D_FOURIER), lambda b: (b, 0)),
        out_shape=jax.ShapeDtypeStruct(f.shape, F32),
        compiler_params=_params(1),
        name="fourier_dense",
    )(f, *consts)


FFT_CHANNELS = 2 * GROUP_W


def _fourier_two_stage_kernel(x_ref, cwc_ref, cwl_ref, mc_ref, ml_ref, f2c_ref, f2l_ref, o_ref,
                              ar_ref, ai_ref, sr_ref, si_ref):
    groups = FFT_CHANNELS // GROUP_W

    def stage1(jb, carry):
        t2_0 = pl.multiple_of(jb * FFT_ROWS, FFT_ROWS)
        x = x_ref[pl.ds(t2_0, FFT_ROWS)].reshape(FFT_ROWS * FFT_N, FFT_CHANNELS)
        zr, zi = [], []
        for g in range(groups):
            z = _dot_data_table(x[:, g * GROUP_W:(g + 1) * GROUP_W], cwc_ref, cwl_ref)
            zr.append(z[:, :GROUP_W])
            zi.append(z[:, GROUP_W:])
        zr = jnp.concatenate(zr, axis=1)
        zi = jnp.concatenate(zi, axis=1)
        for j in range(FFT_ROWS):
            rows = slice(j * FFT_N, (j + 1) * FFT_N)
            a = _dot_table_data(mc_ref[t2_0 + j], ml_ref[t2_0 + j], jnp.concatenate([zr[rows], zi[rows]], axis=0))
            for g in range(FFT_CHANNELS // LANES):
                lanes = slice(g * LANES, (g + 1) * LANES)
                sr_ref[g, pl.ds(j, FFT_N, stride=FFT_ROWS), :] = a[:FFT_N, lanes]
                si_ref[g, pl.ds(j, FFT_N, stride=FFT_ROWS), :] = a[FFT_N:, lanes]
        for g in range(FFT_CHANNELS // LANES):
            lanes = slice(g * LANES, (g + 1) * LANES)
            ar_ref[:, pl.ds(t2_0, FFT_ROWS), lanes] = sr_ref[g].reshape(FFT_N, FFT_ROWS, LANES)
            ai_ref[:, pl.ds(t2_0, FFT_ROWS), lanes] = si_ref[g].reshape(FFT_N, FFT_ROWS, LANES)
        return carry

    def stage2(jb, carry):
        for j in range(FFT_ROWS):
            s1 = jb * FFT_ROWS + j
            aa = jnp.concatenate([ar_ref[s1], ai_ref[s1]], axis=0)
            o_ref[s1] = _dot_table_data(f2c_ref[...], f2l_ref[...], aa)
        return carry

    lax.fori_loop(0, FFT_N // FFT_ROWS, stage1, 0)
    lax.fori_loop(0, FFT_N // FFT_ROWS, stage2, 0)


def _fourier_two_stage(x, nb):
    n = FFT_N
    t_len = n * n
    s1 = np.arange(n)[:, None]
    t1 = np.arange(n)[None, :]
    mats = []
    for t2 in range(n):
        ang = 2.0 * np.pi * ((s1 * t1 * n + s1 * t2) % t_len) / t_len
        c, s = np.cos(ang), np.sin(ang)
        mats.append(np.block([[c, s], [-s, c]]))
    m_cat, m_lo = _table_pieces(np.stack(mats), 2)
    c2, s2 = _dft_cos_sin(n)
    f2 = _table_pieces(np.concatenate([c2, s2], axis=1), 1)
    cw = _table_pieces(_width_dft_table(t_len), 0)
    consts = [*cw, m_cat, m_lo, *f2]

    blk = pl.BlockSpec((None, n, n, FFT_CHANNELS), lambda b, h: (b, 0, 0, h))
    plane = pltpu.VMEM((n, n, FFT_CHANNELS), F32)
    slabs = pltpu.VMEM((FFT_CHANNELS // LANES, n * FFT_ROWS, LANES), F32)
    return pl.pallas_call(
        _fourier_two_stage_kernel,
        grid=(nb, D_FOURIER // FFT_CHANNELS),
        in_specs=[blk] + [_resident(a.shape) for a in consts],
        out_specs=blk,
        out_shape=jax.ShapeDtypeStruct((nb, n, n, D_FOURIER), F32),
        scratch_shapes=[plane, plane, slabs, slabs],
        compiler_params=_params(2),
        name="fourier_two_stage",
    )(x, *consts)


def _gla_masks(chunk):
    idx = np.arange(chunk)
    i = idx[:, None]
    j = idx[None, :]
    masks = [i == j]
    for l in range(GLA_LEVELS):
        same = (i >> (l + 1)) == (j >> (l + 1))
        masks.append(same & (((i >> l) & 1) == 1) & (((j >> l) & 1) == 0))
    mask_f = np.stack(masks).astype(np.float32)
    return mask_f, np.transpose(mask_f, (0, 2, 1)).copy()


_NT = (((1,), (1,)), ((), ()))
_TN = (((0,), (0,)), ((), ()))
SUBLANES = 8


def _block_sums(la, forward):
    c = la.shape[0]
    nv = c // SUBLANES
    x = la.reshape(nv, SUBLANES, DK)
    sub = lax.broadcasted_iota(jnp.int32, (1, SUBLANES, DK), 1)
    near = x
    tot = x
    out = {}
    w = 1
    while w < SUBLANES:
        right = (sub & w) != 0
        partner = jnp.where(right, pltpu.roll(tot, w, axis=1), pltpu.roll(tot, SUBLANES - w, axis=1))
        near = near + jnp.where(right if forward else jnp.logical_not(right), partner, 0.0)
        tot = tot + partner
        w *= 2
        out[w] = (near, tot - near)
    blk = 1
    while w < c:
        takes = (lambda v: v & blk) if forward else (lambda v: not v & blk)
        near = jnp.stack([near[v] + tot[v ^ blk] if takes(v) else near[v] for v in range(nv)])
        tot = jnp.stack([tot[v] + tot[v ^ blk] for v in range(nv)])
        w *= 2
        blk *= 2
        out[w] = (near, tot - near)
    return {w: (a.reshape(c, DK), b.reshape(c, DK)) for w, (a, b) in out.items()}


def _gla_prepare(q_ref, k_ref, laf_ref, lab_ref, qs_ref, ks_ref, tot_ref, rows_f, rows_b):
    c = GLA_CHUNK
    for g in range(len(rows_f)):
        for fwd in (True, False):
            chain = 2 * g + (0 if fwd else 1)
            rows = pl.ds(rows_f[g] if fwd else rows_b[g], c)
            q = q_ref[rows, :]
            k = k_ref[rows, :]
            la = (laf_ref if fwd else lab_ref)[rows, :]
            sums = _block_sums(la, fwd)
            qs_ref[chain, 0:c, :] = q.astype(BF16)
            ks_ref[chain, 0:c, :] = k.astype(BF16)
            for l in range(GLA_LEVELS + 1):
                if l == 0:
                    d, e = jnp.exp2(la), None
                else:
                    near, far = sums[1 << l]
                    d, e = jnp.exp2(near), jnp.exp2(far)
                qs_ref[chain, (l + 1) * c:(l + 2) * c, :] = (q * d).astype(BF16)
                if e is not None:
                    ks_ref[chain, l * c:(l + 1) * c, :] = (k * e).astype(BF16)
            tot_ref[chain] = jnp.broadcast_to(d[c - 1:c, :] if fwd else d[0:1, :], (SUBLANES, DK))


def _gla_apply(v_ref, maskf_ref, maskb_ref, sf_ref, sb_ref, o_ref, work):
    c = GLA_CHUNK
    levels = GLA_LEVELS
    chains = []
    for qs_ref, ks_ref, tot_ref, rows_f, rows_b, accumulate in work:
        for chain in range(2 * len(rows_f)):
            fwd = chain % 2 == 0
            rows = pl.ds((rows_f if fwd else rows_b)[chain // 2], c)
            chains.append((qs_ref, ks_ref, tot_ref, chain, fwd, rows, accumulate))
    scores = []
    for qs_ref, ks_ref, tot_ref, chain, fwd, rows, accumulate in chains:
        m_ref = maskf_ref if fwd else maskb_ref
        p = lax.dot_general(qs_ref[chain, 0:2 * c, :], ks_ref[chain, 0:c, :], _NT, preferred_element_type=F32)
        s = p[:c] * m_ref[0] + p[c:] * m_ref[1]
        for l in range(1, levels):
            p = lax.dot_general(qs_ref[chain, (l + 1) * c:(l + 2) * c, :], ks_ref[chain, l * c:(l + 1) * c, :], _NT,
                                preferred_element_type=F32)
            s = s + p * m_ref[l + 1]
        scores.append(s.astype(BF16))
    for (qs_ref, ks_ref, tot_ref, chain, fwd, rows, accumulate), s in zip(chains, scores):
        st_ref = sf_ref if fwd else sb_ref
        v = v_ref[rows, :]
        st = st_ref[...]
        q_top = qs_ref[chain, (levels + 1) * c:(levels + 2) * c, :]
        k_top = ks_ref[chain, levels * c:(levels + 1) * c, :]
        sv = _dot(jnp.concatenate([s, k_top.T], axis=0), v)
        o = sv[:c] + _dot(q_top, st.astype(BF16))
        tot = jnp.broadcast_to(tot_ref[chain][0:1, :], (DK, DK)).T
        st_ref[...] = st * jnp.concatenate([tot] * (DV // DK), axis=1) + sv[c:]
        if accumulate:
            o_ref[rows, :] += o
        else:
            o_ref[rows, :] = o


def _gla_kernel(q_ref, k_ref, v_ref, laf_ref, lab_ref, s0f_ref, s0b_ref, maskf_ref, maskb_ref,
                o_ref, sf_ref, sb_ref, qs0_ref, ks0_ref, tot0_ref, qs1_ref, ks1_ref, tot1_ref,
                *, n_chunks, group):
    sf_ref[...] = s0f_ref[...]
    sb_ref[...] = s0b_ref[...]
    steps = n_chunks // group
    slots = ((qs0_ref, ks0_ref, tot0_ref), (qs1_ref, ks1_ref, tot1_ref))

    def rows_of(i):
        rows_f = [pl.multiple_of((i * group + g) * GLA_CHUNK, GLA_CHUNK) for g in range(group)]
        rows_b = [pl.multiple_of((n_chunks - 1 - i * group - g) * GLA_CHUNK, GLA_CHUNK) for g in range(group)]
        return rows_f, rows_b

    def prepare(i, slot):
        _gla_prepare(q_ref, k_ref, laf_ref, lab_ref, *slots[slot], *rows_of(i))

    def apply(items):
        _gla_apply(v_ref, maskf_ref, maskb_ref, sf_ref, sb_ref, o_ref,
                   [(*slots[slot], *rows_of(i), accumulate) for i, slot, accumulate in items])

    def pair(j, accumulate, last):
        apply([(2 * j, 0, accumulate[0]), (2 * j + 1, 1, accumulate[1])])
        if not last:
            prepare(2 * j + 2, 0)
            prepare(2 * j + 3, 1)

    n_pairs = steps // 2
    prepare(0, 0)
    prepare(1, 1)
    if n_pairs == 1:
        pair(0, (False, True), True)
    else:
        def body(accumulate):
            def f(j, carry):
                pair(j, (accumulate, accumulate), False)
                return carry
            return f
        lax.fori_loop(0, n_pairs // 2, body(False), 0)
        lax.fori_loop(n_pairs // 2, n_pairs - 1, body(True), 0)
        pair(n_pairs - 1, (True, True), True)


def _gla(q, k, v, laf, lab, s0f, s0b, nb, t_len):
    group = GLA_GROUP
    n_chunks = t_len // GLA_CHUNK
    steps = n_chunks // group
    assert t_len % GLA_CHUNK == 0 and n_chunks % group == 0 and (steps == 2 or steps % 4 == 0)
    consts = [jnp.asarray(m) for m in _gla_masks(GLA_CHUNK)]

    def seq(width):
        return pl.BlockSpec((t_len, width), lambda b, h: (b, h))

    state = pl.BlockSpec((None, None, DK, DV), lambda b, h: (b, h, 0, 0))
    n = nb * t_len
    slot = [pltpu.VMEM((2 * group, (GLA_LEVELS + 2) * GLA_CHUNK, DK), BF16),
            pltpu.VMEM((2 * group, (GLA_LEVELS + 1) * GLA_CHUNK, DK), BF16),
            pltpu.VMEM((2 * group, 8, DK), F32)]
    return pl.pallas_call(
        functools.partial(_gla_kernel, n_chunks=n_chunks, group=group),
        grid=(nb, N_HEADS),
        in_specs=[seq(DK), seq(DK), seq(DV), seq(DK), seq(DK), state, state]
                 + [_resident(a.shape) for a in consts],
        out_specs=[seq(DV), state, state],
        out_shape=[jax.ShapeDtypeStruct((n, DV_TOT), F32),
                   jax.ShapeDtypeStruct((nb, N_HEADS, DK, DV), F32),
                   jax.ShapeDtypeStruct((nb, N_HEADS, DK, DV), F32)],
        scratch_shapes=slot + slot,
        compiler_params=_params(2),
        name="gla",
    )(q, k, v, laf, lab, s0f, s0b, *consts)


def _mixout_kernel(x_ref, mod_ref, m_ref, o_ref, r_ref, g_ref, gn_ref, wpf_ref, wpg_ref, wout_ref, y_ref,
                   *scratch, d_model, fft_layout):
    if fft_layout:
        slab_ref, = scratch
        rows = m_ref.shape[1]
        for g in range(D_FOURIER // LANES):
            slab_ref[g] = m_ref[:, :, g * LANES:(g + 1) * LANES].reshape(FFT_N * rows, LANES)
        m = jnp.concatenate(
            [jnp.concatenate([slab_ref[g, pl.ds(s, FFT_N, stride=rows), :] for g in range(D_FOURIER // LANES)],
                             axis=1) for s in range(rows)], axis=0)
    else:
        m = m_ref[...]
    branch_a = _dot(m.astype(BF16), wpf_ref[...])
    o = o_ref[...]
    parts = []
    for h in range(N_HEADS):
        oh = o[:, h * DV:(h + 1) * DV]
        parts.append(oh * lax.rsqrt(jnp.mean(oh * oh, axis=-1, keepdims=True) + RMS_EPS))
    r = r_ref[...].astype(F32)
    on = jnp.concatenate(parts, axis=1) * (gn_ref[...] * r * _sigmoid(r))
    branch_b = _dot(on.astype(BF16), wpg_ref[...])
    g = _sigmoid(g_ref[...].astype(F32))
    merged = g[:, :d_model] * branch_a + g[:, d_model:] * branch_b
    y = _dot(merged.astype(BF16), wout_ref[...])
    y_ref[...] = x_ref[...] + mod_ref[5:6, :] * y


def _mixout(x, mod, m, o, r, g, gla_norm, wpf, wpg, wout, *, tiles_per_group, fft_layout):
    n, d = x.shape
    tm = TOKEN_TILE

    def tok(width):
        return pl.BlockSpec((tm, width), lambda i: (i, 0))

    m_spec = tok(D_FOURIER)
    scratch = []
    if fft_layout:
        rows = tm // FFT_N
        tiles = FFT_N // rows
        m_spec = pl.BlockSpec((None, FFT_N, rows, D_FOURIER), lambda i: (i // tiles, 0, i % tiles, 0))
        scratch = [pltpu.VMEM((D_FOURIER // LANES, tm, LANES), F32)]
    return pl.pallas_call(
        functools.partial(_mixout_kernel, d_model=d, fft_layout=fft_layout),
        grid=(n // tm,),
        in_specs=[tok(d),
                  pl.BlockSpec((None, N_MOD, d), lambda i: (i // tiles_per_group, 0, 0)),
                  m_spec, tok(DV_TOT), tok(DV_TOT), tok(2 * d),
                  _resident((1, DV_TOT)), _resident(wpf.shape), _resident(wpg.shape), _resident(wout.shape)],
        out_specs=tok(d),
        out_shape=jax.ShapeDtypeStruct((n, d), F32),
        scratch_shapes=scratch,
        compiler_params=_params(1),
        name="mixout",
    )(x, mod, m, o, r, g, gla_norm.reshape(1, DV_TOT), wpf, wpg, wout)


def _grid_pos_tables(n_tokens, d_model):
    n_freq = d_model // 4
    omega = POS_BASE ** (-jnp.arange(n_freq, dtype=F32) / n_freq)
    ra = jnp.arange(n_tokens // GRID_W, dtype=F32)[:, None] * omega
    ca = jnp.arange(GRID_W, dtype=F32)[:, None] * omega
    return (jnp.concatenate([jnp.sin(ra), jnp.cos(ra)], axis=-1),
            jnp.concatenate([jnp.sin(ca), jnp.cos(ca)], axis=-1))


def _trunk_layer(x, mod, s0f, s0b, nb, t_len, w, *, pos, final_gain, two_stage_fft):
    tiles = max(t_len // TOKEN_TILE, 1) if mod.shape[0] > 1 else x.shape[0] // TOKEN_TILE
    x = _ffn(x, mod, w["norm_ffn1"], w["ffn1_gate"], w["ffn1_up"], w["ffn1_down"],
             tiles_per_group=tiles, first_mod_row=0, pos=pos)
    f, q, k, v, r, g, laf, lab = _mixin(x, mod, w["norm_mix"], w["w_main"], w["w_tail"], w["w_alpha"],
                                        w["b_alpha"],
                                        tiles_per_group=tiles, fft_layout=two_stage_fft)
    m = _fourier_two_stage(f, nb) if two_stage_fft else _fourier_dense(f, nb, t_len)
    o, sf, sb = _gla(q, k, v, laf, lab, s0f, s0b, nb, t_len)
    x = _mixout(x, mod, m, o, r, g, w["gla_norm"], w["proj_fourier"], w["proj_gla"], w["w_out"],
                tiles_per_group=tiles, fft_layout=two_stage_fft)
    x = _ffn(x, mod, w["norm_ffn2"], w["ffn2_gate"], w["ffn2_up"], w["ffn2_down"],
             tiles_per_group=tiles, first_mod_row=6, final_gain=final_gain)
    return x, sf, sb


def kernel(x_prompt, x_sample, state_gla_fwd, state_gla_bwd, c, c_ctx, w_ada, b_ada, norm_ffn1, w_ffn1_gate, w_ffn1_up, w_ffn1_down, norm_mix, w_in, w_alpha_fwd, b_alpha_fwd, w_alpha_bwd, b_alpha_bwd, gla_norm, w_proj_fourier, w_proj_gla, w_out, norm_ffn2, w_ffn2_gate, w_ffn2_up, w_ffn2_down, final_norm):
    nb_ctx, t_ctx, d = x_prompt.shape
    nb_lat, t_lat, _ = x_sample.shape
    depth = w_ada.shape[0]
    assert t_lat == FFT_N * FFT_N and (nb_ctx * t_ctx) % TOKEN_TILE == 0 and t_lat % TOKEN_TILE == 0

    xc = x_prompt.reshape(nb_ctx * t_ctx, d)
    xl = x_sample.reshape(nb_lat * t_lat, d)
    pos = _grid_pos_tables(t_lat, d)
    zero_state = jnp.zeros((nb_ctx, N_HEADS, DK, DV), F32)
    pad_rows = (-(nb_lat + 1)) % 8
    cvecs = jnp.concatenate([c, c_ctx[None, :], jnp.zeros((pad_rows, d), F32)], axis=0)

    new_fwd, new_bwd = [], []
    for l in range(depth):
        cut = _C_G
        w_in_l = w_in[l]
        w_main = w_in_l[:, :cut].astype(BF16)
        w_tail = jnp.concatenate(
            [w_in_l[:, cut + 2 * GATE_RANK:].astype(BF16), w_in_l[:, cut:cut + 2 * GATE_RANK].astype(BF16),
             jnp.zeros((d, ALR_PAD - 2 * GATE_RANK), BF16)], axis=1)
        w_alpha = jnp.zeros((ALR_PAD, 2 * DK_TOT), F32)
        w_alpha = w_alpha.at[:GATE_RANK, :DK_TOT].set(w_alpha_fwd[l])
        w_alpha = w_alpha.at[GATE_RANK:2 * GATE_RANK, DK_TOT:].set(w_alpha_bwd[l]).astype(BF16)
        b_alpha = jnp.concatenate([b_alpha_fwd[l], b_alpha_bwd[l]]).reshape(1, 2 * DK_TOT)
        w = {
            "norm_ffn1": norm_ffn1[l], "ffn1_gate": w_ffn1_gate[l].astype(BF16),
            "ffn1_up": w_ffn1_up[l].astype(BF16), "ffn1_down": w_ffn1_down[l].astype(BF16),
            "norm_mix": norm_mix[l], "w_main": w_main, "w_tail": w_tail, "w_alpha": w_alpha, "b_alpha": b_alpha,
            "gla_norm": gla_norm[l], "proj_fourier": w_proj_fourier[l].astype(BF16),
            "proj_gla": w_proj_gla[l].astype(BF16), "w_out": w_out[l].astype(BF16),
            "norm_ffn2": norm_ffn2[l], "ffn2_gate": w_ffn2_gate[l].astype(BF16),
            "ffn2_up": w_ffn2_up[l].astype(BF16), "ffn2_down": w_ffn2_down[l].astype(BF16),
        }
        last = l == depth - 1
        mod = _ada(cvecs, w_ada[l], b_ada[l]).reshape(-1, N_MOD, d)
        mod_lat = mod[:nb_lat]
        mod_ctx = mod[nb_lat:nb_lat + 1]
        xc, sf, sb = _trunk_layer(xc, mod_ctx, zero_state, zero_state, nb_ctx, t_ctx, w,
                                  pos=None, final_gain=final_norm if last else None, two_stage_fft=False)
        new_fwd.append(sf)
        new_bwd.append(sb)
        xl, _, _ = _trunk_layer(xl, mod_lat, state_gla_fwd[:, l], state_gla_bwd[:, l], nb_lat, t_lat, w,
                                pos=pos if l == 0 else None, final_gain=final_norm if last else None,
                                two_stage_fft=True)
    y_prompt = xc.reshape(nb_ctx, t_ctx, d)
    y_sample = xl.reshape(nb_lat, t_lat, d)
    return (y_prompt, y_sample, jnp.stack(new_fwd, axis=1), jnp.stack(new_bwd, axis=1))
```

```python
import functools
import math

import numpy as np
import jax
import jax.numpy as jnp
from jax import lax
from jax.experimental import pallas as pl
from jax.experimental.pallas import tpu as pltpu

F32 = jnp.float32
BF16 = jnp.bfloat16

GRID_W = 64
N_GROUPS = 4
GROUP_W = 128
D_FOURIER = N_GROUPS * GROUP_W
N_HEADS = 4
DK = 128
DV = 256
DK_TOT = N_HEADS * DK
DV_TOT = N_HEADS * DV
GATE_RANK = 16
GATE_TEMP = 16.0
RMS_EPS = 1e-6
POS_BASE = 10000.0
N_MOD = 9

TOKEN_TILE = 512
GLA_CHUNK = 64
GLA_LEVELS = 6
GLA_GROUP = 2
LANES = 128
ALR_PAD = LANES
FFT_N = 64
FFT_ROWS = 8
VMEM_LIMIT = 56 * 1024 * 1024


def _params(n_axes):
    return pltpu.CompilerParams(dimension_semantics=("arbitrary",) * n_axes,
                                vmem_limit_bytes=VMEM_LIMIT)


def _resident(shape):
    nd = len(shape)
    return pl.BlockSpec(shape, lambda *_: (0,) * nd, pipeline_mode=pl.Buffered(1))


LOG2_E = math.log2(math.e)


def _sigmoid(x):
    return 1.0 / (1.0 + jnp.exp2(x * (-LOG2_E)))


def _rms(x, gain):
    return x * lax.rsqrt(jnp.mean(x * x, axis=-1, keepdims=True) + RMS_EPS) * gain


def _modulated_rms(x, gain, scale, shift):
    return x * lax.rsqrt(jnp.mean(x * x, axis=-1, keepdims=True) + RMS_EPS) * (gain * (1.0 + scale)) + shift


def _dot(a, b):
    return jnp.dot(a, b, preferred_element_type=F32)


def _ada_kernel(c_ref, w_ref, b_ref, o_ref):
    c = c_ref[...]
    s = (c * _sigmoid(c)).astype(BF16)
    o_ref[...] = _dot(s, w_ref[...].astype(BF16)) + b_ref[...]


def _ada(cvecs, w_ada, b_ada):
    rows, d = cvecs.shape
    n = w_ada.shape[1]
    tn = d
    return pl.pallas_call(
        _ada_kernel,
        grid=(n // tn,),
        in_specs=[pl.BlockSpec((rows, d), lambda j: (0, 0)),
                  pl.BlockSpec((d, tn), lambda j: (0, j)),
                  pl.BlockSpec((1, tn), lambda j: (0, j))],
        out_specs=pl.BlockSpec((rows, tn), lambda j: (0, j)),
        out_shape=jax.ShapeDtypeStruct((rows, n), F32),
        compiler_params=_params(1),
        name="ada",
    )(cvecs, w_ada, b_ada.reshape(1, n))


def _ffn_kernel(*refs, first_mod_row, add_pos, final_norm):
    x_ref, mod_ref, gain_ref, wg_ref, wu_ref, wd_ref = refs[:6]
    rest = list(refs[6:])
    prow_ref, pcol_ref = (rest.pop(0), rest.pop(0)) if add_pos else (None, None)
    fn_ref = rest.pop(0) if final_norm else None
    o_ref = rest.pop(0)

    x = x_ref[...]
    if add_pos:
        tm, d = x.shape
        rows = tm // GRID_W
        prow = jnp.broadcast_to(prow_ref[...][:, None, :], (rows, GRID_W, d // 2))
        pcol = jnp.broadcast_to(pcol_ref[...][None, :, :], (rows, GRID_W, d // 2))
        x = x + jnp.concatenate([prow, pcol], axis=-1).reshape(tm, d)
    shift = mod_ref[first_mod_row:first_mod_row + 1, :]
    scale = mod_ref[first_mod_row + 1:first_mod_row + 2, :]
    gate = mod_ref[first_mod_row + 2:first_mod_row + 3, :]
    h = _modulated_rms(x, gain_ref[...], scale, shift).astype(BF16)
    g = _dot(h, wg_ref[...])
    u = _dot(h, wu_ref[...])
    a = (g * _sigmoid(g) * u).astype(BF16)
    y = _dot(a, wd_ref[...])
    xn = x + (0.5 * gate) * y
    if final_norm:
        xn = _rms(xn, fn_ref[...])
    o_ref[...] = xn


def _ffn(x, mod, gain, wg, wu, wd, *, tiles_per_group, first_mod_row, pos=None, final_gain=None):
    n, d = x.shape
    dff = wg.shape[1]
    tm = TOKEN_TILE
    add_pos = pos is not None
    final_norm = final_gain is not None
    in_specs = [pl.BlockSpec((tm, d), lambda i: (i, 0)),
                pl.BlockSpec((None, N_MOD, d), lambda i: (i // tiles_per_group, 0, 0)),
                _resident((1, d)), _resident((d, dff)), _resident((d, dff)), _resident((dff, d))]
    args = [x, mod, gain.reshape(1, d), wg, wu, wd]
    if add_pos:
        pos_row, pos_col = pos
        rows = tm // GRID_W
        row_tiles = pos_row.shape[0] // rows
        assert tm % GRID_W == 0 and rows % 8 == 0 and pos_col.shape[0] == GRID_W
        in_specs += [pl.BlockSpec((rows, d // 2), lambda i: (i % row_tiles, 0)), _resident(pos_col.shape)]
        args += [pos_row, pos_col]
    if final_norm:
        in_specs.append(_resident((1, d)))
        args.append(final_gain.reshape(1, d))
    return pl.pallas_call(
        functools.partial(_ffn_kernel, first_mod_row=first_mod_row, add_pos=add_pos, final_norm=final_norm),
        grid=(n // tm,),
        in_specs=in_specs,
        out_specs=pl.BlockSpec((tm, d), lambda i: (i, 0)),
        out_shape=jax.ShapeDtypeStruct((n, d), F32),
        compiler_params=_params(1),
        name="ffn",
    )(*args)


_C_F = 0
_C_Q = _C_F + D_FOURIER
_C_K = _C_Q + DK_TOT
_C_V = _C_K + DK_TOT
_C_R = _C_V + DV_TOT
_C_G = _C_R + DV_TOT


def _mixin_kernel(x_ref, mod_ref, gain_ref, w_ref, wt_ref, wal_ref, bal_ref,
                  f_ref, q_ref, k_ref, v_ref, r_ref, g_ref, laf_ref, lab_ref, *scratch, d_model, fft_layout):
    x = x_ref[...]
    shift = mod_ref[3:4, :]
    scale = mod_ref[4:5, :]
    h = _modulated_rms(x, gain_ref[...], scale, shift).astype(BF16)

    def proj(lo, hi):
        return _dot(h, w_ref[:, lo:hi])

    alr = _dot(h, wt_ref[:, 2 * d_model:2 * d_model + ALR_PAD]).astype(BF16)
    pre = _dot(alr, wal_ref[...]) + bal_ref[...]
    la = (jnp.minimum(pre, 0.0) * (LOG2_E / GATE_TEMP)
          - jnp.log2(1.0 + jnp.exp2(jnp.abs(pre) * (-LOG2_E))) * (1.0 / GATE_TEMP))
    laf_ref[...] = la[:, :DK_TOT]
    lab_ref[...] = la[:, DK_TOT:]

    g_ref[...] = _dot(h, wt_ref[:, :2 * d_model]).astype(BF16)
    r_ref[...] = proj(_C_R, _C_G).astype(BF16)
    v_ref[...] = proj(_C_V, _C_R).astype(BF16)
    f = proj(_C_F, _C_Q)
    if fft_layout:
        slab_ref, = scratch
        rows = x.shape[0] // FFT_N
        for a in range(rows):
            for g in range(D_FOURIER // LANES):
                slab_ref[g, pl.ds(a, FFT_N, stride=rows), :] = f[a * FFT_N:(a + 1) * FFT_N, g * LANES:(g + 1) * LANES]
        for g in range(D_FOURIER // LANES):
            f_ref[:, :, g * LANES:(g + 1) * LANES] = slab_ref[g].reshape(FFT_N, rows, LANES)
    else:
        f_ref[...] = f
    q_ref[...] = proj(_C_Q, _C_K) * (DK ** -0.5)
    k_ref[...] = proj(_C_K, _C_V)


def _mixin(x, mod, gain, w_main, w_tail, w_alpha, b_alpha, *, tiles_per_group, fft_layout):
    n, d = x.shape
    tm = TOKEN_TILE

    def tok(width):
        return pl.BlockSpec((tm, width), lambda i: (i, 0))

    outs = [(D_FOURIER, F32), (DK_TOT, F32), (DK_TOT, F32), (DV_TOT, BF16), (DV_TOT, BF16),
            (2 * d, BF16), (DK_TOT, F32), (DK_TOT, F32)]
    out_specs = [tok(w) for w, _ in outs]
    out_shape = [jax.ShapeDtypeStruct((n, w), dt) for w, dt in outs]
    scratch = []
    if fft_layout:
        rows = tm // FFT_N
        tiles = FFT_N // rows
        assert tm % FFT_N == 0 and rows % 8 == 0 and n % (FFT_N * FFT_N) == 0
        out_specs[0] = pl.BlockSpec((None, FFT_N, rows, D_FOURIER), lambda i: (i // tiles, 0, i % tiles, 0))
        out_shape[0] = jax.ShapeDtypeStruct((n // (FFT_N * FFT_N), FFT_N, FFT_N, D_FOURIER), F32)
        scratch = [pltpu.VMEM((D_FOURIER // LANES, tm, LANES), F32)]
    return pl.pallas_call(
        functools.partial(_mixin_kernel, d_model=d, fft_layout=fft_layout),
        grid=(n // tm,),
        in_specs=[tok(d),
                  pl.BlockSpec((None, N_MOD, d), lambda i: (i // tiles_per_group, 0, 0)),
                  _resident((1, d)), _resident(w_main.shape), _resident(w_tail.shape),
                  _resident((ALR_PAD, 2 * DK_TOT)), _resident((1, 2 * DK_TOT))],
        out_specs=out_specs,
        out_shape=out_shape,
        scratch_shapes=scratch,
        compiler_params=_params(1),
        name="mixin",
    )(x, mod, gain.reshape(1, d), w_main, w_tail, w_alpha, b_alpha)


def _dft_cos_sin(n):
    k = np.arange(n)
    ang = 2.0 * np.pi * ((k[:, None] * k[None, :]) % n) / n
    return np.cos(ang), np.sin(ang)


def _split_hi_lo(x):
    hi = x.astype(BF16)
    return hi, (x - hi.astype(F32)).astype(BF16)


def _table_pieces(table, axis):
    hi, lo = _split_hi_lo(jnp.asarray(table.astype(np.float32)))
    return jnp.concatenate([hi, hi], axis=axis), lo


def _dot_data_table(x, cat_ref, lo_ref):
    hi, lo = _split_hi_lo(x)
    return _dot(jnp.concatenate([hi, lo], axis=1), cat_ref[...]) + _dot(hi, lo_ref[...])


def _dot_table_data(cat, lo_t, x):
    hi, lo = _split_hi_lo(x)
    return _dot(cat, jnp.concatenate([hi, lo], axis=0)) + _dot(lo_t, hi)


def _width_dft_table(t_len):
    c, s = _dft_cos_sin(GROUP_W)
    return np.concatenate([c, -s], axis=1) / math.sqrt(t_len * GROUP_W)


def _width_dft(x, cwc_ref, cwl_ref):
    zr, zi = [], []
    for g in range(N_GROUPS):
        z = _dot_data_table(x[:, g * GROUP_W:(g + 1) * GROUP_W], cwc_ref, cwl_ref)
        zr.append(z[:, :GROUP_W])
        zi.append(z[:, GROUP_W:])
    return jnp.concatenate(zr, axis=1), jnp.concatenate(zi, axis=1)


def _fourier_dense_kernel(x_ref, cwc_ref, cwl_ref, ftc_ref, ftl_ref, o_ref):
    zr, zi = _width_dft(x_ref[...], cwc_ref, cwl_ref)
    zz = jnp.concatenate([zr, zi], axis=0)
    o_ref[...] = _dot_table_data(ftc_ref[...], ftl_ref[...], zz)


def _fourier_dense(f, nb, t_len):
    c, s = _dft_cos_sin(t_len)
    consts = [*_table_pieces(_width_dft_table(t_len), 0), *_table_pieces(np.concatenate([c, s], axis=1), 1)]
    return pl.pallas_call(
        _fourier_dense_kernel,
        grid=(nb,),
        in_specs=[pl.BlockSpec((t_len, D_FOURIER), lambda b: (b, 0))] + [_resident(a.shape) for a in consts],
        out_specs=pl.BlockSpec((t_len, D_FOURIER), lambda b: (b, 0)),
        out_shape=jax.ShapeDtypeStruct(f.shape, F32),
        compiler_params=_params(1),
        name="fourier_dense",
    )(f, *consts)


FFT_CHANNELS = 2 * GROUP_W


def _fourier_two_stage_kernel(x_ref, cwc_ref, cwl_ref, mc_ref, ml_ref, f2c_ref, f2l_ref, o_ref,
                              ar_ref, ai_ref, sr_ref, si_ref):
    groups = FFT_CHANNELS // GROUP_W

    def stage1(jb, carry):
        t2_0 = pl.multiple_of(jb * FFT_ROWS, FFT_ROWS)
        x = x_ref[pl.ds(t2_0, FFT_ROWS)].reshape(FFT_ROWS * FFT_N, FFT_CHANNELS)
        zr, zi = [], []
        for g in range(groups):
            z = _dot_data_table(x[:, g * GROUP_W:(g + 1) * GROUP_W], cwc_ref, cwl_ref)
            zr.append(z[:, :GROUP_W])
            zi.append(z[:, GROUP_W:])
        zr = jnp.concatenate(zr, axis=1)
        zi = jnp.concatenate(zi, axis=1)
        for j in range(FFT_ROWS):
            rows = slice(j * FFT_N, (j + 1) * FFT_N)
            a = _dot_table_data(mc_ref[t2_0 + j], ml_ref[t2_0 + j], jnp.concatenate([zr[rows], zi[rows]], axis=0))
            for g in range(FFT_CHANNELS // LANES):
                lanes = slice(g * LANES, (g + 1) * LANES)
                sr_ref[g, pl.ds(j, FFT_N, stride=FFT_ROWS), :] = a[:FFT_N, lanes]
                si_ref[g, pl.ds(j, FFT_N, stride=FFT_ROWS), :] = a[FFT_N:, lanes]
        for g in range(FFT_CHANNELS // LANES):
            lanes = slice(g * LANES, (g + 1) * LANES)
            ar_ref[:, pl.ds(t2_0, FFT_ROWS), lanes] = sr_ref[g].reshape(FFT_N, FFT_ROWS, LANES)
            ai_ref[:, pl.ds(t2_0, FFT_ROWS), lanes] = si_ref[g].reshape(FFT_N, FFT_ROWS, LANES)
        return carry

    def stage2(jb, carry):
        for j in range(FFT_ROWS):
            s1 = jb * FFT_ROWS + j
            aa = jnp.concatenate([ar_ref[s1], ai_ref[s1]], axis=0)
            o_ref[s1] = _dot_table_data(f2c_ref[...], f2l_ref[...], aa)
        return carry

    lax.fori_loop(0, FFT_N // FFT_ROWS, stage1, 0)
    lax.fori_loop(0, FFT_N // FFT_ROWS, stage2, 0)


def _fourier_two_stage(x, nb):
    n = FFT_N
    t_len = n * n
    s1 = np.arange(n)[:, None]
    t1 = np.arange(n)[None, :]
    mats = []
    for t2 in range(n):
        ang = 2.0 * np.pi * ((s1 * t1 * n + s1 * t2) % t_len) / t_len
        c, s = np.cos(ang), np.sin(ang)
        mats.append(np.block([[c, s], [-s, c]]))
    m_cat, m_lo = _table_pieces(np.stack(mats), 2)
    c2, s2 = _dft_cos_sin(n)
    f2 = _table_pieces(np.concatenate([c2, s2], axis=1), 1)
    cw = _table_pieces(_width_dft_table(t_len), 0)
    consts = [*cw, m_cat, m_lo, *f2]

    blk = pl.BlockSpec((None, n, n, FFT_CHANNELS), lambda b, h: (b, 0, 0, h))
    plane = pltpu.VMEM((n, n, FFT_CHANNELS), F32)
    slabs = pltpu.VMEM((FFT_CHANNELS // LANES, n * FFT_ROWS, LANES), F32)
    return pl.pallas_call(
        _fourier_two_stage_kernel,
        grid=(nb, D_FOURIER // FFT_CHANNELS),
        in_specs=[blk] + [_resident(a.shape) for a in consts],
        out_specs=blk,
        out_shape=jax.ShapeDtypeStruct((nb, n, n, D_FOURIER), F32),
        scratch_shapes=[plane, plane, slabs, slabs],
        compiler_params=_params(2),
        name="fourier_two_stage",
    )(x, *consts)


def _gla_masks(chunk):
    idx = np.arange(chunk)
    i = idx[:, None]
    j = idx[None, :]
    masks = [i == j]
    for l in range(GLA_LEVELS):
        same = (i >> (l + 1)) == (j >> (l + 1))
        masks.append(same & (((i >> l) & 1) == 1) & (((j >> l) & 1) == 0))
    mask_f = np.stack(masks).astype(np.float32)
    return mask_f, np.transpose(mask_f, (0, 2, 1)).copy()


_NT = (((1,), (1,)), ((), ()))
_TN = (((0,), (0,)), ((), ()))
SUBLANES = 8


def _block_sums(la, forward):
    c = la.shape[0]
    nv = c // SUBLANES
    x = la.reshape(nv, SUBLANES, DK)
    sub = lax.broadcasted_iota(jnp.int32, (1, SUBLANES, DK), 1)
    near = x
    tot = x
    out = {}
    w = 1
    while w < SUBLANES:
        right = (sub & w) != 0
        partner = jnp.where(right, pltpu.roll(tot, w, axis=1), pltpu.roll(tot, SUBLANES - w, axis=1))
        near = near + jnp.where(right if forward else jnp.logical_not(right), partner, 0.0)
        tot = tot + partner
        w *= 2
        out[w] = (near, tot - near)
    blk = 1
    while w < c:
        takes = (lambda v: v & blk) if forward else (lambda v: not v & blk)
        near = jnp.stack([near[v] + tot[v ^ blk] if takes(v) else near[v] for v in range(nv)])
        tot = jnp.stack([tot[v] + tot[v ^ blk] for v in range(nv)])
        w *= 2
        blk *= 2
        out[w] = (near, tot - near)
    return {w: (a.reshape(c, DK), b.reshape(c, DK)) for w, (a, b) in out.items()}


def _gla_prepare(q_ref, k_ref, laf_ref, lab_ref, qs_ref, ks_ref, tot_ref, rows_f, rows_b):
    c = GLA_CHUNK
    for g in range(len(rows_f)):
        for fwd in (True, False):
            chain = 2 * g + (0 if fwd else 1)
            rows = pl.ds(rows_f[g] if fwd else rows_b[g], c)
            q = q_ref[rows, :]
            k = k_ref[rows, :]
            la = (laf_ref if fwd else lab_ref)[rows, :]
            sums = _block_sums(la, fwd)
            qs_ref[chain, 0:c, :] = q.astype(BF16)
            ks_ref[chain, 0:c, :] = k.astype(BF16)
            for l in range(GLA_LEVELS + 1):
                if l == 0:
                    d, e = jnp.exp2(la), None
                else:
                    near, far = sums[1 << l]
                    d, e = jnp.exp2(near), jnp.exp2(far)
                qs_ref[chain, (l + 1) * c:(l + 2) * c, :] = (q * d).astype(BF16)
                if e is not None:
                    ks_ref[chain, l * c:(l + 1) * c, :] = (k * e).astype(BF16)
            tot_ref[chain] = jnp.broadcast_to(d[c - 1:c, :] if fwd else d[0:1, :], (SUBLANES, DK))


def _gla_apply(v_ref, maskf_ref, maskb_ref, sf_ref, sb_ref, o_ref, work):
    c = GLA_CHUNK
    levels = GLA_LEVELS
    chains = []
    for qs_ref, ks_ref, tot_ref, rows_f, rows_b, accumulate in work:
        for chain in range(2 * len(rows_f)):
            fwd = chain % 2 == 0
            rows = pl.ds((rows_f if fwd else rows_b)[chain // 2], c)
            chains.append((qs_ref, ks_ref, tot_ref, chain, fwd, rows, accumulate))
    scores = []
    for qs_ref, ks_ref, tot_ref, chain, fwd, rows, accumulate in chains:
        m_ref = maskf_ref if fwd else maskb_ref
        s = None
        for l in range(levels + 1):
            kl = max(l - 1, 0)
            p = lax.dot_general(qs_ref[chain, l * c:(l + 1) * c, :], ks_ref[chain, kl * c:(kl + 1) * c, :], _NT,
                                preferred_element_type=F32) * m_ref[l]
            s = p if s is None else s + p
        scores.append(s.astype(BF16))
    for (qs_ref, ks_ref, tot_ref, chain, fwd, rows, accumulate), s in zip(chains, scores):
        st_ref = sf_ref if fwd else sb_ref
        v = v_ref[rows, :]
        st = st_ref[...]
        q_top = qs_ref[chain, (levels + 1) * c:(levels + 2) * c, :]
        k_top = ks_ref[chain, levels * c:(levels + 1) * c, :]
        o = _dot(s, v) + _dot(q_top, st.astype(BF16))
        tot = jnp.broadcast_to(tot_ref[chain][0:1, :], (DK, DK)).T
        st_ref[...] = (st * jnp.concatenate([tot] * (DV // DK), axis=1)
                       + lax.dot_general(k_top, v, _TN, preferred_element_type=F32))
        if accumulate:
            o_ref[rows, :] += o
        else:
            o_ref[rows, :] = o


def _gla_kernel(q_ref, k_ref, v_ref, laf_ref, lab_ref, s0f_ref, s0b_ref, maskf_ref, maskb_ref,
                o_ref, sf_ref, sb_ref, qs0_ref, ks0_ref, tot0_ref, qs1_ref, ks1_ref, tot1_ref,
                *, n_chunks, group):
    sf_ref[...] = s0f_ref[...]
    sb_ref[...] = s0b_ref[...]
    steps = n_chunks // group
    slots = ((qs0_ref, ks0_ref, tot0_ref), (qs1_ref, ks1_ref, tot1_ref))

    def rows_of(i):
        rows_f = [pl.multiple_of((i * group + g) * GLA_CHUNK, GLA_CHUNK) for g in range(group)]
        rows_b = [pl.multiple_of((n_chunks - 1 - i * group - g) * GLA_CHUNK, GLA_CHUNK) for g in range(group)]
        return rows_f, rows_b

    def prepare(i, slot):
        _gla_prepare(q_ref, k_ref, laf_ref, lab_ref, *slots[slot], *rows_of(i))

    def apply(items):
        _gla_apply(v_ref, maskf_ref, maskb_ref, sf_ref, sb_ref, o_ref,
                   [(*slots[slot], *rows_of(i), accumulate) for i, slot, accumulate in items])

    def pair(j, accumulate, last):
        apply([(2 * j, 0, accumulate[0]), (2 * j + 1, 1, accumulate[1])])
        if not last:
            prepare(2 * j + 2, 0)
            prepare(2 * j + 3, 1)

    n_pairs = steps // 2
    prepare(0, 0)
    prepare(1, 1)
    if n_pairs == 1:
        pair(0, (False, True), True)
    else:
        def body(accumulate):
            def f(j, carry):
                pair(j, (accumulate, accumulate), False)
                return carry
            return f
        lax.fori_loop(0, n_pairs // 2, body(False), 0)
        lax.fori_loop(n_pairs // 2, n_pairs - 1, body(True), 0)
        pair(n_pairs - 1, (True, True), True)


def _gla(q, k, v, laf, lab, s0f, s0b, nb, t_len):
    group = GLA_GROUP
    n_chunks = t_len // GLA_CHUNK
    steps = n_chunks // group
    assert t_len % GLA_CHUNK == 0 and n_chunks % group == 0 and (steps == 2 or steps % 4 == 0)
    consts = [jnp.asarray(m) for m in _gla_masks(GLA_CHUNK)]

    def seq(width):
        return pl.BlockSpec((t_len, width), lambda b, h: (b, h))

    state = pl.BlockSpec((None, None, DK, DV), lambda b, h: (b, h, 0, 0))
    n = nb * t_len
    slot = [pltpu.VMEM((2 * group, (GLA_LEVELS + 2) * GLA_CHUNK, DK), BF16),
            pltpu.VMEM((2 * group, (GLA_LEVELS + 1) * GLA_CHUNK, DK), BF16),
            pltpu.VMEM((2 * group, SUBLANES, DK), F32)]
    return pl.pallas_call(
        functools.partial(_gla_kernel, n_chunks=n_chunks, group=group),
        grid=(nb, N_HEADS),
        in_specs=[seq(DK), seq(DK), seq(DV), seq(DK), seq(DK), state, state]
                 + [_resident(a.shape) for a in consts],
        out_specs=[seq(DV), state, state],
        out_shape=[jax.ShapeDtypeStruct((n, DV_TOT), F32),
                   jax.ShapeDtypeStruct((nb, N_HEADS, DK, DV), F32),
                   jax.ShapeDtypeStruct((nb, N_HEADS, DK, DV), F32)],
        scratch_shapes=slot + slot,
        compiler_params=_params(2),
        name="gla",
    )(q, k, v, laf, lab, s0f, s0b, *consts)


def _mixout_kernel(x_ref, mod_ref, m_ref, o_ref, r_ref, g_ref, gn_ref, wpf_ref, wpg_ref, wout_ref, y_ref,
                   *scratch, d_model, fft_layout):
    if fft_layout:
        slab_ref, = scratch
        rows = m_ref.shape[1]
        for g in range(D_FOURIER // LANES):
            slab_ref[g] = m_ref[:, :, g * LANES:(g + 1) * LANES].reshape(FFT_N * rows, LANES)
        m = jnp.concatenate(
            [jnp.concatenate([slab_ref[g, pl.ds(s, FFT_N, stride=rows), :] for g in range(D_FOURIER // LANES)],
                             axis=1) for s in range(rows)], axis=0)
    else:
        m = m_ref[...]
    branch_a = _dot(m.astype(BF16), wpf_ref[...])
    o = o_ref[...]
    parts = []
    for h in range(N_HEADS):
        oh = o[:, h * DV:(h + 1) * DV]
        parts.append(oh * lax.rsqrt(jnp.mean(oh * oh, axis=-1, keepdims=True) + RMS_EPS))
    r = r_ref[...].astype(F32)
    on = jnp.concatenate(parts, axis=1) * (gn_ref[...] * r * _sigmoid(r))
    branch_b = _dot(on.astype(BF16), wpg_ref[...])
    g = _sigmoid(g_ref[...].astype(F32))
    merged = g[:, :d_model] * branch_a + g[:, d_model:] * branch_b
    y = _dot(merged.astype(BF16), wout_ref[...])
    y_ref[...] = x_ref[...] + mod_ref[5:6, :] * y


def _mixout(x, mod, m, o, r, g, gla_norm, wpf, wpg, wout, *, tiles_per_group, fft_layout):
    n, d = x.shape
    tm = TOKEN_TILE

    def tok(width):
        return pl.BlockSpec((tm, width), lambda i: (i, 0))

    m_spec = tok(D_FOURIER)
    scratch = []
    if fft_layout:
        rows = tm // FFT_N
        tiles = FFT_N // rows
        m_spec = pl.BlockSpec((None, FFT_N, rows, D_FOURIER), lambda i: (i // tiles, 0, i % tiles, 0))
        scratch = [pltpu.VMEM((D_FOURIER // LANES, tm, LANES), F32)]
    return pl.pallas_call(
        functools.partial(_mixout_kernel, d_model=d, fft_layout=fft_layout),
        grid=(n // tm,),
        in_specs=[tok(d),
                  pl.BlockSpec((None, N_MOD, d), lambda i: (i // tiles_per_group, 0, 0)),
                  m_spec, tok(DV_TOT), tok(DV_TOT), tok(2 * d),
                  _resident((1, DV_TOT)), _resident(wpf.shape), _resident(wpg.shape), _resident(wout.shape)],
        out_specs=tok(d),
        out_shape=jax.ShapeDtypeStruct((n, d), F32),
        scratch_shapes=scratch,
        compiler_params=_params(1),
        name="mixout",
    )(x, mod, m, o, r, g, gla_norm.reshape(1, DV_TOT), wpf, wpg, wout)


def _grid_pos_tables(n_tokens, d_model):
    n_freq = d_model // 4
    omega = POS_BASE ** (-jnp.arange(n_freq, dtype=F32) / n_freq)
    ra = jnp.arange(n_tokens // GRID_W, dtype=F32)[:, None] * omega
    ca = jnp.arange(GRID_W, dtype=F32)[:, None] * omega
    return (jnp.concatenate([jnp.sin(ra), jnp.cos(ra)], axis=-1),
            jnp.concatenate([jnp.sin(ca), jnp.cos(ca)], axis=-1))


def _trunk_layer(x, mod, s0f, s0b, nb, t_len, w, *, pos, final_gain, two_stage_fft):
    tiles = max(t_len // TOKEN_TILE, 1) if mod.shape[0] > 1 else x.shape[0] // TOKEN_TILE
    x = _ffn(x, mod, w["norm_ffn1"], w["ffn1_gate"], w["ffn1_up"], w["ffn1_down"],
             tiles_per_group=tiles, first_mod_row=0, pos=pos)
    f, q, k, v, r, g, laf, lab = _mixin(x, mod, w["norm_mix"], w["w_main"], w["w_tail"], w["w_alpha"],
                                        w["b_alpha"],
                                        tiles_per_group=tiles, fft_layout=two_stage_fft)
    m = _fourier_two_stage(f, nb) if two_stage_fft else _fourier_dense(f, nb, t_len)
    o, sf, sb = _gla(q, k, v, laf, lab, s0f, s0b, nb, t_len)
    x = _mixout(x, mod, m, o, r, g, w["gla_norm"], w["proj_fourier"], w["proj_gla"], w["w_out"],
                tiles_per_group=tiles, fft_layout=two_stage_fft)
    x = _ffn(x, mod, w["norm_ffn2"], w["ffn2_gate"], w["ffn2_up"], w["ffn2_down"],
             tiles_per_group=tiles, first_mod_row=6, final_gain=final_gain)
    return x, sf, sb


def kernel(x_prompt, x_sample, state_gla_fwd, state_gla_bwd, c, c_ctx, w_ada, b_ada, norm_ffn1, w_ffn1_gate, w_ffn1_up, w_ffn1_down, norm_mix, w_in, w_alpha_fwd, b_alpha_fwd, w_alpha_bwd, b_alpha_bwd, gla_norm, w_proj_fourier, w_proj_gla, w_out, norm_ffn2, w_ffn2_gate, w_ffn2_up, w_ffn2_down, final_norm):
    nb_ctx, t_ctx, d = x_prompt.shape
    nb_lat, t_lat, _ = x_sample.shape
    depth = w_ada.shape[0]
    assert t_lat == FFT_N * FFT_N and (nb_ctx * t_ctx) % TOKEN_TILE == 0 and t_lat % TOKEN_TILE == 0

    xc = x_prompt.reshape(nb_ctx * t_ctx, d)
    xl = x_sample.reshape(nb_lat * t_lat, d)
    pos = _grid_pos_tables(t_lat, d)
    zero_state = jnp.zeros((nb_ctx, N_HEADS, DK, DV), F32)
    pad_rows = (-(nb_lat + 1)) % 8
    cvecs = jnp.concatenate([c, c_ctx[None, :], jnp.zeros((pad_rows, d), F32)], axis=0)

    new_fwd, new_bwd = [], []
    for l in range(depth):
        cut = _C_G
        w_in_l = w_in[l]
        w_main = w_in_l[:, :cut].astype(BF16)
        w_tail = jnp.concatenate(
            [w_in_l[:, cut + 2 * GATE_RANK:].astype(BF16), w_in_l[:, cut:cut + 2 * GATE_RANK].astype(BF16),
             jnp.zeros((d, ALR_PAD - 2 * GATE_RANK), BF16)], axis=1)
        w_alpha = jnp.zeros((ALR_PAD, 2 * DK_TOT), F32)
        w_alpha = w_alpha.at[:GATE_RANK, :DK_TOT].set(w_alpha_fwd[l])
        w_alpha = w_alpha.at[GATE_RANK:2 * GATE_RANK, DK_TOT:].set(w_alpha_bwd[l]).astype(BF16)
        b_alpha = jnp.concatenate([b_alpha_fwd[l], b_alpha_bwd[l]]).reshape(1, 2 * DK_TOT)
        w = {
            "norm_ffn1": norm_ffn1[l], "ffn1_gate": w_ffn1_gate[l].astype(BF16),
            "ffn1_up": w_ffn1_up[l].astype(BF16), "ffn1_down": w_ffn1_down[l].astype(BF16),
            "norm_mix": norm_mix[l], "w_main": w_main, "w_tail": w_tail, "w_alpha": w_alpha, "b_alpha": b_alpha,
            "gla_norm": gla_norm[l], "proj_fourier": w_proj_fourier[l].astype(BF16),
            "proj_gla": w_proj_gla[l].astype(BF16), "w_out": w_out[l].astype(BF16),
            "norm_ffn2": norm_ffn2[l], "ffn2_gate": w_ffn2_gate[l].astype(BF16),
            "ffn2_up": w_ffn2_up[l].astype(BF16), "ffn2_down": w_ffn2_down[l].astype(BF16),
        }
        last = l == depth - 1
        mod = _ada(cvecs, w_ada[l], b_ada[l]).reshape(-1, N_MOD, d)
        mod_lat = mod[:nb_lat]
        mod_ctx = mod[nb_lat:nb_lat + 1]
        xc, sf, sb = _trunk_layer(xc, mod_ctx, zero_state, zero_state, nb_ctx, t_ctx, w,
                                  pos=None, final_gain=final_norm if last else None, two_stage_fft=False)
        new_fwd.append(sf)
        new_bwd.append(sb)
        xl, _, _ = _trunk_layer(xl, mod_lat, state_gla_fwd[:, l], state_gla_bwd[:, l], nb_lat, t_lat, w,
                                pos=pos if l == 0 else None, final_gain=final_norm if last else None,
                                two_stage_fft=True)
    y_prompt = xc.reshape(nb_ctx, t_ctx, d)
    y_sample = xl.reshape(nb_lat, t_lat, d)
    return (y_prompt, y_sample, jnp.stack(new_fwd, axis=1), jnp.stack(new_bwd, axis=1))
```

```python
import functools
import math

import numpy as np
import jax
import jax.numpy as jnp
from jax import lax
from jax.experimental import pallas as pl
from jax.experimental.pallas import tpu as pltpu

F32 = jnp.float32
BF16 = jnp.bfloat16

GRID_W = 64
N_GROUPS = 4
GROUP_W = 128
D_FOURIER = N_GROUPS * GROUP_W
N_HEADS = 4
DK = 128
DV = 256
DK_TOT = N_HEADS * DK
DV_TOT = N_HEADS * DV
GATE_RANK = 16
GATE_TEMP = 16.0
RMS_EPS = 1e-6
POS_BASE = 10000.0
N_MOD = 9

TOKEN_TILE = 512
GLA_CHUNK = 64
GLA_LEVELS = 6
GLA_GROUP = 4
LANES = 128
ALR_PAD = LANES
FFT_N = 64
FFT_ROWS = 8
VMEM_LIMIT = 56 * 1024 * 1024


def _params(n_axes):
    return pltpu.CompilerParams(dimension_semantics=("arbitrary",) * n_axes,
                                vmem_limit_bytes=VMEM_LIMIT)


def _resident(shape):
    nd = len(shape)
    return pl.BlockSpec(shape, lambda *_: (0,) * nd, pipeline_mode=pl.Buffered(1))


LOG2_E = math.log2(math.e)


def _sigmoid(x):
    return 1.0 / (1.0 + jnp.exp2(x * (-LOG2_E)))


def _rms(x, gain):
    return x * lax.rsqrt(jnp.mean(x * x, axis=-1, keepdims=True) + RMS_EPS) * gain


def _modulated_rms(x, gain, scale, shift):
    return x * lax.rsqrt(jnp.mean(x * x, axis=-1, keepdims=True) + RMS_EPS) * (gain * (1.0 + scale)) + shift


def _dot(a, b):
    return jnp.dot(a, b, preferred_element_type=F32)


def _ada_kernel(c_ref, w_ref, b_ref, o_ref):
    c = c_ref[...]
    s = (c * _sigmoid(c)).astype(BF16)
    o_ref[...] = _dot(s, w_ref[...].astype(BF16)) + b_ref[...]


def _ada(cvecs, w_ada, b_ada):
    rows, d = cvecs.shape
    n = w_ada.shape[1]
    tn = d
    return pl.pallas_call(
        _ada_kernel,
        grid=(n // tn,),
        in_specs=[pl.BlockSpec((rows, d), lambda j: (0, 0)),
                  pl.BlockSpec((d, tn), lambda j: (0, j)),
                  pl.BlockSpec((1, tn), lambda j: (0, j))],
        out_specs=pl.BlockSpec((rows, tn), lambda j: (0, j)),
        out_shape=jax.ShapeDtypeStruct((rows, n), F32),
        compiler_params=_params(1),
        name="ada",
    )(cvecs, w_ada, b_ada.reshape(1, n))


def _ffn_kernel(*refs, first_mod_row, add_pos, final_norm):
    x_ref, mod_ref, gain_ref, wg_ref, wu_ref, wd_ref = refs[:6]
    rest = list(refs[6:])
    prow_ref, pcol_ref = (rest.pop(0), rest.pop(0)) if add_pos else (None, None)
    fn_ref = rest.pop(0) if final_norm else None
    o_ref = rest.pop(0)

    x = x_ref[...]
    if add_pos:
        tm, d = x.shape
        rows = tm // GRID_W
        prow = jnp.broadcast_to(prow_ref[...][:, None, :], (rows, GRID_W, d // 2))
        pcol = jnp.broadcast_to(pcol_ref[...][None, :, :], (rows, GRID_W, d // 2))
        x = x + jnp.concatenate([prow, pcol], axis=-1).reshape(tm, d)
    shift = mod_ref[first_mod_row:first_mod_row + 1, :]
    scale = mod_ref[first_mod_row + 1:first_mod_row + 2, :]
    gate = mod_ref[first_mod_row + 2:first_mod_row + 3, :]
    h = _modulated_rms(x, gain_ref[...], scale, shift).astype(BF16)
    g = _dot(h, wg_ref[...])
    u = _dot(h, wu_ref[...])
    a = (g * _sigmoid(g) * u).astype(BF16)
    y = _dot(a, wd_ref[...])
    xn = x + (0.5 * gate) * y
    if final_norm:
        xn = _rms(xn, fn_ref[...])
    o_ref[...] = xn


def _ffn(x, mod, gain, wg, wu, wd, *, tiles_per_group, first_mod_row, pos=None, final_gain=None):
    n, d = x.shape
    dff = wg.shape[1]
    tm = TOKEN_TILE
    add_pos = pos is not None
    final_norm = final_gain is not None
    in_specs = [pl.BlockSpec((tm, d), lambda i: (i, 0)),
                pl.BlockSpec((None, N_MOD, d), lambda i: (i // tiles_per_group, 0, 0)),
                _resident((1, d)), _resident((d, dff)), _resident((d, dff)), _resident((dff, d))]
    args = [x, mod, gain.reshape(1, d), wg, wu, wd]
    if add_pos:
        pos_row, pos_col = pos
        rows = tm // GRID_W
        row_tiles = pos_row.shape[0] // rows
        assert tm % GRID_W == 0 and rows % 8 == 0 and pos_col.shape[0] == GRID_W
        in_specs += [pl.BlockSpec((rows, d // 2), lambda i: (i % row_tiles, 0)), _resident(pos_col.shape)]
        args += [pos_row, pos_col]
    if final_norm:
        in_specs.append(_resident((1, d)))
        args.append(final_gain.reshape(1, d))
    return pl.pallas_call(
        functools.partial(_ffn_kernel, first_mod_row=first_mod_row, add_pos=add_pos, final_norm=final_norm),
        grid=(n // tm,),
        in_specs=in_specs,
        out_specs=pl.BlockSpec((tm, d), lambda i: (i, 0)),
        out_shape=jax.ShapeDtypeStruct((n, d), F32),
        compiler_params=_params(1),
        name="ffn",
    )(*args)


_C_F = 0
_C_Q = _C_F + D_FOURIER
_C_K = _C_Q + DK_TOT
_C_V = _C_K + DK_TOT
_C_R = _C_V + DV_TOT
_C_G = _C_R + DV_TOT


def _mixin_kernel(x_ref, mod_ref, gain_ref, w_ref, wt_ref, wal_ref, bal_ref,
                  f_ref, q_ref, k_ref, v_ref, r_ref, g_ref, laf_ref, lab_ref, *scratch, d_model, fft_layout):
    x = x_ref[...]
    shift = mod_ref[3:4, :]
    scale = mod_ref[4:5, :]
    h = _modulated_rms(x, gain_ref[...], scale, shift).astype(BF16)

    def proj(lo, hi):
        return _dot(h, w_ref[:, lo:hi])

    alr = _dot(h, wt_ref[:, 2 * d_model:2 * d_model + ALR_PAD]).astype(BF16)
    pre = _dot(alr, wal_ref[...]) + bal_ref[...]
    la = (jnp.minimum(pre, 0.0) * (LOG2_E / GATE_TEMP)
          - jnp.log2(1.0 + jnp.exp2(jnp.abs(pre) * (-LOG2_E))) * (1.0 / GATE_TEMP))
    laf_ref[...] = la[:, :DK_TOT]
    lab_ref[...] = la[:, DK_TOT:]

    g_ref[...] = _dot(h, wt_ref[:, :2 * d_model]).astype(BF16)
    r_ref[...] = proj(_C_R, _C_G).astype(BF16)
    v_ref[...] = proj(_C_V, _C_R).astype(BF16)
    f = proj(_C_F, _C_Q)
    if fft_layout:
        slab_ref, = scratch
        rows = x.shape[0] // FFT_N
        for a in range(rows):
            for g in range(D_FOURIER // LANES):
                slab_ref[g, pl.ds(a, FFT_N, stride=rows), :] = f[a * FFT_N:(a + 1) * FFT_N, g * LANES:(g + 1) * LANES]
        for g in range(D_FOURIER // LANES):
            f_ref[:, :, g * LANES:(g + 1) * LANES] = slab_ref[g].reshape(FFT_N, rows, LANES)
    else:
        f_ref[...] = f
    q_ref[...] = proj(_C_Q, _C_K) * (DK ** -0.5)
    k_ref[...] = proj(_C_K, _C_V)


def _mixin(x, mod, gain, w_main, w_tail, w_alpha, b_alpha, *, tiles_per_group, fft_layout):
    n, d = x.shape
    tm = TOKEN_TILE

    def tok(width):
        return pl.BlockSpec((tm, width), lambda i: (i, 0))

    outs = [(D_FOURIER, F32), (DK_TOT, F32), (DK_TOT, F32), (DV_TOT, BF16), (DV_TOT, BF16),
            (2 * d, BF16), (DK_TOT, F32), (DK_TOT, F32)]
    out_specs = [tok(w) for w, _ in outs]
    out_shape = [jax.ShapeDtypeStruct((n, w), dt) for w, dt in outs]
    scratch = []
    if fft_layout:
        rows = tm // FFT_N
        tiles = FFT_N // rows
        assert tm % FFT_N == 0 and rows % 8 == 0 and n % (FFT_N * FFT_N) == 0
        out_specs[0] = pl.BlockSpec((None, FFT_N, rows, D_FOURIER), lambda i: (i // tiles, 0, i % tiles, 0))
        out_shape[0] = jax.ShapeDtypeStruct((n // (FFT_N * FFT_N), FFT_N, FFT_N, D_FOURIER), F32)
        scratch = [pltpu.VMEM((D_FOURIER // LANES, tm, LANES), F32)]
    return pl.pallas_call(
        functools.partial(_mixin_kernel, d_model=d, fft_layout=fft_layout),
        grid=(n // tm,),
        in_specs=[tok(d),
                  pl.BlockSpec((None, N_MOD, d), lambda i: (i // tiles_per_group, 0, 0)),
                  _resident((1, d)), _resident(w_main.shape), _resident(w_tail.shape),
                  _resident((ALR_PAD, 2 * DK_TOT)), _resident((1, 2 * DK_TOT))],
        out_specs=out_specs,
        out_shape=out_shape,
        scratch_shapes=scratch,
        compiler_params=_params(1),
        name="mixin",
    )(x, mod, gain.reshape(1, d), w_main, w_tail, w_alpha, b_alpha)


def _dft_cos_sin(n):
    k = np.arange(n)
    ang = 2.0 * np.pi * ((k[:, None] * k[None, :]) % n) / n
    return np.cos(ang), np.sin(ang)


def _split_hi_lo(x):
    hi = x.astype(BF16)
    return hi, (x - hi.astype(F32)).astype(BF16)


def _table_pieces(table, axis):
    hi, lo = _split_hi_lo(jnp.asarray(table.astype(np.float32)))
    return jnp.concatenate([hi, hi], axis=axis), lo


def _dot_data_table(x, cat_ref, lo_ref):
    hi, lo = _split_hi_lo(x)
    return _dot(jnp.concatenate([hi, lo], axis=1), cat_ref[...]) + _dot(hi, lo_ref[...])


def _dot_table_data(cat, lo_t, x):
    hi, lo = _split_hi_lo(x)
    return _dot(cat, jnp.concatenate([hi, lo], axis=0)) + _dot(lo_t, hi)


def _width_dft_table(t_len):
    c, s = _dft_cos_sin(GROUP_W)
    return np.concatenate([c, -s], axis=1) / math.sqrt(t_len * GROUP_W)


def _width_dft(x, cwc_ref, cwl_ref):
    zr, zi = [], []
    for g in range(N_GROUPS):
        z = _dot_data_table(x[:, g * GROUP_W:(g + 1) * GROUP_W], cwc_ref, cwl_ref)
        zr.append(z[:, :GROUP_W])
        zi.append(z[:, GROUP_W:])
    return jnp.concatenate(zr, axis=1), jnp.concatenate(zi, axis=1)


def _fourier_dense_kernel(x_ref, cwc_ref, cwl_ref, ftc_ref, ftl_ref, o_ref):
    zr, zi = _width_dft(x_ref[...], cwc_ref, cwl_ref)
    zz = jnp.concatenate([zr, zi], axis=0)
    o_ref[...] = _dot_table_data(ftc_ref[...], ftl_ref[...], zz)


def _fourier_dense(f, nb, t_len):
    c, s = _dft_cos_sin(t_len)
    consts = [*_table_pieces(_width_dft_table(t_len), 0), *_table_pieces(np.concatenate([c, s], axis=1), 1)]
    return pl.pallas_call(
        _fourier_dense_kernel,
        grid=(nb,),
        in_specs=[pl.BlockSpec((t_len, D_FOURIER), lambda b: (b, 0))] + [_resident(a.shape) for a in consts],
        out_specs=pl.BlockSpec((t_len, D_FOURIER), lambda b: (b, 0)),
        out_shape=jax.ShapeDtypeStruct(f.shape, F32),
        compiler_params=_params(1),
        name="fourier_dense",
    )(f, *consts)


FFT_CHANNELS = 2 * GROUP_W


def _fourier_two_stage_kernel(x_ref, cwc_ref, cwl_ref, mc_ref, ml_ref, f2c_ref, f2l_ref, o_ref,
                              ar_ref, ai_ref, sr_ref, si_ref):
    groups = FFT_CHANNELS // GROUP_W

    def stage1(jb, carry):
        t2_0 = pl.multiple_of(jb * FFT_ROWS, FFT_ROWS)
        x = x_ref[pl.ds(t2_0, FFT_ROWS)].reshape(FFT_ROWS * FFT_N, FFT_CHANNELS)
        zr, zi = [], []
        for g in range(groups):
            z = _dot_data_table(x[:, g * GROUP_W:(g + 1) * GROUP_W], cwc_ref, cwl_ref)
            zr.append(z[:, :GROUP_W])
            zi.append(z[:, GROUP_W:])
        zr = jnp.concatenate(zr, axis=1)
        zi = jnp.concatenate(zi, axis=1)
        for j in range(FFT_ROWS):
            rows = slice(j * FFT_N, (j + 1) * FFT_N)
            a = _dot_table_data(mc_ref[t2_0 + j], ml_ref[t2_0 + j], jnp.concatenate([zr[rows], zi[rows]], axis=0))
            for g in range(FFT_CHANNELS // LANES):
                lanes = slice(g * LANES, (g + 1) * LANES)
                sr_ref[g, pl.ds(j, FFT_N, stride=FFT_ROWS), :] = a[:FFT_N, lanes]
                si_ref[g, pl.ds(j, FFT_N, stride=FFT_ROWS), :] = a[FFT_N:, lanes]
        for g in range(FFT_CHANNELS // LANES):
            lanes = slice(g * LANES, (g + 1) * LANES)
            ar_ref[:, pl.ds(t2_0, FFT_ROWS), lanes] = sr_ref[g].reshape(FFT_N, FFT_ROWS, LANES)
            ai_ref[:, pl.ds(t2_0, FFT_ROWS), lanes] = si_ref[g].reshape(FFT_N, FFT_ROWS, LANES)
        return carry

    def stage2(jb, carry):
        for j in range(FFT_ROWS):
            s1 = jb * FFT_ROWS + j
            aa = jnp.concatenate([ar_ref[s1], ai_ref[s1]], axis=0)
            o_ref[s1] = _dot_table_data(f2c_ref[...], f2l_ref[...], aa)
        return carry

    lax.fori_loop(0, FFT_N // FFT_ROWS, stage1, 0)
    lax.fori_loop(0, FFT_N // FFT_ROWS, stage2, 0)


def _fourier_two_stage(x, nb):
    n = FFT_N
    t_len = n * n
    s1 = np.arange(n)[:, None]
    t1 = np.arange(n)[None, :]
    mats = []
    for t2 in range(n):
        ang = 2.0 * np.pi * ((s1 * t1 * n + s1 * t2) % t_len) / t_len
        c, s = np.cos(ang), np.sin(ang)
        mats.append(np.block([[c, s], [-s, c]]))
    m_cat, m_lo = _table_pieces(np.stack(mats), 2)
    c2, s2 = _dft_cos_sin(n)
    f2 = _table_pieces(np.concatenate([c2, s2], axis=1), 1)
    cw = _table_pieces(_width_dft_table(t_len), 0)
    consts = [*cw, m_cat, m_lo, *f2]

    blk = pl.BlockSpec((None, n, n, FFT_CHANNELS), lambda b, h: (b, 0, 0, h))
    plane = pltpu.VMEM((n, n, FFT_CHANNELS), F32)
    slabs = pltpu.VMEM((FFT_CHANNELS // LANES, n * FFT_ROWS, LANES), F32)
    return pl.pallas_call(
        _fourier_two_stage_kernel,
        grid=(nb, D_FOURIER // FFT_CHANNELS),
        in_specs=[blk] + [_resident(a.shape) for a in consts],
        out_specs=blk,
        out_shape=jax.ShapeDtypeStruct((nb, n, n, D_FOURIER), F32),
        scratch_shapes=[plane, plane, slabs, slabs],
        compiler_params=_params(2),
        name="fourier_two_stage",
    )(x, *consts)


def _gla_masks(chunk):
    idx = np.arange(chunk)
    i = idx[:, None]
    j = idx[None, :]
    masks = [i == j]
    for l in range(GLA_LEVELS):
        same = (i >> (l + 1)) == (j >> (l + 1))
        masks.append(same & (((i >> l) & 1) == 1) & (((j >> l) & 1) == 0))
    mask_f = np.stack(masks).astype(np.float32)
    return mask_f, np.transpose(mask_f, (0, 2, 1)).copy()


_NT = (((1,), (1,)), ((), ()))
_TN = (((0,), (0,)), ((), ()))
SUBLANES = 8


def _block_sums(la, forward):
    c = la.shape[0]
    nv = c // SUBLANES
    x = la.reshape(nv, SUBLANES, DK)
    sub = lax.broadcasted_iota(jnp.int32, (1, SUBLANES, DK), 1)
    near = x
    tot = x
    out = {}
    w = 1
    while w < SUBLANES:
        right = (sub & w) != 0
        partner = jnp.where(right, pltpu.roll(tot, w, axis=1), pltpu.roll(tot, SUBLANES - w, axis=1))
        near = near + jnp.where(right if forward else jnp.logical_not(right), partner, 0.0)
        tot = tot + partner
        w *= 2
        out[w] = (near, tot - near)
    blk = 1
    while w < c:
        takes = (lambda v: v & blk) if forward else (lambda v: not v & blk)
        near = jnp.stack([near[v] + tot[v ^ blk] if takes(v) else near[v] for v in range(nv)])
        tot = jnp.stack([tot[v] + tot[v ^ blk] for v in range(nv)])
        w *= 2
        blk *= 2
        out[w] = (near, tot - near)
    return {w: (a.reshape(c, DK), b.reshape(c, DK)) for w, (a, b) in out.items()}


def _gla_prepare(q_ref, k_ref, laf_ref, lab_ref, qs_ref, ks_ref, tot_ref, rows_f, rows_b):
    c = GLA_CHUNK
    for g in range(len(rows_f)):
        for fwd in (True, False):
            chain = 2 * g + (0 if fwd else 1)
            rows = pl.ds(rows_f[g] if fwd else rows_b[g], c)
            q = q_ref[rows, :]
            k = k_ref[rows, :]
            la = (laf_ref if fwd else lab_ref)[rows, :]
            sums = _block_sums(la, fwd)
            qs_ref[chain, 0:c, :] = q.astype(BF16)
            ks_ref[chain, 0:c, :] = k.astype(BF16)
            for l in range(GLA_LEVELS + 1):
                if l == 0:
                    d, e = jnp.exp2(la), None
                else:
                    near, far = sums[1 << l]
                    d, e = jnp.exp2(near), jnp.exp2(far)
                qs_ref[chain, (l + 1) * c:(l + 2) * c, :] = (q * d).astype(BF16)
                if e is not None:
                    ks_ref[chain, l * c:(l + 1) * c, :] = (k * e).astype(BF16)
            tot_ref[chain] = jnp.broadcast_to(d[c - 1:c, :] if fwd else d[0:1, :], (SUBLANES, DK))


def _gla_apply(v_ref, maskf_ref, maskb_ref, sf_ref, sb_ref, o_ref, work):
    c = GLA_CHUNK
    levels = GLA_LEVELS
    chains = []
    for qs_ref, ks_ref, tot_ref, rows_f, rows_b, accumulate in work:
        for chain in range(2 * len(rows_f)):
            fwd = chain % 2 == 0
            rows = pl.ds((rows_f if fwd else rows_b)[chain // 2], c)
            chains.append((qs_ref, ks_ref, tot_ref, chain, fwd, rows, accumulate))
    scores = []
    for qs_ref, ks_ref, tot_ref, chain, fwd, rows, accumulate in chains:
        m_ref = maskf_ref if fwd else maskb_ref
        s = None
        for l in range(levels + 1):
            kl = max(l - 1, 0)
            p = lax.dot_general(qs_ref[chain, l * c:(l + 1) * c, :], ks_ref[chain, kl * c:(kl + 1) * c, :], _NT,
                                preferred_element_type=F32) * m_ref[l]
            s = p if s is None else s + p
        scores.append(s.astype(BF16))
    for (qs_ref, ks_ref, tot_ref, chain, fwd, rows, accumulate), s in zip(chains, scores):
        st_ref = sf_ref if fwd else sb_ref
        v = v_ref[rows, :]
        st = st_ref[...]
        q_top = qs_ref[chain, (levels + 1) * c:(levels + 2) * c, :]
        k_top = ks_ref[chain, levels * c:(levels + 1) * c, :]
        o = _dot(s, v) + _dot(q_top, st.astype(BF16))
        tot = jnp.broadcast_to(tot_ref[chain][0:1, :], (DK, DK)).T
        st_ref[...] = (st * jnp.concatenate([tot] * (DV // DK), axis=1)
                       + lax.dot_general(k_top, v, _TN, preferred_element_type=F32))
        if accumulate:
            o_ref[rows, :] += o
        else:
            o_ref[rows, :] = o


def _gla_kernel(q_ref, k_ref, v_ref, laf_ref, lab_ref, s0f_ref, s0b_ref, maskf_ref, maskb_ref,
                o_ref, sf_ref, sb_ref, qs0_ref, ks0_ref, tot0_ref, qs1_ref, ks1_ref, tot1_ref,
                *, n_chunks, group):
    sf_ref[...] = s0f_ref[...]
    sb_ref[...] = s0b_ref[...]
    steps = n_chunks // group
    slots = ((qs0_ref, ks0_ref, tot0_ref), (qs1_ref, ks1_ref, tot1_ref))

    def rows_of(i):
        rows_f = [pl.multiple_of((i * group + g) * GLA_CHUNK, GLA_CHUNK) for g in range(group)]
        rows_b = [pl.multiple_of((n_chunks - 1 - i * group - g) * GLA_CHUNK, GLA_CHUNK) for g in range(group)]
        return rows_f, rows_b

    def prepare(i, slot):
        _gla_prepare(q_ref, k_ref, laf_ref, lab_ref, *slots[slot], *rows_of(i))

    def apply(items):
        _gla_apply(v_ref, maskf_ref, maskb_ref, sf_ref, sb_ref, o_ref,
                   [(*slots[slot], *rows_of(i), accumulate) for i, slot, accumulate in items])

    def pair(j, accumulate, last):
        apply([(2 * j, 0, accumulate[0]), (2 * j + 1, 1, accumulate[1])])
        if not last:
            prepare(2 * j + 2, 0)
            prepare(2 * j + 3, 1)

    n_pairs = steps // 2
    prepare(0, 0)
    prepare(1, 1)
    if n_pairs == 1:
        pair(0, (False, True), True)
    else:
        def body(accumulate):
            def f(j, carry):
                pair(j, (accumulate, accumulate), False)
                return carry
            return f
        lax.fori_loop(0, n_pairs // 2, body(False), 0)
        lax.fori_loop(n_pairs // 2, n_pairs - 1, body(True), 0)
        pair(n_pairs - 1, (True, True), True)


def _gla(q, k, v, laf, lab, s0f, s0b, nb, t_len):
    n_chunks = t_len // GLA_CHUNK
    group = next(g for g in (GLA_GROUP, GLA_GROUP // 2) if n_chunks % (4 * g) == 0 or n_chunks == 2 * g)
    steps = n_chunks // group
    assert t_len % GLA_CHUNK == 0 and n_chunks % group == 0 and (steps == 2 or steps % 4 == 0)
    consts = [jnp.asarray(m) for m in _gla_masks(GLA_CHUNK)]

    def seq(width):
        return pl.BlockSpec((t_len, width), lambda b, h: (b, h))

    state = pl.BlockSpec((None, None, DK, DV), lambda b, h: (b, h, 0, 0))
    n = nb * t_len
    slot = [pltpu.VMEM((2 * group, (GLA_LEVELS + 2) * GLA_CHUNK, DK), BF16),
            pltpu.VMEM((2 * group, (GLA_LEVELS + 1) * GLA_CHUNK, DK), BF16),
            pltpu.VMEM((2 * group, SUBLANES, DK), F32)]
    return pl.pallas_call(
        functools.partial(_gla_kernel, n_chunks=n_chunks, group=group),
        grid=(nb, N_HEADS),
        in_specs=[seq(DK), seq(DK), seq(DV), seq(DK), seq(DK), state, state]
                 + [_resident(a.shape) for a in consts],
        out_specs=[seq(DV), state, state],
        out_shape=[jax.ShapeDtypeStruct((n, DV_TOT), F32),
                   jax.ShapeDtypeStruct((nb, N_HEADS, DK, DV), F32),
                   jax.ShapeDtypeStruct((nb, N_HEADS, DK, DV), F32)],
        scratch_shapes=slot + slot,
        compiler_params=_params(2),
        name="gla",
    )(q, k, v, laf, lab, s0f, s0b, *consts)


def _mixout_kernel(x_ref, mod_ref, m_ref, o_ref, r_ref, g_ref, gn_ref, wpf_ref, wpg_ref, wout_ref, y_ref,
                   *scratch, d_model, fft_layout):
    if fft_layout:
        slab_ref, = scratch
        rows = m_ref.shape[1]
        for g in range(D_FOURIER // LANES):
            slab_ref[g] = m_ref[:, :, g * LANES:(g + 1) * LANES].reshape(FFT_N * rows, LANES)
        m = jnp.concatenate(
            [jnp.concatenate([slab_ref[g, pl.ds(s, FFT_N, stride=rows), :] for g in range(D_FOURIER // LANES)],
                             axis=1) for s in range(rows)], axis=0)
    else:
        m = m_ref[...]
    branch_a = _dot(m.astype(BF16), wpf_ref[...])
    o = o_ref[...]
    parts = []
    for h in range(N_HEADS):
        oh = o[:, h * DV:(h + 1) * DV]
        parts.append(oh * lax.rsqrt(jnp.mean(oh * oh, axis=-1, keepdims=True) + RMS_EPS))
    r = r_ref[...].astype(F32)
    on = jnp.concatenate(parts, axis=1) * (gn_ref[...] * r * _sigmoid(r))
    branch_b = _dot(on.astype(BF16), wpg_ref[...])
    g = _sigmoid(g_ref[...].astype(F32))
    merged = g[:, :d_model] * branch_a + g[:, d_model:] * branch_b
    y = _dot(merged.astype(BF16), wout_ref[...])
    y_ref[...] = x_ref[...] + mod_ref[5:6, :] * y


def _mixout(x, mod, m, o, r, g, gla_norm, wpf, wpg, wout, *, tiles_per_group, fft_layout):
    n, d = x.shape
    tm = TOKEN_TILE

    def tok(width):
        return pl.BlockSpec((tm, width), lambda i: (i, 0))

    m_spec = tok(D_FOURIER)
    scratch = []
    if fft_layout:
        rows = tm // FFT_N
        tiles = FFT_N // rows
        m_spec = pl.BlockSpec((None, FFT_N, rows, D_FOURIER), lambda i: (i // tiles, 0, i % tiles, 0))
        scratch = [pltpu.VMEM((D_FOURIER // LANES, tm, LANES), F32)]
    return pl.pallas_call(
        functools.partial(_mixout_kernel, d_model=d, fft_layout=fft_layout),
        grid=(n // tm,),
        in_specs=[tok(d),
                  pl.BlockSpec((None, N_MOD, d), lambda i: (i // tiles_per_group, 0, 0)),
                  m_spec, tok(DV_TOT), tok(DV_TOT), tok(2 * d),
                  _resident((1, DV_TOT)), _resident(wpf.shape), _resident(wpg.shape), _resident(wout.shape)],
        out_specs=tok(d),
        out_shape=jax.ShapeDtypeStruct((n, d), F32),
        scratch_shapes=scratch,
        compiler_params=_params(1),
        name="mixout",
    )(x, mod, m, o, r, g, gla_norm.reshape(1, DV_TOT), wpf, wpg, wout)


def _grid_pos_tables(n_tokens, d_model):
    n_freq = d_model // 4
    omega = POS_BASE ** (-jnp.arange(n_freq, dtype=F32) / n_freq)
    ra = jnp.arange(n_tokens // GRID_W, dtype=F32)[:, None] * omega
    ca = jnp.arange(GRID_W, dtype=F32)[:, None] * omega
    return (jnp.concatenate([jnp.sin(ra), jnp.cos(ra)], axis=-1),
            jnp.concatenate([jnp.sin(ca), jnp.cos(ca)], axis=-1))


def _trunk_layer(x, mod, s0f, s0b, nb, t_len, w, *, pos, final_gain, two_stage_fft):
    tiles = max(t_len // TOKEN_TILE, 1) if mod.shape[0] > 1 else x.shape[0] // TOKEN_TILE
    x = _ffn(x, mod, w["norm_ffn1"], w["ffn1_gate"], w["ffn1_up"], w["ffn1_down"],
             tiles_per_group=tiles, first_mod_row=0, pos=pos)
    f, q, k, v, r, g, laf, lab = _mixin(x, mod, w["norm_mix"], w["w_main"], w["w_tail"], w["w_alpha"],
                                        w["b_alpha"],
                                        tiles_per_group=tiles, fft_layout=two_stage_fft)
    m = _fourier_two_stage(f, nb) if two_stage_fft else _fourier_dense(f, nb, t_len)
    o, sf, sb = _gla(q, k, v, laf, lab, s0f, s0b, nb, t_len)
    x = _mixout(x, mod, m, o, r, g, w["gla_norm"], w["proj_fourier"], w["proj_gla"], w["w_out"],
                tiles_per_group=tiles, fft_layout=two_stage_fft)
    x = _ffn(x, mod, w["norm_ffn2"], w["ffn2_gate"], w["ffn2_up"], w["ffn2_down"],
             tiles_per_group=tiles, first_mod_row=6, final_gain=final_gain)
    return x, sf, sb


def kernel(x_prompt, x_sample, state_gla_fwd, state_gla_bwd, c, c_ctx, w_ada, b_ada, norm_ffn1, w_ffn1_gate, w_ffn1_up, w_ffn1_down, norm_mix, w_in, w_alpha_fwd, b_alpha_fwd, w_alpha_bwd, b_alpha_bwd, gla_norm, w_proj_fourier, w_proj_gla, w_out, norm_ffn2, w_ffn2_gate, w_ffn2_up, w_ffn2_down, final_norm):
    nb_ctx, t_ctx, d = x_prompt.shape
    nb_lat, t_lat, _ = x_sample.shape
    depth = w_ada.shape[0]
    assert t_lat == FFT_N * FFT_N and (nb_ctx * t_ctx) % TOKEN_TILE == 0 and t_lat % TOKEN_TILE == 0

    xc = x_prompt.reshape(nb_ctx * t_ctx, d)
    xl = x_sample.reshape(nb_lat * t_lat, d)
    pos = _grid_pos_tables(t_lat, d)
    zero_state = jnp.zeros((nb_ctx, N_HEADS, DK, DV), F32)
    pad_rows = (-(nb_lat + 1)) % 8
    cvecs = jnp.concatenate([c, c_ctx[None, :], jnp.zeros((pad_rows, d), F32)], axis=0)

    new_fwd, new_bwd = [], []
    for l in range(depth):
        cut = _C_G
        w_in_l = w_in[l]
        w_main = w_in_l[:, :cut].astype(BF16)
        w_tail = jnp.concatenate(
            [w_in_l[:, cut + 2 * GATE_RANK:].astype(BF16), w_in_l[:, cut:cut + 2 * GATE_RANK].astype(BF16),
             jnp.zeros((d, ALR_PAD - 2 * GATE_RANK), BF16)], axis=1)
        w_alpha = jnp.zeros((ALR_PAD, 2 * DK_TOT), F32)
        w_alpha = w_alpha.at[:GATE_RANK, :DK_TOT].set(w_alpha_fwd[l])
        w_alpha = w_alpha.at[GATE_RANK:2 * GATE_RANK, DK_TOT:].set(w_alpha_bwd[l]).astype(BF16)
        b_alpha = jnp.concatenate([b_alpha_fwd[l], b_alpha_bwd[l]]).reshape(1, 2 * DK_TOT)
        w = {
            "norm_ffn1": norm_ffn1[l], "ffn1_gate": w_ffn1_gate[l].astype(BF16),
            "ffn1_up": w_ffn1_up[l].astype(BF16), "ffn1_down": w_ffn1_down[l].astype(BF16),
            "norm_mix": norm_mix[l], "w_main": w_main, "w_tail": w_tail, "w_alpha": w_alpha, "b_alpha": b_alpha,
            "gla_norm": gla_norm[l], "proj_fourier": w_proj_fourier[l].astype(BF16),
            "proj_gla": w_proj_gla[l].astype(BF16), "w_out": w_out[l].astype(BF16),
            "norm_ffn2": norm_ffn2[l], "ffn2_gate": w_ffn2_gate[l].astype(BF16),
            "ffn2_up": w_ffn2_up[l].astype(BF16), "ffn2_down": w_ffn2_down[l].astype(BF16),
        }
        last = l == depth - 1
        mod = _ada(cvecs, w_ada[l], b_ada[l]).reshape(-1, N_MOD, d)
        mod_lat = mod[:nb_lat]
        mod_ctx = mod[nb_lat:nb_lat + 1]
        xc, sf, sb = _trunk_layer(xc, mod_ctx, zero_state, zero_state, nb_ctx, t_ctx, w,
                                  pos=None, final_gain=final_norm if last else None, two_stage_fft=False)
        new_fwd.append(sf)
        new_bwd.append(sb)
        xl, _, _ = _trunk_layer(xl, mod_lat, state_gla_fwd[:, l], state_gla_bwd[:, l], nb_lat, t_lat, w,
                                pos=pos if l == 0 else None, final_gain=final_norm if last else None,
                                two_stage_fft=True)
    y_prompt = xc.reshape(nb_ctx, t_ctx, d)
    y_sample = xl.reshape(nb_lat, t_lat, d)
    return (y_prompt, y_sample, jnp.stack(new_fwd, axis=1), jnp.stack(new_bwd, axis=1))
```

```python
import functools
import math

import numpy as np
import jax
import jax.numpy as jnp
from jax import lax
from jax.experimental import pallas as pl
from jax.experimental.pallas import tpu as pltpu

F32 = jnp.float32
BF16 = jnp.bfloat16

GRID_W = 64
N_GROUPS = 4
GROUP_W = 128
D_FOURIER = N_GROUPS * GROUP_W
N_HEADS = 4
DK = 128
DV = 256
DK_TOT = N_HEADS * DK
DV_TOT = N_HEADS * DV
GATE_RANK = 16
GATE_TEMP = 16.0
RMS_EPS = 1e-6
POS_BASE = 10000.0
N_MOD = 9

TOKEN_TILE = 512
GLA_CHUNK = 64
GLA_LEVELS = 6
GLA_GROUP = 4
LANES = 128
ALR_PAD = LANES
FFT_N = 64
FFT_ROWS = 8
VMEM_LIMIT = 56 * 1024 * 1024


def _params(n_axes):
    return pltpu.CompilerParams(dimension_semantics=("arbitrary",) * n_axes,
                                vmem_limit_bytes=VMEM_LIMIT)


def _resident(shape):
    nd = len(shape)
    return pl.BlockSpec(shape, lambda *_: (0,) * nd, pipeline_mode=pl.Buffered(1))


LOG2_E = math.log2(math.e)


def _sigmoid(x):
    return 1.0 / (1.0 + jnp.exp2(x * (-LOG2_E)))


def _rms(x, gain):
    return x * lax.rsqrt(jnp.mean(x * x, axis=-1, keepdims=True) + RMS_EPS) * gain


def _modulated_rms(x, gain, scale, shift):
    return x * lax.rsqrt(jnp.mean(x * x, axis=-1, keepdims=True) + RMS_EPS) * (gain * (1.0 + scale)) + shift


def _dot(a, b):
    return jnp.dot(a, b, preferred_element_type=F32)


def _ada_kernel(c_ref, w_ref, b_ref, o_ref):
    c = c_ref[...]
    s = (c * _sigmoid(c)).astype(BF16)
    o_ref[...] = _dot(s, w_ref[...].astype(BF16)) + b_ref[...]


def _ada(cvecs, w_ada, b_ada):
    rows, d = cvecs.shape
    n = w_ada.shape[1]
    tn = d
    return pl.pallas_call(
        _ada_kernel,
        grid=(n // tn,),
        in_specs=[pl.BlockSpec((rows, d), lambda j: (0, 0)),
                  pl.BlockSpec((d, tn), lambda j: (0, j)),
                  pl.BlockSpec((1, tn), lambda j: (0, j))],
        out_specs=pl.BlockSpec((rows, tn), lambda j: (0, j)),
        out_shape=jax.ShapeDtypeStruct((rows, n), F32),
        compiler_params=_params(1),
        name="ada",
    )(cvecs, w_ada, b_ada.reshape(1, n))


def _ffn_kernel(*refs, first_mod_row, add_pos, final_norm):
    x_ref, mod_ref, gain_ref, wg_ref, wu_ref, wd_ref = refs[:6]
    rest = list(refs[6:])
    prow_ref, pcol_ref = (rest.pop(0), rest.pop(0)) if add_pos else (None, None)
    fn_ref = rest.pop(0) if final_norm else None
    o_ref = rest.pop(0)

    x = x_ref[...]
    if add_pos:
        tm, d = x.shape
        rows = tm // GRID_W
        prow = jnp.broadcast_to(prow_ref[...][:, None, :], (rows, GRID_W, d // 2))
        pcol = jnp.broadcast_to(pcol_ref[...][None, :, :], (rows, GRID_W, d // 2))
        x = x + jnp.concatenate([prow, pcol], axis=-1).reshape(tm, d)
    shift = mod_ref[first_mod_row:first_mod_row + 1, :]
    scale = mod_ref[first_mod_row + 1:first_mod_row + 2, :]
    gate = mod_ref[first_mod_row + 2:first_mod_row + 3, :]
    h = _modulated_rms(x, gain_ref[...], scale, shift).astype(BF16)
    g = _dot(h, wg_ref[...])
    u = _dot(h, wu_ref[...])
    a = (g * _sigmoid(g) * u).astype(BF16)
    y = _dot(a, wd_ref[...])
    xn = x + (0.5 * gate) * y
    if final_norm:
        xn = _rms(xn, fn_ref[...])
    o_ref[...] = xn


def _ffn(x, mod, gain, wg, wu, wd, *, tiles_per_group, first_mod_row, pos=None, final_gain=None):
    n, d = x.shape
    dff = wg.shape[1]
    tm = TOKEN_TILE
    add_pos = pos is not None
    final_norm = final_gain is not None
    in_specs = [pl.BlockSpec((tm, d), lambda i: (i, 0)),
                pl.BlockSpec((None, N_MOD, d), lambda i: (i // tiles_per_group, 0, 0)),
                _resident((1, d)), _resident((d, dff)), _resident((d, dff)), _resident((dff, d))]
    args = [x, mod, gain.reshape(1, d), wg, wu, wd]
    if add_pos:
        pos_row, pos_col = pos
        rows = tm // GRID_W
        row_tiles = pos_row.shape[0] // rows
        assert tm % GRID_W == 0 and rows % 8 == 0 and pos_col.shape[0] == GRID_W
        in_specs += [pl.BlockSpec((rows, d // 2), lambda i: (i % row_tiles, 0)), _resident(pos_col.shape)]
        args += [pos_row, pos_col]
    if final_norm:
        in_specs.append(_resident((1, d)))
        args.append(final_gain.reshape(1, d))
    return pl.pallas_call(
        functools.partial(_ffn_kernel, first_mod_row=first_mod_row, add_pos=add_pos, final_norm=final_norm),
        grid=(n // tm,),
        in_specs=in_specs,
        out_specs=pl.BlockSpec((tm, d), lambda i: (i, 0)),
        out_shape=jax.ShapeDtypeStruct((n, d), F32),
        compiler_params=_params(1),
        name="ffn",
    )(*args)


_C_F = 0
_C_Q = _C_F + D_FOURIER
_C_K = _C_Q + DK_TOT
_C_V = _C_K + DK_TOT
_C_R = _C_V + DV_TOT
_C_G = _C_R + DV_TOT


def _repack_tail_kernel(a_ref, b_ref, o_ref, *, n_gate_tiles):
    shift = 2 * GATE_RANK
    t = pl.program_id(0)

    @pl.when(t < n_gate_tiles)
    def _():
        o_ref[...] = jnp.concatenate([a_ref[:, shift:], b_ref[:, :shift]], axis=1).astype(BF16)

    @pl.when(t == n_gate_tiles)
    def _():
        rows = a_ref.shape[0]
        o_ref[...] = jnp.concatenate([a_ref[:, :shift], jnp.zeros((rows, ALR_PAD - shift), F32)], axis=1).astype(BF16)


def _repack_tail(w_in_l, d_model):
    rows, cols = w_in_l.shape
    first = _C_G // LANES
    n_gate_tiles = 2 * d_model // LANES
    last = pl.cdiv(cols, LANES) - 1
    assert _C_G % LANES == 0 and cols == _C_G + 2 * GATE_RANK + 2 * d_model
    return pl.pallas_call(
        functools.partial(_repack_tail_kernel, n_gate_tiles=n_gate_tiles),
        grid=(n_gate_tiles + 1,),
        in_specs=[pl.BlockSpec((rows, LANES), lambda t: (0, jnp.where(t == n_gate_tiles, first, first + t))),
                  pl.BlockSpec((rows, LANES), lambda t: (0, jnp.minimum(first + t + 1, last)))],
        out_specs=pl.BlockSpec((rows, LANES), lambda t: (0, t)),
        out_shape=jax.ShapeDtypeStruct((rows, 2 * d_model + ALR_PAD), BF16),
        compiler_params=_params(1),
        name="repack_tail",
    )(w_in_l, w_in_l)


def _mixin_kernel(x_ref, mod_ref, gain_ref, w_ref, wt_ref, wal_ref, bal_ref,
                  f_ref, q_ref, k_ref, v_ref, r_ref, g_ref, laf_ref, lab_ref, *scratch, d_model, fft_layout):
    x = x_ref[...]
    shift = mod_ref[3:4, :]
    scale = mod_ref[4:5, :]
    h = _modulated_rms(x, gain_ref[...], scale, shift).astype(BF16)

    def proj(lo, hi):
        return _dot(h, w_ref[:, lo:hi])

    alr = _dot(h, wt_ref[:, 2 * d_model:2 * d_model + ALR_PAD]).astype(BF16)
    pre = _dot(alr, wal_ref[...]) + bal_ref[...]
    la = (jnp.minimum(pre, 0.0) * (LOG2_E / GATE_TEMP)
          - jnp.log2(1.0 + jnp.exp2(jnp.abs(pre) * (-LOG2_E))) * (1.0 / GATE_TEMP))
    laf_ref[...] = la[:, :DK_TOT]
    lab_ref[...] = la[:, DK_TOT:]

    g_ref[...] = _dot(h, wt_ref[:, :2 * d_model]).astype(BF16)
    r_ref[...] = proj(_C_R, _C_G).astype(BF16)
    v_ref[...] = proj(_C_V, _C_R).astype(BF16)
    f = proj(_C_F, _C_Q)
    if fft_layout:
        slab_ref, = scratch
        rows = x.shape[0] // FFT_N
        for a in range(rows):
            for g in range(D_FOURIER // LANES):
                slab_ref[g, pl.ds(a, FFT_N, stride=rows), :] = f[a * FFT_N:(a + 1) * FFT_N, g * LANES:(g + 1) * LANES]
        for g in range(D_FOURIER // LANES):
            f_ref[:, :, g * LANES:(g + 1) * LANES] = slab_ref[g].reshape(FFT_N, rows, LANES)
    else:
        f_ref[...] = f
    q_ref[...] = proj(_C_Q, _C_K) * (DK ** -0.5)
    k_ref[...] = proj(_C_K, _C_V)


def _mixin(x, mod, gain, w_main, w_tail, w_alpha, b_alpha, *, tiles_per_group, fft_layout):
    n, d = x.shape
    tm = TOKEN_TILE

    def tok(width):
        return pl.BlockSpec((tm, width), lambda i: (i, 0))

    outs = [(D_FOURIER, F32), (DK_TOT, F32), (DK_TOT, F32), (DV_TOT, BF16), (DV_TOT, BF16),
            (2 * d, BF16), (DK_TOT, F32), (DK_TOT, F32)]
    out_specs = [tok(w) for w, _ in outs]
    out_shape = [jax.ShapeDtypeStruct((n, w), dt) for w, dt in outs]
    scratch = []
    if fft_layout:
        rows = tm // FFT_N
        tiles = FFT_N // rows
        assert tm % FFT_N == 0 and rows % 8 == 0 and n % (FFT_N * FFT_N) == 0
        out_specs[0] = pl.BlockSpec((None, FFT_N, rows, D_FOURIER), lambda i: (i // tiles, 0, i % tiles, 0))
        out_shape[0] = jax.ShapeDtypeStruct((n // (FFT_N * FFT_N), FFT_N, FFT_N, D_FOURIER), F32)
        scratch = [pltpu.VMEM((D_FOURIER // LANES, tm, LANES), F32)]
    return pl.pallas_call(
        functools.partial(_mixin_kernel, d_model=d, fft_layout=fft_layout),
        grid=(n // tm,),
        in_specs=[tok(d),
                  pl.BlockSpec((None, N_MOD, d), lambda i: (i // tiles_per_group, 0, 0)),
                  _resident((1, d)), _resident(w_main.shape), _resident(w_tail.shape),
                  _resident((ALR_PAD, 2 * DK_TOT)), _resident((1, 2 * DK_TOT))],
        out_specs=out_specs,
        out_shape=out_shape,
        scratch_shapes=scratch,
        compiler_params=_params(1),
        name="mixin",
    )(x, mod, gain.reshape(1, d), w_main, w_tail, w_alpha, b_alpha)


def _dft_cos_sin(n):
    k = np.arange(n)
    ang = 2.0 * np.pi * ((k[:, None] * k[None, :]) % n) / n
    return np.cos(ang), np.sin(ang)


def _split_hi_lo(x):
    hi = x.astype(BF16)
    return hi, (x - hi.astype(F32)).astype(BF16)


def _table_pieces(table, axis):
    hi, lo = _split_hi_lo(jnp.asarray(table.astype(np.float32)))
    return jnp.concatenate([hi, hi], axis=axis), lo


def _dot_data_table(x, cat_ref, lo_ref):
    hi, lo = _split_hi_lo(x)
    return _dot(jnp.concatenate([hi, lo], axis=1), cat_ref[...]) + _dot(hi, lo_ref[...])


def _dot_table_data(cat, lo_t, x):
    hi, lo = _split_hi_lo(x)
    return _dot(cat, jnp.concatenate([hi, lo], axis=0)) + _dot(lo_t, hi)


def _width_dft_table(t_len):
    c, s = _dft_cos_sin(GROUP_W)
    return np.concatenate([c, -s], axis=1) / math.sqrt(t_len * GROUP_W)


def _width_dft(x, cwc_ref, cwl_ref):
    zr, zi = [], []
    for g in range(N_GROUPS):
        z = _dot_data_table(x[:, g * GROUP_W:(g + 1) * GROUP_W], cwc_ref, cwl_ref)
        zr.append(z[:, :GROUP_W])
        zi.append(z[:, GROUP_W:])
    return jnp.concatenate(zr, axis=1), jnp.concatenate(zi, axis=1)


def _fourier_dense_kernel(x_ref, cwc_ref, cwl_ref, ftc_ref, ftl_ref, o_ref):
    zr, zi = _width_dft(x_ref[...], cwc_ref, cwl_ref)
    zz = jnp.concatenate([zr, zi], axis=0)
    o_ref[...] = _dot_table_data(ftc_ref[...], ftl_ref[...], zz)


def _fourier_dense(f, nb, t_len):
    c, s = _dft_cos_sin(t_len)
    consts = [*_table_pieces(_width_dft_table(t_len), 0), *_table_pieces(np.concatenate([c, s], axis=1), 1)]
    return pl.pallas_call(
        _fourier_dense_kernel,
        grid=(nb,),
        in_specs=[pl.BlockSpec((t_len, D_FOURIER), lambda b: (b, 0))] + [_resident(a.shape) for a in consts],
        out_specs=pl.BlockSpec((t_len, D_FOURIER), lambda b: (b, 0)),
        out_shape=jax.ShapeDtypeStruct(f.shape, F32),
        compiler_params=_params(1),
        name="fourier_dense",
    )(f, *consts)


FFT_CHANNELS = 2 * GROUP_W


def _fourier_two_stage_kernel(x_ref, cwc_ref, cwl_ref, mc_ref, ml_ref, f2c_ref, f2l_ref, o_ref,
                              ar_ref, ai_ref, sr_ref, si_ref):
    groups = FFT_CHANNELS // GROUP_W

    def stage1(jb, carry):
        t2_0 = pl.multiple_of(jb * FFT_ROWS, FFT_ROWS)
        x = x_ref[pl.ds(t2_0, FFT_ROWS)].reshape(FFT_ROWS * FFT_N, FFT_CHANNELS)
        zr, zi = [], []
        for g in range(groups):
            z = _dot_data_table(x[:, g * GROUP_W:(g + 1) * GROUP_W], cwc_ref, cwl_ref)
            zr.append(z[:, :GROUP_W])
            zi.append(z[:, GROUP_W:])
        zr = jnp.concatenate(zr, axis=1)
        zi = jnp.concatenate(zi, axis=1)
        for j in range(FFT_ROWS):
            rows = slice(j * FFT_N, (j + 1) * FFT_N)
            a = _dot_table_data(mc_ref[t2_0 + j], ml_ref[t2_0 + j], jnp.concatenate([zr[rows], zi[rows]], axis=0))
            for g in range(FFT_CHANNELS // LANES):
                lanes = slice(g * LANES, (g + 1) * LANES)
                sr_ref[g, pl.ds(j, FFT_N, stride=FFT_ROWS), :] = a[:FFT_N, lanes]
                si_ref[g, pl.ds(j, FFT_N, stride=FFT_ROWS), :] = a[FFT_N:, lanes]
        for g in range(FFT_CHANNELS // LANES):
            lanes = slice(g * LANES, (g + 1) * LANES)
            ar_ref[:, pl.ds(t2_0, FFT_ROWS), lanes] = sr_ref[g].reshape(FFT_N, FFT_ROWS, LANES)
            ai_ref[:, pl.ds(t2_0, FFT_ROWS), lanes] = si_ref[g].reshape(FFT_N, FFT_ROWS, LANES)
        return carry

    def stage2(jb, carry):
        for j in range(FFT_ROWS):
            s1 = jb * FFT_ROWS + j
            aa = jnp.concatenate([ar_ref[s1], ai_ref[s1]], axis=0)
            o_ref[s1] = _dot_table_data(f2c_ref[...], f2l_ref[...], aa)
        return carry

    lax.fori_loop(0, FFT_N // FFT_ROWS, stage1, 0)
    lax.fori_loop(0, FFT_N // FFT_ROWS, stage2, 0)


def _fourier_two_stage(x, nb):
    n = FFT_N
    t_len = n * n
    s1 = np.arange(n)[:, None]
    t1 = np.arange(n)[None, :]
    mats = []
    for t2 in range(n):
        ang = 2.0 * np.pi * ((s1 * t1 * n + s1 * t2) % t_len) / t_len
        c, s = np.cos(ang), np.sin(ang)
        mats.append(np.block([[c, s], [-s, c]]))
    m_cat, m_lo = _table_pieces(np.stack(mats), 2)
    c2, s2 = _dft_cos_sin(n)
    f2 = _table_pieces(np.concatenate([c2, s2], axis=1), 1)
    cw = _table_pieces(_width_dft_table(t_len), 0)
    consts = [*cw, m_cat, m_lo, *f2]

    blk = pl.BlockSpec((None, n, n, FFT_CHANNELS), lambda b, h: (b, 0, 0, h))
    plane = pltpu.VMEM((n, n, FFT_CHANNELS), F32)
    slabs = pltpu.VMEM((FFT_CHANNELS // LANES, n * FFT_ROWS, LANES), F32)
    return pl.pallas_call(
        _fourier_two_stage_kernel,
        grid=(nb, D_FOURIER // FFT_CHANNELS),
        in_specs=[blk] + [_resident(a.shape) for a in consts],
        out_specs=blk,
        out_shape=jax.ShapeDtypeStruct((nb, n, n, D_FOURIER), F32),
        scratch_shapes=[plane, plane, slabs, slabs],
        compiler_params=_params(2),
        name="fourier_two_stage",
    )(x, *consts)


def _gla_masks(chunk):
    idx = np.arange(chunk)
    i = idx[:, None]
    j = idx[None, :]
    masks = [i == j]
    for l in range(GLA_LEVELS):
        same = (i >> (l + 1)) == (j >> (l + 1))
        masks.append(same & (((i >> l) & 1) == 1) & (((j >> l) & 1) == 0))
    mask_f = np.stack(masks).astype(np.float32)
    return mask_f, np.transpose(mask_f, (0, 2, 1)).copy()


_NT = (((1,), (1,)), ((), ()))
_TN = (((0,), (0,)), ((), ()))
SUBLANES = 8


def _block_sums(la, forward):
    c = la.shape[0]
    nv = c // SUBLANES
    x = la.reshape(nv, SUBLANES, DK)
    sub = lax.broadcasted_iota(jnp.int32, (1, SUBLANES, DK), 1)
    near = x
    tot = x
    out = {}
    w = 1
    while w < SUBLANES:
        right = (sub & w) != 0
        partner = jnp.where(right, pltpu.roll(tot, w, axis=1), pltpu.roll(tot, SUBLANES - w, axis=1))
        near = near + jnp.where(right if forward else jnp.logical_not(right), partner, 0.0)
        tot = tot + partner
        w *= 2
        out[w] = (near, tot - near)
    blk = 1
    while w < c:
        takes = (lambda v: v & blk) if forward else (lambda v: not v & blk)
        near = jnp.stack([near[v] + tot[v ^ blk] if takes(v) else near[v] for v in range(nv)])
        tot = jnp.stack([tot[v] + tot[v ^ blk] for v in range(nv)])
        w *= 2
        blk *= 2
        out[w] = (near, tot - near)
    return {w: (a.reshape(c, DK), b.reshape(c, DK)) for w, (a, b) in out.items()}


def _gla_prepare(q_ref, k_ref, laf_ref, lab_ref, qs_ref, ks_ref, tot_ref, rows_f, rows_b):
    c = GLA_CHUNK
    for g in range(len(rows_f)):
        for fwd in (True, False):
            chain = 2 * g + (0 if fwd else 1)
            rows = pl.ds(rows_f[g] if fwd else rows_b[g], c)
            q = q_ref[rows, :]
            k = k_ref[rows, :]
            la = (laf_ref if fwd else lab_ref)[rows, :]
            sums = _block_sums(la, fwd)
            qs_ref[chain, 0:c, :] = q.astype(BF16)
            ks_ref[chain, 0:c, :] = k.astype(BF16)
            for l in range(GLA_LEVELS + 1):
                if l == 0:
                    d, e = jnp.exp2(la), None
                else:
                    near, far = sums[1 << l]
                    d, e = jnp.exp2(near), jnp.exp2(far)
                qs_ref[chain, (l + 1) * c:(l + 2) * c, :] = (q * d).astype(BF16)
                if e is not None:
                    ks_ref[chain, l * c:(l + 1) * c, :] = (k * e).astype(BF16)
            tot_ref[chain] = jnp.broadcast_to(d[c - 1:c, :] if fwd else d[0:1, :], (SUBLANES, DK))


def _gla_apply(v_ref, maskf_ref, maskb_ref, sf_ref, sb_ref, o_ref, work):
    c = GLA_CHUNK
    levels = GLA_LEVELS
    chains = []
    for qs_ref, ks_ref, tot_ref, rows_f, rows_b, accumulate in work:
        for chain in range(2 * len(rows_f)):
            fwd = chain % 2 == 0
            rows = pl.ds((rows_f if fwd else rows_b)[chain // 2], c)
            chains.append((qs_ref, ks_ref, tot_ref, chain, fwd, rows, accumulate))
    scores = []
    for qs_ref, ks_ref, tot_ref, chain, fwd, rows, accumulate in chains:
        m_ref = maskf_ref if fwd else maskb_ref
        s = None
        for l in range(levels + 1):
            kl = max(l - 1, 0)
            p = lax.dot_general(qs_ref[chain, l * c:(l + 1) * c, :], ks_ref[chain, kl * c:(kl + 1) * c, :], _NT,
                                preferred_element_type=F32) * m_ref[l]
            s = p if s is None else s + p
        scores.append(s.astype(BF16))
    for (qs_ref, ks_ref, tot_ref, chain, fwd, rows, accumulate), s in zip(chains, scores):
        st_ref = sf_ref if fwd else sb_ref
        v = v_ref[rows, :]
        st = st_ref[...]
        q_top = qs_ref[chain, (levels + 1) * c:(levels + 2) * c, :]
        k_top = ks_ref[chain, levels * c:(levels + 1) * c, :]
        o = _dot(s, v) + _dot(q_top, st.astype(BF16))
        tot = jnp.broadcast_to(tot_ref[chain][0:1, :], (DK, DK)).T
        st_ref[...] = (st * jnp.concatenate([tot] * (DV // DK), axis=1)
                       + lax.dot_general(k_top, v, _TN, preferred_element_type=F32))
        if accumulate:
            o_ref[rows, :] += o
        else:
            o_ref[rows, :] = o


def _gla_kernel(q_ref, k_ref, v_ref, laf_ref, lab_ref, s0f_ref, s0b_ref, maskf_ref, maskb_ref,
                o_ref, sf_ref, sb_ref, qs0_ref, ks0_ref, tot0_ref, qs1_ref, ks1_ref, tot1_ref,
                *, n_chunks, group):
    sf_ref[...] = s0f_ref[...]
    sb_ref[...] = s0b_ref[...]
    steps = n_chunks // group
    slots = ((qs0_ref, ks0_ref, tot0_ref), (qs1_ref, ks1_ref, tot1_ref))

    def rows_of(i):
        rows_f = [pl.multiple_of((i * group + g) * GLA_CHUNK, GLA_CHUNK) for g in range(group)]
        rows_b = [pl.multiple_of((n_chunks - 1 - i * group - g) * GLA_CHUNK, GLA_CHUNK) for g in range(group)]
        return rows_f, rows_b

    def prepare(i, slot):
        _gla_prepare(q_ref, k_ref, laf_ref, lab_ref, *slots[slot], *rows_of(i))

    def apply(items):
        _gla_apply(v_ref, maskf_ref, maskb_ref, sf_ref, sb_ref, o_ref,
                   [(*slots[slot], *rows_of(i), accumulate) for i, slot, accumulate in items])

    def pair(j, accumulate, last):
        apply([(2 * j, 0, accumulate[0]), (2 * j + 1, 1, accumulate[1])])
        if not last:
            prepare(2 * j + 2, 0)
            prepare(2 * j + 3, 1)

    n_pairs = steps // 2
    prepare(0, 0)
    prepare(1, 1)
    if n_pairs == 1:
        pair(0, (False, True), True)
    else:
        def body(accumulate):
            def f(j, carry):
                pair(j, (accumulate, accumulate), False)
                return carry
            return f
        lax.fori_loop(0, n_pairs // 2, body(False), 0)
        lax.fori_loop(n_pairs // 2, n_pairs - 1, body(True), 0)
        pair(n_pairs - 1, (True, True), True)


def _gla(q, k, v, laf, lab, s0f, s0b, nb, t_len):
    n_chunks = t_len // GLA_CHUNK
    group = next(g for g in (GLA_GROUP, GLA_GROUP // 2) if n_chunks % (4 * g) == 0 or n_chunks == 2 * g)
    steps = n_chunks // group
    assert t_len % GLA_CHUNK == 0 and n_chunks % group == 0 and (steps == 2 or steps % 4 == 0)
    consts = [jnp.asarray(m) for m in _gla_masks(GLA_CHUNK)]

    def seq(width):
        return pl.BlockSpec((t_len, width), lambda b, h: (b, h))

    state = pl.BlockSpec((None, None, DK, DV), lambda b, h: (b, h, 0, 0))
    n = nb * t_len
    slot = [pltpu.VMEM((2 * group, (GLA_LEVELS + 2) * GLA_CHUNK, DK), BF16),
            pltpu.VMEM((2 * group, (GLA_LEVELS + 1) * GLA_CHUNK, DK), BF16),
            pltpu.VMEM((2 * group, SUBLANES, DK), F32)]
    return pl.pallas_call(
        functools.partial(_gla_kernel, n_chunks=n_chunks, group=group),
        grid=(nb, N_HEADS),
        in_specs=[seq(DK), seq(DK), seq(DV), seq(DK), seq(DK), state, state]
                 + [_resident(a.shape) for a in consts],
        out_specs=[seq(DV), state, state],
        out_shape=[jax.ShapeDtypeStruct((n, DV_TOT), F32),
                   jax.ShapeDtypeStruct((nb, N_HEADS, DK, DV), F32),
                   jax.ShapeDtypeStruct((nb, N_HEADS, DK, DV), F32)],
        scratch_shapes=slot + slot,
        compiler_params=_params(2),
        name="gla",
    )(q, k, v, laf, lab, s0f, s0b, *consts)


def _mixout_kernel(x_ref, mod_ref, m_ref, o_ref, r_ref, g_ref, gn_ref, wpf_ref, wpg_ref, wout_ref, y_ref,
                   *scratch, d_model, fft_layout):
    if fft_layout:
        slab_ref, = scratch
        rows = m_ref.shape[1]
        for g in range(D_FOURIER // LANES):
            slab_ref[g] = m_ref[:, :, g * LANES:(g + 1) * LANES].reshape(FFT_N * rows, LANES)
        m = jnp.concatenate(
            [jnp.concatenate([slab_ref[g, pl.ds(s, FFT_N, stride=rows), :] for g in range(D_FOURIER // LANES)],
                             axis=1) for s in range(rows)], axis=0)
    else:
        m = m_ref[...]
    branch_a = _dot(m.astype(BF16), wpf_ref[...])
    o = o_ref[...]
    parts = []
    for h in range(N_HEADS):
        oh = o[:, h * DV:(h + 1) * DV]
        parts.append(oh * lax.rsqrt(jnp.mean(oh * oh, axis=-1, keepdims=True) + RMS_EPS))
    r = r_ref[...].astype(F32)
    on = jnp.concatenate(parts, axis=1) * (gn_ref[...] * r * _sigmoid(r))
    branch_b = _dot(on.astype(BF16), wpg_ref[...])
    g = _sigmoid(g_ref[...].astype(F32))
    merged = g[:, :d_model] * branch_a + g[:, d_model:] * branch_b
    y = _dot(merged.astype(BF16), wout_ref[...])
    y_ref[...] = x_ref[...] + mod_ref[5:6, :] * y


def _mixout(x, mod, m, o, r, g, gla_norm, wpf, wpg, wout, *, tiles_per_group, fft_layout):
    n, d = x.shape
    tm = TOKEN_TILE

    def tok(width):
        return pl.BlockSpec((tm, width), lambda i: (i, 0))

    m_spec = tok(D_FOURIER)
    scratch = []
    if fft_layout:
        rows = tm // FFT_N
        tiles = FFT_N // rows
        m_spec = pl.BlockSpec((None, FFT_N, rows, D_FOURIER), lambda i: (i // tiles, 0, i % tiles, 0))
        scratch = [pltpu.VMEM((D_FOURIER // LANES, tm, LANES), F32)]
    return pl.pallas_call(
        functools.partial(_mixout_kernel, d_model=d, fft_layout=fft_layout),
        grid=(n // tm,),
        in_specs=[tok(d),
                  pl.BlockSpec((None, N_MOD, d), lambda i: (i // tiles_per_group, 0, 0)),
                  m_spec, tok(DV_TOT), tok(DV_TOT), tok(2 * d),
                  _resident((1, DV_TOT)), _resident(wpf.shape), _resident(wpg.shape), _resident(wout.shape)],
        out_specs=tok(d),
        out_shape=jax.ShapeDtypeStruct((n, d), F32),
        scratch_shapes=scratch,
        compiler_params=_params(1),
        name="mixout",
    )(x, mod, m, o, r, g, gla_norm.reshape(1, DV_TOT), wpf, wpg, wout)


def _grid_pos_tables(n_tokens, d_model):
    n_freq = d_model // 4
    omega = POS_BASE ** (-jnp.arange(n_freq, dtype=F32) / n_freq)
    ra = jnp.arange(n_tokens // GRID_W, dtype=F32)[:, None] * omega
    ca = jnp.arange(GRID_W, dtype=F32)[:, None] * omega
    return (jnp.concatenate([jnp.sin(ra), jnp.cos(ra)], axis=-1),
            jnp.concatenate([jnp.sin(ca), jnp.cos(ca)], axis=-1))


def _trunk_layer(x, mod, s0f, s0b, nb, t_len, w, *, pos, final_gain, two_stage_fft):
    tiles = max(t_len // TOKEN_TILE, 1) if mod.shape[0] > 1 else x.shape[0] // TOKEN_TILE
    x = _ffn(x, mod, w["norm_ffn1"], w["ffn1_gate"], w["ffn1_up"], w["ffn1_down"],
             tiles_per_group=tiles, first_mod_row=0, pos=pos)
    f, q, k, v, r, g, laf, lab = _mixin(x, mod, w["norm_mix"], w["w_main"], w["w_tail"], w["w_alpha"],
                                        w["b_alpha"],
                                        tiles_per_group=tiles, fft_layout=two_stage_fft)
    m = _fourier_two_stage(f, nb) if two_stage_fft else _fourier_dense(f, nb, t_len)
    o, sf, sb = _gla(q, k, v, laf, lab, s0f, s0b, nb, t_len)
    x = _mixout(x, mod, m, o, r, g, w["gla_norm"], w["proj_fourier"], w["proj_gla"], w["w_out"],
                tiles_per_group=tiles, fft_layout=two_stage_fft)
    x = _ffn(x, mod, w["norm_ffn2"], w["ffn2_gate"], w["ffn2_up"], w["ffn2_down"],
             tiles_per_group=tiles, first_mod_row=6, final_gain=final_gain)
    return x, sf, sb


def kernel(x_prompt, x_sample, state_gla_fwd, state_gla_bwd, c, c_ctx, w_ada, b_ada, norm_ffn1, w_ffn1_gate, w_ffn1_up, w_ffn1_down, norm_mix, w_in, w_alpha_fwd, b_alpha_fwd, w_alpha_bwd, b_alpha_bwd, gla_norm, w_proj_fourier, w_proj_gla, w_out, norm_ffn2, w_ffn2_gate, w_ffn2_up, w_ffn2_down, final_norm):
    nb_ctx, t_ctx, d = x_prompt.shape
    nb_lat, t_lat, _ = x_sample.shape
    depth = w_ada.shape[0]
    assert t_lat == FFT_N * FFT_N and (nb_ctx * t_ctx) % TOKEN_TILE == 0 and t_lat % TOKEN_TILE == 0

    xc = x_prompt.reshape(nb_ctx * t_ctx, d)
    xl = x_sample.reshape(nb_lat * t_lat, d)
    pos = _grid_pos_tables(t_lat, d)
    zero_state = jnp.zeros((nb_ctx, N_HEADS, DK, DV), F32)
    pad_rows = (-(nb_lat + 1)) % 8
    cvecs = jnp.concatenate([c, c_ctx[None, :], jnp.zeros((pad_rows, d), F32)], axis=0)

    new_fwd, new_bwd = [], []
    for l in range(depth):
        w_main = w_in[l][:, :_C_G].astype(BF16)
        w_tail = _repack_tail(w_in[l], d)
        w_alpha = jnp.zeros((ALR_PAD, 2 * DK_TOT), F32)
        w_alpha = w_alpha.at[:GATE_RANK, :DK_TOT].set(w_alpha_fwd[l])
        w_alpha = w_alpha.at[GATE_RANK:2 * GATE_RANK, DK_TOT:].set(w_alpha_bwd[l]).astype(BF16)
        b_alpha = jnp.concatenate([b_alpha_fwd[l], b_alpha_bwd[l]]).reshape(1, 2 * DK_TOT)
        w = {
            "norm_ffn1": norm_ffn1[l], "ffn1_gate": w_ffn1_gate[l].astype(BF16),
            "ffn1_up": w_ffn1_up[l].astype(BF16), "ffn1_down": w_ffn1_down[l].astype(BF16),
            "norm_mix": norm_mix[l], "w_main": w_main, "w_tail": w_tail, "w_alpha": w_alpha, "b_alpha": b_alpha,
            "gla_norm": gla_norm[l], "proj_fourier": w_proj_fourier[l].astype(BF16),
            "proj_gla": w_proj_gla[l].astype(BF16), "w_out": w_out[l].astype(BF16),
            "norm_ffn2": norm_ffn2[l], "ffn2_gate": w_ffn2_gate[l].astype(BF16),
            "ffn2_up": w_ffn2_up[l].astype(BF16), "ffn2_down": w_ffn2_down[l].astype(BF16),
        }
        last = l == depth - 1
        mod = _ada(cvecs, w_ada[l], b_ada[l]).reshape(-1, N_MOD, d)
        mod_lat = mod[:nb_lat]
        mod_ctx = mod[nb_lat:nb_lat + 1]
        xc, sf, sb = _trunk_layer(xc, mod_ctx, zero_state, zero_state, nb_ctx, t_ctx, w,
                                  pos=None, final_gain=final_norm if last else None, two_stage_fft=False)
        new_fwd.append(sf)
        new_bwd.append(sb)
        xl, _, _ = _trunk_layer(xl, mod_lat, state_gla_fwd[:, l], state_gla_bwd[:, l], nb_lat, t_lat, w,
                                pos=pos if l == 0 else None, final_gain=final_norm if last else None,
                                two_stage_fft=True)
    y_prompt = xc.reshape(nb_ctx, t_ctx, d)
    y_sample = xl.reshape(nb_lat, t_lat, d)
    return (y_prompt, y_sample, jnp.stack(new_fwd, axis=1), jnp.stack(new_bwd, axis=1))
```

```python
import functools
import math

import numpy as np
import jax
import jax.numpy as jnp
from jax import lax
from jax.experimental import pallas as pl
from jax.experimental.pallas import tpu as pltpu

F32 = jnp.float32
BF16 = jnp.bfloat16

GRID_W = 64
N_GROUPS = 4
GROUP_W = 128
D_FOURIER = N_GROUPS * GROUP_W
N_HEADS = 4
DK = 128
DV = 256
DK_TOT = N_HEADS * DK
DV_TOT = N_HEADS * DV
GATE_RANK = 16
GATE_TEMP = 16.0
RMS_EPS = 1e-6
POS_BASE = 10000.0
N_MOD = 9

TOKEN_TILE = 512
GLA_CHUNK = 64
GLA_LEVELS = 6
GLA_GROUP = 4
LANES = 128
ALR_PAD = LANES
FFT_N = 64
FFT_ROWS = 8
VMEM_LIMIT = 56 * 1024 * 1024


def _params(n_axes):
    return pltpu.CompilerParams(dimension_semantics=("arbitrary",) * n_axes,
                                vmem_limit_bytes=VMEM_LIMIT)


def _resident(shape):
    nd = len(shape)
    return pl.BlockSpec(shape, lambda *_: (0,) * nd, pipeline_mode=pl.Buffered(1))


LOG2_E = math.log2(math.e)


def _sigmoid(x):
    return 1.0 / (1.0 + jnp.exp2(x * (-LOG2_E)))


def _rms(x, gain):
    return x * lax.rsqrt(jnp.mean(x * x, axis=-1, keepdims=True) + RMS_EPS) * gain


def _modulated_rms(x, gain, scale, shift):
    return x * lax.rsqrt(jnp.mean(x * x, axis=-1, keepdims=True) + RMS_EPS) * (gain * (1.0 + scale)) + shift


def _dot(a, b):
    return jnp.dot(a, b, preferred_element_type=F32)


def _ada_kernel(c_ref, w_ref, b_ref, o_ref):
    c = c_ref[...]
    s = (c * _sigmoid(c)).astype(BF16)
    o_ref[...] = _dot(s, w_ref[...].astype(BF16)) + b_ref[...]


def _ada(cvecs, w_ada, b_ada):
    rows, d = cvecs.shape
    n = w_ada.shape[1]
    tn = d
    return pl.pallas_call(
        _ada_kernel,
        grid=(n // tn,),
        in_specs=[pl.BlockSpec((rows, d), lambda j: (0, 0)),
                  pl.BlockSpec((d, tn), lambda j: (0, j)),
                  pl.BlockSpec((1, tn), lambda j: (0, j))],
        out_specs=pl.BlockSpec((rows, tn), lambda j: (0, j)),
        out_shape=jax.ShapeDtypeStruct((rows, n), F32),
        compiler_params=_params(1),
        name="ada",
    )(cvecs, w_ada, b_ada.reshape(1, n))


def _ffn_kernel(*refs, first_mod_row, add_pos, final_norm):
    x_ref, mod_ref, gain_ref, wg_ref, wu_ref, wd_ref = refs[:6]
    rest = list(refs[6:])
    prow_ref, pcol_ref = (rest.pop(0), rest.pop(0)) if add_pos else (None, None)
    fn_ref = rest.pop(0) if final_norm else None
    o_ref = rest.pop(0)

    x = x_ref[...]
    if add_pos:
        tm, d = x.shape
        rows = tm // GRID_W
        prow = jnp.broadcast_to(prow_ref[...][:, None, :], (rows, GRID_W, d // 2))
        pcol = jnp.broadcast_to(pcol_ref[...][None, :, :], (rows, GRID_W, d // 2))
        x = x + jnp.concatenate([prow, pcol], axis=-1).reshape(tm, d)
    shift = mod_ref[first_mod_row:first_mod_row + 1, :]
    scale = mod_ref[first_mod_row + 1:first_mod_row + 2, :]
    gate = mod_ref[first_mod_row + 2:first_mod_row + 3, :]
    h = _modulated_rms(x, gain_ref[...], scale, shift).astype(BF16)
    g = _dot(h, wg_ref[...])
    u = _dot(h, wu_ref[...])
    a = (g * _sigmoid(g) * u).astype(BF16)
    y = _dot(a, wd_ref[...])
    xn = x + (0.5 * gate) * y
    if final_norm:
        xn = _rms(xn, fn_ref[...])
    o_ref[...] = xn


def _ffn(x, mod, gain, wg, wu, wd, *, tiles_per_group, first_mod_row, pos=None, final_gain=None):
    n, d = x.shape
    dff = wg.shape[1]
    tm = TOKEN_TILE
    add_pos = pos is not None
    final_norm = final_gain is not None
    in_specs = [pl.BlockSpec((tm, d), lambda i: (i, 0)),
                pl.BlockSpec((None, N_MOD, d), lambda i: (i // tiles_per_group, 0, 0)),
                _resident((1, d)), _resident((d, dff)), _resident((d, dff)), _resident((dff, d))]
    args = [x, mod, gain.reshape(1, d), wg, wu, wd]
    if add_pos:
        pos_row, pos_col = pos
        rows = tm // GRID_W
        row_tiles = pos_row.shape[0] // rows
        assert tm % GRID_W == 0 and rows % 8 == 0 and pos_col.shape[0] == GRID_W
        in_specs += [pl.BlockSpec((rows, d // 2), lambda i: (i % row_tiles, 0)), _resident(pos_col.shape)]
        args += [pos_row, pos_col]
    if final_norm:
        in_specs.append(_resident((1, d)))
        args.append(final_gain.reshape(1, d))
    return pl.pallas_call(
        functools.partial(_ffn_kernel, first_mod_row=first_mod_row, add_pos=add_pos, final_norm=final_norm),
        grid=(n // tm,),
        in_specs=in_specs,
        out_specs=pl.BlockSpec((tm, d), lambda i: (i, 0)),
        out_shape=jax.ShapeDtypeStruct((n, d), F32),
        compiler_params=_params(1),
        name="ffn",
    )(*args)


_C_F = 0
_C_Q = _C_F + D_FOURIER
_C_K = _C_Q + DK_TOT
_C_V = _C_K + DK_TOT
_C_R = _C_V + DV_TOT
_C_G = _C_R + DV_TOT


def _mixin_kernel(x_ref, mod_ref, gain_ref, w_ref, wt_ref, wal_ref, bal_ref,
                  f_ref, q_ref, k_ref, v_ref, r_ref, g_ref, laf_ref, lab_ref, *scratch, d_model, fft_layout):
    x = x_ref[...]
    shift = mod_ref[3:4, :]
    scale = mod_ref[4:5, :]
    h = _modulated_rms(x, gain_ref[...], scale, shift).astype(BF16)

    def proj(lo, hi):
        return _dot(h, w_ref[:, lo:hi])

    alr = _dot(h, wt_ref[:, 2 * d_model:2 * d_model + ALR_PAD]).astype(BF16)
    g_ref[...] = _dot(h, wt_ref[:, :2 * d_model]).astype(BF16)
    pre = _dot(alr, wal_ref[...]) + bal_ref[...]
    la = (jnp.minimum(pre, 0.0) * (LOG2_E / GATE_TEMP)
          - jnp.log2(1.0 + jnp.exp2(jnp.abs(pre) * (-LOG2_E))) * (1.0 / GATE_TEMP))
    laf_ref[...] = la[:, :DK_TOT]
    lab_ref[...] = la[:, DK_TOT:]

    r_ref[...] = proj(_C_R, _C_G).astype(BF16)
    v_ref[...] = proj(_C_V, _C_R).astype(BF16)
    f = proj(_C_F, _C_Q)
    if fft_layout:
        slab_ref, = scratch
        rows = x.shape[0] // FFT_N
        for a in range(rows):
            for g in range(D_FOURIER // LANES):
                slab_ref[g, pl.ds(a, FFT_N, stride=rows), :] = f[a * FFT_N:(a + 1) * FFT_N, g * LANES:(g + 1) * LANES]
        for g in range(D_FOURIER // LANES):
            f_ref[:, :, g * LANES:(g + 1) * LANES] = slab_ref[g].reshape(FFT_N, rows, LANES)
    else:
        f_ref[...] = f
    q_ref[...] = proj(_C_Q, _C_K) * (DK ** -0.5)
    k_ref[...] = proj(_C_K, _C_V)


def _mixin(x, mod, gain, w_main, w_tail, w_alpha, b_alpha, *, tiles_per_group, fft_layout):
    n, d = x.shape
    tm = TOKEN_TILE

    def tok(width):
        return pl.BlockSpec((tm, width), lambda i: (i, 0))

    outs = [(D_FOURIER, F32), (DK_TOT, F32), (DK_TOT, F32), (DV_TOT, BF16), (DV_TOT, BF16),
            (2 * d, BF16), (DK_TOT, F32), (DK_TOT, F32)]
    out_specs = [tok(w) for w, _ in outs]
    out_shape = [jax.ShapeDtypeStruct((n, w), dt) for w, dt in outs]
    scratch = []
    if fft_layout:
        rows = tm // FFT_N
        tiles = FFT_N // rows
        assert tm % FFT_N == 0 and rows % 8 == 0 and n % (FFT_N * FFT_N) == 0
        out_specs[0] = pl.BlockSpec((None, FFT_N, rows, D_FOURIER), lambda i: (i // tiles, 0, i % tiles, 0))
        out_shape[0] = jax.ShapeDtypeStruct((n // (FFT_N * FFT_N), FFT_N, FFT_N, D_FOURIER), F32)
        scratch = [pltpu.VMEM((D_FOURIER // LANES, tm, LANES), F32)]
    return pl.pallas_call(
        functools.partial(_mixin_kernel, d_model=d, fft_layout=fft_layout),
        grid=(n // tm,),
        in_specs=[tok(d),
                  pl.BlockSpec((None, N_MOD, d), lambda i: (i // tiles_per_group, 0, 0)),
                  _resident((1, d)), _resident(w_main.shape), _resident(w_tail.shape),
                  _resident((ALR_PAD, 2 * DK_TOT)), _resident((1, 2 * DK_TOT))],
        out_specs=out_specs,
        out_shape=out_shape,
        scratch_shapes=scratch,
        compiler_params=_params(1),
        name="mixin",
    )(x, mod, gain.reshape(1, d), w_main, w_tail, w_alpha, b_alpha)


def _dft_cos_sin(n):
    k = np.arange(n)
    ang = 2.0 * np.pi * ((k[:, None] * k[None, :]) % n) / n
    return np.cos(ang), np.sin(ang)


def _split_hi_lo(x):
    hi = x.astype(BF16)
    return hi, (x - hi.astype(F32)).astype(BF16)


def _table_pieces(table, axis):
    hi, lo = _split_hi_lo(jnp.asarray(table.astype(np.float32)))
    return jnp.concatenate([hi, hi], axis=axis), lo


def _dot_data_table(x, cat_ref, lo_ref):
    hi, lo = _split_hi_lo(x)
    return _dot(jnp.concatenate([hi, lo], axis=1), cat_ref[...]) + _dot(hi, lo_ref[...])


def _dot_table_data(cat, lo_t, x):
    hi, lo = _split_hi_lo(x)
    return _dot(cat, jnp.concatenate([hi, lo], axis=0)) + _dot(lo_t, hi)


def _width_dft_table(t_len):
    c, s = _dft_cos_sin(GROUP_W)
    return np.concatenate([c, -s], axis=1) / math.sqrt(t_len * GROUP_W)


def _width_dft(x, cwc_ref, cwl_ref):
    zr, zi = [], []
    for g in range(N_GROUPS):
        z = _dot_data_table(x[:, g * GROUP_W:(g + 1) * GROUP_W], cwc_ref, cwl_ref)
        zr.append(z[:, :GROUP_W])
        zi.append(z[:, GROUP_W:])
    return jnp.concatenate(zr, axis=1), jnp.concatenate(zi, axis=1)


def _fourier_dense_kernel(x_ref, cwc_ref, cwl_ref, ftc_ref, ftl_ref, o_ref):
    zr, zi = _width_dft(x_ref[...], cwc_ref, cwl_ref)
    zz = jnp.concatenate([zr, zi], axis=0)
    o_ref[...] = _dot_table_data(ftc_ref[...], ftl_ref[...], zz)


def _fourier_dense(f, nb, t_len):
    c, s = _dft_cos_sin(t_len)
    consts = [*_table_pieces(_width_dft_table(t_len), 0), *_table_pieces(np.concatenate([c, s], axis=1), 1)]
    return pl.pallas_call(
        _fourier_dense_kernel,
        grid=(nb,),
        in_specs=[pl.BlockSpec((t_len, D_FOURIER), lambda b: (b, 0))] + [_resident(a.shape) for a in consts],
        out_specs=pl.BlockSpec((t_len, D_FOURIER), lambda b: (b, 0)),
        out_shape=jax.ShapeDtypeStruct(f.shape, F32),
        compiler_params=_params(1),
        name="fourier_dense",
    )(f, *consts)


FFT_CHANNELS = 2 * GROUP_W


def _fourier_two_stage_kernel(x_ref, cwc_ref, cwl_ref, mc_ref, ml_ref, f2c_ref, f2l_ref, o_ref,
                              ar_ref, ai_ref, sr_ref, si_ref):
    groups = FFT_CHANNELS // GROUP_W

    def stage1(jb, carry):
        t2_0 = pl.multiple_of(jb * FFT_ROWS, FFT_ROWS)
        x = x_ref[pl.ds(t2_0, FFT_ROWS)].reshape(FFT_ROWS * FFT_N, FFT_CHANNELS)
        zr, zi = [], []
        for g in range(groups):
            z = _dot_data_table(x[:, g * GROUP_W:(g + 1) * GROUP_W], cwc_ref, cwl_ref)
            zr.append(z[:, :GROUP_W])
            zi.append(z[:, GROUP_W:])
        zr = jnp.concatenate(zr, axis=1)
        zi = jnp.concatenate(zi, axis=1)
        for j in range(FFT_ROWS):
            rows = slice(j * FFT_N, (j + 1) * FFT_N)
            a = _dot_table_data(mc_ref[t2_0 + j], ml_ref[t2_0 + j], jnp.concatenate([zr[rows], zi[rows]], axis=0))
            for g in range(FFT_CHANNELS // LANES):
                lanes = slice(g * LANES, (g + 1) * LANES)
                sr_ref[g, pl.ds(j, FFT_N, stride=FFT_ROWS), :] = a[:FFT_N, lanes]
                si_ref[g, pl.ds(j, FFT_N, stride=FFT_ROWS), :] = a[FFT_N:, lanes]
        for g in range(FFT_CHANNELS // LANES):
            lanes = slice(g * LANES, (g + 1) * LANES)
            ar_ref[:, pl.ds(t2_0, FFT_ROWS), lanes] = sr_ref[g].reshape(FFT_N, FFT_ROWS, LANES)
            ai_ref[:, pl.ds(t2_0, FFT_ROWS), lanes] = si_ref[g].reshape(FFT_N, FFT_ROWS, LANES)
        return carry

    def stage2(jb, carry):
        for j in range(FFT_ROWS):
            s1 = jb * FFT_ROWS + j
            aa = jnp.concatenate([ar_ref[s1], ai_ref[s1]], axis=0)
            o_ref[s1] = _dot_table_data(f2c_ref[...], f2l_ref[...], aa)
        return carry

    lax.fori_loop(0, FFT_N // FFT_ROWS, stage1, 0)
    lax.fori_loop(0, FFT_N // FFT_ROWS, stage2, 0)


def _fourier_two_stage(x, nb):
    n = FFT_N
    t_len = n * n
    s1 = np.arange(n)[:, None]
    t1 = np.arange(n)[None, :]
    mats = []
    for t2 in range(n):
        ang = 2.0 * np.pi * ((s1 * t1 * n + s1 * t2) % t_len) / t_len
        c, s = np.cos(ang), np.sin(ang)
        mats.append(np.block([[c, s], [-s, c]]))
    m_cat, m_lo = _table_pieces(np.stack(mats), 2)
    c2, s2 = _dft_cos_sin(n)
    f2 = _table_pieces(np.concatenate([c2, s2], axis=1), 1)
    cw = _table_pieces(_width_dft_table(t_len), 0)
    consts = [*cw, m_cat, m_lo, *f2]

    blk = pl.BlockSpec((None, n, n, FFT_CHANNELS), lambda b, h: (b, 0, 0, h))
    plane = pltpu.VMEM((n, n, FFT_CHANNELS), F32)
    slabs = pltpu.VMEM((FFT_CHANNELS // LANES, n * FFT_ROWS, LANES), F32)
    return pl.pallas_call(
        _fourier_two_stage_kernel,
        grid=(nb, D_FOURIER // FFT_CHANNELS),
        in_specs=[blk] + [_resident(a.shape) for a in consts],
        out_specs=blk,
        out_shape=jax.ShapeDtypeStruct((nb, n, n, D_FOURIER), F32),
        scratch_shapes=[plane, plane, slabs, slabs],
        compiler_params=_params(2),
        name="fourier_two_stage",
    )(x, *consts)


def _gla_masks(chunk):
    idx = np.arange(chunk)
    i = idx[:, None]
    j = idx[None, :]
    masks = [i == j]
    for l in range(GLA_LEVELS):
        same = (i >> (l + 1)) == (j >> (l + 1))
        masks.append(same & (((i >> l) & 1) == 1) & (((j >> l) & 1) == 0))
    mask_f = np.stack(masks).astype(np.float32)
    return mask_f, np.transpose(mask_f, (0, 2, 1)).copy()


_NT = (((1,), (1,)), ((), ()))
_TN = (((0,), (0,)), ((), ()))
SUBLANES = 8


def _block_sums(la, forward):
    c = la.shape[0]
    nv = c // SUBLANES
    x = la.reshape(nv, SUBLANES, DK)
    sub = lax.broadcasted_iota(jnp.int32, (1, SUBLANES, DK), 1)
    near = x
    tot = x
    out = {}
    w = 1
    while w < SUBLANES:
        right = (sub & w) != 0
        partner = jnp.where(right, pltpu.roll(tot, w, axis=1), pltpu.roll(tot, SUBLANES - w, axis=1))
        near = near + jnp.where(right if forward else jnp.logical_not(right), partner, 0.0)
        tot = tot + partner
        w *= 2
        out[w] = (near, tot - near)
    blk = 1
    while w < c:
        takes = (lambda v: v & blk) if forward else (lambda v: not v & blk)
        near = jnp.stack([near[v] + tot[v ^ blk] if takes(v) else near[v] for v in range(nv)])
        tot = jnp.stack([tot[v] + tot[v ^ blk] for v in range(nv)])
        w *= 2
        blk *= 2
        out[w] = (near, tot - near)
    return {w: (a.reshape(c, DK), b.reshape(c, DK)) for w, (a, b) in out.items()}


def _gla_prepare(q_ref, k_ref, laf_ref, lab_ref, qs_ref, ks_ref, tot_ref, rows_f, rows_b):
    c = GLA_CHUNK
    for g in range(len(rows_f)):
        for fwd in (True, False):
            chain = 2 * g + (0 if fwd else 1)
            rows = pl.ds(rows_f[g] if fwd else rows_b[g], c)
            q = q_ref[rows, :]
            k = k_ref[rows, :]
            la = (laf_ref if fwd else lab_ref)[rows, :]
            sums = _block_sums(la, fwd)
            qs_ref[chain, 0:c, :] = q.astype(BF16)
            ks_ref[chain, 0:c, :] = k.astype(BF16)
            for l in range(GLA_LEVELS + 1):
                if l == 0:
                    d, e = jnp.exp2(la), None
                else:
                    near, far = sums[1 << l]
                    d, e = jnp.exp2(near), jnp.exp2(far)
                qs_ref[chain, (l + 1) * c:(l + 2) * c, :] = (q * d).astype(BF16)
                if e is not None:
                    ks_ref[chain, l * c:(l + 1) * c, :] = (k * e).astype(BF16)
            tot_ref[chain] = jnp.broadcast_to(d[c - 1:c, :] if fwd else d[0:1, :], (SUBLANES, DK))


def _gla_apply(v_ref, maskf_ref, maskb_ref, sf_ref, sb_ref, o_ref, work):
    c = GLA_CHUNK
    levels = GLA_LEVELS
    chains = []
    for qs_ref, ks_ref, tot_ref, rows_f, rows_b, accumulate in work:
        for chain in range(2 * len(rows_f)):
            fwd = chain % 2 == 0
            rows = pl.ds((rows_f if fwd else rows_b)[chain // 2], c)
            chains.append((qs_ref, ks_ref, tot_ref, chain, fwd, rows, accumulate))
    scores = []
    for qs_ref, ks_ref, tot_ref, chain, fwd, rows, accumulate in chains:
        m_ref = maskf_ref if fwd else maskb_ref
        s = None
        for l in range(levels + 1):
            kl = max(l - 1, 0)
            p = lax.dot_general(qs_ref[chain, l * c:(l + 1) * c, :], ks_ref[chain, kl * c:(kl + 1) * c, :], _NT,
                                preferred_element_type=F32) * m_ref[l]
            s = p if s is None else s + p
        scores.append(s.astype(BF16))
    for (qs_ref, ks_ref, tot_ref, chain, fwd, rows, accumulate), s in zip(chains, scores):
        st_ref = sf_ref if fwd else sb_ref
        v = v_ref[rows, :]
        st = st_ref[...]
        q_top = qs_ref[chain, (levels + 1) * c:(levels + 2) * c, :]
        k_top = ks_ref[chain, levels * c:(levels + 1) * c, :]
        o = _dot(s, v) + _dot(q_top, st.astype(BF16))
        tot = jnp.broadcast_to(tot_ref[chain][0:1, :], (DK, DK)).T
        st_ref[...] = (st * jnp.concatenate([tot] * (DV // DK), axis=1)
                       + lax.dot_general(k_top, v, _TN, preferred_element_type=F32))
        if accumulate:
            o_ref[rows, :] += o
        else:
            o_ref[rows, :] = o


def _gla_kernel(q_ref, k_ref, v_ref, laf_ref, lab_ref, s0f_ref, s0b_ref, maskf_ref, maskb_ref,
                o_ref, sf_ref, sb_ref, qs0_ref, ks0_ref, tot0_ref, qs1_ref, ks1_ref, tot1_ref,
                *, n_chunks, group):
    sf_ref[...] = s0f_ref[...]
    sb_ref[...] = s0b_ref[...]
    steps = n_chunks // group
    slots = ((qs0_ref, ks0_ref, tot0_ref), (qs1_ref, ks1_ref, tot1_ref))

    def rows_of(i):
        rows_f = [pl.multiple_of((i * group + g) * GLA_CHUNK, GLA_CHUNK) for g in range(group)]
        rows_b = [pl.multiple_of((n_chunks - 1 - i * group - g) * GLA_CHUNK, GLA_CHUNK) for g in range(group)]
        return rows_f, rows_b

    def prepare(i, slot):
        _gla_prepare(q_ref, k_ref, laf_ref, lab_ref, *slots[slot], *rows_of(i))

    def apply(items):
        _gla_apply(v_ref, maskf_ref, maskb_ref, sf_ref, sb_ref, o_ref,
                   [(*slots[slot], *rows_of(i), accumulate) for i, slot, accumulate in items])

    def pair(j, accumulate, last):
        apply([(2 * j, 0, accumulate[0]), (2 * j + 1, 1, accumulate[1])])
        if not last:
            prepare(2 * j + 2, 0)
            prepare(2 * j + 3, 1)

    n_pairs = steps // 2
    prepare(0, 0)
    prepare(1, 1)
    if n_pairs == 1:
        pair(0, (False, True), True)
    else:
        def body(accumulate):
            def f(j, carry):
                pair(j, (accumulate, accumulate), False)
                return carry
            return f
        lax.fori_loop(0, n_pairs // 2, body(False), 0)
        lax.fori_loop(n_pairs // 2, n_pairs - 1, body(True), 0)
        pair(n_pairs - 1, (True, True), True)


def _gla(q, k, v, laf, lab, s0f, s0b, nb, t_len):
    n_chunks = t_len // GLA_CHUNK
    group = next(g for g in (GLA_GROUP, GLA_GROUP // 2) if n_chunks % (4 * g) == 0 or n_chunks == 2 * g)
    steps = n_chunks // group
    assert t_len % GLA_CHUNK == 0 and n_chunks % group == 0 and (steps == 2 or steps % 4 == 0)
    consts = [jnp.asarray(m) for m in _gla_masks(GLA_CHUNK)]

    def seq(width):
        return pl.BlockSpec((t_len, width), lambda b, h: (b, h))

    state = pl.BlockSpec((None, None, DK, DV), lambda b, h: (b, h, 0, 0))
    n = nb * t_len
    slot = [pltpu.VMEM((2 * group, (GLA_LEVELS + 2) * GLA_CHUNK, DK), BF16),
            pltpu.VMEM((2 * group, (GLA_LEVELS + 1) * GLA_CHUNK, DK), BF16),
            pltpu.VMEM((2 * group, SUBLANES, DK), F32)]
    return pl.pallas_call(
        functools.partial(_gla_kernel, n_chunks=n_chunks, group=group),
        grid=(nb, N_HEADS),
        in_specs=[seq(DK), seq(DK), seq(DV), seq(DK), seq(DK), state, state]
                 + [_resident(a.shape) for a in consts],
        out_specs=[seq(DV), state, state],
        out_shape=[jax.ShapeDtypeStruct((n, DV_TOT), F32),
                   jax.ShapeDtypeStruct((nb, N_HEADS, DK, DV), F32),
                   jax.ShapeDtypeStruct((nb, N_HEADS, DK, DV), F32)],
        scratch_shapes=slot + slot,
        compiler_params=_params(2),
        name="gla",
    )(q, k, v, laf, lab, s0f, s0b, *consts)


def _mixout_kernel(x_ref, mod_ref, m_ref, o_ref, r_ref, g_ref, gn_ref, wpf_ref, wpg_ref, wout_ref, y_ref,
                   *scratch, d_model, fft_layout):
    if fft_layout:
        slab_ref, = scratch
        rows = m_ref.shape[1]
        for g in range(D_FOURIER // LANES):
            slab_ref[g] = m_ref[:, :, g * LANES:(g + 1) * LANES].reshape(FFT_N * rows, LANES)
        m = jnp.concatenate(
            [jnp.concatenate([slab_ref[g, pl.ds(s, FFT_N, stride=rows), :] for g in range(D_FOURIER // LANES)],
                             axis=1) for s in range(rows)], axis=0)
    else:
        m = m_ref[...]
    branch_a = _dot(m.astype(BF16), wpf_ref[...])
    o = o_ref[...]
    parts = []
    for h in range(N_HEADS):
        oh = o[:, h * DV:(h + 1) * DV]
        parts.append(oh * lax.rsqrt(jnp.mean(oh * oh, axis=-1, keepdims=True) + RMS_EPS))
    r = r_ref[...].astype(F32)
    on = jnp.concatenate(parts, axis=1) * (gn_ref[...] * r * _sigmoid(r))
    branch_b = _dot(on.astype(BF16), wpg_ref[...])
    g = _sigmoid(g_ref[...].astype(F32))
    merged = g[:, :d_model] * branch_a + g[:, d_model:] * branch_b
    y = _dot(merged.astype(BF16), wout_ref[...])
    y_ref[...] = x_ref[...] + mod_ref[5:6, :] * y


def _mixout(x, mod, m, o, r, g, gla_norm, wpf, wpg, wout, *, tiles_per_group, fft_layout):
    n, d = x.shape
    tm = TOKEN_TILE

    def tok(width):
        return pl.BlockSpec((tm, width), lambda i: (i, 0))

    m_spec = tok(D_FOURIER)
    scratch = []
    if fft_layout:
        rows = tm // FFT_N
        tiles = FFT_N // rows
        m_spec = pl.BlockSpec((None, FFT_N, rows, D_FOURIER), lambda i: (i // tiles, 0, i % tiles, 0))
        scratch = [pltpu.VMEM((D_FOURIER // LANES, tm, LANES), F32)]
    return pl.pallas_call(
        functools.partial(_mixout_kernel, d_model=d, fft_layout=fft_layout),
        grid=(n // tm,),
        in_specs=[tok(d),
                  pl.BlockSpec((None, N_MOD, d), lambda i: (i // tiles_per_group, 0, 0)),
                  m_spec, tok(DV_TOT), tok(DV_TOT), tok(2 * d),
                  _resident((1, DV_TOT)), _resident(wpf.shape), _resident(wpg.shape), _resident(wout.shape)],
        out_specs=tok(d),
        out_shape=jax.ShapeDtypeStruct((n, d), F32),
        scratch_shapes=scratch,
        compiler_params=_params(1),
        name="mixout",
    )(x, mod, m, o, r, g, gla_norm.reshape(1, DV_TOT), wpf, wpg, wout)


def _grid_pos_tables(n_tokens, d_model):
    n_freq = d_model // 4
    omega = POS_BASE ** (-jnp.arange(n_freq, dtype=F32) / n_freq)
    ra = jnp.arange(n_tokens // GRID_W, dtype=F32)[:, None] * omega
    ca = jnp.arange(GRID_W, dtype=F32)[:, None] * omega
    return (jnp.concatenate([jnp.sin(ra), jnp.cos(ra)], axis=-1),
            jnp.concatenate([jnp.sin(ca), jnp.cos(ca)], axis=-1))


def _trunk_layer(x, mod, s0f, s0b, nb, t_len, w, *, pos, final_gain, two_stage_fft):
    tiles = max(t_len // TOKEN_TILE, 1) if mod.shape[0] > 1 else x.shape[0] // TOKEN_TILE
    x = _ffn(x, mod, w["norm_ffn1"], w["ffn1_gate"], w["ffn1_up"], w["ffn1_down"],
             tiles_per_group=tiles, first_mod_row=0, pos=pos)
    f, q, k, v, r, g, laf, lab = _mixin(x, mod, w["norm_mix"], w["w_main"], w["w_tail"], w["w_alpha"],
                                        w["b_alpha"],
                                        tiles_per_group=tiles, fft_layout=two_stage_fft)
    m = _fourier_two_stage(f, nb) if two_stage_fft else _fourier_dense(f, nb, t_len)
    o, sf, sb = _gla(q, k, v, laf, lab, s0f, s0b, nb, t_len)
    x = _mixout(x, mod, m, o, r, g, w["gla_norm"], w["proj_fourier"], w["proj_gla"], w["w_out"],
                tiles_per_group=tiles, fft_layout=two_stage_fft)
    x = _ffn(x, mod, w["norm_ffn2"], w["ffn2_gate"], w["ffn2_up"], w["ffn2_down"],
             tiles_per_group=tiles, first_mod_row=6, final_gain=final_gain)
    return x, sf, sb


def kernel(x_prompt, x_sample, state_gla_fwd, state_gla_bwd, c, c_ctx, w_ada, b_ada, norm_ffn1, w_ffn1_gate, w_ffn1_up, w_ffn1_down, norm_mix, w_in, w_alpha_fwd, b_alpha_fwd, w_alpha_bwd, b_alpha_bwd, gla_norm, w_proj_fourier, w_proj_gla, w_out, norm_ffn2, w_ffn2_gate, w_ffn2_up, w_ffn2_down, final_norm):
    nb_ctx, t_ctx, d = x_prompt.shape
    nb_lat, t_lat, _ = x_sample.shape
    depth = w_ada.shape[0]
    assert t_lat == FFT_N * FFT_N and (nb_ctx * t_ctx) % TOKEN_TILE == 0 and t_lat % TOKEN_TILE == 0

    xc = x_prompt.reshape(nb_ctx * t_ctx, d)
    xl = x_sample.reshape(nb_lat * t_lat, d)
    pos = _grid_pos_tables(t_lat, d)
    zero_state = jnp.zeros((nb_ctx, N_HEADS, DK, DV), F32)
    pad_rows = (-(nb_lat + 1)) % 8
    cvecs = jnp.concatenate([c, c_ctx[None, :], jnp.zeros((pad_rows, d), F32)], axis=0)

    new_fwd, new_bwd = [], []
    for l in range(depth):
        cut = _C_G
        w_in_l = w_in[l]
        w_main = w_in_l[:, :cut].astype(BF16)
        w_tail = jnp.concatenate(
            [w_in_l[:, cut + 2 * GATE_RANK:].astype(BF16), w_in_l[:, cut:cut + 2 * GATE_RANK].astype(BF16),
             jnp.zeros((d, ALR_PAD - 2 * GATE_RANK), BF16)], axis=1)
        w_alpha = jnp.zeros((ALR_PAD, 2 * DK_TOT), F32)
        w_alpha = w_alpha.at[:GATE_RANK, :DK_TOT].set(w_alpha_fwd[l])
        w_alpha = w_alpha.at[GATE_RANK:2 * GATE_RANK, DK_TOT:].set(w_alpha_bwd[l]).astype(BF16)
        b_alpha = jnp.concatenate([b_alpha_fwd[l], b_alpha_bwd[l]]).reshape(1, 2 * DK_TOT)
        w = {
            "norm_ffn1": norm_ffn1[l], "ffn1_gate": w_ffn1_gate[l].astype(BF16),
            "ffn1_up": w_ffn1_up[l].astype(BF16), "ffn1_down": w_ffn1_down[l].astype(BF16),
            "norm_mix": norm_mix[l], "w_main": w_main, "w_tail": w_tail, "w_alpha": w_alpha, "b_alpha": b_alpha,
            "gla_norm": gla_norm[l], "proj_fourier": w_proj_fourier[l].astype(BF16),
            "proj_gla": w_proj_gla[l].astype(BF16), "w_out": w_out[l].astype(BF16),
            "norm_ffn2": norm_ffn2[l], "ffn2_gate": w_ffn2_gate[l].astype(BF16),
            "ffn2_up": w_ffn2_up[l].astype(BF16), "ffn2_down": w_ffn2_down[l].astype(BF16),
        }
        last = l == depth - 1
        mod = _ada(cvecs, w_ada[l], b_ada[l]).reshape(-1, N_MOD, d)
        mod_lat = mod[:nb_lat]
        mod_ctx = mod[nb_lat:nb_lat + 1]
        xc, sf, sb = _trunk_layer(xc, mod_ctx, zero_state, zero_state, nb_ctx, t_ctx, w,
                                  pos=None, final_gain=final_norm if last else None, two_stage_fft=False)
        new_fwd.append(sf)
        new_bwd.append(sb)
        xl, _, _ = _trunk_layer(xl, mod_lat, state_gla_fwd[:, l], state_gla_bwd[:, l], nb_lat, t_lat, w,
                                pos=pos if l == 0 else None, final_gain=final_norm if last else None,
                                two_stage_fft=True)
    y_prompt = xc.reshape(nb_ctx, t_ctx, d)
    y_sample = xl.reshape(nb_lat, t_lat, d)
    return (y_prompt, y_sample, jnp.stack(new_fwd, axis=1), jnp.stack(new_bwd, axis=1))
```

```python
import functools
import math

import numpy as np
import jax
import jax.numpy as jnp
from jax import lax
from jax.experimental import pallas as pl
from jax.experimental.pallas import tpu as pltpu

F32 = jnp.float32
BF16 = jnp.bfloat16

GRID_W = 64
N_GROUPS = 4
GROUP_W = 128
D_FOURIER = N_GROUPS * GROUP_W
N_HEADS = 4
DK = 128
DV = 256
DK_TOT = N_HEADS * DK
DV_TOT = N_HEADS * DV
GATE_RANK = 16
GATE_TEMP = 16.0
RMS_EPS = 1e-6
POS_BASE = 10000.0
N_MOD = 9

TOKEN_TILE = 512
GLA_CHUNK = 64
GLA_LEVELS = 6
GLA_GROUP = 4
LANES = 128
ALR_PAD = LANES
FFT_N = 64
FFT_ROWS = 8
VMEM_LIMIT = 56 * 1024 * 1024


def _params(n_axes):
    return pltpu.CompilerParams(dimension_semantics=("arbitrary",) * n_axes,
                                vmem_limit_bytes=VMEM_LIMIT)


def _resident(shape):
    nd = len(shape)
    return pl.BlockSpec(shape, lambda *_: (0,) * nd, pipeline_mode=pl.Buffered(1))


LOG2_E = math.log2(math.e)


def _sigmoid(x):
    return 1.0 / (1.0 + jnp.exp2(x * (-LOG2_E)))


def _rms(x, gain):
    return x * lax.rsqrt(jnp.mean(x * x, axis=-1, keepdims=True) + RMS_EPS) * gain


def _modulated_rms(x, gain, scale, shift):
    return x * lax.rsqrt(jnp.mean(x * x, axis=-1, keepdims=True) + RMS_EPS) * (gain * (1.0 + scale)) + shift


def _dot(a, b):
    return jnp.dot(a, b, preferred_element_type=F32)


def _ada_kernel(c_ref, w_ref, b_ref, o_ref):
    c = c_ref[...]
    s = (c * _sigmoid(c)).astype(BF16)
    o_ref[...] = _dot(s, w_ref[...].astype(BF16)) + b_ref[...]


def _ada(cvecs, w_ada, b_ada):
    rows, d = cvecs.shape
    n = w_ada.shape[1]
    tn = d
    return pl.pallas_call(
        _ada_kernel,
        grid=(n // tn,),
        in_specs=[pl.BlockSpec((rows, d), lambda j: (0, 0)),
                  pl.BlockSpec((d, tn), lambda j: (0, j)),
                  pl.BlockSpec((1, tn), lambda j: (0, j))],
        out_specs=pl.BlockSpec((rows, tn), lambda j: (0, j)),
        out_shape=jax.ShapeDtypeStruct((rows, n), F32),
        compiler_params=_params(1),
        name="ada",
    )(cvecs, w_ada, b_ada.reshape(1, n))


def _ffn_kernel(*refs, first_mod_row, add_pos, final_norm):
    x_ref, mod_ref, gain_ref, wg_ref, wu_ref, wd_ref = refs[:6]
    rest = list(refs[6:])
    prow_ref, pcol_ref = (rest.pop(0), rest.pop(0)) if add_pos else (None, None)
    fn_ref = rest.pop(0) if final_norm else None
    o_ref = rest.pop(0)

    x = x_ref[...]
    if add_pos:
        tm, d = x.shape
        rows = tm // GRID_W
        prow = jnp.broadcast_to(prow_ref[...][:, None, :], (rows, GRID_W, d // 2))
        pcol = jnp.broadcast_to(pcol_ref[...][None, :, :], (rows, GRID_W, d // 2))
        x = x + jnp.concatenate([prow, pcol], axis=-1).reshape(tm, d)
    shift = mod_ref[first_mod_row:first_mod_row + 1, :]
    scale = mod_ref[first_mod_row + 1:first_mod_row + 2, :]
    gate = mod_ref[first_mod_row + 2:first_mod_row + 3, :]
    h = _modulated_rms(x, gain_ref[...], scale, shift).astype(BF16)
    g = _dot(h, wg_ref[...])
    u = _dot(h, wu_ref[...])
    a = (g * _sigmoid(g) * u).astype(BF16)
    y = _dot(a, wd_ref[...])
    xn = x + (0.5 * gate) * y
    if final_norm:
        xn = _rms(xn, fn_ref[...])
    o_ref[...] = xn


def _ffn(x, mod, gain, wg, wu, wd, *, tiles_per_group, first_mod_row, pos=None, final_gain=None):
    n, d = x.shape
    dff = wg.shape[1]
    tm = TOKEN_TILE
    add_pos = pos is not None
    final_norm = final_gain is not None
    in_specs = [pl.BlockSpec((tm, d), lambda i: (i, 0)),
                pl.BlockSpec((None, N_MOD, d), lambda i: (i // tiles_per_group, 0, 0)),
                _resident((1, d)), _resident((d, dff)), _resident((d, dff)), _resident((dff, d))]
    args = [x, mod, gain.reshape(1, d), wg, wu, wd]
    if add_pos:
        pos_row, pos_col = pos
        rows = tm // GRID_W
        row_tiles = pos_row.shape[0] // rows
        assert tm % GRID_W == 0 and rows % 8 == 0 and pos_col.shape[0] == GRID_W
        in_specs += [pl.BlockSpec((rows, d // 2), lambda i: (i % row_tiles, 0)), _resident(pos_col.shape)]
        args += [pos_row, pos_col]
    if final_norm:
        in_specs.append(_resident((1, d)))
        args.append(final_gain.reshape(1, d))
    return pl.pallas_call(
        functools.partial(_ffn_kernel, first_mod_row=first_mod_row, add_pos=add_pos, final_norm=final_norm),
        grid=(n // tm,),
        in_specs=in_specs,
        out_specs=pl.BlockSpec((tm, d), lambda i: (i, 0)),
        out_shape=jax.ShapeDtypeStruct((n, d), F32),
        compiler_params=_params(1),
        name="ffn",
    )(*args)


_C_F = 0
_C_Q = _C_F + D_FOURIER
_C_K = _C_Q + DK_TOT
_C_V = _C_K + DK_TOT
_C_R = _C_V + DV_TOT
_C_G = _C_R + DV_TOT


def _mixin_kernel(x_ref, mod_ref, gain_ref, w_ref, wt_ref, wal_ref, bal_ref,
                  f_ref, q_ref, k_ref, v_ref, r_ref, g_ref, laf_ref, lab_ref, *scratch, d_model, fft_layout):
    x = x_ref[...]
    shift = mod_ref[3:4, :]
    scale = mod_ref[4:5, :]
    h = _modulated_rms(x, gain_ref[...], scale, shift).astype(BF16)

    def proj(lo, hi):
        return _dot(h, w_ref[:, lo:hi])

    alr = _dot(h, wt_ref[:, 2 * d_model:2 * d_model + ALR_PAD]).astype(BF16)
    g_ref[...] = _dot(h, wt_ref[:, :2 * d_model]).astype(BF16)
    pre = _dot(alr, wal_ref[...]) + bal_ref[...]
    la = (jnp.minimum(pre, 0.0) * (LOG2_E / GATE_TEMP)
          - jnp.log2(1.0 + jnp.exp2(jnp.abs(pre) * (-LOG2_E))) * (1.0 / GATE_TEMP))
    laf_ref[...] = la[:, :DK_TOT]
    lab_ref[...] = la[:, DK_TOT:]

    r_ref[...] = proj(_C_R, _C_G).astype(BF16)
    v_ref[...] = proj(_C_V, _C_R).astype(BF16)
    f = proj(_C_F, _C_Q)
    if fft_layout:
        slab_ref, = scratch
        rows = x.shape[0] // FFT_N
        for a in range(rows):
            for g in range(D_FOURIER // LANES):
                slab_ref[g, pl.ds(a, FFT_N, stride=rows), :] = f[a * FFT_N:(a + 1) * FFT_N, g * LANES:(g + 1) * LANES]
        for g in range(D_FOURIER // LANES):
            f_ref[:, :, g * LANES:(g + 1) * LANES] = slab_ref[g].reshape(FFT_N, rows, LANES)
    else:
        f_ref[...] = f
    q_ref[...] = proj(_C_Q, _C_K) * (DK ** -0.5)
    k_ref[...] = proj(_C_K, _C_V)


def _mixin(x, mod, gain, w_main, w_tail, w_alpha, b_alpha, *, tiles_per_group, fft_layout):
    n, d = x.shape
    tm = TOKEN_TILE

    def tok(width):
        return pl.BlockSpec((tm, width), lambda i: (i, 0))

    outs = [(D_FOURIER, F32), (DK_TOT, F32), (DK_TOT, F32), (DV_TOT, BF16), (DV_TOT, BF16),
            (2 * d, BF16), (DK_TOT, F32), (DK_TOT, F32)]
    out_specs = [tok(w) for w, _ in outs]
    out_shape = [jax.ShapeDtypeStruct((n, w), dt) for w, dt in outs]
    scratch = []
    if fft_layout:
        rows = tm // FFT_N
        tiles = FFT_N // rows
        assert tm % FFT_N == 0 and rows % 8 == 0 and n % (FFT_N * FFT_N) == 0
        out_specs[0] = pl.BlockSpec((None, FFT_N, rows, D_FOURIER), lambda i: (i // tiles, 0, i % tiles, 0))
        out_shape[0] = jax.ShapeDtypeStruct((n // (FFT_N * FFT_N), FFT_N, FFT_N, D_FOURIER), F32)
        scratch = [pltpu.VMEM((D_FOURIER // LANES, tm, LANES), F32)]
    return pl.pallas_call(
        functools.partial(_mixin_kernel, d_model=d, fft_layout=fft_layout),
        grid=(n // tm,),
        in_specs=[tok(d),
                  pl.BlockSpec((None, N_MOD, d), lambda i: (i // tiles_per_group, 0, 0)),
                  _resident((1, d)), _resident(w_main.shape), _resident(w_tail.shape),
                  _resident((ALR_PAD, 2 * DK_TOT)), _resident((1, 2 * DK_TOT))],
        out_specs=out_specs,
        out_shape=out_shape,
        scratch_shapes=scratch,
        compiler_params=_params(1),
        name="mixin",
    )(x, mod, gain.reshape(1, d), w_main, w_tail, w_alpha, b_alpha)


def _dft_cos_sin(n):
    k = np.arange(n)
    ang = 2.0 * np.pi * ((k[:, None] * k[None, :]) % n) / n
    return np.cos(ang), np.sin(ang)


def _split_hi_lo(x):
    hi = x.astype(BF16)
    return hi, (x - hi.astype(F32)).astype(BF16)


def _table_pieces(table, axis):
    hi, lo = _split_hi_lo(jnp.asarray(table.astype(np.float32)))
    return jnp.concatenate([hi, hi], axis=axis), lo


def _dot_data_table(x, cat_ref, lo_ref):
    hi, lo = _split_hi_lo(x)
    return _dot(jnp.concatenate([hi, lo], axis=1), cat_ref[...]) + _dot(hi, lo_ref[...])


def _dot_table_data(cat, lo_t, x):
    hi, lo = _split_hi_lo(x)
    return _dot(cat, jnp.concatenate([hi, lo], axis=0)) + _dot(lo_t, hi)


def _width_dft_table(t_len):
    c, s = _dft_cos_sin(GROUP_W)
    return np.concatenate([c, -s], axis=1) / math.sqrt(t_len * GROUP_W)


def _width_dft(x, cwc_ref, cwl_ref):
    zr, zi = [], []
    for g in range(N_GROUPS):
        z = _dot_data_table(x[:, g * GROUP_W:(g + 1) * GROUP_W], cwc_ref, cwl_ref)
        zr.append(z[:, :GROUP_W])
        zi.append(z[:, GROUP_W:])
    return jnp.concatenate(zr, axis=1), jnp.concatenate(zi, axis=1)


DENSE_SEQS = 4


def _fourier_dense_kernel(x_ref, cwc_ref, cwl_ref, ftc_ref, ftl_ref, o_ref, *, t_len):
    zr, zi = _width_dft(x_ref[...], cwc_ref, cwl_ref)
    for s in range(x_ref.shape[0] // t_len):
        rows = slice(s * t_len, (s + 1) * t_len)
        zz = jnp.concatenate([zr[rows], zi[rows]], axis=0)
        o_ref[rows, :] = _dot_table_data(ftc_ref[...], ftl_ref[...], zz)


def _fourier_dense(f, nb, t_len):
    c, s = _dft_cos_sin(t_len)
    consts = [*_table_pieces(_width_dft_table(t_len), 0), *_table_pieces(np.concatenate([c, s], axis=1), 1)]
    seqs = math.gcd(nb, DENSE_SEQS)
    blk = pl.BlockSpec((seqs * t_len, D_FOURIER), lambda b: (b, 0))
    return pl.pallas_call(
        functools.partial(_fourier_dense_kernel, t_len=t_len),
        grid=(nb // seqs,),
        in_specs=[blk] + [_resident(a.shape) for a in consts],
        out_specs=blk,
        out_shape=jax.ShapeDtypeStruct(f.shape, F32),
        compiler_params=_params(1),
        name="fourier_dense",
    )(f, *consts)


FFT_CHANNELS = 2 * GROUP_W


def _fourier_two_stage_kernel(x_ref, cwc_ref, cwl_ref, mc_ref, ml_ref, f2c_ref, f2l_ref, o_ref,
                              ar_ref, ai_ref, sr_ref, si_ref):
    groups = FFT_CHANNELS // GROUP_W

    def stage1(jb, carry):
        t2_0 = pl.multiple_of(jb * FFT_ROWS, FFT_ROWS)
        x = x_ref[pl.ds(t2_0, FFT_ROWS)].reshape(FFT_ROWS * FFT_N, FFT_CHANNELS)
        zr, zi = [], []
        for g in range(groups):
            z = _dot_data_table(x[:, g * GROUP_W:(g + 1) * GROUP_W], cwc_ref, cwl_ref)
            zr.append(z[:, :GROUP_W])
            zi.append(z[:, GROUP_W:])
        zr = jnp.concatenate(zr, axis=1)
        zi = jnp.concatenate(zi, axis=1)
        for j in range(FFT_ROWS):
            rows = slice(j * FFT_N, (j + 1) * FFT_N)
            a = _dot_table_data(mc_ref[t2_0 + j], ml_ref[t2_0 + j], jnp.concatenate([zr[rows], zi[rows]], axis=0))
            for g in range(FFT_CHANNELS // LANES):
                lanes = slice(g * LANES, (g + 1) * LANES)
                sr_ref[g, pl.ds(j, FFT_N, stride=FFT_ROWS), :] = a[:FFT_N, lanes]
                si_ref[g, pl.ds(j, FFT_N, stride=FFT_ROWS), :] = a[FFT_N:, lanes]
        for g in range(FFT_CHANNELS // LANES):
            lanes = slice(g * LANES, (g + 1) * LANES)
            ar_ref[:, pl.ds(t2_0, FFT_ROWS), lanes] = sr_ref[g].reshape(FFT_N, FFT_ROWS, LANES)
            ai_ref[:, pl.ds(t2_0, FFT_ROWS), lanes] = si_ref[g].reshape(FFT_N, FFT_ROWS, LANES)
        return carry

    def stage2(jb, carry):
        for j in range(FFT_ROWS):
            s1 = jb * FFT_ROWS + j
            aa = jnp.concatenate([ar_ref[s1], ai_ref[s1]], axis=0)
            o_ref[s1] = _dot_table_data(f2c_ref[...], f2l_ref[...], aa)
        return carry

    lax.fori_loop(0, FFT_N // FFT_ROWS, stage1, 0)
    lax.fori_loop(0, FFT_N // FFT_ROWS, stage2, 0)


def _fourier_two_stage(x, nb):
    n = FFT_N
    t_len = n * n
    s1 = np.arange(n)[:, None]
    t1 = np.arange(n)[None, :]
    mats = []
    for t2 in range(n):
        ang = 2.0 * np.pi * ((s1 * t1 * n + s1 * t2) % t_len) / t_len
        c, s = np.cos(ang), np.sin(ang)
        mats.append(np.block([[c, s], [-s, c]]))
    m_cat, m_lo = _table_pieces(np.stack(mats), 2)
    c2, s2 = _dft_cos_sin(n)
    f2 = _table_pieces(np.concatenate([c2, s2], axis=1), 1)
    cw = _table_pieces(_width_dft_table(t_len), 0)
    consts = [*cw, m_cat, m_lo, *f2]

    blk = pl.BlockSpec((None, n, n, FFT_CHANNELS), lambda b, h: (b, 0, 0, h))
    plane = pltpu.VMEM((n, n, FFT_CHANNELS), F32)
    slabs = pltpu.VMEM((FFT_CHANNELS // LANES, n * FFT_ROWS, LANES), F32)
    return pl.pallas_call(
        _fourier_two_stage_kernel,
        grid=(nb, D_FOURIER // FFT_CHANNELS),
        in_specs=[blk] + [_resident(a.shape) for a in consts],
        out_specs=blk,
        out_shape=jax.ShapeDtypeStruct((nb, n, n, D_FOURIER), F32),
        scratch_shapes=[plane, plane, slabs, slabs],
        compiler_params=_params(2),
        name="fourier_two_stage",
    )(x, *consts)


def _gla_masks(chunk):
    idx = np.arange(chunk)
    i = idx[:, None]
    j = idx[None, :]
    masks = [i == j]
    for l in range(GLA_LEVELS):
        same = (i >> (l + 1)) == (j >> (l + 1))
        masks.append(same & (((i >> l) & 1) == 1) & (((j >> l) & 1) == 0))
    mask_f = np.stack(masks).astype(np.float32)
    return mask_f, np.transpose(mask_f, (0, 2, 1)).copy()


_NT = (((1,), (1,)), ((), ()))
_TN = (((0,), (0,)), ((), ()))
SUBLANES = 8


def _block_sums(la, forward):
    c = la.shape[0]
    nv = c // SUBLANES
    x = la.reshape(nv, SUBLANES, DK)
    sub = lax.broadcasted_iota(jnp.int32, (1, SUBLANES, DK), 1)
    near = x
    tot = x
    out = {}
    w = 1
    while w < SUBLANES:
        right = (sub & w) != 0
        partner = jnp.where(right, pltpu.roll(tot, w, axis=1), pltpu.roll(tot, SUBLANES - w, axis=1))
        near = near + jnp.where(right if forward else jnp.logical_not(right), partner, 0.0)
        tot = tot + partner
        w *= 2
        out[w] = (near, tot - near)
    blk = 1
    while w < c:
        takes = (lambda v: v & blk) if forward else (lambda v: not v & blk)
        near = jnp.stack([near[v] + tot[v ^ blk] if takes(v) else near[v] for v in range(nv)])
        tot = jnp.stack([tot[v] + tot[v ^ blk] for v in range(nv)])
        w *= 2
        blk *= 2
        out[w] = (near, tot - near)
    return {w: (a.reshape(c, DK), b.reshape(c, DK)) for w, (a, b) in out.items()}


def _gla_prepare(q_ref, k_ref, laf_ref, lab_ref, qs_ref, ks_ref, tot_ref, rows_f, rows_b):
    c = GLA_CHUNK
    for g in range(len(rows_f)):
        for fwd in (True, False):
            chain = 2 * g + (0 if fwd else 1)
            rows = pl.ds(rows_f[g] if fwd else rows_b[g], c)
            q = q_ref[rows, :]
            k = k_ref[rows, :]
            la = (laf_ref if fwd else lab_ref)[rows, :]
            sums = _block_sums(la, fwd)
            qs_ref[chain, 0:c, :] = q.astype(BF16)
            ks_ref[chain, 0:c, :] = k.astype(BF16)
            for l in range(GLA_LEVELS + 1):
                if l == 0:
                    d, e = jnp.exp2(la), None
                else:
                    near, far = sums[1 << l]
                    d, e = jnp.exp2(near), jnp.exp2(far)
                qs_ref[chain, (l + 1) * c:(l + 2) * c, :] = (q * d).astype(BF16)
                if e is not None:
                    ks_ref[chain, l * c:(l + 1) * c, :] = (k * e).astype(BF16)
            tot_ref[chain] = jnp.broadcast_to(d[c - 1:c, :] if fwd else d[0:1, :], (SUBLANES, DK))


def _gla_apply(v_ref, maskf_ref, maskb_ref, sf_ref, sb_ref, o_ref, work):
    c = GLA_CHUNK
    levels = GLA_LEVELS
    chains = []
    for qs_ref, ks_ref, tot_ref, rows_f, rows_b, accumulate in work:
        for chain in range(2 * len(rows_f)):
            fwd = chain % 2 == 0
            rows = pl.ds((rows_f if fwd else rows_b)[chain // 2], c)
            chains.append((qs_ref, ks_ref, tot_ref, chain, fwd, rows, accumulate))
    scores = []
    for qs_ref, ks_ref, tot_ref, chain, fwd, rows, accumulate in chains:
        m_ref = maskf_ref if fwd else maskb_ref
        s = None
        for l in range(levels + 1):
            kl = max(l - 1, 0)
            p = lax.dot_general(qs_ref[chain, l * c:(l + 1) * c, :], ks_ref[chain, kl * c:(kl + 1) * c, :], _NT,
                                preferred_element_type=F32) * m_ref[l]
            s = p if s is None else s + p
        scores.append(s.astype(BF16))
    for (qs_ref, ks_ref, tot_ref, chain, fwd, rows, accumulate), s in zip(chains, scores):
        st_ref = sf_ref if fwd else sb_ref
        v = v_ref[rows, :]
        st = st_ref[...]
        q_top = qs_ref[chain, (levels + 1) * c:(levels + 2) * c, :]
        k_top = ks_ref[chain, levels * c:(levels + 1) * c, :]
        o = _dot(s, v) + _dot(q_top, st.astype(BF16))
        tot = jnp.broadcast_to(tot_ref[chain][0:1, :], (DK, DK)).T
        st_ref[...] = (st * jnp.concatenate([tot] * (DV // DK), axis=1)
                       + lax.dot_general(k_top, v, _TN, preferred_element_type=F32))
        if accumulate:
            o_ref[rows, :] += o
        else:
            o_ref[rows, :] = o


def _gla_kernel(q_ref, k_ref, v_ref, laf_ref, lab_ref, *refs, n_chunks, group, zero_start):
    (s0f_ref, s0b_ref), refs = (refs[:2], refs[2:]) if not zero_start else ((None, None), refs)
    maskf_ref, maskb_ref, o_ref, sf_ref, sb_ref, qs0_ref, ks0_ref, tot0_ref, qs1_ref, ks1_ref, tot1_ref = refs
    sf_ref[...] = jnp.zeros(sf_ref.shape, F32) if zero_start else s0f_ref[...]
    sb_ref[...] = jnp.zeros(sb_ref.shape, F32) if zero_start else s0b_ref[...]
    steps = n_chunks // group
    slots = ((qs0_ref, ks0_ref, tot0_ref), (qs1_ref, ks1_ref, tot1_ref))

    def rows_of(i):
        rows_f = [pl.multiple_of((i * group + g) * GLA_CHUNK, GLA_CHUNK) for g in range(group)]
        rows_b = [pl.multiple_of((n_chunks - 1 - i * group - g) * GLA_CHUNK, GLA_CHUNK) for g in range(group)]
        return rows_f, rows_b

    def prepare(i, slot):
        _gla_prepare(q_ref, k_ref, laf_ref, lab_ref, *slots[slot], *rows_of(i))

    def apply(items):
        _gla_apply(v_ref, maskf_ref, maskb_ref, sf_ref, sb_ref, o_ref,
                   [(*slots[slot], *rows_of(i), accumulate) for i, slot, accumulate in items])

    def pair(j, accumulate, last):
        apply([(2 * j, 0, accumulate[0]), (2 * j + 1, 1, accumulate[1])])
        if not last:
            prepare(2 * j + 2, 0)
            prepare(2 * j + 3, 1)

    n_pairs = steps // 2
    prepare(0, 0)
    prepare(1, 1)
    if n_pairs == 1:
        pair(0, (False, True), True)
    else:
        def body(accumulate):
            def f(j, carry):
                pair(j, (accumulate, accumulate), False)
                return carry
            return f
        lax.fori_loop(0, n_pairs // 2, body(False), 0)
        lax.fori_loop(n_pairs // 2, n_pairs - 1, body(True), 0)
        pair(n_pairs - 1, (True, True), True)


def _gla(q, k, v, laf, lab, s0, nb, t_len):
    n_chunks = t_len // GLA_CHUNK
    group = next(g for g in (GLA_GROUP, GLA_GROUP // 2) if n_chunks % (4 * g) == 0 or n_chunks == 2 * g)
    steps = n_chunks // group
    assert t_len % GLA_CHUNK == 0 and n_chunks % group == 0 and (steps == 2 or steps % 4 == 0)
    consts = [jnp.asarray(m) for m in _gla_masks(GLA_CHUNK)]

    def seq(width):
        return pl.BlockSpec((t_len, width), lambda b, h: (b, h))

    state = pl.BlockSpec((None, None, DK, DV), lambda b, h: (b, h, 0, 0))
    n = nb * t_len
    slot = [pltpu.VMEM((2 * group, (GLA_LEVELS + 2) * GLA_CHUNK, DK), BF16),
            pltpu.VMEM((2 * group, (GLA_LEVELS + 1) * GLA_CHUNK, DK), BF16),
            pltpu.VMEM((2 * group, SUBLANES, DK), F32)]
    starts = [] if s0 is None else list(s0)
    return pl.pallas_call(
        functools.partial(_gla_kernel, n_chunks=n_chunks, group=group, zero_start=s0 is None),
        grid=(nb, N_HEADS),
        in_specs=[seq(DK), seq(DK), seq(DV), seq(DK), seq(DK)] + [state] * len(starts)
                 + [_resident(a.shape) for a in consts],
        out_specs=[seq(DV), state, state],
        out_shape=[jax.ShapeDtypeStruct((n, DV_TOT), F32),
                   jax.ShapeDtypeStruct((nb, N_HEADS, DK, DV), F32),
                   jax.ShapeDtypeStruct((nb, N_HEADS, DK, DV), F32)],
        scratch_shapes=slot + slot,
        compiler_params=_params(2),
        name="gla",
    )(q, k, v, laf, lab, *starts, *consts)


def _mixout_kernel(x_ref, mod_ref, m_ref, o_ref, r_ref, g_ref, gn_ref, wpf_ref, wpg_ref, wout_ref, y_ref,
                   *scratch, d_model, fft_layout):
    if fft_layout:
        slab_ref, = scratch
        rows = m_ref.shape[1]
        for g in range(D_FOURIER // LANES):
            slab_ref[g] = m_ref[:, :, g * LANES:(g + 1) * LANES].reshape(FFT_N * rows, LANES)
        m = jnp.concatenate(
            [jnp.concatenate([slab_ref[g, pl.ds(s, FFT_N, stride=rows), :] for g in range(D_FOURIER // LANES)],
                             axis=1) for s in range(rows)], axis=0)
    else:
        m = m_ref[...]
    branch_a = _dot(m.astype(BF16), wpf_ref[...])
    o = o_ref[...]
    parts = []
    for h in range(N_HEADS):
        oh = o[:, h * DV:(h + 1) * DV]
        parts.append(oh * lax.rsqrt(jnp.mean(oh * oh, axis=-1, keepdims=True) + RMS_EPS))
    r = r_ref[...].astype(F32)
    on = jnp.concatenate(parts, axis=1) * (gn_ref[...] * r * _sigmoid(r))
    branch_b = _dot(on.astype(BF16), wpg_ref[...])
    g = _sigmoid(g_ref[...].astype(F32))
    merged = g[:, :d_model] * branch_a + g[:, d_model:] * branch_b
    y = _dot(merged.astype(BF16), wout_ref[...])
    y_ref[...] = x_ref[...] + mod_ref[5:6, :] * y


def _mixout(x, mod, m, o, r, g, gla_norm, wpf, wpg, wout, *, tiles_per_group, fft_layout):
    n, d = x.shape
    tm = TOKEN_TILE

    def tok(width):
        return pl.BlockSpec((tm, width), lambda i: (i, 0))

    m_spec = tok(D_FOURIER)
    scratch = []
    if fft_layout:
        rows = tm // FFT_N
        tiles = FFT_N // rows
        m_spec = pl.BlockSpec((None, FFT_N, rows, D_FOURIER), lambda i: (i // tiles, 0, i % tiles, 0))
        scratch = [pltpu.VMEM((D_FOURIER // LANES, tm, LANES), F32)]
    return pl.pallas_call(
        functools.partial(_mixout_kernel, d_model=d, fft_layout=fft_layout),
        grid=(n // tm,),
        in_specs=[tok(d),
                  pl.BlockSpec((None, N_MOD, d), lambda i: (i // tiles_per_group, 0, 0)),
                  m_spec, tok(DV_TOT), tok(DV_TOT), tok(2 * d),
                  _resident((1, DV_TOT)), _resident(wpf.shape), _resident(wpg.shape), _resident(wout.shape)],
        out_specs=tok(d),
        out_shape=jax.ShapeDtypeStruct((n, d), F32),
        scratch_shapes=scratch,
        compiler_params=_params(1),
        name="mixout",
    )(x, mod, m, o, r, g, gla_norm.reshape(1, DV_TOT), wpf, wpg, wout)


def _grid_pos_tables(n_tokens, d_model):
    n_freq = d_model // 4
    omega = POS_BASE ** (-jnp.arange(n_freq, dtype=F32) / n_freq)
    ra = jnp.arange(n_tokens // GRID_W, dtype=F32)[:, None] * omega
    ca = jnp.arange(GRID_W, dtype=F32)[:, None] * omega
    return (jnp.concatenate([jnp.sin(ra), jnp.cos(ra)], axis=-1),
            jnp.concatenate([jnp.sin(ca), jnp.cos(ca)], axis=-1))


def _trunk_layer(x, mod, s0, nb, t_len, w, *, pos, final_gain, two_stage_fft):
    tiles = max(t_len // TOKEN_TILE, 1) if mod.shape[0] > 1 else x.shape[0] // TOKEN_TILE
    x = _ffn(x, mod, w["norm_ffn1"], w["ffn1_gate"], w["ffn1_up"], w["ffn1_down"],
             tiles_per_group=tiles, first_mod_row=0, pos=pos)
    f, q, k, v, r, g, laf, lab = _mixin(x, mod, w["norm_mix"], w["w_main"], w["w_tail"], w["w_alpha"],
                                        w["b_alpha"],
                                        tiles_per_group=tiles, fft_layout=two_stage_fft)
    m = _fourier_two_stage(f, nb) if two_stage_fft else _fourier_dense(f, nb, t_len)
    o, sf, sb = _gla(q, k, v, laf, lab, s0, nb, t_len)
    x = _mixout(x, mod, m, o, r, g, w["gla_norm"], w["proj_fourier"], w["proj_gla"], w["w_out"],
                tiles_per_group=tiles, fft_layout=two_stage_fft)
    x = _ffn(x, mod, w["norm_ffn2"], w["ffn2_gate"], w["ffn2_up"], w["ffn2_down"],
             tiles_per_group=tiles, first_mod_row=6, final_gain=final_gain)
    return x, sf, sb


def kernel(x_prompt, x_sample, state_gla_fwd, state_gla_bwd, c, c_ctx, w_ada, b_ada, norm_ffn1, w_ffn1_gate, w_ffn1_up, w_ffn1_down, norm_mix, w_in, w_alpha_fwd, b_alpha_fwd, w_alpha_bwd, b_alpha_bwd, gla_norm, w_proj_fourier, w_proj_gla, w_out, norm_ffn2, w_ffn2_gate, w_ffn2_up, w_ffn2_down, final_norm):
    nb_ctx, t_ctx, d = x_prompt.shape
    nb_lat, t_lat, _ = x_sample.shape
    depth = w_ada.shape[0]
    assert t_lat == FFT_N * FFT_N and (nb_ctx * t_ctx) % TOKEN_TILE == 0 and t_lat % TOKEN_TILE == 0

    xc = x_prompt.reshape(nb_ctx * t_ctx, d)
    xl = x_sample.reshape(nb_lat * t_lat, d)
    pos = _grid_pos_tables(t_lat, d)
    pad_rows = (-(nb_lat + 1)) % 8
    cvecs = jnp.concatenate([c, c_ctx[None, :], jnp.zeros((pad_rows, d), F32)], axis=0)

    new_fwd, new_bwd = [], []
    for l in range(depth):
        cut = _C_G
        w_in_l = w_in[l]
        w_main = w_in_l[:, :cut].astype(BF16)
        w_tail = jnp.concatenate(
            [w_in_l[:, cut + 2 * GATE_RANK:].astype(BF16), w_in_l[:, cut:cut + 2 * GATE_RANK].astype(BF16),
             jnp.zeros((d, ALR_PAD - 2 * GATE_RANK), BF16)], axis=1)
        w_alpha = jnp.zeros((ALR_PAD, 2 * DK_TOT), F32)
        w_alpha = w_alpha.at[:GATE_RANK, :DK_TOT].set(w_alpha_fwd[l])
        w_alpha = w_alpha.at[GATE_RANK:2 * GATE_RANK, DK_TOT:].set(w_alpha_bwd[l]).astype(BF16)
        b_alpha = jnp.concatenate([b_alpha_fwd[l], b_alpha_bwd[l]]).reshape(1, 2 * DK_TOT)
        w = {
            "norm_ffn1": norm_ffn1[l], "ffn1_gate": w_ffn1_gate[l].astype(BF16),
            "ffn1_up": w_ffn1_up[l].astype(BF16), "ffn1_down": w_ffn1_down[l].astype(BF16),
            "norm_mix": norm_mix[l], "w_main": w_main, "w_tail": w_tail, "w_alpha": w_alpha, "b_alpha": b_alpha,
            "gla_norm": gla_norm[l], "proj_fourier": w_proj_fourier[l].astype(BF16),
            "proj_gla": w_proj_gla[l].astype(BF16), "w_out": w_out[l].astype(BF16),
            "norm_ffn2": norm_ffn2[l], "ffn2_gate": w_ffn2_gate[l].astype(BF16),
            "ffn2_up": w_ffn2_up[l].astype(BF16), "ffn2_down": w_ffn2_down[l].astype(BF16),
        }
        last = l == depth - 1
        mod = _ada(cvecs, w_ada[l], b_ada[l]).reshape(-1, N_MOD, d)
        mod_lat = mod[:nb_lat]
        mod_ctx = mod[nb_lat:nb_lat + 1]
        xc, sf, sb = _trunk_layer(xc, mod_ctx, None, nb_ctx, t_ctx, w,
                                  pos=None, final_gain=final_norm if last else None, two_stage_fft=False)
        new_fwd.append(sf)
        new_bwd.append(sb)
        xl, _, _ = _trunk_layer(xl, mod_lat, (state_gla_fwd[:, l], state_gla_bwd[:, l]), nb_lat, t_lat, w,
                                pos=pos if l == 0 else None, final_gain=final_norm if last else None,
                                two_stage_fft=True)
    y_prompt = xc.reshape(nb_ctx, t_ctx, d)
    y_sample = xl.reshape(nb_lat, t_lat, d)
    return (y_prompt, y_sample, jnp.stack(new_fwd, axis=1), jnp.stack(new_bwd, axis=1))
```

```python
import functools
import math

import numpy as np
import jax
import jax.numpy as jnp
from jax import lax
from jax.experimental import pallas as pl
from jax.experimental.pallas import tpu as pltpu

F32 = jnp.float32
BF16 = jnp.bfloat16

GRID_W = 64
N_GROUPS = 4
GROUP_W = 128
D_FOURIER = N_GROUPS * GROUP_W
N_HEADS = 4
DK = 128
DV = 256
DK_TOT = N_HEADS * DK
DV_TOT = N_HEADS * DV
GATE_RANK = 16
GATE_TEMP = 16.0
RMS_EPS = 1e-6
POS_BASE = 10000.0
N_MOD = 9

TOKEN_TILE = 512
GLA_CHUNK = 64
GLA_LEVELS = 6
GLA_GROUP = 4
LANES = 128
ALR_PAD = LANES
FFT_N = 64
FFT_ROWS = 8
VMEM_LIMIT = 56 * 1024 * 1024


def _params(n_axes):
    return pltpu.CompilerParams(dimension_semantics=("arbitrary",) * n_axes,
                                vmem_limit_bytes=VMEM_LIMIT)


def _resident(shape):
    nd = len(shape)
    return pl.BlockSpec(shape, lambda *_: (0,) * nd, pipeline_mode=pl.Buffered(1))


LOG2_E = math.log2(math.e)


def _sigmoid(x):
    return 1.0 / (1.0 + jnp.exp2(x * (-LOG2_E)))


def _rms(x, gain):
    return x * lax.rsqrt(jnp.mean(x * x, axis=-1, keepdims=True) + RMS_EPS) * gain


def _modulated_rms(x, gain, scale, shift):
    return x * lax.rsqrt(jnp.mean(x * x, axis=-1, keepdims=True) + RMS_EPS) * (gain * (1.0 + scale)) + shift


def _dot(a, b):
    return jnp.dot(a, b, preferred_element_type=F32)


def _ada_kernel(c_ref, w_ref, b_ref, o_ref):
    c = c_ref[...]
    s = (c * _sigmoid(c)).astype(BF16)
    o_ref[...] = _dot(s, w_ref[...].astype(BF16)) + b_ref[...]


def _ada(cvecs, w_ada, b_ada):
    rows, d = cvecs.shape
    n = w_ada.shape[1]
    tn = d
    return pl.pallas_call(
        _ada_kernel,
        grid=(n // tn,),
        in_specs=[pl.BlockSpec((rows, d), lambda j: (0, 0)),
                  pl.BlockSpec((d, tn), lambda j: (0, j)),
                  pl.BlockSpec((1, tn), lambda j: (0, j))],
        out_specs=pl.BlockSpec((rows, tn), lambda j: (0, j)),
        out_shape=jax.ShapeDtypeStruct((rows, n), F32),
        compiler_params=_params(1),
        name="ada",
    )(cvecs, w_ada, b_ada.reshape(1, n))


def _ffn_kernel(*refs, first_mod_row, add_branch, add_pos, final_norm):
    x_ref, mod_ref, gain_ref, wg_ref, wu_ref, wd_ref = refs[:6]
    rest = list(refs[6:])
    branch_ref = rest.pop(0) if add_branch else None
    prow_ref, pcol_ref = (rest.pop(0), rest.pop(0)) if add_pos else (None, None)
    fn_ref = rest.pop(0) if final_norm else None
    o_ref = rest.pop(0)

    x = x_ref[...]
    if add_branch:
        x = x + branch_ref[...]
    if add_pos:
        tm, d = x.shape
        rows = tm // GRID_W
        prow = jnp.broadcast_to(prow_ref[...][:, None, :], (rows, GRID_W, d // 2))
        pcol = jnp.broadcast_to(pcol_ref[...][None, :, :], (rows, GRID_W, d // 2))
        x = x + jnp.concatenate([prow, pcol], axis=-1).reshape(tm, d)
    shift = mod_ref[first_mod_row:first_mod_row + 1, :]
    scale = mod_ref[first_mod_row + 1:first_mod_row + 2, :]
    gate = mod_ref[first_mod_row + 2:first_mod_row + 3, :]
    h = _modulated_rms(x, gain_ref[...], scale, shift).astype(BF16)
    g = _dot(h, wg_ref[...])
    u = _dot(h, wu_ref[...])
    a = (g * _sigmoid(g) * u).astype(BF16)
    y = _dot(a, wd_ref[...])
    xn = x + (0.5 * gate) * y
    if final_norm:
        xn = _rms(xn, fn_ref[...])
    o_ref[...] = xn


def _ffn(x, mod, gain, wg, wu, wd, *, tiles_per_group, first_mod_row, branch=None, pos=None, final_gain=None):
    n, d = x.shape
    dff = wg.shape[1]
    tm = TOKEN_TILE
    add_branch = branch is not None
    add_pos = pos is not None
    final_norm = final_gain is not None
    in_specs = [pl.BlockSpec((tm, d), lambda i: (i, 0)),
                pl.BlockSpec((None, N_MOD, d), lambda i: (i // tiles_per_group, 0, 0)),
                _resident((1, d)), _resident((d, dff)), _resident((d, dff)), _resident((dff, d))]
    args = [x, mod, gain.reshape(1, d), wg, wu, wd]
    if add_branch:
        in_specs.append(pl.BlockSpec((tm, d), lambda i: (i, 0)))
        args.append(branch)
    if add_pos:
        pos_row, pos_col = pos
        rows = tm // GRID_W
        row_tiles = pos_row.shape[0] // rows
        assert tm % GRID_W == 0 and rows % 8 == 0 and pos_col.shape[0] == GRID_W
        in_specs += [pl.BlockSpec((rows, d // 2), lambda i: (i % row_tiles, 0)), _resident(pos_col.shape)]
        args += [pos_row, pos_col]
    if final_norm:
        in_specs.append(_resident((1, d)))
        args.append(final_gain.reshape(1, d))
    return pl.pallas_call(
        functools.partial(_ffn_kernel, first_mod_row=first_mod_row, add_branch=add_branch, add_pos=add_pos,
                          final_norm=final_norm),
        grid=(n // tm,),
        in_specs=in_specs,
        out_specs=pl.BlockSpec((tm, d), lambda i: (i, 0)),
        out_shape=jax.ShapeDtypeStruct((n, d), F32),
        compiler_params=_params(1),
        name="ffn",
    )(*args)


_C_F = 0
_C_Q = _C_F + D_FOURIER
_C_K = _C_Q + DK_TOT
_C_V = _C_K + DK_TOT
_C_R = _C_V + DV_TOT
_C_G = _C_R + DV_TOT


def _mixin_kernel(x_ref, mod_ref, gain_ref, w_ref, wt_ref, wal_ref, bal_ref,
                  f_ref, q_ref, k_ref, v_ref, r_ref, g_ref, laf_ref, lab_ref, *scratch, d_model, fft_layout):
    x = x_ref[...]
    shift = mod_ref[3:4, :]
    scale = mod_ref[4:5, :]
    h = _modulated_rms(x, gain_ref[...], scale, shift).astype(BF16)

    def proj(lo, hi):
        return _dot(h, w_ref[:, lo:hi])

    alr = _dot(h, wt_ref[:, 2 * d_model:2 * d_model + ALR_PAD]).astype(BF16)
    g_ref[...] = _dot(h, wt_ref[:, :2 * d_model]).astype(BF16)
    pre = _dot(alr, wal_ref[...]) + bal_ref[...]
    la = (jnp.minimum(pre, 0.0) * (LOG2_E / GATE_TEMP)
          - jnp.log2(1.0 + jnp.exp2(jnp.abs(pre) * (-LOG2_E))) * (1.0 / GATE_TEMP))
    laf_ref[...] = la[:, :DK_TOT]
    lab_ref[...] = la[:, DK_TOT:]

    r_ref[...] = proj(_C_R, _C_G).astype(BF16)
    v_ref[...] = proj(_C_V, _C_R).astype(BF16)
    f = proj(_C_F, _C_Q)
    if fft_layout:
        slab_ref, = scratch
        rows = x.shape[0] // FFT_N
        for a in range(rows):
            for g in range(D_FOURIER // LANES):
                slab_ref[g, pl.ds(a, FFT_N, stride=rows), :] = f[a * FFT_N:(a + 1) * FFT_N, g * LANES:(g + 1) * LANES]
        for g in range(D_FOURIER // LANES):
            f_ref[:, :, g * LANES:(g + 1) * LANES] = slab_ref[g].reshape(FFT_N, rows, LANES)
    else:
        f_ref[...] = f
    q_ref[...] = proj(_C_Q, _C_K) * (DK ** -0.5)
    k_ref[...] = proj(_C_K, _C_V)


def _mixin(x, mod, gain, w_main, w_tail, w_alpha, b_alpha, *, tiles_per_group, fft_layout):
    n, d = x.shape
    tm = TOKEN_TILE

    def tok(width):
        return pl.BlockSpec((tm, width), lambda i: (i, 0))

    outs = [(D_FOURIER, F32), (DK_TOT, F32), (DK_TOT, F32), (DV_TOT, BF16), (DV_TOT, BF16),
            (2 * d, BF16), (DK_TOT, F32), (DK_TOT, F32)]
    out_specs = [tok(w) for w, _ in outs]
    out_shape = [jax.ShapeDtypeStruct((n, w), dt) for w, dt in outs]
    scratch = []
    if fft_layout:
        rows = tm // FFT_N
        tiles = FFT_N // rows
        assert tm % FFT_N == 0 and rows % 8 == 0 and n % (FFT_N * FFT_N) == 0
        out_specs[0] = pl.BlockSpec((None, FFT_N, rows, D_FOURIER), lambda i: (i // tiles, 0, i % tiles, 0))
        out_shape[0] = jax.ShapeDtypeStruct((n // (FFT_N * FFT_N), FFT_N, FFT_N, D_FOURIER), F32)
        scratch = [pltpu.VMEM((D_FOURIER // LANES, tm, LANES), F32)]
    return pl.pallas_call(
        functools.partial(_mixin_kernel, d_model=d, fft_layout=fft_layout),
        grid=(n // tm,),
        in_specs=[tok(d),
                  pl.BlockSpec((None, N_MOD, d), lambda i: (i // tiles_per_group, 0, 0)),
                  _resident((1, d)), _resident(w_main.shape), _resident(w_tail.shape),
                  _resident((ALR_PAD, 2 * DK_TOT)), _resident((1, 2 * DK_TOT))],
        out_specs=out_specs,
        out_shape=out_shape,
        scratch_shapes=scratch,
        compiler_params=_params(1),
        name="mixin",
    )(x, mod, gain.reshape(1, d), w_main, w_tail, w_alpha, b_alpha)


def _dft_cos_sin(n):
    k = np.arange(n)
    ang = 2.0 * np.pi * ((k[:, None] * k[None, :]) % n) / n
    return np.cos(ang), np.sin(ang)


def _split_hi_lo(x):
    hi = x.astype(BF16)
    return hi, (x - hi.astype(F32)).astype(BF16)


def _table_pieces(table, axis):
    hi, lo = _split_hi_lo(jnp.asarray(table.astype(np.float32)))
    return jnp.concatenate([hi, hi], axis=axis), lo


def _dot_data_table(x, cat_ref, lo_ref):
    hi, lo = _split_hi_lo(x)
    return _dot(jnp.concatenate([hi, lo], axis=1), cat_ref[...]) + _dot(hi, lo_ref[...])


def _dot_table_data(cat, lo_t, x):
    hi, lo = _split_hi_lo(x)
    return _dot(cat, jnp.concatenate([hi, lo], axis=0)) + _dot(lo_t, hi)


def _width_dft_table(t_len):
    c, s = _dft_cos_sin(GROUP_W)
    return np.concatenate([c, -s], axis=1) / math.sqrt(t_len * GROUP_W)


def _width_dft(x, cwc_ref, cwl_ref):
    zr, zi = [], []
    for g in range(N_GROUPS):
        z = _dot_data_table(x[:, g * GROUP_W:(g + 1) * GROUP_W], cwc_ref, cwl_ref)
        zr.append(z[:, :GROUP_W])
        zi.append(z[:, GROUP_W:])
    return jnp.concatenate(zr, axis=1), jnp.concatenate(zi, axis=1)


DENSE_SEQS = 4


def _fourier_dense_kernel(x_ref, cwc_ref, cwl_ref, ftc_ref, ftl_ref, o_ref, *, t_len):
    zr, zi = _width_dft(x_ref[...], cwc_ref, cwl_ref)
    for s in range(x_ref.shape[0] // t_len):
        rows = slice(s * t_len, (s + 1) * t_len)
        zz = jnp.concatenate([zr[rows], zi[rows]], axis=0)
        o_ref[rows, :] = _dot_table_data(ftc_ref[...], ftl_ref[...], zz)


def _fourier_dense(f, nb, t_len):
    c, s = _dft_cos_sin(t_len)
    consts = [*_table_pieces(_width_dft_table(t_len), 0), *_table_pieces(np.concatenate([c, s], axis=1), 1)]
    seqs = math.gcd(nb, DENSE_SEQS)
    blk = pl.BlockSpec((seqs * t_len, D_FOURIER), lambda b: (b, 0))
    return pl.pallas_call(
        functools.partial(_fourier_dense_kernel, t_len=t_len),
        grid=(nb // seqs,),
        in_specs=[blk] + [_resident(a.shape) for a in consts],
        out_specs=blk,
        out_shape=jax.ShapeDtypeStruct(f.shape, F32),
        compiler_params=_params(1),
        name="fourier_dense",
    )(f, *consts)


FFT_CHANNELS = 2 * GROUP_W


def _fourier_two_stage_kernel(x_ref, cwc_ref, cwl_ref, mc_ref, ml_ref, f2c_ref, f2l_ref, o_ref,
                              ar_ref, ai_ref, sr_ref, si_ref):
    groups = FFT_CHANNELS // GROUP_W

    def stage1(jb, carry):
        t2_0 = pl.multiple_of(jb * FFT_ROWS, FFT_ROWS)
        x = x_ref[pl.ds(t2_0, FFT_ROWS)].reshape(FFT_ROWS * FFT_N, FFT_CHANNELS)
        zr, zi = [], []
        for g in range(groups):
            z = _dot_data_table(x[:, g * GROUP_W:(g + 1) * GROUP_W], cwc_ref, cwl_ref)
            zr.append(z[:, :GROUP_W])
            zi.append(z[:, GROUP_W:])
        zr = jnp.concatenate(zr, axis=1)
        zi = jnp.concatenate(zi, axis=1)
        for j in range(FFT_ROWS):
            rows = slice(j * FFT_N, (j + 1) * FFT_N)
            a = _dot_table_data(mc_ref[t2_0 + j], ml_ref[t2_0 + j], jnp.concatenate([zr[rows], zi[rows]], axis=0))
            for g in range(FFT_CHANNELS // LANES):
                lanes = slice(g * LANES, (g + 1) * LANES)
                sr_ref[g, pl.ds(j, FFT_N, stride=FFT_ROWS), :] = a[:FFT_N, lanes]
                si_ref[g, pl.ds(j, FFT_N, stride=FFT_ROWS), :] = a[FFT_N:, lanes]
        for g in range(FFT_CHANNELS // LANES):
            lanes = slice(g * LANES, (g + 1) * LANES)
            ar_ref[:, pl.ds(t2_0, FFT_ROWS), lanes] = sr_ref[g].reshape(FFT_N, FFT_ROWS, LANES)
            ai_ref[:, pl.ds(t2_0, FFT_ROWS), lanes] = si_ref[g].reshape(FFT_N, FFT_ROWS, LANES)
        return carry

    def stage2(jb, carry):
        for j in range(FFT_ROWS):
            s1 = jb * FFT_ROWS + j
            aa = jnp.concatenate([ar_ref[s1], ai_ref[s1]], axis=0)
            o_ref[s1] = _dot_table_data(f2c_ref[...], f2l_ref[...], aa)
        return carry

    lax.fori_loop(0, FFT_N // FFT_ROWS, stage1, 0)
    lax.fori_loop(0, FFT_N // FFT_ROWS, stage2, 0)


def _fourier_two_stage(x, nb):
    n = FFT_N
    t_len = n * n
    s1 = np.arange(n)[:, None]
    t1 = np.arange(n)[None, :]
    mats = []
    for t2 in range(n):
        ang = 2.0 * np.pi * ((s1 * t1 * n + s1 * t2) % t_len) / t_len
        c, s = np.cos(ang), np.sin(ang)
        mats.append(np.block([[c, s], [-s, c]]))
    m_cat, m_lo = _table_pieces(np.stack(mats), 2)
    c2, s2 = _dft_cos_sin(n)
    f2 = _table_pieces(np.concatenate([c2, s2], axis=1), 1)
    cw = _table_pieces(_width_dft_table(t_len), 0)
    consts = [*cw, m_cat, m_lo, *f2]

    blk = pl.BlockSpec((None, n, n, FFT_CHANNELS), lambda b, h: (b, 0, 0, h))
    plane = pltpu.VMEM((n, n, FFT_CHANNELS), F32)
    slabs = pltpu.VMEM((FFT_CHANNELS // LANES, n * FFT_ROWS, LANES), F32)
    return pl.pallas_call(
        _fourier_two_stage_kernel,
        grid=(nb, D_FOURIER // FFT_CHANNELS),
        in_specs=[blk] + [_resident(a.shape) for a in consts],
        out_specs=blk,
        out_shape=jax.ShapeDtypeStruct((nb, n, n, D_FOURIER), F32),
        scratch_shapes=[plane, plane, slabs, slabs],
        compiler_params=_params(2),
        name="fourier_two_stage",
    )(x, *consts)


def _gla_masks(chunk):
    idx = np.arange(chunk)
    i = idx[:, None]
    j = idx[None, :]
    masks = [i == j]
    for l in range(GLA_LEVELS):
        same = (i >> (l + 1)) == (j >> (l + 1))
        masks.append(same & (((i >> l) & 1) == 1) & (((j >> l) & 1) == 0))
    mask_f = np.stack(masks).astype(np.float32)
    return mask_f, np.transpose(mask_f, (0, 2, 1)).copy()


_NT = (((1,), (1,)), ((), ()))
_TN = (((0,), (0,)), ((), ()))
SUBLANES = 8


def _block_sums(la, forward):
    c = la.shape[0]
    nv = c // SUBLANES
    x = la.reshape(nv, SUBLANES, DK)
    sub = lax.broadcasted_iota(jnp.int32, (1, SUBLANES, DK), 1)
    near = x
    tot = x
    out = {}
    w = 1
    while w < SUBLANES:
        right = (sub & w) != 0
        partner = jnp.where(right, pltpu.roll(tot, w, axis=1), pltpu.roll(tot, SUBLANES - w, axis=1))
        near = near + jnp.where(right if forward else jnp.logical_not(right), partner, 0.0)
        tot = tot + partner
        w *= 2
        out[w] = (near, tot - near)
    blk = 1
    while w < c:
        takes = (lambda v: v & blk) if forward else (lambda v: not v & blk)
        near = jnp.stack([near[v] + tot[v ^ blk] if takes(v) else near[v] for v in range(nv)])
        tot = jnp.stack([tot[v] + tot[v ^ blk] for v in range(nv)])
        w *= 2
        blk *= 2
        out[w] = (near, tot - near)
    return {w: (a.reshape(c, DK), b.reshape(c, DK)) for w, (a, b) in out.items()}


def _gla_prepare(q_ref, k_ref, laf_ref, lab_ref, qs_ref, ks_ref, tot_ref, rows_f, rows_b):
    c = GLA_CHUNK
    for g in range(len(rows_f)):
        for fwd in (True, False):
            chain = 2 * g + (0 if fwd else 1)
            rows = pl.ds(rows_f[g] if fwd else rows_b[g], c)
            q = q_ref[rows, :]
            k = k_ref[rows, :]
            la = (laf_ref if fwd else lab_ref)[rows, :]
            sums = _block_sums(la, fwd)
            qs_ref[chain, 0:c, :] = q.astype(BF16)
            ks_ref[chain, 0:c, :] = k.astype(BF16)
            for l in range(GLA_LEVELS + 1):
                if l == 0:
                    d, e = jnp.exp2(la), None
                else:
                    near, far = sums[1 << l]
                    d, e = jnp.exp2(near), jnp.exp2(far)
                qs_ref[chain, (l + 1) * c:(l + 2) * c, :] = (q * d).astype(BF16)
                if e is not None:
                    ks_ref[chain, l * c:(l + 1) * c, :] = (k * e).astype(BF16)
            tot_ref[chain] = jnp.broadcast_to(d[c - 1:c, :] if fwd else d[0:1, :], (SUBLANES, DK))


def _gla_apply(v_ref, maskf_ref, maskb_ref, sf_ref, sb_ref, o_ref, work):
    c = GLA_CHUNK
    levels = GLA_LEVELS
    chains = []
    for qs_ref, ks_ref, tot_ref, rows_f, rows_b, accumulate in work:
        for chain in range(2 * len(rows_f)):
            fwd = chain % 2 == 0
            rows = pl.ds((rows_f if fwd else rows_b)[chain // 2], c)
            chains.append((qs_ref, ks_ref, tot_ref, chain, fwd, rows, accumulate))
    scores = []
    for qs_ref, ks_ref, tot_ref, chain, fwd, rows, accumulate in chains:
        m_ref = maskf_ref if fwd else maskb_ref
        s = None
        for l in range(levels + 1):
            kl = max(l - 1, 0)
            p = lax.dot_general(qs_ref[chain, l * c:(l + 1) * c, :], ks_ref[chain, kl * c:(kl + 1) * c, :], _NT,
                                preferred_element_type=F32) * m_ref[l]
            s = p if s is None else s + p
        scores.append(s.astype(BF16))
    for (qs_ref, ks_ref, tot_ref, chain, fwd, rows, accumulate), s in zip(chains, scores):
        st_ref = sf_ref if fwd else sb_ref
        v = v_ref[rows, :]
        st = st_ref[...]
        q_top = qs_ref[chain, (levels + 1) * c:(levels + 2) * c, :]
        k_top = ks_ref[chain, levels * c:(levels + 1) * c, :]
        o = _dot(s, v) + _dot(q_top, st.astype(BF16))
        tot = jnp.broadcast_to(tot_ref[chain][0:1, :], (DK, DK)).T
        st_ref[...] = (st * jnp.concatenate([tot] * (DV // DK), axis=1)
                       + lax.dot_general(k_top, v, _TN, preferred_element_type=F32))
        if accumulate:
            o_ref[rows, :] += o
        else:
            o_ref[rows, :] = o


def _gla_kernel(q_ref, k_ref, v_ref, laf_ref, lab_ref, *refs, n_chunks, group, zero_start):
    (s0f_ref, s0b_ref), refs = (refs[:2], refs[2:]) if not zero_start else ((None, None), refs)
    maskf_ref, maskb_ref, o_ref, sf_ref, sb_ref, qs0_ref, ks0_ref, tot0_ref, qs1_ref, ks1_ref, tot1_ref = refs
    sf_ref[...] = jnp.zeros(sf_ref.shape, F32) if zero_start else s0f_ref[...]
    sb_ref[...] = jnp.zeros(sb_ref.shape, F32) if zero_start else s0b_ref[...]
    steps = n_chunks // group
    slots = ((qs0_ref, ks0_ref, tot0_ref), (qs1_ref, ks1_ref, tot1_ref))

    def rows_of(i):
        rows_f = [pl.multiple_of((i * group + g) * GLA_CHUNK, GLA_CHUNK) for g in range(group)]
        rows_b = [pl.multiple_of((n_chunks - 1 - i * group - g) * GLA_CHUNK, GLA_CHUNK) for g in range(group)]
        return rows_f, rows_b

    def prepare(i, slot):
        _gla_prepare(q_ref, k_ref, laf_ref, lab_ref, *slots[slot], *rows_of(i))

    def apply(items):
        _gla_apply(v_ref, maskf_ref, maskb_ref, sf_ref, sb_ref, o_ref,
                   [(*slots[slot], *rows_of(i), accumulate) for i, slot, accumulate in items])

    def pair(j, accumulate, last):
        apply([(2 * j, 0, accumulate[0]), (2 * j + 1, 1, accumulate[1])])
        if not last:
            prepare(2 * j + 2, 0)
            prepare(2 * j + 3, 1)

    n_pairs = steps // 2
    prepare(0, 0)
    prepare(1, 1)
    if n_pairs == 1:
        pair(0, (False, True), True)
    else:
        def body(accumulate):
            def f(j, carry):
                pair(j, (accumulate, accumulate), False)
                return carry
            return f
        lax.fori_loop(0, n_pairs // 2, body(False), 0)
        lax.fori_loop(n_pairs // 2, n_pairs - 1, body(True), 0)
        pair(n_pairs - 1, (True, True), True)


def _gla(q, k, v, laf, lab, s0, nb, t_len):
    n_chunks = t_len // GLA_CHUNK
    group = next(g for g in (GLA_GROUP, GLA_GROUP // 2) if n_chunks % (4 * g) == 0 or n_chunks == 2 * g)
    steps = n_chunks // group
    assert t_len % GLA_CHUNK == 0 and n_chunks % group == 0 and (steps == 2 or steps % 4 == 0)
    consts = [jnp.asarray(m) for m in _gla_masks(GLA_CHUNK)]

    def seq(width):
        return pl.BlockSpec((t_len, width), lambda b, h: (b, h))

    state = pl.BlockSpec((None, None, DK, DV), lambda b, h: (b, h, 0, 0))
    n = nb * t_len
    slot = [pltpu.VMEM((2 * group, (GLA_LEVELS + 2) * GLA_CHUNK, DK), BF16),
            pltpu.VMEM((2 * group, (GLA_LEVELS + 1) * GLA_CHUNK, DK), BF16),
            pltpu.VMEM((2 * group, SUBLANES, DK), F32)]
    starts = [] if s0 is None else list(s0)
    return pl.pallas_call(
        functools.partial(_gla_kernel, n_chunks=n_chunks, group=group, zero_start=s0 is None),
        grid=(nb, N_HEADS),
        in_specs=[seq(DK), seq(DK), seq(DV), seq(DK), seq(DK)] + [state] * len(starts)
                 + [_resident(a.shape) for a in consts],
        out_specs=[seq(DV), state, state],
        out_shape=[jax.ShapeDtypeStruct((n, DV_TOT), F32),
                   jax.ShapeDtypeStruct((nb, N_HEADS, DK, DV), F32),
                   jax.ShapeDtypeStruct((nb, N_HEADS, DK, DV), F32)],
        scratch_shapes=slot + slot,
        compiler_params=_params(2),
        name="gla",
    )(q, k, v, laf, lab, *starts, *consts)


def _mixout_kernel(mod_ref, m_ref, o_ref, r_ref, g_ref, gn_ref, wpf_ref, wpg_ref, wout_ref, y_ref,
                   *scratch, d_model, fft_layout):
    if fft_layout:
        slab_ref, = scratch
        rows = m_ref.shape[1]
        for g in range(D_FOURIER // LANES):
            slab_ref[g] = m_ref[:, :, g * LANES:(g + 1) * LANES].reshape(FFT_N * rows, LANES)
        m = jnp.concatenate(
            [jnp.concatenate([slab_ref[g, pl.ds(s, FFT_N, stride=rows), :] for g in range(D_FOURIER // LANES)],
                             axis=1) for s in range(rows)], axis=0)
    else:
        m = m_ref[...]
    branch_a = _dot(m.astype(BF16), wpf_ref[...])
    o = o_ref[...]
    parts = []
    for h in range(N_HEADS):
        oh = o[:, h * DV:(h + 1) * DV]
        parts.append(oh * lax.rsqrt(jnp.mean(oh * oh, axis=-1, keepdims=True) + RMS_EPS))
    r = r_ref[...].astype(F32)
    on = jnp.concatenate(parts, axis=1) * (gn_ref[...] * r * _sigmoid(r))
    branch_b = _dot(on.astype(BF16), wpg_ref[...])
    g = _sigmoid(g_ref[...].astype(F32))
    merged = g[:, :d_model] * branch_a + g[:, d_model:] * branch_b
    y = _dot(merged.astype(BF16), wout_ref[...])
    y_ref[...] = mod_ref[5:6, :] * y


def _mixout(mod, m, o, r, g, gla_norm, wpf, wpg, wout, *, tiles_per_group, fft_layout):
    n, d = o.shape[0], wout.shape[1]
    tm = TOKEN_TILE

    def tok(width):
        return pl.BlockSpec((tm, width), lambda i: (i, 0))

    m_spec = tok(D_FOURIER)
    scratch = []
    if fft_layout:
        rows = tm // FFT_N
        tiles = FFT_N // rows
        m_spec = pl.BlockSpec((None, FFT_N, rows, D_FOURIER), lambda i: (i // tiles, 0, i % tiles, 0))
        scratch = [pltpu.VMEM((D_FOURIER // LANES, tm, LANES), F32)]
    return pl.pallas_call(
        functools.partial(_mixout_kernel, d_model=d, fft_layout=fft_layout),
        grid=(n // tm,),
        in_specs=[pl.BlockSpec((None, N_MOD, d), lambda i: (i // tiles_per_group, 0, 0)),
                  m_spec, tok(DV_TOT), tok(DV_TOT), tok(2 * d),
                  _resident((1, DV_TOT)), _resident(wpf.shape), _resident(wpg.shape), _resident(wout.shape)],
        out_specs=tok(d),
        out_shape=jax.ShapeDtypeStruct((n, d), F32),
        scratch_shapes=scratch,
        compiler_params=_params(1),
        name="mixout",
    )(mod, m, o, r, g, gla_norm.reshape(1, DV_TOT), wpf, wpg, wout)


def _grid_pos_tables(n_tokens, d_model):
    n_freq = d_model // 4
    omega = POS_BASE ** (-jnp.arange(n_freq, dtype=F32) / n_freq)
    ra = jnp.arange(n_tokens // GRID_W, dtype=F32)[:, None] * omega
    ca = jnp.arange(GRID_W, dtype=F32)[:, None] * omega
    return (jnp.concatenate([jnp.sin(ra), jnp.cos(ra)], axis=-1),
            jnp.concatenate([jnp.sin(ca), jnp.cos(ca)], axis=-1))


def _trunk_layer(x, mod, s0, nb, t_len, w, *, pos, final_gain, two_stage_fft):
    tiles = max(t_len // TOKEN_TILE, 1) if mod.shape[0] > 1 else x.shape[0] // TOKEN_TILE
    x = _ffn(x, mod, w["norm_ffn1"], w["ffn1_gate"], w["ffn1_up"], w["ffn1_down"],
             tiles_per_group=tiles, first_mod_row=0, pos=pos)
    f, q, k, v, r, g, laf, lab = _mixin(x, mod, w["norm_mix"], w["w_main"], w["w_tail"], w["w_alpha"],
                                        w["b_alpha"],
                                        tiles_per_group=tiles, fft_layout=two_stage_fft)
    m = _fourier_two_stage(f, nb) if two_stage_fft else _fourier_dense(f, nb, t_len)
    o, sf, sb = _gla(q, k, v, laf, lab, s0, nb, t_len)
    y = _mixout(mod, m, o, r, g, w["gla_norm"], w["proj_fourier"], w["proj_gla"], w["w_out"],
                tiles_per_group=tiles, fft_layout=two_stage_fft)
    x = _ffn(x, mod, w["norm_ffn2"], w["ffn2_gate"], w["ffn2_up"], w["ffn2_down"],
             tiles_per_group=tiles, first_mod_row=6, branch=y, final_gain=final_gain)
    return x, sf, sb


def kernel(x_prompt, x_sample, state_gla_fwd, state_gla_bwd, c, c_ctx, w_ada, b_ada, norm_ffn1, w_ffn1_gate, w_ffn1_up, w_ffn1_down, norm_mix, w_in, w_alpha_fwd, b_alpha_fwd, w_alpha_bwd, b_alpha_bwd, gla_norm, w_proj_fourier, w_proj_gla, w_out, norm_ffn2, w_ffn2_gate, w_ffn2_up, w_ffn2_down, final_norm):
    nb_ctx, t_ctx, d = x_prompt.shape
    nb_lat, t_lat, _ = x_sample.shape
    depth = w_ada.shape[0]
    assert t_lat == FFT_N * FFT_N and (nb_ctx * t_ctx) % TOKEN_TILE == 0 and t_lat % TOKEN_TILE == 0

    xc = x_prompt.reshape(nb_ctx * t_ctx, d)
    xl = x_sample.reshape(nb_lat * t_lat, d)
    pos = _grid_pos_tables(t_lat, d)
    pad_rows = (-(nb_lat + 1)) % 8
    cvecs = jnp.concatenate([c, c_ctx[None, :], jnp.zeros((pad_rows, d), F32)], axis=0)

    new_fwd, new_bwd = [], []
    for l in range(depth):
        cut = _C_G
        w_in_l = w_in[l]
        w_main = w_in_l[:, :cut].astype(BF16)
        w_tail = jnp.concatenate(
            [w_in_l[:, cut + 2 * GATE_RANK:].astype(BF16), w_in_l[:, cut:cut + 2 * GATE_RANK].astype(BF16),
             jnp.zeros((d, ALR_PAD - 2 * GATE_RANK), BF16)], axis=1)
        w_alpha = jnp.zeros((ALR_PAD, 2 * DK_TOT), F32)
        w_alpha = w_alpha.at[:GATE_RANK, :DK_TOT].set(w_alpha_fwd[l])
        w_alpha = w_alpha.at[GATE_RANK:2 * GATE_RANK, DK_TOT:].set(w_alpha_bwd[l]).astype(BF16)
        b_alpha = jnp.concatenate([b_alpha_fwd[l], b_alpha_bwd[l]]).reshape(1, 2 * DK_TOT)
        w = {
            "norm_ffn1": norm_ffn1[l], "ffn1_gate": w_ffn1_gate[l].astype(BF16),
            "ffn1_up": w_ffn1_up[l].astype(BF16), "ffn1_down": w_ffn1_down[l].astype(BF16),
            "norm_mix": norm_mix[l], "w_main": w_main, "w_tail": w_tail, "w_alpha": w_alpha, "b_alpha": b_alpha,
            "gla_norm": gla_norm[l], "proj_fourier": w_proj_fourier[l].astype(BF16),
            "proj_gla": w_proj_gla[l].astype(BF16), "w_out": w_out[l].astype(BF16),
            "norm_ffn2": norm_ffn2[l], "ffn2_gate": w_ffn2_gate[l].astype(BF16),
            "ffn2_up": w_ffn2_up[l].astype(BF16), "ffn2_down": w_ffn2_down[l].astype(BF16),
        }
        last = l == depth - 1
        mod = _ada(cvecs, w_ada[l], b_ada[l]).reshape(-1, N_MOD, d)
        mod_lat = mod[:nb_lat]
        mod_ctx = mod[nb_lat:nb_lat + 1]
        xc, sf, sb = _trunk_layer(xc, mod_ctx, None, nb_ctx, t_ctx, w,
                                  pos=None, final_gain=final_norm if last else None, two_stage_fft=False)
        new_fwd.append(sf)
        new_bwd.append(sb)
        xl, _, _ = _trunk_layer(xl, mod_lat, (state_gla_fwd[:, l], state_gla_bwd[:, l]), nb_lat, t_lat, w,
                                pos=pos if l == 0 else None, final_gain=final_norm if last else None,
                                two_stage_fft=True)
    y_prompt = xc.reshape(nb_ctx, t_ctx, d)
    y_sample = xl.reshape(nb_lat, t_lat, d)
    return (y_prompt, y_sample, jnp.stack(new_fwd, axis=1), jnp.stack(new_bwd, axis=1))
```

```python
import functools
import math

import numpy as np
import jax
import jax.numpy as jnp
from jax import lax
from jax.experimental import pallas as pl
from jax.experimental.pallas import tpu as pltpu

F32 = jnp.float32
BF16 = jnp.bfloat16

GRID_W = 64
N_GROUPS = 4
GROUP_W = 128
D_FOURIER = N_GROUPS * GROUP_W
N_HEADS = 4
DK = 128
DV = 256
DK_TOT = N_HEADS * DK
DV_TOT = N_HEADS * DV
GATE_RANK = 16
GATE_TEMP = 16.0
RMS_EPS = 1e-6
POS_BASE = 10000.0
N_MOD = 9

TOKEN_TILE = 512
GLA_CHUNK = 64
GLA_LEVELS = GLA_CHUNK.bit_length() - 1
GLA_GROUP = 4
LANES = 128
SUBLANES = 8
ALR_PAD = LANES
FFT_N = 64
FFT_ROWS = SUBLANES
VMEM_LIMIT = 56 * 1024 * 1024


def _params(n_axes):
    return pltpu.CompilerParams(dimension_semantics=("arbitrary",) * n_axes,
                                vmem_limit_bytes=VMEM_LIMIT)


def _resident(shape):
    nd = len(shape)
    return pl.BlockSpec(shape, lambda *_: (0,) * nd, pipeline_mode=pl.Buffered(1))


LOG2_E = math.log2(math.e)


def _sigmoid(x):
    return 1.0 / (1.0 + jnp.exp2(x * (-LOG2_E)))


def _rms(x, gain):
    return x * lax.rsqrt(jnp.mean(x * x, axis=-1, keepdims=True) + RMS_EPS) * gain


def _modulated_rms(x, gain, scale, shift):
    return x * lax.rsqrt(jnp.mean(x * x, axis=-1, keepdims=True) + RMS_EPS) * (gain * (1.0 + scale)) + shift


def _dot(a, b):
    return jnp.dot(a, b, preferred_element_type=F32)


def _ada_kernel(c_ref, w_ref, b_ref, o_ref):
    c = c_ref[...]
    s = (c * _sigmoid(c)).astype(BF16)
    o_ref[...] = _dot(s, w_ref[...].astype(BF16)) + b_ref[...]


def _ada(cvecs, w_ada, b_ada):
    rows, d = cvecs.shape
    n = w_ada.shape[1]
    tn = d
    return pl.pallas_call(
        _ada_kernel,
        grid=(n // tn,),
        in_specs=[pl.BlockSpec((rows, d), lambda j: (0, 0)),
                  pl.BlockSpec((d, tn), lambda j: (0, j)),
                  pl.BlockSpec((1, tn), lambda j: (0, j))],
        out_specs=pl.BlockSpec((rows, tn), lambda j: (0, j)),
        out_shape=jax.ShapeDtypeStruct((rows, n), F32),
        compiler_params=_params(1),
        name="ada",
    )(cvecs, w_ada, b_ada.reshape(1, n))


def _ffn_kernel(*refs, first_mod_row, add_branch, add_pos, final_norm):
    x_ref, mod_ref, gain_ref, wg_ref, wu_ref, wd_ref = refs[:6]
    rest = list(refs[6:])
    branch_ref = rest.pop(0) if add_branch else None
    prow_ref, pcol_ref = (rest.pop(0), rest.pop(0)) if add_pos else (None, None)
    fn_ref = rest.pop(0) if final_norm else None
    o_ref = rest.pop(0)

    x = x_ref[...]
    if add_branch:
        x = x + branch_ref[...]
    if add_pos:
        tm, d = x.shape
        rows = tm // GRID_W
        prow = jnp.broadcast_to(prow_ref[...][:, None, :], (rows, GRID_W, d // 2))
        pcol = jnp.broadcast_to(pcol_ref[...][None, :, :], (rows, GRID_W, d // 2))
        x = x + jnp.concatenate([prow, pcol], axis=-1).reshape(tm, d)
    shift = mod_ref[first_mod_row:first_mod_row + 1, :]
    scale = mod_ref[first_mod_row + 1:first_mod_row + 2, :]
    gate = mod_ref[first_mod_row + 2:first_mod_row + 3, :]
    h = _modulated_rms(x, gain_ref[...], scale, shift).astype(BF16)
    g = _dot(h, wg_ref[...])
    u = _dot(h, wu_ref[...])
    a = (g * _sigmoid(g) * u).astype(BF16)
    y = _dot(a, wd_ref[...])
    xn = x + (0.5 * gate) * y
    if final_norm:
        xn = _rms(xn, fn_ref[...])
    o_ref[...] = xn


def _ffn(x, mod, gain, wg, wu, wd, *, tiles_per_group, first_mod_row, branch=None, pos=None, final_gain=None):
    n, d = x.shape
    dff = wg.shape[1]
    tm = TOKEN_TILE
    add_branch = branch is not None
    add_pos = pos is not None
    final_norm = final_gain is not None
    in_specs = [pl.BlockSpec((tm, d), lambda i: (i, 0)),
                pl.BlockSpec((None, N_MOD, d), lambda i: (i // tiles_per_group, 0, 0)),
                _resident((1, d)), _resident((d, dff)), _resident((d, dff)), _resident((dff, d))]
    args = [x, mod, gain.reshape(1, d), wg, wu, wd]
    if add_branch:
        in_specs.append(pl.BlockSpec((tm, d), lambda i: (i, 0)))
        args.append(branch)
    if add_pos:
        pos_row, pos_col = pos
        rows = tm // GRID_W
        row_tiles = pos_row.shape[0] // rows
        assert tm % GRID_W == 0 and rows % 8 == 0 and pos_col.shape[0] == GRID_W
        in_specs += [pl.BlockSpec((rows, d // 2), lambda i: (i % row_tiles, 0)), _resident(pos_col.shape)]
        args += [pos_row, pos_col]
    if final_norm:
        in_specs.append(_resident((1, d)))
        args.append(final_gain.reshape(1, d))
    return pl.pallas_call(
        functools.partial(_ffn_kernel, first_mod_row=first_mod_row, add_branch=add_branch, add_pos=add_pos,
                          final_norm=final_norm),
        grid=(n // tm,),
        in_specs=in_specs,
        out_specs=pl.BlockSpec((tm, d), lambda i: (i, 0)),
        out_shape=jax.ShapeDtypeStruct((n, d), F32),
        compiler_params=_params(1),
        name="ffn",
    )(*args)


_C_F = 0
_C_Q = _C_F + D_FOURIER
_C_K = _C_Q + DK_TOT
_C_V = _C_K + DK_TOT
_C_R = _C_V + DV_TOT
_C_G = _C_R + DV_TOT


def _mixin_kernel(x_ref, mod_ref, gain_ref, w_ref, wt_ref, wal_ref, bal_ref,
                  f_ref, q_ref, k_ref, v_ref, r_ref, g_ref, laf_ref, lab_ref, *scratch, d_model, fft_layout):
    x = x_ref[...]
    shift = mod_ref[3:4, :]
    scale = mod_ref[4:5, :]
    h = _modulated_rms(x, gain_ref[...], scale, shift).astype(BF16)

    def proj(lo, hi):
        return _dot(h, w_ref[:, lo:hi])

    alr = _dot(h, wt_ref[:, 2 * d_model:2 * d_model + ALR_PAD]).astype(BF16)
    g_ref[...] = _dot(h, wt_ref[:, :2 * d_model]).astype(BF16)
    pre = _dot(alr, wal_ref[...]) + bal_ref[...]
    la = (jnp.minimum(pre, 0.0) * (LOG2_E / GATE_TEMP)
          - jnp.log2(1.0 + jnp.exp2(jnp.abs(pre) * (-LOG2_E))) * (1.0 / GATE_TEMP))
    laf_ref[...] = la[:, :DK_TOT]
    lab_ref[...] = la[:, DK_TOT:]

    r_ref[...] = proj(_C_R, _C_G).astype(BF16)
    v_ref[...] = proj(_C_V, _C_R).astype(BF16)
    f = proj(_C_F, _C_Q)
    if fft_layout:
        slab_ref, = scratch
        rows = x.shape[0] // FFT_N
        for a in range(rows):
            for g in range(D_FOURIER // LANES):
                slab_ref[g, pl.ds(a, FFT_N, stride=rows), :] = f[a * FFT_N:(a + 1) * FFT_N, g * LANES:(g + 1) * LANES]
        for g in range(D_FOURIER // LANES):
            f_ref[:, :, g * LANES:(g + 1) * LANES] = slab_ref[g].reshape(FFT_N, rows, LANES)
    else:
        f_ref[...] = f
    q_ref[...] = proj(_C_Q, _C_K) * (DK ** -0.5)
    k_ref[...] = proj(_C_K, _C_V)


def _mixin(x, mod, gain, w_main, w_tail, w_alpha, b_alpha, *, tiles_per_group, fft_layout):
    n, d = x.shape
    tm = TOKEN_TILE

    def tok(width):
        return pl.BlockSpec((tm, width), lambda i: (i, 0))

    outs = [(D_FOURIER, F32), (DK_TOT, F32), (DK_TOT, F32), (DV_TOT, BF16), (DV_TOT, BF16),
            (2 * d, BF16), (DK_TOT, F32), (DK_TOT, F32)]
    out_specs = [tok(w) for w, _ in outs]
    out_shape = [jax.ShapeDtypeStruct((n, w), dt) for w, dt in outs]
    scratch = []
    if fft_layout:
        rows = tm // FFT_N
        tiles = FFT_N // rows
        assert tm % FFT_N == 0 and rows % 8 == 0 and n % (FFT_N * FFT_N) == 0
        out_specs[0] = pl.BlockSpec((None, FFT_N, rows, D_FOURIER), lambda i: (i // tiles, 0, i % tiles, 0))
        out_shape[0] = jax.ShapeDtypeStruct((n // (FFT_N * FFT_N), FFT_N, FFT_N, D_FOURIER), F32)
        scratch = [pltpu.VMEM((D_FOURIER // LANES, tm, LANES), F32)]
    return pl.pallas_call(
        functools.partial(_mixin_kernel, d_model=d, fft_layout=fft_layout),
        grid=(n // tm,),
        in_specs=[tok(d),
                  pl.BlockSpec((None, N_MOD, d), lambda i: (i // tiles_per_group, 0, 0)),
                  _resident((1, d)), _resident(w_main.shape), _resident(w_tail.shape),
                  _resident((ALR_PAD, 2 * DK_TOT)), _resident((1, 2 * DK_TOT))],
        out_specs=out_specs,
        out_shape=out_shape,
        scratch_shapes=scratch,
        compiler_params=_params(1),
        name="mixin",
    )(x, mod, gain.reshape(1, d), w_main, w_tail, w_alpha, b_alpha)


def _dft_cos_sin(n):
    k = np.arange(n)
    ang = 2.0 * np.pi * ((k[:, None] * k[None, :]) % n) / n
    return np.cos(ang), np.sin(ang)


def _split_hi_lo(x):
    hi = x.astype(BF16)
    return hi, (x - hi.astype(F32)).astype(BF16)


def _table_pieces(table, axis):
    hi, lo = _split_hi_lo(jnp.asarray(table.astype(np.float32)))
    return jnp.concatenate([hi, hi], axis=axis), lo


def _dot_data_table(x, cat_ref, lo_ref):
    hi, lo = _split_hi_lo(x)
    return _dot(jnp.concatenate([hi, lo], axis=1), cat_ref[...]) + _dot(hi, lo_ref[...])


def _dot_table_data(cat, lo_t, x):
    hi, lo = _split_hi_lo(x)
    return _dot(cat, jnp.concatenate([hi, lo], axis=0)) + _dot(lo_t, hi)


def _width_dft_table(t_len):
    c, s = _dft_cos_sin(GROUP_W)
    return np.concatenate([c, -s], axis=1) / math.sqrt(t_len * GROUP_W)


def _width_dft(x, cwc_ref, cwl_ref):
    zr, zi = [], []
    for g in range(N_GROUPS):
        z = _dot_data_table(x[:, g * GROUP_W:(g + 1) * GROUP_W], cwc_ref, cwl_ref)
        zr.append(z[:, :GROUP_W])
        zi.append(z[:, GROUP_W:])
    return jnp.concatenate(zr, axis=1), jnp.concatenate(zi, axis=1)


DENSE_SEQS = 4


def _fourier_dense_kernel(x_ref, cwc_ref, cwl_ref, ftc_ref, ftl_ref, o_ref, *, t_len):
    zr, zi = _width_dft(x_ref[...], cwc_ref, cwl_ref)
    for s in range(x_ref.shape[0] // t_len):
        rows = slice(s * t_len, (s + 1) * t_len)
        zz = jnp.concatenate([zr[rows], zi[rows]], axis=0)
        o_ref[rows, :] = _dot_table_data(ftc_ref[...], ftl_ref[...], zz)


def _fourier_dense(f, nb, t_len):
    c, s = _dft_cos_sin(t_len)
    consts = [*_table_pieces(_width_dft_table(t_len), 0), *_table_pieces(np.concatenate([c, s], axis=1), 1)]
    seqs = math.gcd(nb, DENSE_SEQS)
    blk = pl.BlockSpec((seqs * t_len, D_FOURIER), lambda b: (b, 0))
    return pl.pallas_call(
        functools.partial(_fourier_dense_kernel, t_len=t_len),
        grid=(nb // seqs,),
        in_specs=[blk] + [_resident(a.shape) for a in consts],
        out_specs=blk,
        out_shape=jax.ShapeDtypeStruct(f.shape, F32),
        compiler_params=_params(1),
        name="fourier_dense",
    )(f, *consts)


FFT_CHANNELS = 2 * GROUP_W


def _fourier_two_stage_kernel(x_ref, cwc_ref, cwl_ref, mc_ref, ml_ref, f2c_ref, f2l_ref, o_ref,
                              ar_ref, ai_ref, sr_ref, si_ref):
    groups = FFT_CHANNELS // GROUP_W

    def stage1(jb, carry):
        t2_0 = pl.multiple_of(jb * FFT_ROWS, FFT_ROWS)
        x = x_ref[pl.ds(t2_0, FFT_ROWS)].reshape(FFT_ROWS * FFT_N, FFT_CHANNELS)
        zr, zi = [], []
        for g in range(groups):
            z = _dot_data_table(x[:, g * GROUP_W:(g + 1) * GROUP_W], cwc_ref, cwl_ref)
            zr.append(z[:, :GROUP_W])
            zi.append(z[:, GROUP_W:])
        zr = jnp.concatenate(zr, axis=1)
        zi = jnp.concatenate(zi, axis=1)
        for j in range(FFT_ROWS):
            rows = slice(j * FFT_N, (j + 1) * FFT_N)
            a = _dot_table_data(mc_ref[t2_0 + j], ml_ref[t2_0 + j], jnp.concatenate([zr[rows], zi[rows]], axis=0))
            for g in range(FFT_CHANNELS // LANES):
                lanes = slice(g * LANES, (g + 1) * LANES)
                sr_ref[g, pl.ds(j, FFT_N, stride=FFT_ROWS), :] = a[:FFT_N, lanes]
                si_ref[g, pl.ds(j, FFT_N, stride=FFT_ROWS), :] = a[FFT_N:, lanes]
        for g in range(FFT_CHANNELS // LANES):
            lanes = slice(g * LANES, (g + 1) * LANES)
            ar_ref[:, pl.ds(t2_0, FFT_ROWS), lanes] = sr_ref[g].reshape(FFT_N, FFT_ROWS, LANES)
            ai_ref[:, pl.ds(t2_0, FFT_ROWS), lanes] = si_ref[g].reshape(FFT_N, FFT_ROWS, LANES)
        return carry

    def stage2(jb, carry):
        for j in range(FFT_ROWS):
            s1 = jb * FFT_ROWS + j
            aa = jnp.concatenate([ar_ref[s1], ai_ref[s1]], axis=0)
            o_ref[s1] = _dot_table_data(f2c_ref[...], f2l_ref[...], aa)
        return carry

    lax.fori_loop(0, FFT_N // FFT_ROWS, stage1, 0)
    lax.fori_loop(0, FFT_N // FFT_ROWS, stage2, 0)


def _fourier_two_stage(x, nb):
    n = FFT_N
    t_len = n * n
    s1 = np.arange(n)[:, None]
    t1 = np.arange(n)[None, :]
    mats = []
    for t2 in range(n):
        ang = 2.0 * np.pi * ((s1 * t1 * n + s1 * t2) % t_len) / t_len
        c, s = np.cos(ang), np.sin(ang)
        mats.append(np.block([[c, s], [-s, c]]))
    m_cat, m_lo = _table_pieces(np.stack(mats), 2)
    c2, s2 = _dft_cos_sin(n)
    f2 = _table_pieces(np.concatenate([c2, s2], axis=1), 1)
    cw = _table_pieces(_width_dft_table(t_len), 0)
    consts = [*cw, m_cat, m_lo, *f2]

    blk = pl.BlockSpec((None, n, n, FFT_CHANNELS), lambda b, h: (b, 0, 0, h))
    plane = pltpu.VMEM((n, n, FFT_CHANNELS), F32)
    slabs = pltpu.VMEM((FFT_CHANNELS // LANES, n * FFT_ROWS, LANES), F32)
    return pl.pallas_call(
        _fourier_two_stage_kernel,
        grid=(nb, D_FOURIER // FFT_CHANNELS),
        in_specs=[blk] + [_resident(a.shape) for a in consts],
        out_specs=blk,
        out_shape=jax.ShapeDtypeStruct((nb, n, n, D_FOURIER), F32),
        scratch_shapes=[plane, plane, slabs, slabs],
        compiler_params=_params(2),
        name="fourier_two_stage",
    )(x, *consts)


def _gla_masks(chunk):
    idx = np.arange(chunk)
    i = idx[:, None]
    j = idx[None, :]
    masks = [i == j]
    for l in range(GLA_LEVELS):
        same = (i >> (l + 1)) == (j >> (l + 1))
        masks.append(same & (((i >> l) & 1) == 1) & (((j >> l) & 1) == 0))
    mask_f = np.stack(masks).astype(np.float32)
    return mask_f, np.transpose(mask_f, (0, 2, 1)).copy()


_NT = (((1,), (1,)), ((), ()))
_TN = (((0,), (0,)), ((), ()))


def _block_sums(la, forward):
    c = la.shape[0]
    nv = c // SUBLANES
    x = la.reshape(nv, SUBLANES, DK)
    sub = lax.broadcasted_iota(jnp.int32, (1, SUBLANES, DK), 1)
    near = x
    tot = x
    out = {}
    w = 1
    while w < SUBLANES:
        right = (sub & w) != 0
        partner = jnp.where(right, pltpu.roll(tot, w, axis=1), pltpu.roll(tot, SUBLANES - w, axis=1))
        near = near + jnp.where(right if forward else jnp.logical_not(right), partner, 0.0)
        tot = tot + partner
        w *= 2
        out[w] = (near, tot - near)
    blk = 1
    while w < c:
        takes = (lambda v: v & blk) if forward else (lambda v: not v & blk)
        near = jnp.stack([near[v] + tot[v ^ blk] if takes(v) else near[v] for v in range(nv)])
        tot = jnp.stack([tot[v] + tot[v ^ blk] for v in range(nv)])
        w *= 2
        blk *= 2
        out[w] = (near, tot - near)
    return {w: (a.reshape(c, DK), b.reshape(c, DK)) for w, (a, b) in out.items()}


def _gla_prepare(q_ref, k_ref, laf_ref, lab_ref, qs_ref, ks_ref, tot_ref, rows_f, rows_b):
    c = GLA_CHUNK
    for g in range(len(rows_f)):
        for fwd in (True, False):
            chain = 2 * g + (0 if fwd else 1)
            rows = pl.ds(rows_f[g] if fwd else rows_b[g], c)
            q = q_ref[rows, :]
            k = k_ref[rows, :]
            la = (laf_ref if fwd else lab_ref)[rows, :]
            sums = _block_sums(la, fwd)
            qs_ref[chain, 0:c, :] = q.astype(BF16)
            ks_ref[chain, 0:c, :] = k.astype(BF16)
            for l in range(GLA_LEVELS + 1):
                if l == 0:
                    d, e = jnp.exp2(la), None
                else:
                    near, far = sums[1 << l]
                    d, e = jnp.exp2(near), jnp.exp2(far)
                qs_ref[chain, (l + 1) * c:(l + 2) * c, :] = (q * d).astype(BF16)
                if e is not None:
                    ks_ref[chain, l * c:(l + 1) * c, :] = (k * e).astype(BF16)
            tot_ref[chain] = jnp.broadcast_to(d[c - 1:c, :] if fwd else d[0:1, :], (SUBLANES, DK))


def _gla_apply(v_ref, maskf_ref, maskb_ref, sf_ref, sb_ref, o_ref, work):
    c = GLA_CHUNK
    levels = GLA_LEVELS
    chains = []
    for qs_ref, ks_ref, tot_ref, rows_f, rows_b, accumulate in work:
        for chain in range(2 * len(rows_f)):
            fwd = chain % 2 == 0
            rows = pl.ds((rows_f if fwd else rows_b)[chain // 2], c)
            chains.append((qs_ref, ks_ref, tot_ref, chain, fwd, rows, accumulate))
    scores = []
    for qs_ref, ks_ref, tot_ref, chain, fwd, rows, accumulate in chains:
        m_ref = maskf_ref if fwd else maskb_ref
        nv = c // SUBLANES
        tiles = [None] * nv
        for l in range(levels + 1):
            kl = max(l - 1, 0)
            p = lax.dot_general(qs_ref[chain, l * c:(l + 1) * c, :], ks_ref[chain, kl * c:(kl + 1) * c, :], _NT,
                                preferred_element_type=F32)
            level = l - 1
            for r in range(nv):
                if level >= 3 and ((r >> (level - 3)) & 1) != (1 if fwd else 0):
                    continue
                rows_r = slice(r * SUBLANES, (r + 1) * SUBLANES)
                t = p[rows_r] * m_ref[l, rows_r, :]
                tiles[r] = t if tiles[r] is None else tiles[r] + t
        scores.append(jnp.concatenate(tiles, axis=0).astype(BF16))
    for (qs_ref, ks_ref, tot_ref, chain, fwd, rows, accumulate), s in zip(chains, scores):
        st_ref = sf_ref if fwd else sb_ref
        v = v_ref[rows, :]
        st = st_ref[...]
        q_top = qs_ref[chain, (levels + 1) * c:(levels + 2) * c, :]
        k_top = ks_ref[chain, levels * c:(levels + 1) * c, :]
        o = _dot(s, v) + _dot(q_top, st.astype(BF16))
        tot = jnp.broadcast_to(tot_ref[chain][0:1, :], (DK, DK)).T
        st_ref[...] = (st * jnp.concatenate([tot] * (DV // DK), axis=1)
                       + lax.dot_general(k_top, v, _TN, preferred_element_type=F32))
        if accumulate:
            o_ref[rows, :] += o
        else:
            o_ref[rows, :] = o


def _gla_kernel(q_ref, k_ref, v_ref, laf_ref, lab_ref, *refs, n_chunks, group, zero_start):
    (s0f_ref, s0b_ref), refs = (refs[:2], refs[2:]) if not zero_start else ((None, None), refs)
    maskf_ref, maskb_ref, o_ref, sf_ref, sb_ref, qs0_ref, ks0_ref, tot0_ref, qs1_ref, ks1_ref, tot1_ref = refs
    sf_ref[...] = jnp.zeros(sf_ref.shape, F32) if zero_start else s0f_ref[...]
    sb_ref[...] = jnp.zeros(sb_ref.shape, F32) if zero_start else s0b_ref[...]
    steps = n_chunks // group
    slots = ((qs0_ref, ks0_ref, tot0_ref), (qs1_ref, ks1_ref, tot1_ref))

    def rows_of(i):
        rows_f = [pl.multiple_of((i * group + g) * GLA_CHUNK, GLA_CHUNK) for g in range(group)]
        rows_b = [pl.multiple_of((n_chunks - 1 - i * group - g) * GLA_CHUNK, GLA_CHUNK) for g in range(group)]
        return rows_f, rows_b

    def prepare(i, slot):
        _gla_prepare(q_ref, k_ref, laf_ref, lab_ref, *slots[slot], *rows_of(i))

    def apply(items):
        _gla_apply(v_ref, maskf_ref, maskb_ref, sf_ref, sb_ref, o_ref,
                   [(*slots[slot], *rows_of(i), accumulate) for i, slot, accumulate in items])

    def pair(j, accumulate, last):
        apply([(2 * j, 0, accumulate[0]), (2 * j + 1, 1, accumulate[1])])
        if not last:
            prepare(2 * j + 2, 0)
            prepare(2 * j + 3, 1)

    n_pairs = steps // 2
    prepare(0, 0)
    prepare(1, 1)
    if n_pairs == 1:
        pair(0, (False, True), True)
    else:
        def body(accumulate):
            def f(j, carry):
                pair(j, (accumulate, accumulate), False)
                return carry
            return f
        lax.fori_loop(0, n_pairs // 2, body(False), 0)
        lax.fori_loop(n_pairs // 2, n_pairs - 1, body(True), 0)
        pair(n_pairs - 1, (True, True), True)


def _gla(q, k, v, laf, lab, s0, nb, t_len):
    n_chunks = t_len // GLA_CHUNK
    group = next(g for g in (GLA_GROUP, GLA_GROUP // 2) if n_chunks % (4 * g) == 0 or n_chunks == 2 * g)
    steps = n_chunks // group
    assert t_len % GLA_CHUNK == 0 and n_chunks % group == 0 and (steps == 2 or steps % 4 == 0)
    consts = [jnp.asarray(m) for m in _gla_masks(GLA_CHUNK)]

    def seq(width):
        return pl.BlockSpec((t_len, width), lambda b, h: (b, h))

    state = pl.BlockSpec((None, None, DK, DV), lambda b, h: (b, h, 0, 0))
    n = nb * t_len
    slot = [pltpu.VMEM((2 * group, (GLA_LEVELS + 2) * GLA_CHUNK, DK), BF16),
            pltpu.VMEM((2 * group, (GLA_LEVELS + 1) * GLA_CHUNK, DK), BF16),
            pltpu.VMEM((2 * group, SUBLANES, DK), F32)]
    starts = [] if s0 is None else list(s0)
    return pl.pallas_call(
        functools.partial(_gla_kernel, n_chunks=n_chunks, group=group, zero_start=s0 is None),
        grid=(nb, N_HEADS),
        in_specs=[seq(DK), seq(DK), seq(DV), seq(DK), seq(DK)] + [state] * len(starts)
                 + [_resident(a.shape) for a in consts],
        out_specs=[seq(DV), state, state],
        out_shape=[jax.ShapeDtypeStruct((n, DV_TOT), F32),
                   jax.ShapeDtypeStruct((nb, N_HEADS, DK, DV), F32),
                   jax.ShapeDtypeStruct((nb, N_HEADS, DK, DV), F32)],
        scratch_shapes=slot + slot,
        compiler_params=_params(2),
        name="gla",
    )(q, k, v, laf, lab, *starts, *consts)


def _mixout_kernel(mod_ref, m_ref, o_ref, r_ref, g_ref, gn_ref, wpf_ref, wpg_ref, wout_ref, y_ref,
                   *scratch, d_model, fft_layout):
    if fft_layout:
        slab_ref, = scratch
        rows = m_ref.shape[1]
        for g in range(D_FOURIER // LANES):
            slab_ref[g] = m_ref[:, :, g * LANES:(g + 1) * LANES].reshape(FFT_N * rows, LANES)
        m = jnp.concatenate(
            [jnp.concatenate([slab_ref[g, pl.ds(s, FFT_N, stride=rows), :] for g in range(D_FOURIER // LANES)],
                             axis=1) for s in range(rows)], axis=0)
    else:
        m = m_ref[...]
    branch_a = _dot(m.astype(BF16), wpf_ref[...])
    o = o_ref[...]
    parts = []
    for h in range(N_HEADS):
        oh = o[:, h * DV:(h + 1) * DV]
        parts.append(oh * lax.rsqrt(jnp.mean(oh * oh, axis=-1, keepdims=True) + RMS_EPS))
    r = r_ref[...].astype(F32)
    on = jnp.concatenate(parts, axis=1) * (gn_ref[...] * r * _sigmoid(r))
    branch_b = _dot(on.astype(BF16), wpg_ref[...])
    g = _sigmoid(g_ref[...].astype(F32))
    merged = g[:, :d_model] * branch_a + g[:, d_model:] * branch_b
    y = _dot(merged.astype(BF16), wout_ref[...])
    y_ref[...] = mod_ref[5:6, :] * y


def _mixout(mod, m, o, r, g, gla_norm, wpf, wpg, wout, *, tiles_per_group, fft_layout):
    n, d = o.shape[0], wout.shape[1]
    tm = TOKEN_TILE

    def tok(width):
        return pl.BlockSpec((tm, width), lambda i: (i, 0))

    m_spec = tok(D_FOURIER)
    scratch = []
    if fft_layout:
        rows = tm // FFT_N
        tiles = FFT_N // rows
        m_spec = pl.BlockSpec((None, FFT_N, rows, D_FOURIER), lambda i: (i // tiles, 0, i % tiles, 0))
        scratch = [pltpu.VMEM((D_FOURIER // LANES, tm, LANES), F32)]
    return pl.pallas_call(
        functools.partial(_mixout_kernel, d_model=d, fft_layout=fft_layout),
        grid=(n // tm,),
        in_specs=[pl.BlockSpec((None, N_MOD, d), lambda i: (i // tiles_per_group, 0, 0)),
                  m_spec, tok(DV_TOT), tok(DV_TOT), tok(2 * d),
                  _resident((1, DV_TOT)), _resident(wpf.shape), _resident(wpg.shape), _resident(wout.shape)],
        out_specs=tok(d),
        out_shape=jax.ShapeDtypeStruct((n, d), F32),
        scratch_shapes=scratch,
        compiler_params=_params(1),
        name="mixout",
    )(mod, m, o, r, g, gla_norm.reshape(1, DV_TOT), wpf, wpg, wout)


def _grid_pos_tables(n_tokens, d_model):
    n_freq = d_model // 4
    omega = POS_BASE ** (-jnp.arange(n_freq, dtype=F32) / n_freq)
    ra = jnp.arange(n_tokens // GRID_W, dtype=F32)[:, None] * omega
    ca = jnp.arange(GRID_W, dtype=F32)[:, None] * omega
    return (jnp.concatenate([jnp.sin(ra), jnp.cos(ra)], axis=-1),
            jnp.concatenate([jnp.sin(ca), jnp.cos(ca)], axis=-1))


def _trunk_layer(x, mod, s0, nb, t_len, w, *, pos, final_gain, two_stage_fft):
    tiles = max(t_len // TOKEN_TILE, 1) if mod.shape[0] > 1 else x.shape[0] // TOKEN_TILE
    x = _ffn(x, mod, w["norm_ffn1"], w["ffn1_gate"], w["ffn1_up"], w["ffn1_down"],
             tiles_per_group=tiles, first_mod_row=0, pos=pos)
    f, q, k, v, r, g, laf, lab = _mixin(x, mod, w["norm_mix"], w["w_main"], w["w_tail"], w["w_alpha"],
                                        w["b_alpha"],
                                        tiles_per_group=tiles, fft_layout=two_stage_fft)
    m = _fourier_two_stage(f, nb) if two_stage_fft else _fourier_dense(f, nb, t_len)
    o, sf, sb = _gla(q, k, v, laf, lab, s0, nb, t_len)
    y = _mixout(mod, m, o, r, g, w["gla_norm"], w["proj_fourier"], w["proj_gla"], w["w_out"],
                tiles_per_group=tiles, fft_layout=two_stage_fft)
    x = _ffn(x, mod, w["norm_ffn2"], w["ffn2_gate"], w["ffn2_up"], w["ffn2_down"],
             tiles_per_group=tiles, first_mod_row=6, branch=y, final_gain=final_gain)
    return x, sf, sb


def kernel(x_prompt, x_sample, state_gla_fwd, state_gla_bwd, c, c_ctx, w_ada, b_ada, norm_ffn1, w_ffn1_gate, w_ffn1_up, w_ffn1_down, norm_mix, w_in, w_alpha_fwd, b_alpha_fwd, w_alpha_bwd, b_alpha_bwd, gla_norm, w_proj_fourier, w_proj_gla, w_out, norm_ffn2, w_ffn2_gate, w_ffn2_up, w_ffn2_down, final_norm):
    nb_ctx, t_ctx, d = x_prompt.shape
    nb_lat, t_lat, _ = x_sample.shape
    depth = w_ada.shape[0]
    assert t_lat == FFT_N * FFT_N and (nb_ctx * t_ctx) % TOKEN_TILE == 0 and t_lat % TOKEN_TILE == 0

    xc = x_prompt.reshape(nb_ctx * t_ctx, d)
    xl = x_sample.reshape(nb_lat * t_lat, d)
    pos = _grid_pos_tables(t_lat, d)
    pad_rows = (-(nb_lat + 1)) % 8
    cvecs = jnp.concatenate([c, c_ctx[None, :], jnp.zeros((pad_rows, d), F32)], axis=0)

    new_fwd, new_bwd = [], []
    for l in range(depth):
        cut = _C_G
        w_in_l = w_in[l]
        w_main = w_in_l[:, :cut].astype(BF16)
        w_tail = jnp.concatenate(
            [w_in_l[:, cut + 2 * GATE_RANK:].astype(BF16), w_in_l[:, cut:cut + 2 * GATE_RANK].astype(BF16),
             jnp.zeros((d, ALR_PAD - 2 * GATE_RANK), BF16)], axis=1)
        w_alpha = jnp.zeros((ALR_PAD, 2 * DK_TOT), F32)
        w_alpha = w_alpha.at[:GATE_RANK, :DK_TOT].set(w_alpha_fwd[l])
        w_alpha = w_alpha.at[GATE_RANK:2 * GATE_RANK, DK_TOT:].set(w_alpha_bwd[l]).astype(BF16)
        b_alpha = jnp.concatenate([b_alpha_fwd[l], b_alpha_bwd[l]]).reshape(1, 2 * DK_TOT)
        w = {
            "norm_ffn1": norm_ffn1[l], "ffn1_gate": w_ffn1_gate[l].astype(BF16),
            "ffn1_up": w_ffn1_up[l].astype(BF16), "ffn1_down": w_ffn1_down[l].astype(BF16),
            "norm_mix": norm_mix[l], "w_main": w_main, "w_tail": w_tail, "w_alpha": w_alpha, "b_alpha": b_alpha,
            "gla_norm": gla_norm[l], "proj_fourier": w_proj_fourier[l].astype(BF16),
            "proj_gla": w_proj_gla[l].astype(BF16), "w_out": w_out[l].astype(BF16),
            "norm_ffn2": norm_ffn2[l], "ffn2_gate": w_ffn2_gate[l].astype(BF16),
            "ffn2_up": w_ffn2_up[l].astype(BF16), "ffn2_down": w_ffn2_down[l].astype(BF16),
        }
        last = l == depth - 1
        mod = _ada(cvecs, w_ada[l], b_ada[l]).reshape(-1, N_MOD, d)
        mod_lat = mod[:nb_lat]
        mod_ctx = mod[nb_lat:nb_lat + 1]
        xc, sf, sb = _trunk_layer(xc, mod_ctx, None, nb_ctx, t_ctx, w,
                                  pos=None, final_gain=final_norm if last else None, two_stage_fft=False)
        new_fwd.append(sf)
        new_bwd.append(sb)
        xl, _, _ = _trunk_layer(xl, mod_lat, (state_gla_fwd[:, l], state_gla_bwd[:, l]), nb_lat, t_lat, w,
                                pos=pos if l == 0 else None, final_gain=final_norm if last else None,
                                two_stage_fft=True)
    y_prompt = xc.reshape(nb_ctx, t_ctx, d)
    y_sample = xl.reshape(nb_lat, t_lat, d)
    return (y_prompt, y_sample, jnp.stack(new_fwd, axis=1), jnp.stack(new_bwd, axis=1))
```

```python
import functools
import math

import numpy as np
import jax
import jax.numpy as jnp
from jax import lax
from jax.experimental import pallas as pl
from jax.experimental.pallas import tpu as pltpu

F32 = jnp.float32
BF16 = jnp.bfloat16

GRID_W = 64
N_GROUPS = 4
GROUP_W = 128
D_FOURIER = N_GROUPS * GROUP_W
N_HEADS = 4
DK = 128
DV = 256
DK_TOT = N_HEADS * DK
DV_TOT = N_HEADS * DV
GATE_RANK = 16
GATE_TEMP = 16.0
RMS_EPS = 1e-6
POS_BASE = 10000.0
N_MOD = 9

TOKEN_TILE = 512
GLA_CHUNK = 64
GLA_LEVELS = GLA_CHUNK.bit_length() - 1
GLA_GROUP = 4
LANES = 128
SUBLANES = 8
ALR_PAD = LANES
FFT_N = 64
FFT_ROWS = SUBLANES
VMEM_LIMIT = 56 * 1024 * 1024


def _params(n_axes):
    return pltpu.CompilerParams(dimension_semantics=("arbitrary",) * n_axes,
                                vmem_limit_bytes=VMEM_LIMIT)


def _resident(shape):
    nd = len(shape)
    return pl.BlockSpec(shape, lambda *_: (0,) * nd, pipeline_mode=pl.Buffered(1))


LOG2_E = math.log2(math.e)


def _sigmoid(x):
    return 1.0 / (1.0 + jnp.exp2(x * (-LOG2_E)))


def _rms(x, gain):
    return x * lax.rsqrt(jnp.mean(x * x, axis=-1, keepdims=True) + RMS_EPS) * gain


def _modulated_rms(x, gain, scale, shift):
    return x * lax.rsqrt(jnp.mean(x * x, axis=-1, keepdims=True) + RMS_EPS) * (gain * (1.0 + scale)) + shift


def _dot(a, b):
    return jnp.dot(a, b, preferred_element_type=F32)


def _ada_kernel(c_ref, w_ref, b_ref, o_ref):
    c = c_ref[...]
    s = (c * _sigmoid(c)).astype(BF16)
    o_ref[...] = _dot(s, w_ref[...].astype(BF16)) + b_ref[...]


def _ada(cvecs, w_ada, b_ada):
    rows, d = cvecs.shape
    n = w_ada.shape[1]
    tn = d
    return pl.pallas_call(
        _ada_kernel,
        grid=(n // tn,),
        in_specs=[pl.BlockSpec((rows, d), lambda j: (0, 0)),
                  pl.BlockSpec((d, tn), lambda j: (0, j)),
                  pl.BlockSpec((1, tn), lambda j: (0, j))],
        out_specs=pl.BlockSpec((rows, tn), lambda j: (0, j)),
        out_shape=jax.ShapeDtypeStruct((rows, n), F32),
        compiler_params=_params(1),
        name="ada",
    )(cvecs, w_ada, b_ada.reshape(1, n))


def _ffn_kernel(*refs, first_mod_row, add_branch, add_pos, final_norm):
    x_ref, mod_ref, gain_ref, wg_ref, wu_ref, wd_ref = refs[:6]
    rest = list(refs[6:])
    branch_ref = rest.pop(0) if add_branch else None
    prow_ref, pcol_ref = (rest.pop(0), rest.pop(0)) if add_pos else (None, None)
    fn_ref = rest.pop(0) if final_norm else None
    o_ref = rest.pop(0)

    x = x_ref[...]
    if add_branch:
        x = x + branch_ref[...]
    if add_pos:
        tm, d = x.shape
        rows = tm // GRID_W
        prow = jnp.broadcast_to(prow_ref[...][:, None, :], (rows, GRID_W, d // 2))
        pcol = jnp.broadcast_to(pcol_ref[...][None, :, :], (rows, GRID_W, d // 2))
        x = x + jnp.concatenate([prow, pcol], axis=-1).reshape(tm, d)
    shift = mod_ref[first_mod_row:first_mod_row + 1, :]
    scale = mod_ref[first_mod_row + 1:first_mod_row + 2, :]
    gate = mod_ref[first_mod_row + 2:first_mod_row + 3, :]
    h = _modulated_rms(x, gain_ref[...], scale, shift).astype(BF16)
    g = _dot(h, wg_ref[...])
    u = _dot(h, wu_ref[...])
    a = (g * _sigmoid(g) * u).astype(BF16)
    y = _dot(a, wd_ref[...])
    xn = x + (0.5 * gate) * y
    if final_norm:
        xn = _rms(xn, fn_ref[...])
    o_ref[...] = xn


def _ffn(x, mod, gain, wg, wu, wd, *, tiles_per_group, first_mod_row, branch=None, pos=None, final_gain=None):
    n, d = x.shape
    dff = wg.shape[1]
    tm = TOKEN_TILE
    add_branch = branch is not None
    add_pos = pos is not None
    final_norm = final_gain is not None
    in_specs = [pl.BlockSpec((tm, d), lambda i: (i, 0)),
                pl.BlockSpec((None, N_MOD, d), lambda i: (i // tiles_per_group, 0, 0)),
                _resident((1, d)), _resident((d, dff)), _resident((d, dff)), _resident((dff, d))]
    args = [x, mod, gain.reshape(1, d), wg, wu, wd]
    if add_branch:
        in_specs.append(pl.BlockSpec((tm, d), lambda i: (i, 0)))
        args.append(branch)
    if add_pos:
        pos_row, pos_col = pos
        rows = tm // GRID_W
        row_tiles = pos_row.shape[0] // rows
        assert tm % GRID_W == 0 and rows % 8 == 0 and pos_col.shape[0] == GRID_W
        in_specs += [pl.BlockSpec((rows, d // 2), lambda i: (i % row_tiles, 0)), _resident(pos_col.shape)]
        args += [pos_row, pos_col]
    if final_norm:
        in_specs.append(_resident((1, d)))
        args.append(final_gain.reshape(1, d))
    return pl.pallas_call(
        functools.partial(_ffn_kernel, first_mod_row=first_mod_row, add_branch=add_branch, add_pos=add_pos,
                          final_norm=final_norm),
        grid=(n // tm,),
        in_specs=in_specs,
        out_specs=pl.BlockSpec((tm, d), lambda i: (i, 0)),
        out_shape=jax.ShapeDtypeStruct((n, d), F32),
        compiler_params=_params(1),
        name="ffn",
    )(*args)


_C_F = 0
_C_Q = _C_F + D_FOURIER
_C_K = _C_Q + DK_TOT
_C_V = _C_K + DK_TOT
_C_R = _C_V + DV_TOT
_C_G = _C_R + DV_TOT


def _mixin_kernel(x_ref, mod_ref, gain_ref, w_ref, wt_ref, wal_ref, bal_ref,
                  f_ref, q_ref, k_ref, v_ref, r_ref, g_ref, laf_ref, lab_ref, *scratch, d_model, fft_layout):
    x = x_ref[...]
    shift = mod_ref[3:4, :]
    scale = mod_ref[4:5, :]
    h = _modulated_rms(x, gain_ref[...], scale, shift).astype(BF16)

    def proj(lo, hi):
        return _dot(h, w_ref[:, lo:hi])

    alr = _dot(h, wt_ref[:, 2 * d_model:2 * d_model + ALR_PAD]).astype(BF16)
    g_ref[...] = _dot(h, wt_ref[:, :2 * d_model]).astype(BF16)
    pre = _dot(alr, wal_ref[...]) + bal_ref[...]
    la = (jnp.minimum(pre, 0.0) * (LOG2_E / GATE_TEMP)
          - jnp.log2(1.0 + jnp.exp2(jnp.abs(pre) * (-LOG2_E))) * (1.0 / GATE_TEMP))
    laf_ref[...] = la[:, :DK_TOT]
    lab_ref[...] = la[:, DK_TOT:]

    r_ref[...] = proj(_C_R, _C_G).astype(BF16)
    v_ref[...] = proj(_C_V, _C_R).astype(BF16)
    f = proj(_C_F, _C_Q)
    if fft_layout:
        slab_ref, = scratch
        rows = x.shape[0] // FFT_N
        for a in range(rows):
            for g in range(D_FOURIER // LANES):
                slab_ref[g, pl.ds(a, FFT_N, stride=rows), :] = f[a * FFT_N:(a + 1) * FFT_N, g * LANES:(g + 1) * LANES]
        for g in range(D_FOURIER // LANES):
            f_ref[:, :, g * LANES:(g + 1) * LANES] = slab_ref[g].reshape(FFT_N, rows, LANES)
    else:
        f_ref[...] = f
    q_ref[...] = proj(_C_Q, _C_K) * (DK ** -0.5)
    k_ref[...] = proj(_C_K, _C_V)


def _mixin(x, mod, gain, w_main, w_tail, w_alpha, b_alpha, *, tiles_per_group, fft_layout):
    n, d = x.shape
    tm = TOKEN_TILE

    def tok(width):
        return pl.BlockSpec((tm, width), lambda i: (i, 0))

    outs = [(D_FOURIER, F32), (DK_TOT, F32), (DK_TOT, F32), (DV_TOT, BF16), (DV_TOT, BF16),
            (2 * d, BF16), (DK_TOT, F32), (DK_TOT, F32)]
    out_specs = [tok(w) for w, _ in outs]
    out_shape = [jax.ShapeDtypeStruct((n, w), dt) for w, dt in outs]
    scratch = []
    if fft_layout:
        rows = tm // FFT_N
        tiles = FFT_N // rows
        assert tm % FFT_N == 0 and rows % 8 == 0 and n % (FFT_N * FFT_N) == 0
        out_specs[0] = pl.BlockSpec((None, FFT_N, rows, D_FOURIER), lambda i: (i // tiles, 0, i % tiles, 0))
        out_shape[0] = jax.ShapeDtypeStruct((n // (FFT_N * FFT_N), FFT_N, FFT_N, D_FOURIER), F32)
        scratch = [pltpu.VMEM((D_FOURIER // LANES, tm, LANES), F32)]
    return pl.pallas_call(
        functools.partial(_mixin_kernel, d_model=d, fft_layout=fft_layout),
        grid=(n // tm,),
        in_specs=[tok(d),
                  pl.BlockSpec((None, N_MOD, d), lambda i: (i // tiles_per_group, 0, 0)),
                  _resident((1, d)), _resident(w_main.shape), _resident(w_tail.shape),
                  _resident((ALR_PAD, 2 * DK_TOT)), _resident((1, 2 * DK_TOT))],
        out_specs=out_specs,
        out_shape=out_shape,
        scratch_shapes=scratch,
        compiler_params=_params(1),
        name="mixin",
    )(x, mod, gain.reshape(1, d), w_main, w_tail, w_alpha, b_alpha)


def _dft_cos_sin(n):
    k = np.arange(n)
    ang = 2.0 * np.pi * ((k[:, None] * k[None, :]) % n) / n
    return np.cos(ang), np.sin(ang)


def _split_hi_lo(x):
    hi = x.astype(BF16)
    return hi, (x - hi.astype(F32)).astype(BF16)


def _table_pieces(table, axis):
    hi, lo = _split_hi_lo(jnp.asarray(table.astype(np.float32)))
    return jnp.concatenate([hi, hi], axis=axis), lo


def _dot_data_table(x, cat_ref, lo_ref):
    hi, lo = _split_hi_lo(x)
    return _dot(jnp.concatenate([hi, lo], axis=1), cat_ref[...]) + _dot(hi, lo_ref[...])


def _dot_table_data(cat, lo_t, x):
    hi, lo = _split_hi_lo(x)
    return _dot(cat, jnp.concatenate([hi, lo], axis=0)) + _dot(lo_t, hi)


def _width_dft_table(t_len):
    c, s = _dft_cos_sin(GROUP_W)
    return np.concatenate([c, -s], axis=1) / math.sqrt(t_len * GROUP_W)


def _width_dft(x, cwc_ref, cwl_ref):
    zr, zi = [], []
    for g in range(N_GROUPS):
        z = _dot_data_table(x[:, g * GROUP_W:(g + 1) * GROUP_W], cwc_ref, cwl_ref)
        zr.append(z[:, :GROUP_W])
        zi.append(z[:, GROUP_W:])
    return jnp.concatenate(zr, axis=1), jnp.concatenate(zi, axis=1)


DENSE_SEQS = 4


def _fourier_dense_kernel(x_ref, cwc_ref, cwl_ref, ftc_ref, ftl_ref, o_ref, *, t_len):
    zr, zi = _width_dft(x_ref[...], cwc_ref, cwl_ref)
    for s in range(x_ref.shape[0] // t_len):
        rows = slice(s * t_len, (s + 1) * t_len)
        zz = jnp.concatenate([zr[rows], zi[rows]], axis=0)
        o_ref[rows, :] = _dot_table_data(ftc_ref[...], ftl_ref[...], zz)


def _fourier_dense(f, nb, t_len):
    c, s = _dft_cos_sin(t_len)
    consts = [*_table_pieces(_width_dft_table(t_len), 0), *_table_pieces(np.concatenate([c, s], axis=1), 1)]
    seqs = math.gcd(nb, DENSE_SEQS)
    blk = pl.BlockSpec((seqs * t_len, D_FOURIER), lambda b: (b, 0))
    return pl.pallas_call(
        functools.partial(_fourier_dense_kernel, t_len=t_len),
        grid=(nb // seqs,),
        in_specs=[blk] + [_resident(a.shape) for a in consts],
        out_specs=blk,
        out_shape=jax.ShapeDtypeStruct(f.shape, F32),
        compiler_params=_params(1),
        name="fourier_dense",
    )(f, *consts)


FFT_CHANNELS = 2 * GROUP_W


def _fourier_two_stage_kernel(x_ref, cwc_ref, cwl_ref, mc_ref, ml_ref, f2c_ref, f2l_ref, o_ref,
                              ar_ref, ai_ref, sr_ref, si_ref):
    groups = FFT_CHANNELS // GROUP_W

    def stage1(jb, carry):
        t2_0 = pl.multiple_of(jb * FFT_ROWS, FFT_ROWS)
        x = x_ref[pl.ds(t2_0, FFT_ROWS)].reshape(FFT_ROWS * FFT_N, FFT_CHANNELS)
        zr, zi = [], []
        for g in range(groups):
            z = _dot_data_table(x[:, g * GROUP_W:(g + 1) * GROUP_W], cwc_ref, cwl_ref)
            zr.append(z[:, :GROUP_W])
            zi.append(z[:, GROUP_W:])
        zr = jnp.concatenate(zr, axis=1)
        zi = jnp.concatenate(zi, axis=1)
        for j in range(FFT_ROWS):
            rows = slice(j * FFT_N, (j + 1) * FFT_N)
            a = _dot_table_data(mc_ref[t2_0 + j], ml_ref[t2_0 + j], jnp.concatenate([zr[rows], zi[rows]], axis=0))
            for g in range(FFT_CHANNELS // LANES):
                lanes = slice(g * LANES, (g + 1) * LANES)
                sr_ref[g, pl.ds(j, FFT_N, stride=FFT_ROWS), :] = a[:FFT_N, lanes]
                si_ref[g, pl.ds(j, FFT_N, stride=FFT_ROWS), :] = a[FFT_N:, lanes]
        for g in range(FFT_CHANNELS // LANES):
            lanes = slice(g * LANES, (g + 1) * LANES)
            ar_ref[:, pl.ds(t2_0, FFT_ROWS), lanes] = sr_ref[g].reshape(FFT_N, FFT_ROWS, LANES)
            ai_ref[:, pl.ds(t2_0, FFT_ROWS), lanes] = si_ref[g].reshape(FFT_N, FFT_ROWS, LANES)
        return carry

    def stage2(jb, carry):
        for j in range(FFT_ROWS):
            s1 = jb * FFT_ROWS + j
            aa = jnp.concatenate([ar_ref[s1], ai_ref[s1]], axis=0)
            o_ref[s1] = _dot_table_data(f2c_ref[...], f2l_ref[...], aa)
        return carry

    lax.fori_loop(0, FFT_N // FFT_ROWS, stage1, 0)
    lax.fori_loop(0, FFT_N // FFT_ROWS, stage2, 0)


def _fourier_two_stage(x, nb):
    n = FFT_N
    t_len = n * n
    s1 = np.arange(n)[:, None]
    t1 = np.arange(n)[None, :]
    mats = []
    for t2 in range(n):
        ang = 2.0 * np.pi * ((s1 * t1 * n + s1 * t2) % t_len) / t_len
        c, s = np.cos(ang), np.sin(ang)
        mats.append(np.block([[c, s], [-s, c]]))
    m_cat, m_lo = _table_pieces(np.stack(mats), 2)
    c2, s2 = _dft_cos_sin(n)
    f2 = _table_pieces(np.concatenate([c2, s2], axis=1), 1)
    cw = _table_pieces(_width_dft_table(t_len), 0)
    consts = [*cw, m_cat, m_lo, *f2]

    blk = pl.BlockSpec((None, n, n, FFT_CHANNELS), lambda b, h: (b, 0, 0, h))
    plane = pltpu.VMEM((n, n, FFT_CHANNELS), F32)
    slabs = pltpu.VMEM((FFT_CHANNELS // LANES, n * FFT_ROWS, LANES), F32)
    return pl.pallas_call(
        _fourier_two_stage_kernel,
        grid=(nb, D_FOURIER // FFT_CHANNELS),
        in_specs=[blk] + [_resident(a.shape) for a in consts],
        out_specs=blk,
        out_shape=jax.ShapeDtypeStruct((nb, n, n, D_FOURIER), F32),
        scratch_shapes=[plane, plane, slabs, slabs],
        compiler_params=_params(2),
        name="fourier_two_stage",
    )(x, *consts)


def _gla_masks(chunk):
    idx = np.arange(chunk)
    i = idx[:, None]
    j = idx[None, :]
    masks = [i == j]
    for l in range(GLA_LEVELS):
        same = (i >> (l + 1)) == (j >> (l + 1))
        masks.append(same & (((i >> l) & 1) == 1) & (((j >> l) & 1) == 0))
    mask_f = np.stack(masks).astype(np.float32)
    return mask_f, np.transpose(mask_f, (0, 2, 1)).copy()


_NT = (((1,), (1,)), ((), ()))
_TN = (((0,), (0,)), ((), ()))


def _block_sums(la, forward):
    c = la.shape[0]
    nv = c // SUBLANES
    x = la.reshape(nv, SUBLANES, DK)
    sub = lax.broadcasted_iota(jnp.int32, (1, SUBLANES, DK), 1)
    near = x
    tot = x
    out = {}
    w = 1
    while w < SUBLANES:
        right = (sub & w) != 0
        partner = jnp.where(right, pltpu.roll(tot, w, axis=1), pltpu.roll(tot, SUBLANES - w, axis=1))
        near = near + jnp.where(right if forward else jnp.logical_not(right), partner, 0.0)
        tot = tot + partner
        w *= 2
        out[w] = (near, tot - near)
    blk = 1
    while w < c:
        takes = (lambda v: v & blk) if forward else (lambda v: not v & blk)
        near = jnp.stack([near[v] + tot[v ^ blk] if takes(v) else near[v] for v in range(nv)])
        tot = jnp.stack([tot[v] + tot[v ^ blk] for v in range(nv)])
        w *= 2
        blk *= 2
        out[w] = (near, tot - near)
    return {w: (a.reshape(c, DK), b.reshape(c, DK)) for w, (a, b) in out.items()}


def _gla_prepare(q_ref, k_ref, laf_ref, lab_ref, qs_ref, ks_ref, tot_ref, rows_f, rows_b):
    c = GLA_CHUNK
    for g in range(len(rows_f)):
        for fwd in (True, False):
            chain = 2 * g + (0 if fwd else 1)
            rows = pl.ds(rows_f[g] if fwd else rows_b[g], c)
            q = q_ref[rows, :]
            k = k_ref[rows, :]
            la = (laf_ref if fwd else lab_ref)[rows, :]
            sums = _block_sums(la, fwd)
            qs_ref[chain, 0:c, :] = q.astype(BF16)
            ks_ref[chain, 0:c, :] = k.astype(BF16)
            for l in range(GLA_LEVELS + 1):
                w = 1 << l
                if l == 0:
                    d, e = jnp.exp2(la), None
                elif 2 * SUBLANES <= w < c:
                    near, far = sums[w]
                    for b in range(c // w):
                        blk = slice(b * w, (b + 1) * w)
                        if bool(b & 1) == fwd:
                            qs_ref[chain, (l + 1) * c + b * w:(l + 1) * c + (b + 1) * w, :] = (
                                q[blk] * jnp.exp2(near[blk])).astype(BF16)
                        else:
                            ks_ref[chain, l * c + b * w:l * c + (b + 1) * w, :] = (
                                k[blk] * jnp.exp2(far[blk])).astype(BF16)
                    continue
                else:
                    near, far = sums[w]
                    d, e = jnp.exp2(near), jnp.exp2(far)
                qs_ref[chain, (l + 1) * c:(l + 2) * c, :] = (q * d).astype(BF16)
                if e is not None:
                    ks_ref[chain, l * c:(l + 1) * c, :] = (k * e).astype(BF16)
            tot_ref[chain] = jnp.broadcast_to(d[c - 1:c, :] if fwd else d[0:1, :], (SUBLANES, DK))


def _gla_apply(v_ref, maskf_ref, maskb_ref, sf_ref, sb_ref, o_ref, work):
    c = GLA_CHUNK
    levels = GLA_LEVELS
    chains = []
    for qs_ref, ks_ref, tot_ref, rows_f, rows_b, accumulate in work:
        for chain in range(2 * len(rows_f)):
            fwd = chain % 2 == 0
            rows = pl.ds((rows_f if fwd else rows_b)[chain // 2], c)
            chains.append((qs_ref, ks_ref, tot_ref, chain, fwd, rows, accumulate))
    scores = []
    for qs_ref, ks_ref, tot_ref, chain, fwd, rows, accumulate in chains:
        m_ref = maskf_ref if fwd else maskb_ref
        nv = c // SUBLANES
        tiles = [None] * nv
        for l in range(levels + 1):
            kl = max(l - 1, 0)
            p = lax.dot_general(qs_ref[chain, l * c:(l + 1) * c, :], ks_ref[chain, kl * c:(kl + 1) * c, :], _NT,
                                preferred_element_type=F32)
            level = l - 1
            for r in range(nv):
                if level >= 3 and ((r >> (level - 3)) & 1) != (1 if fwd else 0):
                    continue
                rows_r = slice(r * SUBLANES, (r + 1) * SUBLANES)
                t = p[rows_r] * m_ref[l, rows_r, :]
                tiles[r] = t if tiles[r] is None else tiles[r] + t
        scores.append(jnp.concatenate(tiles, axis=0).astype(BF16))
    for (qs_ref, ks_ref, tot_ref, chain, fwd, rows, accumulate), s in zip(chains, scores):
        st_ref = sf_ref if fwd else sb_ref
        v = v_ref[rows, :]
        st = st_ref[...]
        q_top = qs_ref[chain, (levels + 1) * c:(levels + 2) * c, :]
        k_top = ks_ref[chain, levels * c:(levels + 1) * c, :]
        o = _dot(s, v) + _dot(q_top, st.astype(BF16))
        tot = jnp.broadcast_to(tot_ref[chain][0:1, :], (DK, DK)).T
        st_ref[...] = (st * jnp.concatenate([tot] * (DV // DK), axis=1)
                       + lax.dot_general(k_top, v, _TN, preferred_element_type=F32))
        if accumulate:
            o_ref[rows, :] += o
        else:
            o_ref[rows, :] = o


def _gla_kernel(q_ref, k_ref, v_ref, laf_ref, lab_ref, *refs, n_chunks, group, zero_start):
    (s0f_ref, s0b_ref), refs = (refs[:2], refs[2:]) if not zero_start else ((None, None), refs)
    maskf_ref, maskb_ref, o_ref, sf_ref, sb_ref, qs0_ref, ks0_ref, tot0_ref, qs1_ref, ks1_ref, tot1_ref = refs
    sf_ref[...] = jnp.zeros(sf_ref.shape, F32) if zero_start else s0f_ref[...]
    sb_ref[...] = jnp.zeros(sb_ref.shape, F32) if zero_start else s0b_ref[...]
    steps = n_chunks // group
    slots = ((qs0_ref, ks0_ref, tot0_ref), (qs1_ref, ks1_ref, tot1_ref))

    @pl.when((pl.program_id(0) == 0) & (pl.program_id(1) == 0))
    def _():
        for qs_ref, ks_ref, _ in slots:
            qs_ref[...] = jnp.zeros(qs_ref.shape, BF16)
            ks_ref[...] = jnp.zeros(ks_ref.shape, BF16)

    def rows_of(i):
        rows_f = [pl.multiple_of((i * group + g) * GLA_CHUNK, GLA_CHUNK) for g in range(group)]
        rows_b = [pl.multiple_of((n_chunks - 1 - i * group - g) * GLA_CHUNK, GLA_CHUNK) for g in range(group)]
        return rows_f, rows_b

    def prepare(i, slot):
        _gla_prepare(q_ref, k_ref, laf_ref, lab_ref, *slots[slot], *rows_of(i))

    def apply(items):
        _gla_apply(v_ref, maskf_ref, maskb_ref, sf_ref, sb_ref, o_ref,
                   [(*slots[slot], *rows_of(i), accumulate) for i, slot, accumulate in items])

    def pair(j, accumulate, last):
        apply([(2 * j, 0, accumulate[0]), (2 * j + 1, 1, accumulate[1])])
        if not last:
            prepare(2 * j + 2, 0)
            prepare(2 * j + 3, 1)

    n_pairs = steps // 2
    prepare(0, 0)
    prepare(1, 1)
    if n_pairs == 1:
        pair(0, (False, True), True)
    else:
        def body(accumulate):
            def f(j, carry):
                pair(j, (accumulate, accumulate), False)
                return carry
            return f
        lax.fori_loop(0, n_pairs // 2, body(False), 0)
        lax.fori_loop(n_pairs // 2, n_pairs - 1, body(True), 0)
        pair(n_pairs - 1, (True, True), True)


def _gla(q, k, v, laf, lab, s0, nb, t_len):
    n_chunks = t_len // GLA_CHUNK
    group = next(g for g in (GLA_GROUP, GLA_GROUP // 2) if n_chunks % (4 * g) == 0 or n_chunks == 2 * g)
    steps = n_chunks // group
    assert t_len % GLA_CHUNK == 0 and n_chunks % group == 0 and (steps == 2 or steps % 4 == 0)
    consts = [jnp.asarray(m) for m in _gla_masks(GLA_CHUNK)]

    def seq(width):
        return pl.BlockSpec((t_len, width), lambda b, h: (b, h))

    state = pl.BlockSpec((None, None, DK, DV), lambda b, h: (b, h, 0, 0))
    n = nb * t_len
    slot = [pltpu.VMEM((2 * group, (GLA_LEVELS + 2) * GLA_CHUNK, DK), BF16),
            pltpu.VMEM((2 * group, (GLA_LEVELS + 1) * GLA_CHUNK, DK), BF16),
            pltpu.VMEM((2 * group, SUBLANES, DK), F32)]
    starts = [] if s0 is None else list(s0)
    return pl.pallas_call(
        functools.partial(_gla_kernel, n_chunks=n_chunks, group=group, zero_start=s0 is None),
        grid=(nb, N_HEADS),
        in_specs=[seq(DK), seq(DK), seq(DV), seq(DK), seq(DK)] + [state] * len(starts)
                 + [_resident(a.shape) for a in consts],
        out_specs=[seq(DV), state, state],
        out_shape=[jax.ShapeDtypeStruct((n, DV_TOT), F32),
                   jax.ShapeDtypeStruct((nb, N_HEADS, DK, DV), F32),
                   jax.ShapeDtypeStruct((nb, N_HEADS, DK, DV), F32)],
        scratch_shapes=slot + slot,
        compiler_params=_params(2),
        name="gla",
    )(q, k, v, laf, lab, *starts, *consts)


def _mixout_kernel(mod_ref, m_ref, o_ref, r_ref, g_ref, gn_ref, wpf_ref, wpg_ref, wout_ref, y_ref,
                   *scratch, d_model, fft_layout):
    if fft_layout:
        slab_ref, = scratch
        rows = m_ref.shape[1]
        for g in range(D_FOURIER // LANES):
            slab_ref[g] = m_ref[:, :, g * LANES:(g + 1) * LANES].reshape(FFT_N * rows, LANES)
        m = jnp.concatenate(
            [jnp.concatenate([slab_ref[g, pl.ds(s, FFT_N, stride=rows), :] for g in range(D_FOURIER // LANES)],
                             axis=1) for s in range(rows)], axis=0)
    else:
        m = m_ref[...]
    branch_a = _dot(m.astype(BF16), wpf_ref[...])
    o = o_ref[...]
    parts = []
    for h in range(N_HEADS):
        oh = o[:, h * DV:(h + 1) * DV]
        parts.append(oh * lax.rsqrt(jnp.mean(oh * oh, axis=-1, keepdims=True) + RMS_EPS))
    r = r_ref[...].astype(F32)
    on = jnp.concatenate(parts, axis=1) * (gn_ref[...] * r * _sigmoid(r))
    branch_b = _dot(on.astype(BF16), wpg_ref[...])
    g = _sigmoid(g_ref[...].astype(F32))
    merged = g[:, :d_model] * branch_a + g[:, d_model:] * branch_b
    y = _dot(merged.astype(BF16), wout_ref[...])
    y_ref[...] = mod_ref[5:6, :] * y


def _mixout(mod, m, o, r, g, gla_norm, wpf, wpg, wout, *, tiles_per_group, fft_layout):
    n, d = o.shape[0], wout.shape[1]
    tm = TOKEN_TILE

    def tok(width):
        return pl.BlockSpec((tm, width), lambda i: (i, 0))

    m_spec = tok(D_FOURIER)
    scratch = []
    if fft_layout:
        rows = tm // FFT_N
        tiles = FFT_N // rows
        m_spec = pl.BlockSpec((None, FFT_N, rows, D_FOURIER), lambda i: (i // tiles, 0, i % tiles, 0))
        scratch = [pltpu.VMEM((D_FOURIER // LANES, tm, LANES), F32)]
    return pl.pallas_call(
        functools.partial(_mixout_kernel, d_model=d, fft_layout=fft_layout),
        grid=(n // tm,),
        in_specs=[pl.BlockSpec((None, N_MOD, d), lambda i: (i // tiles_per_group, 0, 0)),
                  m_spec, tok(DV_TOT), tok(DV_TOT), tok(2 * d),
                  _resident((1, DV_TOT)), _resident(wpf.shape), _resident(wpg.shape), _resident(wout.shape)],
        out_specs=tok(d),
        out_shape=jax.ShapeDtypeStruct((n, d), F32),
        scratch_shapes=scratch,
        compiler_params=_params(1),
        name="mixout",
    )(mod, m, o, r, g, gla_norm.reshape(1, DV_TOT), wpf, wpg, wout)


def _grid_pos_tables(n_tokens, d_model):
    n_freq = d_model // 4
    omega = POS_BASE ** (-jnp.arange(n_freq, dtype=F32) / n_freq)
    ra = jnp.arange(n_tokens // GRID_W, dtype=F32)[:, None] * omega
    ca = jnp.arange(GRID_W, dtype=F32)[:, None] * omega
    return (jnp.concatenate([jnp.sin(ra), jnp.cos(ra)], axis=-1),
            jnp.concatenate([jnp.sin(ca), jnp.cos(ca)], axis=-1))


def _trunk_layer(x, mod, s0, nb, t_len, w, *, pos, final_gain, two_stage_fft):
    tiles = max(t_len // TOKEN_TILE, 1) if mod.shape[0] > 1 else x.shape[0] // TOKEN_TILE
    x = _ffn(x, mod, w["norm_ffn1"], w["ffn1_gate"], w["ffn1_up"], w["ffn1_down"],
             tiles_per_group=tiles, first_mod_row=0, pos=pos)
    f, q, k, v, r, g, laf, lab = _mixin(x, mod, w["norm_mix"], w["w_main"], w["w_tail"], w["w_alpha"],
                                        w["b_alpha"],
                                        tiles_per_group=tiles, fft_layout=two_stage_fft)
    m = _fourier_two_stage(f, nb) if two_stage_fft else _fourier_dense(f, nb, t_len)
    o, sf, sb = _gla(q, k, v, laf, lab, s0, nb, t_len)
    y = _mixout(mod, m, o, r, g, w["gla_norm"], w["proj_fourier"], w["proj_gla"], w["w_out"],
                tiles_per_group=tiles, fft_layout=two_stage_fft)
    x = _ffn(x, mod, w["norm_ffn2"], w["ffn2_gate"], w["ffn2_up"], w["ffn2_down"],
             tiles_per_group=tiles, first_mod_row=6, branch=y, final_gain=final_gain)
    return x, sf, sb


def kernel(x_prompt, x_sample, state_gla_fwd, state_gla_bwd, c, c_ctx, w_ada, b_ada, norm_ffn1, w_ffn1_gate, w_ffn1_up, w_ffn1_down, norm_mix, w_in, w_alpha_fwd, b_alpha_fwd, w_alpha_bwd, b_alpha_bwd, gla_norm, w_proj_fourier, w_proj_gla, w_out, norm_ffn2, w_ffn2_gate, w_ffn2_up, w_ffn2_down, final_norm):
    nb_ctx, t_ctx, d = x_prompt.shape
    nb_lat, t_lat, _ = x_sample.shape
    depth = w_ada.shape[0]
    assert t_lat == FFT_N * FFT_N and (nb_ctx * t_ctx) % TOKEN_TILE == 0 and t_lat % TOKEN_TILE == 0

    xc = x_prompt.reshape(nb_ctx * t_ctx, d)
    xl = x_sample.reshape(nb_lat * t_lat, d)
    pos = _grid_pos_tables(t_lat, d)
    pad_rows = (-(nb_lat + 1)) % 8
    cvecs = jnp.concatenate([c, c_ctx[None, :], jnp.zeros((pad_rows, d), F32)], axis=0)

    new_fwd, new_bwd = [], []
    for l in range(depth):
        cut = _C_G
        w_in_l = w_in[l]
        w_main = w_in_l[:, :cut].astype(BF16)
        w_tail = jnp.concatenate(
            [w_in_l[:, cut + 2 * GATE_RANK:].astype(BF16), w_in_l[:, cut:cut + 2 * GATE_RANK].astype(BF16),
             jnp.zeros((d, ALR_PAD - 2 * GATE_RANK), BF16)], axis=1)
        w_alpha = jnp.zeros((ALR_PAD, 2 * DK_TOT), F32)
        w_alpha = w_alpha.at[:GATE_RANK, :DK_TOT].set(w_alpha_fwd[l])
        w_alpha = w_alpha.at[GATE_RANK:2 * GATE_RANK, DK_TOT:].set(w_alpha_bwd[l]).astype(BF16)
        b_alpha = jnp.concatenate([b_alpha_fwd[l], b_alpha_bwd[l]]).reshape(1, 2 * DK_TOT)
        w = {
            "norm_ffn1": norm_ffn1[l], "ffn1_gate": w_ffn1_gate[l].astype(BF16),
            "ffn1_up": w_ffn1_up[l].astype(BF16), "ffn1_down": w_ffn1_down[l].astype(BF16),
            "norm_mix": norm_mix[l], "w_main": w_main, "w_tail": w_tail, "w_alpha": w_alpha, "b_alpha": b_alpha,
            "gla_norm": gla_norm[l], "proj_fourier": w_proj_fourier[l].astype(BF16),
            "proj_gla": w_proj_gla[l].astype(BF16), "w_out": w_out[l].astype(BF16),
            "norm_ffn2": norm_ffn2[l], "ffn2_gate": w_ffn2_gate[l].astype(BF16),
            "ffn2_up": w_ffn2_up[l].astype(BF16), "ffn2_down": w_ffn2_down[l].astype(BF16),
        }
        last = l == depth - 1
        mod = _ada(cvecs, w_ada[l], b_ada[l]).reshape(-1, N_MOD, d)
        mod_lat = mod[:nb_lat]
        mod_ctx = mod[nb_lat:nb_lat + 1]
        xc, sf, sb = _trunk_layer(xc, mod_ctx, None, nb_ctx, t_ctx, w,
                                  pos=None, final_gain=final_norm if last else None, two_stage_fft=False)
        new_fwd.append(sf)
        new_bwd.append(sb)
        xl, _, _ = _trunk_layer(xl, mod_lat, (state_gla_fwd[:, l], state_gla_bwd[:, l]), nb_lat, t_lat, w,
                                pos=pos if l == 0 else None, final_gain=final_norm if last else None,
                                two_stage_fft=True)
    y_prompt = xc.reshape(nb_ctx, t_ctx, d)
    y_sample = xl.reshape(nb_lat, t_lat, d)
    return (y_prompt, y_sample, jnp.stack(new_fwd, axis=1), jnp.stack(new_bwd, axis=1))
```

```python
import functools
import math

import numpy as np
import jax
import jax.numpy as jnp
from jax import lax
from jax.experimental import pallas as pl
from jax.experimental.pallas import tpu as pltpu

F32 = jnp.float32
BF16 = jnp.bfloat16

GRID_W = 64
N_GROUPS = 4
GROUP_W = 128
D_FOURIER = N_GROUPS * GROUP_W
N_HEADS = 4
DK = 128
DV = 256
DK_TOT = N_HEADS * DK
DV_TOT = N_HEADS * DV
GATE_RANK = 16
GATE_TEMP = 16.0
RMS_EPS = 1e-6
POS_BASE = 10000.0
N_MOD = 9

TOKEN_TILE = 512
GLA_CHUNK = 64
GLA_LEVELS = GLA_CHUNK.bit_length() - 1
GLA_GROUP = 4
LANES = 128
SUBLANES = 8
ALR_PAD = LANES
FFT_N = 64
FFT_ROWS = SUBLANES
VMEM_LIMIT = 56 * 1024 * 1024


def _params(n_axes):
    return pltpu.CompilerParams(dimension_semantics=("arbitrary",) * n_axes,
                                vmem_limit_bytes=VMEM_LIMIT)


def _resident(shape):
    nd = len(shape)
    return pl.BlockSpec(shape, lambda *_: (0,) * nd, pipeline_mode=pl.Buffered(1))


LOG2_E = math.log2(math.e)


def _sigmoid(x):
    return 1.0 / (1.0 + jnp.exp2(x * (-LOG2_E)))


def _rms(x, gain):
    return x * lax.rsqrt(jnp.mean(x * x, axis=-1, keepdims=True) + RMS_EPS) * gain


def _modulated_rms(x, gain, scale, shift):
    return x * lax.rsqrt(jnp.mean(x * x, axis=-1, keepdims=True) + RMS_EPS) * (gain * (1.0 + scale)) + shift


def _dot(a, b):
    return jnp.dot(a, b, preferred_element_type=F32)


def _ada_kernel(c_ref, w_ref, b_ref, o_ref):
    c = c_ref[...]
    s = (c * _sigmoid(c)).astype(BF16)
    o_ref[...] = _dot(s, w_ref[...].astype(BF16)) + b_ref[...]


def _ada(cvecs, w_ada, b_ada):
    rows, d = cvecs.shape
    n = w_ada.shape[1]
    tn = d
    return pl.pallas_call(
        _ada_kernel,
        grid=(n // tn,),
        in_specs=[pl.BlockSpec((rows, d), lambda j: (0, 0)),
                  pl.BlockSpec((d, tn), lambda j: (0, j)),
                  pl.BlockSpec((1, tn), lambda j: (0, j))],
        out_specs=pl.BlockSpec((rows, tn), lambda j: (0, j)),
        out_shape=jax.ShapeDtypeStruct((rows, n), F32),
        compiler_params=_params(1),
        name="ada",
    )(cvecs, w_ada, b_ada.reshape(1, n))


def _bits_zero(v):
    bits = pltpu.bitcast(v, jnp.uint32)
    r, c = bits.shape
    t = bits.reshape(r // SUBLANES, SUBLANES, c)
    acc = t[0]
    for i in range(1, r // SUBLANES):
        acc = acc | t[i]
    row = acc[:, 0:LANES]
    for i in range(1, c // LANES):
        row = row | acc[:, i * LANES:(i + 1) * LANES]
    zero = jnp.right_shift(jnp.right_shift(row, jnp.uint32(16)), jnp.uint32(16))
    return pltpu.bitcast(zero, F32)[0:1, :]


def _ffn_kernel(*refs, first_mod_row, add_branch, add_pos, final_norm):
    x_ref, xnext_ref, mod_ref, modnext_ref, gain_ref, wg_ref, wu_ref, wd_ref = refs[:8]
    rest = list(refs[8:])
    branch_ref, branchnext_ref = (rest.pop(0), rest.pop(0)) if add_branch else (None, None)
    prow_ref, prownext_ref, pcol_ref = (rest.pop(0), rest.pop(0), rest.pop(0)) if add_pos else (None, None, None)
    fn_ref = rest.pop(0) if final_norm else None
    o_ref, h_ref = rest

    def residual_input(xr, br, pr):
        x = xr[...]
        if add_branch:
            x = x + br[...]
        if add_pos:
            tm, d = x.shape
            rows = tm // GRID_W
            prow = jnp.broadcast_to(pr[...][:, None, :], (rows, GRID_W, d // 2))
            pcol = jnp.broadcast_to(pcol_ref[...][None, :, :], (rows, GRID_W, d // 2))
            x = x + jnp.concatenate([prow, pcol], axis=-1).reshape(tm, d)
        return x

    def normalised(xr, br, mr, pr):
        shift = mr[first_mod_row:first_mod_row + 1, :]
        scale = mr[first_mod_row + 1:first_mod_row + 2, :]
        return _modulated_rms(residual_input(xr, br, pr), gain_ref[...], scale, shift).astype(BF16)

    @pl.when(pl.program_id(0) == 0)
    def _():
        h_ref[...] = normalised(x_ref, branch_ref, mod_ref, prow_ref)

    h = h_ref[...]
    g = _dot(h, wg_ref[...])
    h_next = normalised(xnext_ref, branchnext_ref, modnext_ref, prownext_ref)
    u = _dot(h + jnp.tile(_bits_zero(h_next), (1, h.shape[1] // LANES)).astype(BF16), wu_ref[...])
    h_ref[...] = h_next
    a = (g * _sigmoid(g) * u).astype(BF16)
    y = _dot(a, wd_ref[...])
    gate = mod_ref[first_mod_row + 2:first_mod_row + 3, :]
    xn = residual_input(x_ref, branch_ref, prow_ref) + (0.5 * gate) * y
    if final_norm:
        xn = _rms(xn, fn_ref[...])
    o_ref[...] = xn


def _ffn(x, mod, gain, wg, wu, wd, *, tiles_per_group, first_mod_row, branch=None, pos=None, final_gain=None):
    n, d = x.shape
    dff = wg.shape[1]
    tm = TOKEN_TILE
    last = n // tm - 1
    add_branch = branch is not None
    add_pos = pos is not None
    final_norm = final_gain is not None

    def nxt(i):
        return jnp.minimum(i + 1, last)

    def tile(at):
        return pl.BlockSpec((tm, d), lambda i: (at(i), 0))

    def mods(at):
        return pl.BlockSpec((None, N_MOD, d), lambda i: (at(i) // tiles_per_group, 0, 0))

    in_specs = [tile(lambda i: i), tile(nxt), mods(lambda i: i), mods(nxt),
                _resident((1, d)), _resident((d, dff)), _resident((d, dff)), _resident((dff, d))]
    args = [x, x, mod, mod, gain.reshape(1, d), wg, wu, wd]
    if add_branch:
        in_specs += [tile(lambda i: i), tile(nxt)]
        args += [branch, branch]
    if add_pos:
        pos_row, pos_col = pos
        rows = tm // GRID_W
        row_tiles = pos_row.shape[0] // rows
        assert tm % GRID_W == 0 and rows % 8 == 0 and pos_col.shape[0] == GRID_W
        in_specs += [pl.BlockSpec((rows, d // 2), lambda i: (i % row_tiles, 0)),
                     pl.BlockSpec((rows, d // 2), lambda i: (nxt(i) % row_tiles, 0)),
                     _resident(pos_col.shape)]
        args += [pos_row, pos_row, pos_col]
    if final_norm:
        in_specs.append(_resident((1, d)))
        args.append(final_gain.reshape(1, d))
    return pl.pallas_call(
        functools.partial(_ffn_kernel, first_mod_row=first_mod_row, add_branch=add_branch, add_pos=add_pos,
                          final_norm=final_norm),
        grid=(n // tm,),
        in_specs=in_specs,
        out_specs=pl.BlockSpec((tm, d), lambda i: (i, 0)),
        out_shape=jax.ShapeDtypeStruct((n, d), F32),
        scratch_shapes=[pltpu.VMEM((tm, d), BF16)],
        compiler_params=_params(1),
        name="ffn",
    )(*args)


_C_F = 0
_C_Q = _C_F + D_FOURIER
_C_K = _C_Q + DK_TOT
_C_V = _C_K + DK_TOT
_C_R = _C_V + DV_TOT
_C_G = _C_R + DV_TOT


def _mixin_kernel(x_ref, mod_ref, gain_ref, w_ref, wt_ref, wal_ref, bal_ref,
                  f_ref, q_ref, k_ref, v_ref, r_ref, g_ref, laf_ref, lab_ref, *scratch, d_model, fft_layout):
    x = x_ref[...]
    shift = mod_ref[3:4, :]
    scale = mod_ref[4:5, :]
    h = _modulated_rms(x, gain_ref[...], scale, shift).astype(BF16)

    def proj(lo, hi):
        return _dot(h, w_ref[:, lo:hi])

    alr = _dot(h, wt_ref[:, 2 * d_model:2 * d_model + ALR_PAD]).astype(BF16)
    g_ref[...] = _dot(h, wt_ref[:, :2 * d_model]).astype(BF16)
    pre = _dot(alr, wal_ref[...]) + bal_ref[...]
    la = (jnp.minimum(pre, 0.0) * (LOG2_E / GATE_TEMP)
          - jnp.log2(1.0 + jnp.exp2(jnp.abs(pre) * (-LOG2_E))) * (1.0 / GATE_TEMP))
    laf_ref[...] = la[:, :DK_TOT]
    lab_ref[...] = la[:, DK_TOT:]

    r_ref[...] = proj(_C_R, _C_G).astype(BF16)
    v_ref[...] = proj(_C_V, _C_R).astype(BF16)
    f = proj(_C_F, _C_Q)
    if fft_layout:
        slab_ref, = scratch
        rows = x.shape[0] // FFT_N
        for a in range(rows):
            for g in range(D_FOURIER // LANES):
                slab_ref[g, pl.ds(a, FFT_N, stride=rows), :] = f[a * FFT_N:(a + 1) * FFT_N, g * LANES:(g + 1) * LANES]
        for g in range(D_FOURIER // LANES):
            f_ref[:, :, g * LANES:(g + 1) * LANES] = slab_ref[g].reshape(FFT_N, rows, LANES)
    else:
        f_ref[...] = f
    q_ref[...] = proj(_C_Q, _C_K) * (DK ** -0.5)
    k_ref[...] = proj(_C_K, _C_V)


def _mixin(x, mod, gain, w_main, w_tail, w_alpha, b_alpha, *, tiles_per_group, fft_layout):
    n, d = x.shape
    tm = TOKEN_TILE

    def tok(width):
        return pl.BlockSpec((tm, width), lambda i: (i, 0))

    outs = [(D_FOURIER, F32), (DK_TOT, F32), (DK_TOT, F32), (DV_TOT, BF16), (DV_TOT, BF16),
            (2 * d, BF16), (DK_TOT, F32), (DK_TOT, F32)]
    out_specs = [tok(w) for w, _ in outs]
    out_shape = [jax.ShapeDtypeStruct((n, w), dt) for w, dt in outs]
    scratch = []
    if fft_layout:
        rows = tm // FFT_N
        tiles = FFT_N // rows
        assert tm % FFT_N == 0 and rows % 8 == 0 and n % (FFT_N * FFT_N) == 0
        out_specs[0] = pl.BlockSpec((None, FFT_N, rows, D_FOURIER), lambda i: (i // tiles, 0, i % tiles, 0))
        out_shape[0] = jax.ShapeDtypeStruct((n // (FFT_N * FFT_N), FFT_N, FFT_N, D_FOURIER), F32)
        scratch = [pltpu.VMEM((D_FOURIER // LANES, tm, LANES), F32)]
    return pl.pallas_call(
        functools.partial(_mixin_kernel, d_model=d, fft_layout=fft_layout),
        grid=(n // tm,),
        in_specs=[tok(d),
                  pl.BlockSpec((None, N_MOD, d), lambda i: (i // tiles_per_group, 0, 0)),
                  _resident((1, d)), _resident(w_main.shape), _resident(w_tail.shape),
                  _resident((ALR_PAD, 2 * DK_TOT)), _resident((1, 2 * DK_TOT))],
        out_specs=out_specs,
        out_shape=out_shape,
        scratch_shapes=scratch,
        compiler_params=_params(1),
        name="mixin",
    )(x, mod, gain.reshape(1, d), w_main, w_tail, w_alpha, b_alpha)


def _dft_cos_sin(n):
    k = np.arange(n)
    ang = 2.0 * np.pi * ((k[:, None] * k[None, :]) % n) / n
    return np.cos(ang), np.sin(ang)


def _split_hi_lo(x):
    hi = x.astype(BF16)
    return hi, (x - hi.astype(F32)).astype(BF16)


def _table_pieces(table, axis):
    hi, lo = _split_hi_lo(jnp.asarray(table.astype(np.float32)))
    return jnp.concatenate([hi, hi], axis=axis), lo


def _dot_data_table(x, cat_ref, lo_ref):
    hi, lo = _split_hi_lo(x)
    return _dot(jnp.concatenate([hi, lo], axis=1), cat_ref[...]) + _dot(hi, lo_ref[...])


def _dot_table_data(cat, lo_t, x):
    hi, lo = _split_hi_lo(x)
    return _dot(cat, jnp.concatenate([hi, lo], axis=0)) + _dot(lo_t, hi)


def _width_dft_table(t_len):
    c, s = _dft_cos_sin(GROUP_W)
    return np.concatenate([c, -s], axis=1) / math.sqrt(t_len * GROUP_W)


def _width_dft(x, cwc_ref, cwl_ref):
    zr, zi = [], []
    for g in range(N_GROUPS):
        z = _dot_data_table(x[:, g * GROUP_W:(g + 1) * GROUP_W], cwc_ref, cwl_ref)
        zr.append(z[:, :GROUP_W])
        zi.append(z[:, GROUP_W:])
    return jnp.concatenate(zr, axis=1), jnp.concatenate(zi, axis=1)


DENSE_SEQS = 4


def _fourier_dense_kernel(x_ref, cwc_ref, cwl_ref, ftc_ref, ftl_ref, o_ref, *, t_len):
    zr, zi = _width_dft(x_ref[...], cwc_ref, cwl_ref)
    for s in range(x_ref.shape[0] // t_len):
        rows = slice(s * t_len, (s + 1) * t_len)
        zz = jnp.concatenate([zr[rows], zi[rows]], axis=0)
        o_ref[rows, :] = _dot_table_data(ftc_ref[...], ftl_ref[...], zz)


def _fourier_dense(f, nb, t_len):
    c, s = _dft_cos_sin(t_len)
    consts = [*_table_pieces(_width_dft_table(t_len), 0), *_table_pieces(np.concatenate([c, s], axis=1), 1)]
    seqs = math.gcd(nb, DENSE_SEQS)
    blk = pl.BlockSpec((seqs * t_len, D_FOURIER), lambda b: (b, 0))
    return pl.pallas_call(
        functools.partial(_fourier_dense_kernel, t_len=t_len),
        grid=(nb // seqs,),
        in_specs=[blk] + [_resident(a.shape) for a in consts],
        out_specs=blk,
        out_shape=jax.ShapeDtypeStruct(f.shape, F32),
        compiler_params=_params(1),
        name="fourier_dense",
    )(f, *consts)


FFT_CHANNELS = 2 * GROUP_W


def _fourier_two_stage_kernel(x_ref, cwc_ref, cwl_ref, mc_ref, ml_ref, f2c_ref, f2l_ref, o_ref,
                              ar_ref, ai_ref, sr_ref, si_ref):
    groups = FFT_CHANNELS // GROUP_W

    def stage1(jb, carry):
        t2_0 = pl.multiple_of(jb * FFT_ROWS, FFT_ROWS)
        x = x_ref[pl.ds(t2_0, FFT_ROWS)].reshape(FFT_ROWS * FFT_N, FFT_CHANNELS)
        zr, zi = [], []
        for g in range(groups):
            z = _dot_data_table(x[:, g * GROUP_W:(g + 1) * GROUP_W], cwc_ref, cwl_ref)
            zr.append(z[:, :GROUP_W])
            zi.append(z[:, GROUP_W:])
        zr = jnp.concatenate(zr, axis=1)
        zi = jnp.concatenate(zi, axis=1)
        for j in range(FFT_ROWS):
            rows = slice(j * FFT_N, (j + 1) * FFT_N)
            a = _dot_table_data(mc_ref[t2_0 + j], ml_ref[t2_0 + j], jnp.concatenate([zr[rows], zi[rows]], axis=0))
            for g in range(FFT_CHANNELS // LANES):
                lanes = slice(g * LANES, (g + 1) * LANES)
                sr_ref[g, pl.ds(j, FFT_N, stride=FFT_ROWS), :] = a[:FFT_N, lanes]
                si_ref[g, pl.ds(j, FFT_N, stride=FFT_ROWS), :] = a[FFT_N:, lanes]
        for g in range(FFT_CHANNELS // LANES):
            lanes = slice(g * LANES, (g + 1) * LANES)
            ar_ref[:, pl.ds(t2_0, FFT_ROWS), lanes] = sr_ref[g].reshape(FFT_N, FFT_ROWS, LANES)
            ai_ref[:, pl.ds(t2_0, FFT_ROWS), lanes] = si_ref[g].reshape(FFT_N, FFT_ROWS, LANES)
        return carry

    def stage2(jb, carry):
        for j in range(FFT_ROWS):
            s1 = jb * FFT_ROWS + j
            aa = jnp.concatenate([ar_ref[s1], ai_ref[s1]], axis=0)
            o_ref[s1] = _dot_table_data(f2c_ref[...], f2l_ref[...], aa)
        return carry

    lax.fori_loop(0, FFT_N // FFT_ROWS, stage1, 0)
    lax.fori_loop(0, FFT_N // FFT_ROWS, stage2, 0)


def _fourier_two_stage(x, nb):
    n = FFT_N
    t_len = n * n
    s1 = np.arange(n)[:, None]
    t1 = np.arange(n)[None, :]
    mats = []
    for t2 in range(n):
        ang = 2.0 * np.pi * ((s1 * t1 * n + s1 * t2) % t_len) / t_len
        c, s = np.cos(ang), np.sin(ang)
        mats.append(np.block([[c, s], [-s, c]]))
    m_cat, m_lo = _table_pieces(np.stack(mats), 2)
    c2, s2 = _dft_cos_sin(n)
    f2 = _table_pieces(np.concatenate([c2, s2], axis=1), 1)
    cw = _table_pieces(_width_dft_table(t_len), 0)
    consts = [*cw, m_cat, m_lo, *f2]

    blk = pl.BlockSpec((None, n, n, FFT_CHANNELS), lambda b, h: (b, 0, 0, h))
    plane = pltpu.VMEM((n, n, FFT_CHANNELS), F32)
    slabs = pltpu.VMEM((FFT_CHANNELS // LANES, n * FFT_ROWS, LANES), F32)
    return pl.pallas_call(
        _fourier_two_stage_kernel,
        grid=(nb, D_FOURIER // FFT_CHANNELS),
        in_specs=[blk] + [_resident(a.shape) for a in consts],
        out_specs=blk,
        out_shape=jax.ShapeDtypeStruct((nb, n, n, D_FOURIER), F32),
        scratch_shapes=[plane, plane, slabs, slabs],
        compiler_params=_params(2),
        name="fourier_two_stage",
    )(x, *consts)


def _gla_masks(chunk):
    idx = np.arange(chunk)
    i = idx[:, None]
    j = idx[None, :]
    masks = [i == j]
    for l in range(GLA_LEVELS):
        same = (i >> (l + 1)) == (j >> (l + 1))
        masks.append(same & (((i >> l) & 1) == 1) & (((j >> l) & 1) == 0))
    mask_f = np.stack(masks).astype(np.float32)
    return mask_f, np.transpose(mask_f, (0, 2, 1)).copy()


_NT = (((1,), (1,)), ((), ()))
_TN = (((0,), (0,)), ((), ()))


def _block_sums(la, forward):
    c = la.shape[0]
    nv = c // SUBLANES
    x = la.reshape(nv, SUBLANES, DK)
    sub = lax.broadcasted_iota(jnp.int32, (1, SUBLANES, DK), 1)
    near = x
    tot = x
    out = {}
    w = 1
    while w < SUBLANES:
        right = (sub & w) != 0
        partner = jnp.where(right, pltpu.roll(tot, w, axis=1), pltpu.roll(tot, SUBLANES - w, axis=1))
        near = near + jnp.where(right if forward else jnp.logical_not(right), partner, 0.0)
        tot = tot + partner
        w *= 2
        out[w] = (near, tot - near)
    blk = 1
    while w < c:
        takes = (lambda v: v & blk) if forward else (lambda v: not v & blk)
        near = jnp.stack([near[v] + tot[v ^ blk] if takes(v) else near[v] for v in range(nv)])
        tot = jnp.stack([tot[v] + tot[v ^ blk] for v in range(nv)])
        w *= 2
        blk *= 2
        out[w] = (near, tot - near)
    return {w: (a.reshape(c, DK), b.reshape(c, DK)) for w, (a, b) in out.items()}


def _gla_prepare(q_ref, k_ref, laf_ref, lab_ref, qs_ref, ks_ref, tot_ref, rows_f, rows_b):
    c = GLA_CHUNK
    for g in range(len(rows_f)):
        for fwd in (True, False):
            chain = 2 * g + (0 if fwd else 1)
            rows = pl.ds(rows_f[g] if fwd else rows_b[g], c)
            q = q_ref[rows, :]
            k = k_ref[rows, :]
            la = (laf_ref if fwd else lab_ref)[rows, :]
            sums = _block_sums(la, fwd)
            qs_ref[chain, 0:c, :] = q.astype(BF16)
            ks_ref[chain, 0:c, :] = k.astype(BF16)
            for l in range(GLA_LEVELS + 1):
                w = 1 << l
                if l == 0:
                    d, e = jnp.exp2(la), None
                elif 2 * SUBLANES <= w < c:
                    near, far = sums[w]
                    for b in range(c // w):
                        blk = slice(b * w, (b + 1) * w)
                        if bool(b & 1) == fwd:
                            qs_ref[chain, (l + 1) * c + b * w:(l + 1) * c + (b + 1) * w, :] = (
                                q[blk] * jnp.exp2(near[blk])).astype(BF16)
                        else:
                            ks_ref[chain, l * c + b * w:l * c + (b + 1) * w, :] = (
                                k[blk] * jnp.exp2(far[blk])).astype(BF16)
                    continue
                else:
                    near, far = sums[w]
                    d, e = jnp.exp2(near), jnp.exp2(far)
                qs_ref[chain, (l + 1) * c:(l + 2) * c, :] = (q * d).astype(BF16)
                if e is not None:
                    ks_ref[chain, l * c:(l + 1) * c, :] = (k * e).astype(BF16)
            tot_ref[chain] = jnp.broadcast_to(d[c - 1:c, :] if fwd else d[0:1, :], (SUBLANES, DK))


def _gla_apply(v_ref, maskf_ref, maskb_ref, sf_ref, sb_ref, o_ref, work):
    c = GLA_CHUNK
    levels = GLA_LEVELS
    chains = []
    for qs_ref, ks_ref, tot_ref, rows_f, rows_b, accumulate in work:
        for chain in range(2 * len(rows_f)):
            fwd = chain % 2 == 0
            rows = pl.ds((rows_f if fwd else rows_b)[chain // 2], c)
            chains.append((qs_ref, ks_ref, tot_ref, chain, fwd, rows, accumulate))
    scores = []
    for qs_ref, ks_ref, tot_ref, chain, fwd, rows, accumulate in chains:
        m_ref = maskf_ref if fwd else maskb_ref
        nv = c // SUBLANES
        tiles = [None] * nv
        for l in range(levels + 1):
            kl = max(l - 1, 0)
            p = lax.dot_general(qs_ref[chain, l * c:(l + 1) * c, :], ks_ref[chain, kl * c:(kl + 1) * c, :], _NT,
                                preferred_element_type=F32)
            level = l - 1
            for r in range(nv):
                if level >= 3 and ((r >> (level - 3)) & 1) != (1 if fwd else 0):
                    continue
                rows_r = slice(r * SUBLANES, (r + 1) * SUBLANES)
                t = p[rows_r] * m_ref[l, rows_r, :]
                tiles[r] = t if tiles[r] is None else tiles[r] + t
        scores.append(jnp.concatenate(tiles, axis=0).astype(BF16))
    for (qs_ref, ks_ref, tot_ref, chain, fwd, rows, accumulate), s in zip(chains, scores):
        st_ref = sf_ref if fwd else sb_ref
        v = v_ref[rows, :]
        st = st_ref[...]
        q_top = qs_ref[chain, (levels + 1) * c:(levels + 2) * c, :]
        k_top = ks_ref[chain, levels * c:(levels + 1) * c, :]
        o = _dot(s, v) + _dot(q_top, st.astype(BF16))
        tot = jnp.broadcast_to(tot_ref[chain][0:1, :], (DK, DK)).T
        st_ref[...] = (st * jnp.concatenate([tot] * (DV // DK), axis=1)
                       + lax.dot_general(k_top, v, _TN, preferred_element_type=F32))
        if accumulate:
            o_ref[rows, :] += o
        else:
            o_ref[rows, :] = o


def _gla_kernel(q_ref, k_ref, v_ref, laf_ref, lab_ref, *refs, n_chunks, group, zero_start):
    (s0f_ref, s0b_ref), refs = (refs[:2], refs[2:]) if not zero_start else ((None, None), refs)
    maskf_ref, maskb_ref, o_ref, sf_ref, sb_ref, qs0_ref, ks0_ref, tot0_ref, qs1_ref, ks1_ref, tot1_ref = refs
    sf_ref[...] = jnp.zeros(sf_ref.shape, F32) if zero_start else s0f_ref[...]
    sb_ref[...] = jnp.zeros(sb_ref.shape, F32) if zero_start else s0b_ref[...]
    steps = n_chunks // group
    slots = ((qs0_ref, ks0_ref, tot0_ref), (qs1_ref, ks1_ref, tot1_ref))

    @pl.when((pl.program_id(0) == 0) & (pl.program_id(1) == 0))
    def _():
        for qs_ref, ks_ref, _ in slots:
            qs_ref[...] = jnp.zeros(qs_ref.shape, BF16)
            ks_ref[...] = jnp.zeros(ks_ref.shape, BF16)

    def rows_of(i):
        rows_f = [pl.multiple_of((i * group + g) * GLA_CHUNK, GLA_CHUNK) for g in range(group)]
        rows_b = [pl.multiple_of((n_chunks - 1 - i * group - g) * GLA_CHUNK, GLA_CHUNK) for g in range(group)]
        return rows_f, rows_b

    def prepare(i, slot):
        _gla_prepare(q_ref, k_ref, laf_ref, lab_ref, *slots[slot], *rows_of(i))

    def apply(items):
        _gla_apply(v_ref, maskf_ref, maskb_ref, sf_ref, sb_ref, o_ref,
                   [(*slots[slot], *rows_of(i), accumulate) for i, slot, accumulate in items])

    def pair(j, accumulate, last):
        apply([(2 * j, 0, accumulate[0]), (2 * j + 1, 1, accumulate[1])])
        if not last:
            prepare(2 * j + 2, 0)
            prepare(2 * j + 3, 1)

    n_pairs = steps // 2
    prepare(0, 0)
    prepare(1, 1)
    if n_pairs == 1:
        pair(0, (False, True), True)
    else:
        def body(accumulate):
            def f(j, carry):
                pair(j, (accumulate, accumulate), False)
                return carry
            return f
        lax.fori_loop(0, n_pairs // 2, body(False), 0)
        lax.fori_loop(n_pairs // 2, n_pairs - 1, body(True), 0)
        pair(n_pairs - 1, (True, True), True)


def _gla(q, k, v, laf, lab, s0, nb, t_len):
    n_chunks = t_len // GLA_CHUNK
    group = next(g for g in (GLA_GROUP, GLA_GROUP // 2) if n_chunks % (4 * g) == 0 or n_chunks == 2 * g)
    steps = n_chunks // group
    assert t_len % GLA_CHUNK == 0 and n_chunks % group == 0 and (steps == 2 or steps % 4 == 0)
    consts = [jnp.asarray(m) for m in _gla_masks(GLA_CHUNK)]

    def seq(width):
        return pl.BlockSpec((t_len, width), lambda b, h: (b, h))

    state = pl.BlockSpec((None, None, DK, DV), lambda b, h: (b, h, 0, 0))
    n = nb * t_len
    slot = [pltpu.VMEM((2 * group, (GLA_LEVELS + 2) * GLA_CHUNK, DK), BF16),
            pltpu.VMEM((2 * group, (GLA_LEVELS + 1) * GLA_CHUNK, DK), BF16),
            pltpu.VMEM((2 * group, SUBLANES, DK), F32)]
    starts = [] if s0 is None else list(s0)
    return pl.pallas_call(
        functools.partial(_gla_kernel, n_chunks=n_chunks, group=group, zero_start=s0 is None),
        grid=(nb, N_HEADS),
        in_specs=[seq(DK), seq(DK), seq(DV), seq(DK), seq(DK)] + [state] * len(starts)
                 + [_resident(a.shape) for a in consts],
        out_specs=[seq(DV), state, state],
        out_shape=[jax.ShapeDtypeStruct((n, DV_TOT), F32),
                   jax.ShapeDtypeStruct((nb, N_HEADS, DK, DV), F32),
                   jax.ShapeDtypeStruct((nb, N_HEADS, DK, DV), F32)],
        scratch_shapes=slot + slot,
        compiler_params=_params(2),
        name="gla",
    )(q, k, v, laf, lab, *starts, *consts)


def _mixout_kernel(mod_ref, m_ref, o_ref, r_ref, g_ref, gn_ref, wpf_ref, wpg_ref, wout_ref, y_ref,
                   *scratch, d_model, fft_layout):
    if fft_layout:
        slab_ref, = scratch
        rows = m_ref.shape[1]
        for g in range(D_FOURIER // LANES):
            slab_ref[g] = m_ref[:, :, g * LANES:(g + 1) * LANES].reshape(FFT_N * rows, LANES)
        m = jnp.concatenate(
            [jnp.concatenate([slab_ref[g, pl.ds(s, FFT_N, stride=rows), :] for g in range(D_FOURIER // LANES)],
                             axis=1) for s in range(rows)], axis=0)
    else:
        m = m_ref[...]
    branch_a = _dot(m.astype(BF16), wpf_ref[...])
    o = o_ref[...]
    parts = []
    for h in range(N_HEADS):
        oh = o[:, h * DV:(h + 1) * DV]
        parts.append(oh * lax.rsqrt(jnp.mean(oh * oh, axis=-1, keepdims=True) + RMS_EPS))
    r = r_ref[...].astype(F32)
    on = jnp.concatenate(parts, axis=1) * (gn_ref[...] * r * _sigmoid(r))
    branch_b = _dot(on.astype(BF16), wpg_ref[...])
    g = _sigmoid(g_ref[...].astype(F32))
    merged = g[:, :d_model] * branch_a + g[:, d_model:] * branch_b
    y = _dot(merged.astype(BF16), wout_ref[...])
    y_ref[...] = mod_ref[5:6, :] * y


def _mixout(mod, m, o, r, g, gla_norm, wpf, wpg, wout, *, tiles_per_group, fft_layout):
    n, d = o.shape[0], wout.shape[1]
    tm = TOKEN_TILE

    def tok(width):
        return pl.BlockSpec((tm, width), lambda i: (i, 0))

    m_spec = tok(D_FOURIER)
    scratch = []
    if fft_layout:
        rows = tm // FFT_N
        tiles = FFT_N // rows
        m_spec = pl.BlockSpec((None, FFT_N, rows, D_FOURIER), lambda i: (i // tiles, 0, i % tiles, 0))
        scratch = [pltpu.VMEM((D_FOURIER // LANES, tm, LANES), F32)]
    return pl.pallas_call(
        functools.partial(_mixout_kernel, d_model=d, fft_layout=fft_layout),
        grid=(n // tm,),
        in_specs=[pl.BlockSpec((None, N_MOD, d), lambda i: (i // tiles_per_group, 0, 0)),
                  m_spec, tok(DV_TOT), tok(DV_TOT), tok(2 * d),
                  _resident((1, DV_TOT)), _resident(wpf.shape), _resident(wpg.shape), _resident(wout.shape)],
        out_specs=tok(d),
        out_shape=jax.ShapeDtypeStruct((n, d), F32),
        scratch_shapes=scratch,
        compiler_params=_params(1),
        name="mixout",
    )(mod, m, o, r, g, gla_norm.reshape(1, DV_TOT), wpf, wpg, wout)


def _grid_pos_tables(n_tokens, d_model):
    n_freq = d_model // 4
    omega = POS_BASE ** (-jnp.arange(n_freq, dtype=F32) / n_freq)
    ra = jnp.arange(n_tokens // GRID_W, dtype=F32)[:, None] * omega
    ca = jnp.arange(GRID_W, dtype=F32)[:, None] * omega
    return (jnp.concatenate([jnp.sin(ra), jnp.cos(ra)], axis=-1),
            jnp.concatenate([jnp.sin(ca), jnp.cos(ca)], axis=-1))


def _trunk_layer(x, mod, s0, nb, t_len, w, *, pos, final_gain, two_stage_fft):
    tiles = max(t_len // TOKEN_TILE, 1) if mod.shape[0] > 1 else x.shape[0] // TOKEN_TILE
    x = _ffn(x, mod, w["norm_ffn1"], w["ffn1_gate"], w["ffn1_up"], w["ffn1_down"],
             tiles_per_group=tiles, first_mod_row=0, pos=pos)
    f, q, k, v, r, g, laf, lab = _mixin(x, mod, w["norm_mix"], w["w_main"], w["w_tail"], w["w_alpha"],
                                        w["b_alpha"],
                                        tiles_per_group=tiles, fft_layout=two_stage_fft)
    m = _fourier_two_stage(f, nb) if two_stage_fft else _fourier_dense(f, nb, t_len)
    o, sf, sb = _gla(q, k, v, laf, lab, s0, nb, t_len)
    y = _mixout(mod, m, o, r, g, w["gla_norm"], w["proj_fourier"], w["proj_gla"], w["w_out"],
                tiles_per_group=tiles, fft_layout=two_stage_fft)
    x = _ffn(x, mod, w["norm_ffn2"], w["ffn2_gate"], w["ffn2_up"], w["ffn2_down"],
             tiles_per_group=tiles, first_mod_row=6, branch=y, final_gain=final_gain)
    return x, sf, sb


def kernel(x_prompt, x_sample, state_gla_fwd, state_gla_bwd, c, c_ctx, w_ada, b_ada, norm_ffn1, w_ffn1_gate, w_ffn1_up, w_ffn1_down, norm_mix, w_in, w_alpha_fwd, b_alpha_fwd, w_alpha_bwd, b_alpha_bwd, gla_norm, w_proj_fourier, w_proj_gla, w_out, norm_ffn2, w_ffn2_gate, w_ffn2_up, w_ffn2_down, final_norm):
    nb_ctx, t_ctx, d = x_prompt.shape
    nb_lat, t_lat, _ = x_sample.shape
    depth = w_ada.shape[0]
    assert t_lat == FFT_N * FFT_N and (nb_ctx * t_ctx) % TOKEN_TILE == 0 and t_lat % TOKEN_TILE == 0

    xc = x_prompt.reshape(nb_ctx * t_ctx, d)
    xl = x_sample.reshape(nb_lat * t_lat, d)
    pos = _grid_pos_tables(t_lat, d)
    pad_rows = (-(nb_lat + 1)) % 8
    cvecs = jnp.concatenate([c, c_ctx[None, :], jnp.zeros((pad_rows, d), F32)], axis=0)

    new_fwd, new_bwd = [], []
    for l in range(depth):
        cut = _C_G
        w_in_l = w_in[l]
        w_main = w_in_l[:, :cut].astype(BF16)
        w_tail = jnp.concatenate(
            [w_in_l[:, cut + 2 * GATE_RANK:].astype(BF16), w_in_l[:, cut:cut + 2 * GATE_RANK].astype(BF16),
             jnp.zeros((d, ALR_PAD - 2 * GATE_RANK), BF16)], axis=1)
        w_alpha = jnp.zeros((ALR_PAD, 2 * DK_TOT), F32)
        w_alpha = w_alpha.at[:GATE_RANK, :DK_TOT].set(w_alpha_fwd[l])
        w_alpha = w_alpha.at[GATE_RANK:2 * GATE_RANK, DK_TOT:].set(w_alpha_bwd[l]).astype(BF16)
        b_alpha = jnp.concatenate([b_alpha_fwd[l], b_alpha_bwd[l]]).reshape(1, 2 * DK_TOT)
        w = {
            "norm_ffn1": norm_ffn1[l], "ffn1_gate": w_ffn1_gate[l].astype(BF16),
            "ffn1_up": w_ffn1_up[l].astype(BF16), "ffn1_down": w_ffn1_down[l].astype(BF16),
            "norm_mix": norm_mix[l], "w_main": w_main, "w_tail": w_tail, "w_alpha": w_alpha, "b_alpha": b_alpha,
            "gla_norm": gla_norm[l], "proj_fourier": w_proj_fourier[l].astype(BF16),
            "proj_gla": w_proj_gla[l].astype(BF16), "w_out": w_out[l].astype(BF16),
            "norm_ffn2": norm_ffn2[l], "ffn2_gate": w_ffn2_gate[l].astype(BF16),
            "ffn2_up": w_ffn2_up[l].astype(BF16), "ffn2_down": w_ffn2_down[l].astype(BF16),
        }
        last = l == depth - 1
        mod = _ada(cvecs, w_ada[l], b_ada[l]).reshape(-1, N_MOD, d)
        mod_lat = mod[:nb_lat]
        mod_ctx = mod[nb_lat:nb_lat + 1]
        xc, sf, sb = _trunk_layer(xc, mod_ctx, None, nb_ctx, t_ctx, w,
                                  pos=None, final_gain=final_norm if last else None, two_stage_fft=False)
        new_fwd.append(sf)
        new_bwd.append(sb)
        xl, _, _ = _trunk_layer(xl, mod_lat, (state_gla_fwd[:, l], state_gla_bwd[:, l]), nb_lat, t_lat, w,
                                pos=pos if l == 0 else None, final_gain=final_norm if last else None,
                                two_stage_fft=True)
    y_prompt = xc.reshape(nb_ctx, t_ctx, d)
    y_sample = xl.reshape(nb_lat, t_lat, d)
    return (y_prompt, y_sample, jnp.stack(new_fwd, axis=1), jnp.stack(new_bwd, axis=1))
```
